```python
import jax
import jax.numpy as jnp
from jax import lax
import numpy as np

D_MODEL = 1024
BATCH = 1
SEQ = 16384
DEPTH = 4

HEAD_DIM = 64
RWKV_HEADS = (3 * D_MODEL) // (8 * HEAD_DIM)
DSA_HEADS = D_MODEL // (4 * HEAD_DIM)
SWA_HEADS = D_MODEL // HEAD_DIM - RWKV_HEADS - DSA_HEADS
SWA_KV_HEADS = 2
RWKV_W = RWKV_HEADS * HEAD_DIM
DSA_W = DSA_HEADS * HEAD_DIM
SWA_W = SWA_HEADS * HEAD_DIM
DECAY_LORA = 64
AAA_LORA = 64
GATE_LORA = 128
GN_EPS = 64e-5
KV_LORA = 128
IDX_HEADS = 4
IDX_DIM = 64
TOPK_MAX = 256
Q_BLOCK = 128
WINDOW = 128
N_EXPERTS = 64
TOP_K = 8
N_GROUPS = 8
TOPK_GROUPS = 4
D_EXPERT = 256
ROUTED_SCALE = 2.5
MOE_BLOCK = 128
ALPHA = (2 * DEPTH) ** 0.25
BETA = (8 * DEPTH) ** -0.25
LN_EPS = 1e-5
NEG = -1e30

RWKV_SIZES = (RWKV_W, RWKV_W, RWKV_W, DECAY_LORA, AAA_LORA, GATE_LORA)
DSA_SIZES = (DSA_W, KV_LORA, IDX_HEADS * IDX_DIM, IDX_DIM, IDX_HEADS)
SWA_SIZES = (SWA_W, SWA_KV_HEADS * HEAD_DIM, SWA_KV_HEADS * HEAD_DIM)
N_RWKV_COLS = sum(RWKV_SIZES)
N_DSA_COLS = sum(DSA_SIZES)
N_SWA_COLS = sum(SWA_SIZES)
P_IN = N_RWKV_COLS + N_DSA_COLS + N_SWA_COLS

kernel_name = 'hybrid_rwkv7_dsa_swa_moe_deepnorm'


def split_cols(t, sizes):
    return jnp.split(t, [int(s) for s in np.cumsum(sizes)[:-1]], axis=-1)


def layer_norm(x, g, b):
    xf = x.astype(jnp.float32)
    mu = xf.mean(-1, keepdims=True)
    var = jnp.square(xf - mu).mean(-1, keepdims=True)
    return ((xf - mu) * lax.rsqrt(var + LN_EPS)).astype(x.dtype) * g + b


def rms_norm(x, g):
    xf = x.astype(jnp.float32)
    return (xf * lax.rsqrt(jnp.mean(xf * xf, -1, keepdims=True) + 1e-6)).astype(x.dtype) * g


def token_shift(y):
    return jnp.pad(y, ((0, 0), (1, 0), (0, 0)))[:, :-1]


def alibi_slopes(n):
    return 2.0 ** (-8.0 * (jnp.arange(n, dtype=jnp.float32) + 1.0) / n)


def swiglu(t, w1, w3, w2):
    return (jax.nn.silu(t @ w1) * (t @ w3)) @ w2


def rwkv7_mix(cols, mu, w0, w2, a0, a2, g2, k_k, k_a, r_k, ln_g, ln_b):
    B, L, _ = cols.shape
    f32 = jnp.float32
    cols = cols + (token_shift(cols) - cols) * mu
    r, k, v, wl, al, gl = split_cols(cols, RWKV_SIZES)
    w = -jax.nn.softplus(-(w0 + jnp.tanh(wl) @ w2)) - 0.5
    a = jax.nn.sigmoid(a0 + al @ a2)
    g = jax.nn.sigmoid(gl) @ g2
    heads = lambda t: t.reshape(B, L, RWKV_HEADS, HEAD_DIM)
    kk = heads(k * k_k).astype(f32)
    kk = kk / jnp.maximum(jnp.sqrt(jnp.sum(kk * kk, -1, keepdims=True)), 1e-12)
    k = k * (1 + (a - 1) * k_a)
    rh, kh, vh, ah = heads(r), heads(k), heads(v), heads(a)
    decay = jnp.exp(-jnp.exp(heads(w).astype(f32)))
    tmaj = lambda t: jnp.moveaxis(t.astype(f32), 1, 0)

    def step(S, inp):
        r_t, w_t, k_t, v_t, kk_t, a_t = inp
        sa = jnp.einsum('bhvk,bhk->bhv', S, -kk_t)
        S = (S * w_t[:, :, None, :] + sa[..., None] * (kk_t * a_t)[:, :, None, :]
             + v_t[..., None] * k_t[:, :, None, :])
        return S, jnp.einsum('bhvk,bhk->bhv', S, r_t)

    S0 = jnp.zeros((B, RWKV_HEADS, HEAD_DIM, HEAD_DIM), f32)
    _, o = lax.scan(step, S0, (tmaj(rh), tmaj(decay), tmaj(kh), tmaj(vh), tmaj(kk), tmaj(ah)))
    o = jnp.moveaxis(o, 0, 1)
    mean = o.mean(-1, keepdims=True)
    var = jnp.square(o - mean).mean(-1, keepdims=True)
    o = ((o - mean) * lax.rsqrt(var + GN_EPS)).reshape(B, L, RWKV_W).astype(cols.dtype) * ln_g + ln_b
    bonus = jnp.sum(rh * kh * r_k, -1, keepdims=True) * vh
    return (o + bonus.reshape(B, L, RWKV_W)) * g


def dsa_mix(cols, kv_norm, w_uk, w_uv, ik_g, ik_b, slopes):
    B, L, _ = cols.shape
    f32 = jnp.float32
    topk = min(TOPK_MAX, L // 4)
    q, c_kv, iq, ik, iw = split_cols(cols, DSA_SIZES)
    c_kv = rms_norm(c_kv, kv_norm)
    q = q.reshape(B, L, DSA_HEADS, HEAD_DIM)
    q_lat = jnp.einsum('blhd,hdr->blhr', q, w_uk) * HEAD_DIM ** -0.5
    iq = iq.reshape(B, L, IDX_HEADS, IDX_DIM).astype(f32)
    ik = layer_norm(ik, ik_g, ik_b).astype(f32)
    iw = iw.astype(f32) * (IDX_HEADS ** -0.5 * IDX_DIM ** -0.5)
    nb = L // Q_BLOCK
    blk = lambda t: jnp.moveaxis(t.reshape(B, nb, Q_BLOCK, *t.shape[2:]), 1, 0)
    key_pos = jnp.arange(L)
    gather = jax.vmap(lambda c, i: c[i])

    def one_block(args):
        q_b, iq_b, iw_b, pos_b = args
        s = jnp.einsum('bqhd,bkd->bqhk', iq_b, ik)
        score = jnp.einsum('bqhk,bqh->bqk', jax.nn.relu(s), iw_b)
        causal = key_pos[None, :] <= pos_b[:, None]
        score = jnp.where(causal[None], score, NEG)
        _, idx = lax.top_k(score, topk)
        c_sel = gather(c_kv, idx)
        valid = idx <= pos_b[None, :, None]
        dist = (pos_b[None, :, None] - idx).astype(f32)
        logits = jnp.einsum('bqhr,bqkr->bhqk', q_b, c_sel).astype(f32)
        logits = logits - slopes[:, None, None] * dist[:, None]
        logits = jnp.where(valid[:, None], logits, NEG)
        p = jax.nn.softmax(logits, axis=-1).astype(c_sel.dtype)
        return jnp.einsum('bhqk,bqkr->bqhr', p, c_sel)

    pos = jnp.arange(L).reshape(nb, Q_BLOCK)
    o_lat = lax.map(one_block, (blk(q_lat), blk(iq), blk(iw), pos))
    o_lat = jnp.moveaxis(o_lat, 0, 1).reshape(B, L, DSA_HEADS, KV_LORA)
    o = jnp.einsum('blhr,hrd->blhd', o_lat, w_uv)
    return o.reshape(B, L, DSA_W)


def swa_mix(cols, sinks, slopes):
    B, L, _ = cols.shape
    f32 = jnp.float32
    nb = L // WINDOW
    G = SWA_HEADS // SWA_KV_HEADS
    q, k, v = split_cols(cols, SWA_SIZES)
    q = q.reshape(B, nb, WINDOW, SWA_KV_HEADS, G, HEAD_DIM)
    k = k.reshape(B, nb, WINDOW, SWA_KV_HEADS, HEAD_DIM)
    v = v.reshape(B, nb, WINDOW, SWA_KV_HEADS, HEAD_DIM)
    prev = lambda t: jnp.pad(t, ((0, 0), (1, 0), (0, 0), (0, 0), (0, 0)))[:, :-1]
    k2 = jnp.concatenate([prev(k), k], axis=2)
    v2 = jnp.concatenate([prev(v), v], axis=2)
    s = jnp.einsum('bnqgrd,bnkgd->bngrqk', q, k2).astype(f32) * HEAD_DIM ** -0.5
    qi = jnp.arange(WINDOW)
    kj = jnp.arange(2 * WINDOW)
    dist = qi[:, None] + WINDOW - kj[None, :]
    key_abs = jnp.arange(nb)[:, None] * WINDOW - WINDOW + kj[None, :]
    valid = ((dist >= 0) & (dist < WINDOW))[None] & (key_abs >= 0)[:, None, :]
    s = s - slopes.reshape(SWA_KV_HEADS, G)[:, :, None, None] * dist.astype(f32)
    s = jnp.where(valid[None, :, None, None], s, NEG)
    sink = sinks.astype(f32).reshape(SWA_KV_HEADS, G)[None, None, :, :, None, None]
    m = jnp.maximum(s.max(-1, keepdims=True), sink)
    e = jnp.exp(s - m)
    p = e / (e.sum(-1, keepdims=True) + jnp.exp(sink - m))
    o = jnp.einsum('bngrqk,bnkgd->bnqgrd', p.astype(v2.dtype), v2)
    return o.reshape(B, L, SWA_W)


def grouped_experts(t, idx, gate, w1, w3, w2):
    T, D = t.shape
    A = T * TOP_K
    n_blocks = -(-A // MOE_BLOCK) + N_EXPERTS
    cap = n_blocks * MOE_BLOCK
    flat_e = idx.reshape(A)
    order = jnp.argsort(flat_e)
    e_sorted = flat_e[order]
    tok_sorted = (order // TOP_K).astype(jnp.int32)
    gate_sorted = gate.reshape(A)[order]
    counts = jnp.bincount(flat_e, length=N_EXPERTS)
    padded = (counts + MOE_BLOCK - 1) // MOE_BLOCK * MOE_BLOCK
    pad_end = jnp.cumsum(padded)
    pad_start = pad_end - padded
    start = jnp.cumsum(counts) - counts
    slot = pad_start[e_sorted] + jnp.arange(A) - start[e_sorted]
    slot_tok = jnp.full((cap,), T, jnp.int32).at[slot].set(tok_sorted)
    slot_gate = jnp.zeros((cap,), t.dtype).at[slot].set(gate_sorted)
    block_e = jnp.minimum(jnp.searchsorted(pad_end, jnp.arange(n_blocks) * MOE_BLOCK, side='right'), N_EXPERTS - 1)
    t_pad = jnp.concatenate([t, jnp.zeros((1, D), t.dtype)], axis=0)

    def one_block(args):
        tok_b, g_b, e = args
        xb = t_pad[tok_b]
        return swiglu(xb, w1[e], w3[e], w2[e]) * g_b[:, None]

    ys = lax.map(one_block, (slot_tok.reshape(n_blocks, MOE_BLOCK), slot_gate.reshape(n_blocks, MOE_BLOCK), block_e))
    return jnp.zeros((T + 1, D), t.dtype).at[slot_tok].add(ys.reshape(cap, D))[:T]


def moe_ffn(h, router_w, router_bias, w1, w3, w2, sw1, sw3, sw2):
    B, L, D = h.shape
    T = B * L
    t = h.reshape(T, D)
    scores = jax.nn.sigmoid((t @ router_w).astype(jnp.float32))
    sel = scores + router_bias
    grp_score = lax.top_k(sel.reshape(T, N_GROUPS, N_EXPERTS // N_GROUPS), 2)[0].sum(-1)
    _, top_g = lax.top_k(grp_score, TOPK_GROUPS)
    gmask = jax.nn.one_hot(top_g, N_GROUPS, dtype=jnp.float32).sum(1) > 0
    sel = jnp.where(jnp.repeat(gmask, N_EXPERTS // N_GROUPS, axis=1), sel, NEG)
    _, idx = lax.top_k(sel, TOP_K)
    gate = jnp.take_along_axis(scores, idx, axis=1)
    gate = (gate / gate.sum(-1, keepdims=True) * ROUTED_SCALE).astype(t.dtype)
    out = grouped_experts(t, idx, gate, w1, w3, w2) + swiglu(t, sw1, sw3, sw2)
    return out.reshape(B, L, D)


def setup_inputs(seed: int = 0) -> dict:
    key = jax.random.key(seed)
    ks = iter(jax.random.split(key, 64))
    nrm = lambda shape, s: jax.random.normal(next(ks), shape, jnp.float32) * s
    uni = lambda shape, lo, hi: jax.random.uniform(next(ks), shape, jnp.float32, lo, hi)
    Lr, D = DEPTH, D_MODEL
    return {
        'x': nrm((BATCH, SEQ, D), 1.0),
        'c': nrm((BATCH, D), 1.0),
        'w_mod': nrm((Lr, D, 6 * D), 0.5 * D ** -0.5),
        'b_mod': nrm((Lr, 6 * D), 0.01),
        'w_in': nrm((Lr, D, P_IN), D ** -0.5),
        'rwkv_mu': uni((Lr, N_RWKV_COLS), 0.0, 1.0),
        'rwkv_w0': uni((Lr, RWKV_W), -6.0, 0.0),
        'rwkv_w2': nrm((Lr, DECAY_LORA, RWKV_W), 0.5 * DECAY_LORA ** -0.5),
        'rwkv_a0': nrm((Lr, RWKV_W), 0.1),
        'rwkv_a2': nrm((Lr, AAA_LORA, RWKV_W), AAA_LORA ** -0.5),
        'rwkv_g2': nrm((Lr, GATE_LORA, RWKV_W), GATE_LORA ** -0.5),
        'rwkv_k_k': 0.85 + nrm((Lr, RWKV_W), 0.05),
        'rwkv_k_a': 1.0 + nrm((Lr, RWKV_W), 0.05),
        'rwkv_r_k': nrm((Lr, RWKV_HEADS, HEAD_DIM), 0.1),
        'rwkv_ln_g': 1.0 + nrm((Lr, RWKV_W), 0.02),
        'rwkv_ln_b': nrm((Lr, RWKV_W), 0.02),
        'dsa_kv_norm': 1.0 + nrm((Lr, KV_LORA), 0.02),
        'dsa_w_uk': nrm((Lr, DSA_HEADS, HEAD_DIM, KV_LORA), HEAD_DIM ** -0.5),
        'dsa_w_uv': nrm((Lr, DSA_HEADS, KV_LORA, HEAD_DIM), KV_LORA ** -0.5),
        'dsa_ik_g': 1.0 + nrm((Lr, IDX_DIM), 0.02),
        'dsa_ik_b': nrm((Lr, IDX_DIM), 0.02),
        'swa_sinks': nrm((Lr, SWA_HEADS), 1.0),
        'w_out': nrm((Lr, D, D), BETA * D ** -0.5),
        'ln_mix_g': 1.0 + nrm((Lr, D), 0.02),
        'ln_mix_b': nrm((Lr, D), 0.02),
        'router_w': nrm((Lr, D, N_EXPERTS), D ** -0.5),
        'router_bias': nrm((Lr, N_EXPERTS), 0.01),
        'exp_w1': nrm((Lr, N_EXPERTS, D, D_EXPERT), D ** -0.5),
        'exp_w3': nrm((Lr, N_EXPERTS, D, D_EXPERT), D ** -0.5),
        'exp_w2': nrm((Lr, N_EXPERTS, D_EXPERT, D), BETA * D_EXPERT ** -0.5),
        'sh_w1': nrm((Lr, D, D_EXPERT), D ** -0.5),
        'sh_w3': nrm((Lr, D, D_EXPERT), D ** -0.5),
        'sh_w2': nrm((Lr, D_EXPERT, D), BETA * D_EXPERT ** -0.5),
        'ln_ffn_g': 1.0 + nrm((Lr, D), 0.02),
        'ln_ffn_b': nrm((Lr, D), 0.02),
    }


def reference(x, c, w_mod, b_mod, w_in, rwkv_mu, rwkv_w0, rwkv_w2, rwkv_a0, rwkv_a2, rwkv_g2,
              rwkv_k_k, rwkv_k_a, rwkv_r_k, rwkv_ln_g, rwkv_ln_b, dsa_kv_norm, dsa_w_uk, dsa_w_uv,
              dsa_ik_g, dsa_ik_b, swa_sinks, w_out, ln_mix_g, ln_mix_b, router_w, router_bias,
              exp_w1, exp_w3, exp_w2, sh_w1, sh_w3, sh_w2, ln_ffn_g, ln_ffn_b):
    slopes = alibi_slopes(SWA_HEADS + DSA_HEADS)
    swa_slopes = slopes[:SWA_HEADS]
    dsa_slopes = slopes[SWA_HEADS:]
    cond = jax.nn.silu(c)
    for l in range(DEPTH):
        mod = cond @ w_mod[l] + b_mod[l]
        sh1, sc1, g1, sh2, sc2, g2 = [m[:, None, :] for m in jnp.split(mod, 6, axis=-1)]
        h = x * (1 + sc1) + sh1
        proj = h @ w_in[l]
        p_rwkv, p_dsa, p_swa = split_cols(proj, (N_RWKV_COLS, N_DSA_COLS, N_SWA_COLS))
        mixed = jnp.concatenate([
            rwkv7_mix(p_rwkv, rwkv_mu[l], rwkv_w0[l], rwkv_w2[l], rwkv_a0[l], rwkv_a2[l], rwkv_g2[l],
                      rwkv_k_k[l], rwkv_k_a[l], rwkv_r_k[l], rwkv_ln_g[l], rwkv_ln_b[l]),
            dsa_mix(p_dsa, dsa_kv_norm[l], dsa_w_uk[l], dsa_w_uv[l], dsa_ik_g[l], dsa_ik_b[l], dsa_slopes),
            swa_mix(p_swa, swa_sinks[l], swa_slopes),
        ], axis=-1)
        x = layer_norm(ALPHA * x + g1 * (mixed @ w_out[l]), ln_mix_g[l], ln_mix_b[l])
        h = x * (1 + sc2) + sh2
        y = moe_ffn(h, router_w[l], router_bias[l], exp_w1[l], exp_w3[l], exp_w2[l],
                    sh_w1[l], sh_w3[l], sh_w2[l])
        x = layer_norm(ALPHA * x + g2 * y, ln_ffn_g[l], ln_ffn_b[l])
    return x
```

```python
import functools
import math

import jax
import jax.numpy as jnp
import numpy as np
from jax import lax
from jax.experimental import pallas as pl
from jax.experimental.pallas import tpu as pltpu

F32 = jnp.float32
BF16 = jnp.bfloat16
I32 = jnp.int32
HI = lax.Precision.HIGHEST

D_MODEL = 1024
DEPTH = 4
HEAD_DIM = 64
RWKV_HEADS = 6
DSA_HEADS = 4
SWA_HEADS = 6
SWA_KV_HEADS = 2
RWKV_W = RWKV_HEADS * HEAD_DIM
DSA_W = DSA_HEADS * HEAD_DIM
SWA_W = SWA_HEADS * HEAD_DIM
DECAY_LORA = 64
AAA_LORA = 64
GATE_LORA = 128
GN_EPS = 64e-5
KV_LORA = 128
IDX_HEADS = 4
IDX_DIM = 64
TOPK_MAX = 256
WINDOW = 128
N_EXPERTS = 64
TOP_K = 8
N_GROUPS = 8
TOPK_GROUPS = 4
D_EXPERT = 256
ROUTED_SCALE = 2.5
ALPHA = (2 * DEPTH) ** 0.25
LN_EPS = 1e-5
NEG = -1e30
INT_MIN = -(2 ** 31)

LANES = 128
VMEM_LIMIT = 56 * 1024 * 1024

C_RKV = (0, 1152)
C_LORA = (1152, 1408)
C_DQ = (1408, 1664)
C_CKV = (1664, 1792)
C_IDX = (1792, 2176)
C_SQ = (2176, 2560)
C_SKV = (2560, 2816)
P_PAD = 2816
N_ORIG_BEFORE_PAD = 2116


def _dot(a, b, prec=None):
    return jnp.dot(a, b, preferred_element_type=F32, precision=prec)


def _dot_nt(a, b, prec=None):
    return lax.dot_general(a, b, (((1,), (1,)), ((), ())), preferred_element_type=F32, precision=prec)


def _iota(shape, dim):
    return lax.broadcasted_iota(I32, shape, dim)


def _sigmoid(x):
    return 1.0 / (1.0 + jnp.exp(-x))


def _layer_norm_rows(v, g, b):
    mu = jnp.mean(v, axis=-1, keepdims=True)
    d = v - mu
    var = jnp.mean(d * d, axis=-1, keepdims=True)
    return d * lax.rsqrt(var + LN_EPS) * g + b


def _params(sem):
    return pltpu.CompilerParams(dimension_semantics=sem, vmem_limit_bytes=VMEM_LIMIT)


def _mod_kernel(c_ref, w_ref, b_ref, o_ref):
    c = c_ref[...]
    cond = c * _sigmoid(c)
    o_ref[0] = _dot(cond, w_ref[0], HI) + b_ref[0]


def _modulation(c, w_mod, b_mod):
    depth, d, d6 = w_mod.shape
    c8 = jnp.broadcast_to(c, (8, d))
    nj = d6 // d
    out = pl.pallas_call(
        _mod_kernel,
        grid=(depth, nj),
        in_specs=[
            pl.BlockSpec((8, d), lambda l, j: (0, 0)),
            pl.BlockSpec((1, d, d), lambda l, j: (l, 0, j)),
            pl.BlockSpec((1, 1, d), lambda l, j: (l, 0, j)),
        ],
        out_specs=pl.BlockSpec((1, 8, d), lambda l, j: (l, 0, j)),
        out_shape=jax.ShapeDtypeStruct((depth, 8, d6), F32),
        compiler_params=_params(("arbitrary", "arbitrary")),
        name="modulation",
    )(c8, w_mod, b_mod.reshape(depth, 1, d6))
    return out[:, 0:1, :]


def _proj_kernel(x_ref, sc_ref, sh_ref, w_ref, wlo_ref, kvn_ref, ikg_ref, ikb_ref,
                 rkv_ref, lora_ref, dq_ref, ckv_ref, iq_ref, ikw_ref, sq_ref, skv_ref):
    h = x_ref[...] * (1.0 + sc_ref[...]) + sh_ref[...]
    hb = h.astype(BF16)
    hl = (h - hb.astype(F32)).astype(BF16)

    def mm(c):
        return _dot(hb, w_ref[:, c[0]:c[1]])

    rkv_ref[...] = mm(C_RKV)
    lora_ref[...] = mm(C_LORA)
    dq_ref[...] = mm(C_DQ)
    sq_ref[...] = mm(C_SQ)
    skv_ref[...] = mm(C_SKV)
    ckv = mm(C_CKV)
    ckv_ref[...] = ckv * lax.rsqrt(jnp.mean(ckv * ckv, axis=-1, keepdims=True) + 1e-6) * kvn_ref[...]
    idx = mm(C_IDX) + _dot(hl, w_ref[:, C_IDX[0]:C_IDX[1]]) + _dot(hb, wlo_ref[...])
    iq_ref[...] = idx[:, 0:256]
    g3 = idx[:, 256:384]
    lane = _iota(g3.shape, 1)
    isk = lane < IDX_DIM
    mu = jnp.sum(jnp.where(isk, g3, 0.0), axis=-1, keepdims=True) * (1.0 / IDX_DIM)
    dk = jnp.where(isk, g3 - mu, 0.0)
    var = jnp.sum(dk * dk, axis=-1, keepdims=True) * (1.0 / IDX_DIM)
    ikn = dk * lax.rsqrt(var + LN_EPS) * ikg_ref[...] + ikb_ref[...]
    ikw_ref[...] = jnp.where(isk, ikn, g3 * (IDX_HEADS ** -0.5 * IDX_DIM ** -0.5))


def _input_proj(x, sc, sh, w_hi, w_idx_lo, kvn, ikg, ikb, tm=512):
    t, d = x.shape
    widths = [C_RKV, C_LORA, C_DQ, C_CKV, (0, 256), (0, 128), C_SQ, C_SKV]
    widths = [c[1] - c[0] for c in widths]
    const = lambda i: (0, 0)
    row = lambda i: (i, 0)
    return pl.pallas_call(
        _proj_kernel,
        grid=(t // tm,),
        in_specs=[
            pl.BlockSpec((tm, d), row),
            pl.BlockSpec((1, d), const),
            pl.BlockSpec((1, d), const),
            pl.BlockSpec((d, P_PAD), const),
            pl.BlockSpec((d, C_IDX[1] - C_IDX[0]), const),
            pl.BlockSpec((1, KV_LORA), const),
            pl.BlockSpec((1, LANES), const),
            pl.BlockSpec((1, LANES), const),
        ],
        out_specs=[pl.BlockSpec((tm, w), row) for w in widths],
        out_shape=[jax.ShapeDtypeStruct((t, w), F32) for w in widths],
        compiler_params=_params(("arbitrary",)),
        name="input_proj",
    )(x, sc, sh, w_hi, w_idx_lo, kvn, ikg, ikb)


RW_CHUNK = 64


def _rwkv_kernel(r_ref, k_ref, v_ref, lora_ref, rp_ref, kp_ref, vp_ref, lp_ref,
                 mur_ref, muk_ref, muv_ref, mul_ref, w0_ref, w2_ref, a0_ref, a2_ref, g2_ref,
                 kk_ref, ka_ref, rk_ref, lng_ref, lnb_ref, o_ref,
                 h_ref, y_ref, st_ref, *, tg):
    g = pl.program_id(1)
    c64 = RW_CHUNK
    lane = _iota((1, LANES), 1)
    first = g == 0

    @pl.when(first)
    def _():
        h_ref[...] = jnp.zeros_like(h_ref)

    rowid = _iota((tg, 1), 0)

    def shift_mix(cur_ref, prev_ref, mu_ref):
        cur = cur_ref[...]
        prev_row = jnp.where(first, 0.0, prev_ref[7:8, :])
        rolled = pltpu.roll(cur, 1, 0)
        shifted = jnp.where(rowid == 0, prev_row, rolled)
        return cur + (shifted - cur) * mu_ref[...]

    r = shift_mix(r_ref, rp_ref, mur_ref)
    k = shift_mix(k_ref, kp_ref, muk_ref)
    v = shift_mix(v_ref, vp_ref, muv_ref)
    lo = shift_mix(lora_ref, lp_ref, mul_ref)
    wl = lo[:, 0:DECAY_LORA]
    al = lo[:, DECAY_LORA:DECAY_LORA + AAA_LORA]
    gl = lo[:, 128:256]

    zw = -(w0_ref[...] + _dot(jnp.tanh(wl), w2_ref[...], HI))
    softplus = jnp.maximum(zw, 0.0) + jnp.log(1.0 + jnp.exp(-jnp.abs(zw)))
    lw = -jnp.exp(-softplus - 0.5)
    a = _sigmoid(a0_ref[...] + _dot(al, a2_ref[...], HI))
    gate = _dot(_sigmoid(gl), g2_ref[...], HI)

    ri = _iota((LANES, LANES), 0) // HEAD_DIM
    ci = _iota((LANES, LANES), 1) // HEAD_DIM
    bones = jnp.where(ri == ci, 1.0, 0.0)
    kk = k * kk_ref[...]
    n2 = _dot(kk * kk, bones, HI)
    kk = kk / jnp.maximum(jnp.sqrt(n2), 1e-12)
    k2 = k * (1.0 + (a - 1.0) * ka_ref[...])
    bonus = _dot(r * k2 * rk_ref[...], bones, HI) * v
    bvec = a * kk

    st_ref[0] = r
    st_ref[1] = k2
    st_ref[2] = v
    st_ref[3] = lw
    st_ref[4] = kk
    st_ref[5] = bvec

    rr = _iota((LANES, LANES), 0)
    cc = _iota((LANES, LANES), 1)
    same = (rr // c64) == (cc // c64)
    strict = same & ((rr % c64) > (cc % c64))
    incl = same & ((rr % c64) >= (cc % c64))
    eye = jnp.where(rr == cc, 1.0, 0.0)
    tril = jnp.where(_iota((c64, c64), 0) >= _iota((c64, c64), 1), 1.0, 0.0)
    lo_half = lane < HEAD_DIM

    def stack(xc):
        return jnp.concatenate([jnp.where(lo_half, xc, 0.0), jnp.where(lo_half, 0.0, xc)], axis=0)

    def chunk(c, carry):
        sl = pl.ds(pl.multiple_of(c * c64, c64), c64)
        rc = st_ref[0, sl, :]
        kc = st_ref[1, sl, :]
        vc = st_ref[2, sl, :]
        lwc = st_ref[3, sl, :]
        kkc = st_ref[4, sl, :]
        bc = st_ref[5, sl, :]
        cum = _dot(tril, lwc, HI)
        tot = cum[c64 - 1:c64, :]
        g_in = jnp.exp(cum)
        g_ex = jnp.exp(cum - lwc)
        g_inv = jnp.exp(-cum)
        g_rest = jnp.exp(tot - cum)
        a_s = stack(-kkc * g_ex)
        b_s = stack(bc * g_inv)
        k_s = stack(kc * g_inv)
        r_s = stack(rc * g_in)
        v_s = stack(vc)
        bg_s = stack(bc * g_rest)
        kg_s = stack(kc * g_rest)
        nmat = jnp.where(strict, _dot_nt(a_s, b_s, HI), 0.0)
        aak = jnp.where(strict, _dot_nt(a_s, k_s, HI), 0.0)
        arb = jnp.where(incl, _dot_nt(r_s, b_s, HI), 0.0)
        ark = jnp.where(incl, _dot_nt(r_s, k_s, HI), 0.0)
        tinv = eye + nmat
        pw = nmat
        for _ in range(5):
            pw = _dot(pw, pw, HI)
            tinv = _dot(tinv, eye + pw, HI)
        wmat = _dot(tinv, a_s, HI)
        zmat = _dot(tinv, _dot(aak, v_s, HI), HI)
        y0 = _dot(ark, v_s, HI)
        hst = h_ref[...]
        u = _dot(wmat, hst, HI) + zmat
        ys = _dot(r_s, hst, HI) + _dot(arb, u, HI) + y0
        gdiag = eye * jnp.exp(tot)
        lhs_t = jnp.concatenate([bg_s, kg_s, gdiag], axis=0).T
        rhs = jnp.concatenate([u, v_s, hst], axis=0)
        h_ref[...] = _dot(lhs_t, rhs, HI)
        y_ref[sl, :] = ys[0:c64, :] + ys[c64:2 * c64, :]
        return carry

    lax.fori_loop(0, tg // c64, chunk, 0)

    y = y_ref[...]
    bavg = bones * (1.0 / HEAD_DIM)
    mean = _dot(y, bavg, HI)
    dy = y - mean
    var = _dot(dy * dy, bavg, HI)
    o = dy * lax.rsqrt(var + GN_EPS) * lng_ref[...] + lnb_ref[...]
    o_ref[...] = (o + bonus) * gate


def _rwkv_mix(rkv, lora, mu, w0, w2, a0, a2, g2, k_k, k_a, r_k, ln_g, ln_b, tg=512):
    t = rkv.shape[0]
    npair = RWKV_W // LANES
    mu_r, mu_k, mu_v, mu_l = mu[:, 0:384], mu[:, 384:768], mu[:, 768:1152], mu[:, 1152:1408]
    blk = lambda off: pl.BlockSpec((tg, LANES), lambda p, g: (g, off + p))
    prev = lambda off: pl.BlockSpec((8, LANES), lambda p, g: (jnp.maximum(g * (tg // 8) - 1, 0), off + p))
    vec = pl.BlockSpec((1, LANES), lambda p, g: (0, p))
    return pl.pallas_call(
        functools.partial(_rwkv_kernel, tg=tg),
        grid=(npair, t // tg),
        in_specs=[
            blk(0), blk(3), blk(6),
            pl.BlockSpec((tg, 256), lambda p, g: (g, 0)),
            prev(0), prev(3), prev(6),
            pl.BlockSpec((8, 256), lambda p, g: (jnp.maximum(g * (tg // 8) - 1, 0), 0)),
            vec, vec, vec,
            pl.BlockSpec((1, 256), lambda p, g: (0, 0)),
            vec,
            pl.BlockSpec((DECAY_LORA, LANES), lambda p, g: (0, p)),
            vec,
            pl.BlockSpec((AAA_LORA, LANES), lambda p, g: (0, p)),
            pl.BlockSpec((GATE_LORA, LANES), lambda p, g: (0, p)),
            vec, vec, vec, vec, vec,
        ],
        out_specs=pl.BlockSpec((tg, LANES), lambda p, g: (g, p)),
        out_shape=jax.ShapeDtypeStruct((t, RWKV_W), F32),
        scratch_shapes=[
            pltpu.VMEM((LANES, LANES), F32),
            pltpu.VMEM((tg, LANES), F32),
            pltpu.VMEM((6, tg, LANES), F32),
        ],
        compiler_params=_params(("arbitrary", "arbitrary")),
        name="rwkv7_mix",
    )(rkv, rkv, rkv, lora, rkv, rkv, rkv, lora,
      mu_r, mu_k, mu_v, mu_l, w0, w2, a0, a2, g2, k_k, k_a, r_k, ln_g, ln_b)


DSA_QB = 128
DSA_KC = 512


def _dsa_kernel(dq_ref, iq_ref, ikw_ref, ikt_ref, ckv_ref, ckvt_ref, wuk_ref, wuv_ref, triu_ref,
                o_ref, sc_ref, *, slopes):
    i = pl.program_id(0)
    qb, kc = DSA_QB, DSA_KC
    t0 = i * qb
    nch = (t0 + qb + kc - 1) // kc
    row_t = t0 + _iota((qb, 1), 0)

    iq = iq_ref[...]
    iq_hi = iq.astype(BF16).astype(F32)
    iq_lo = iq - iq_hi
    ikw = ikw_ref[...]
    lhs = []
    iw = []
    for h in range(IDX_HEADS):
        s = slice(h * IDX_DIM, (h + 1) * IDX_DIM)
        lhs.append(jnp.concatenate([iq_hi[:, s], iq_hi[:, s], iq_lo[:, s], iq_lo[:, s]], axis=1).astype(BF16))
        iw.append(ikw[:, IDX_DIM + h:IDX_DIM + h + 1])

    def score_body(ch, carry):
        k0 = pl.multiple_of(ch * kc, kc)
        kt = ikt_ref[:, pl.ds(k0, kc)]
        acc = jnp.zeros((qb, kc), F32)
        for h in range(IDX_HEADS):
            acc = acc + jnp.maximum(_dot(lhs[h], kt), 0.0) * iw[h]
        acc = jnp.where(acc == 0.0, 0.0, acc)
        bits = lax.bitcast_convert_type(acc, I32)
        key = bits ^ ((bits >> 31) & 0x7FFFFFFF)
        col = k0 + _iota((qb, kc), 1)
        sc_ref[:, pl.ds(k0, kc)] = jnp.where(col <= row_t, key, INT_MIN)
        return carry

    lax.fori_loop(0, nch, score_body, 0)

    def count_ge(cand):
        def body(ch, acc):
            k0 = pl.multiple_of(ch * kc, kc)
            blk = sc_ref[:, pl.ds(k0, kc)]
            m = jnp.where(blk >= cand, 1.0, 0.0)
            for j in range(kc // LANES):
                acc = acc + m[:, j * LANES:(j + 1) * LANES]
            return acc
        acc = lax.fori_loop(0, nch, body, jnp.zeros((qb, LANES), F32))
        return jnp.sum(acc, axis=1, keepdims=True)

    k_row = jnp.minimum(row_t + 1, TOPK_MAX).astype(F32)
    zero = jnp.zeros((qb, 1), I32)
    base = jnp.where(count_ge(zero) >= k_row, zero, zero + INT_MIN)

    def bit_body(j, base):
        cand = base + jnp.left_shift(jnp.int32(1), 30 - j)
        return jnp.where(count_ge(cand) >= k_row, cand, base)

    tau = lax.fori_loop(0, 31, bit_body, base)
    n_gt = count_ge(tau + 1)
    need = k_row - n_gt

    dq = dq_ref[...]
    qlat = []
    for h in range(DSA_HEADS):
        ql = _dot(dq[:, h * HEAD_DIM:(h + 1) * HEAD_DIM], wuk_ref[h], HI) * HEAD_DIM ** -0.5
        qlat.append(ql.astype(BF16))
    triu = triu_ref[...]

    def attn_body(ch, carry):
        tie_run, ms, ls, accs = carry
        k0 = pl.multiple_of(ch * kc, kc)
        key = sc_ref[:, pl.ds(k0, kc)]
        eq = key == tau
        pref = _dot(jnp.where(eq, 1.0, 0.0).astype(BF16), triu)
        sel = (key > tau) | (eq & (tie_run + pref <= need))
        tie_run = tie_run + pref[:, kc - 1:kc]
        col = k0 + _iota((qb, kc), 1)
        dist = (row_t - col).astype(F32)
        ckt = ckvt_ref[:, pl.ds(k0, kc)]
        ckc = ckv_ref[pl.ds(k0, kc), :]
        new_m, new_l, new_acc = [], [], []
        for h in range(DSA_HEADS):
            lg = jnp.where(sel, _dot(qlat[h], ckt) - slopes[h] * dist, NEG)
            m_new = jnp.maximum(ms[h], jnp.max(lg, axis=1, keepdims=True))
            scale = jnp.exp(ms[h] - m_new)
            p = jnp.exp(lg - m_new)
            new_l.append(scale * ls[h] + jnp.sum(p, axis=1, keepdims=True))
            new_acc.append(scale * accs[h] + _dot(p.astype(BF16), ckc))
            new_m.append(m_new)
        return tie_run, tuple(new_m), tuple(new_l), tuple(new_acc)

    init = (jnp.zeros((qb, 1), F32),
            tuple(jnp.full((qb, 1), NEG, F32) for _ in range(DSA_HEADS)),
            tuple(jnp.zeros((qb, 1), F32) for _ in range(DSA_HEADS)),
            tuple(jnp.zeros((qb, KV_LORA), F32) for _ in range(DSA_HEADS)))
    _, _, ls, accs = lax.fori_loop(0, nch, attn_body, init)
    outs = [_dot(accs[h] / ls[h], wuv_ref[h], HI) for h in range(DSA_HEADS)]
    o_ref[...] = jnp.concatenate(outs, axis=1)


def _dsa_mix(dq, iq, ikw, ckv, w_uk, w_uv, slopes):
    t = dq.shape[0]
    ikn = ikw[:, 0:IDX_DIM]
    ik_hi = ikn.astype(BF16)
    ik_lo = (ikn - ik_hi.astype(F32)).astype(BF16)
    ikt = jnp.concatenate([ik_hi, ik_lo, ik_hi, ik_lo], axis=1).T
    ckv_b = ckv.astype(BF16)
    ckvt = ckv_b.T
    kc = DSA_KC
    triu = (np.arange(kc)[:, None] <= np.arange(kc)[None, :]).astype(np.float32)
    triu = jnp.asarray(triu, BF16)
    row = lambda i: (i, 0)
    const2 = lambda i: (0, 0)
    const3 = lambda i: (0, 0, 0)
    return pl.pallas_call(
        functools.partial(_dsa_kernel, slopes=slopes),
        grid=(t // DSA_QB,),
        in_specs=[
            pl.BlockSpec((DSA_QB, DSA_W), row),
            pl.BlockSpec((DSA_QB, IDX_HEADS * IDX_DIM), row),
            pl.BlockSpec((DSA_QB, LANES), row),
            pl.BlockSpec((4 * IDX_DIM, t), const2, pipeline_mode=pl.Buffered(1)),
            pl.BlockSpec((t, KV_LORA), const2, pipeline_mode=pl.Buffered(1)),
            pl.BlockSpec((KV_LORA, t), const2, pipeline_mode=pl.Buffered(1)),
            pl.BlockSpec((DSA_HEADS, HEAD_DIM, KV_LORA), const3),
            pl.BlockSpec((DSA_HEADS, KV_LORA, HEAD_DIM), const3),
            pl.BlockSpec((kc, kc), const2, pipeline_mode=pl.Buffered(1)),
        ],
        out_specs=pl.BlockSpec((DSA_QB, DSA_W), row),
        out_shape=jax.ShapeDtypeStruct((t, DSA_W), F32),
        scratch_shapes=[pltpu.VMEM((DSA_QB, t), I32)],
        compiler_params=_params(("arbitrary",)),
        name="dsa_mix",
    )(dq, iq, ikw, ikt, ckv_b, ckvt, w_uk, w_uv, triu)


def _swa_kernel(q_ref, kv_ref, kvp_ref, sink_ref, o_ref, *, slopes):
    i = pl.program_id(0)
    w = WINDOW
    gsz = SWA_HEADS // SWA_KV_HEADS
    q = q_ref[...]
    kv = kv_ref[...]
    kvp = kvp_ref[...]
    qi = _iota((w, 2 * w), 0)
    kj = _iota((w, 2 * w), 1)
    dist = qi + w - kj
    valid = (dist >= 0) & (dist < w) & ((kj >= w) | (i > 0))
    distf = dist.astype(F32)
    sinks = sink_ref[...]
    outs = []
    for g in range(SWA_KV_HEADS):
        k2 = jnp.concatenate([kvp[:, g * HEAD_DIM:(g + 1) * HEAD_DIM],
                              kv[:, g * HEAD_DIM:(g + 1) * HEAD_DIM]], axis=0).astype(BF16)
        v2 = jnp.concatenate([kvp[:, w + g * HEAD_DIM:w + (g + 1) * HEAD_DIM],
                              kv[:, w + g * HEAD_DIM:w + (g + 1) * HEAD_DIM]], axis=0).astype(BF16)
        for r in range(gsz):
            hd = g * gsz + r
            qh = q[:, hd * HEAD_DIM:(hd + 1) * HEAD_DIM].astype(BF16)
            s = _dot_nt(qh, k2) * HEAD_DIM ** -0.5
            s = jnp.where(valid, s - slopes[hd] * distf, NEG)
            sink = sinks[0:1, hd:hd + 1]
            m = jnp.maximum(jnp.max(s, axis=1, keepdims=True), sink)
            e = jnp.exp(s - m)
            p = e / (jnp.sum(e, axis=1, keepdims=True) + jnp.exp(sink - m))
            outs.append(_dot(p.astype(BF16), v2))
    o_ref[...] = jnp.concatenate(outs, axis=1)


def _swa_mix(sq, skv, sinks, slopes):
    t = sq.shape[0]
    w = WINDOW
    return pl.pallas_call(
        functools.partial(_swa_kernel, slopes=slopes),
        grid=(t // w,),
        in_specs=[
            pl.BlockSpec((w, SWA_W), lambda i: (i, 0)),
            pl.BlockSpec((w, 2 * w), lambda i: (i, 0)),
            pl.BlockSpec((w, 2 * w), lambda i: (jnp.maximum(i - 1, 0), 0)),
            pl.BlockSpec((1, LANES), lambda i: (0, 0)),
        ],
        out_specs=pl.BlockSpec((w, SWA_W), lambda i: (i, 0)),
        out_shape=jax.ShapeDtypeStruct((t, SWA_W), F32),
        compiler_params=_params(("arbitrary",)),
        name="swa_mix",
    )(sq, skv, skv, sinks)


def _post_mix_kernel(x_ref, orw_ref, ods_ref, osw_ref, wout_ref, g1_ref, lng_ref, lnb_ref,
                     sc2_ref, sh2_ref, rwt_ref, rb_ref, tri_ref,
                     x1_ref, h2_ref, eidx_ref, rank_ref, gate_ref, cnt_ref, carry_ref):
    i = pl.program_id(0)

    @pl.when(i == 0)
    def _():
        carry_ref[...] = jnp.zeros_like(carry_ref)

    y = (_dot(orw_ref[...].astype(BF16), wout_ref[0:RWKV_W, :])
         + _dot(ods_ref[...].astype(BF16), wout_ref[RWKV_W:RWKV_W + DSA_W, :])
         + _dot(osw_ref[...].astype(BF16), wout_ref[RWKV_W + DSA_W:D_MODEL, :]))
    x1 = _layer_norm_rows(ALPHA * x_ref[...] + g1_ref[...] * y, lng_ref[...], lnb_ref[...])
    x1_ref[...] = x1
    h2 = x1 * (1.0 + sc2_ref[...]) + sh2_ref[...]
    h2_ref[...] = h2

    tm = h2.shape[0]
    ne = N_EXPERTS
    gs = ne // N_GROUPS
    scores = _sigmoid(_dot_nt(rwt_ref[...], h2, HI))
    sel = scores + rb_ref[...]
    sub = _iota((gs, tm), 0).astype(F32)
    gsc = []
    for j in range(N_GROUPS):
        gj = sel[j * gs:(j + 1) * gs, :]
        m1 = jnp.max(gj, axis=0, keepdims=True)
        f1 = jnp.min(jnp.where(gj == m1, sub, float(gs)), axis=0, keepdims=True)
        m2 = jnp.max(jnp.where(sub == f1, -jnp.inf, gj), axis=0, keepdims=True)
        gsc.append(m1 + m2)
    gsc = jnp.concatenate(gsc, axis=0)
    gid = _iota((N_GROUPS, tm), 0).astype(F32)
    gmask = jnp.zeros((N_GROUPS, tm), F32)
    for _ in range(TOPK_GROUPS):
        mx = jnp.max(gsc, axis=0, keepdims=True)
        fi = jnp.min(jnp.where(gsc == mx, gid, float(N_GROUPS)), axis=0, keepdims=True)
        pick = gid == fi
        gmask = jnp.where(pick, 1.0, gmask)
        gsc = jnp.where(pick, -jnp.inf, gsc)
    selm = jnp.concatenate(
        [jnp.where(gmask[j:j + 1, :] > 0.5, sel[j * gs:(j + 1) * gs, :], NEG) for j in range(N_GROUPS)], axis=0)
    eid = _iota((ne, tm), 0).astype(F32)
    gsel, eids = [], []
    chosen_f = jnp.zeros((ne, tm), F32)
    for _ in range(TOP_K):
        mx = jnp.max(selm, axis=0, keepdims=True)
        fi = jnp.min(jnp.where(selm == mx, eid, float(ne)), axis=0, keepdims=True)
        pick = eid == fi
        eids.append(fi)
        gsel.append(jnp.sum(jnp.where(pick, scores, 0.0), axis=0, keepdims=True))
        chosen_f = jnp.where(pick, 1.0, chosen_f)
        selm = jnp.where(pick, -jnp.inf, selm)
    gsum = gsel[0]
    for kx in range(1, TOP_K):
        gsum = gsum + gsel[kx]
    before = _dot(chosen_f.astype(BF16), tri_ref[...]) + carry_ref[:, 0:1]
    ranks = [jnp.sum(jnp.where(eid == eids[kx], before, 0.0), axis=0, keepdims=True) for kx in range(TOP_K)]
    eidx_ref[...] = jnp.concatenate(eids, axis=0).astype(I32)
    rank_ref[...] = jnp.concatenate(ranks, axis=0).astype(I32)
    gate_ref[...] = jnp.concatenate(gsel, axis=0) / gsum * ROUTED_SCALE
    carry_ref[...] = carry_ref[...] + jnp.sum(chosen_f, axis=1, keepdims=True)
    cnt_ref[...] = carry_ref[...]


def _post_mix(x, o_rw, o_ds, o_sw, w_out, g1, ln_g, ln_b, sc2, sh2, router_wt, router_b, tm=256):
    t, d = x.shape
    tri = (np.arange(tm)[:, None] < np.arange(tm)[None, :]).astype(np.float32)
    tri = jnp.asarray(tri, BF16)
    row = lambda i: (i, 0)
    const = lambda i: (0, 0)
    col = lambda i: (0, i)
    vec = pl.BlockSpec((1, d), const)
    return pl.pallas_call(
        _post_mix_kernel,
        grid=(t // tm,),
        in_specs=[
            pl.BlockSpec((tm, d), row),
            pl.BlockSpec((tm, RWKV_W), row),
            pl.BlockSpec((tm, DSA_W), row),
            pl.BlockSpec((tm, SWA_W), row),
            pl.BlockSpec((d, d), const),
            vec, vec, vec, vec, vec,
            pl.BlockSpec((N_EXPERTS, d), const),
            pl.BlockSpec((N_EXPERTS, 1), const),
            pl.BlockSpec((tm, tm), const),
        ],
        out_specs=[
            pl.BlockSpec((tm, d), row),
            pl.BlockSpec((tm, d), row),
            pl.BlockSpec((TOP_K, tm), col),
            pl.BlockSpec((TOP_K, tm), col),
            pl.BlockSpec((TOP_K, tm), col),
            pl.BlockSpec((N_EXPERTS, LANES), const),
        ],
        out_shape=[
            jax.ShapeDtypeStruct((t, d), F32),
            jax.ShapeDtypeStruct((t, d), F32),
            jax.ShapeDtypeStruct((TOP_K, t), I32),
            jax.ShapeDtypeStruct((TOP_K, t), I32),
            jax.ShapeDtypeStruct((TOP_K, t), F32),
            jax.ShapeDtypeStruct((N_EXPERTS, LANES), F32),
        ],
        scratch_shapes=[pltpu.VMEM((N_EXPERTS, LANES), F32)],
        compiler_params=_params(("arbitrary",)),
        name="post_mix_router",
    )(x, o_rw, o_ds, o_sw, w_out, g1, ln_g, ln_b, sc2, sh2, router_wt, router_b, tri)


MOE_ROWS = 256
MOE_TILE = 256


def _row_copy(src_ref, src_row, dst_ref, dst_row, sem):
    return pltpu.make_async_copy(src_ref.at[pl.ds(src_row, 1), :], dst_ref.at[pl.ds(dst_row, 1), :], sem)


def _dispatch_kernel(slot_hbm, h_hbm, xs_in, xs_out, slot_smem, sem_tab, sem_rows):
    del xs_in
    i = pl.program_id(0)
    tab = pltpu.make_async_copy(slot_hbm.at[i], slot_smem, sem_tab)
    tab.start()
    tab.wait()
    t0 = i * MOE_TILE

    def issue(tt, carry):
        for kx in range(TOP_K):
            _row_copy(h_hbm, t0 + tt, xs_out, slot_smem[kx, tt], sem_rows).start()
        return carry

    lax.fori_loop(0, MOE_TILE, issue, 0)

    def drain(tt, carry):
        for kx in range(TOP_K):
            _row_copy(h_hbm, 0, xs_out, 0, sem_rows).wait()
        return carry

    lax.fori_loop(0, MOE_TILE, drain, 0)


def _dispatch(slot_tiles, h2, cap):
    t, d = h2.shape
    xs0 = jnp.zeros((cap, d), F32)
    return pl.pallas_call(
        _dispatch_kernel,
        grid=(t // MOE_TILE,),
        in_specs=[
            pl.BlockSpec(memory_space=pl.ANY),
            pl.BlockSpec(memory_space=pl.ANY),
            pl.BlockSpec(memory_space=pl.ANY),
        ],
        out_specs=pl.BlockSpec(memory_space=pl.ANY),
        out_shape=jax.ShapeDtypeStruct((cap, d), F32),
        scratch_shapes=[
            pltpu.SMEM((TOP_K, MOE_TILE), I32),
            pltpu.SemaphoreType.DMA,
            pltpu.SemaphoreType.DMA,
        ],
        input_output_aliases={2: 0},
        compiler_params=_params(("arbitrary",)),
        name="moe_dispatch",
    )(slot_tiles, h2, xs0)


def _expert_kernel(be_ref, nb_ref, xs_ref, w1_ref, w3_ref, w2_ref, ys_ref):
    b = pl.program_id(0)

    @pl.when(b < nb_ref[0])
    def _():
        xb = xs_ref[...].astype(BF16)
        a = _dot(xb, w1_ref[0])
        gte = _dot(xb, w3_ref[0])
        hmid = (a * _sigmoid(a) * gte).astype(BF16)
        ys_ref[...] = _dot(hmid, w2_ref[0])

    @pl.when(b >= nb_ref[0])
    def _():
        ys_ref[...] = jnp.zeros_like(ys_ref)


def _experts(block_e, n_used, xs, w1, w3, w2):
    cap, d = xs.shape
    nb = cap // MOE_ROWS
    grid_spec = pltpu.PrefetchScalarGridSpec(
        num_scalar_prefetch=2,
        grid=(nb,),
        in_specs=[
            pl.BlockSpec((MOE_ROWS, d), lambda b, be, nu: (b, 0)),
            pl.BlockSpec((1, d, D_EXPERT), lambda b, be, nu: (be[b], 0, 0)),
            pl.BlockSpec((1, d, D_EXPERT), lambda b, be, nu: (be[b], 0, 0)),
            pl.BlockSpec((1, D_EXPERT, d), lambda b, be, nu: (be[b], 0, 0)),
        ],
        out_specs=pl.BlockSpec((MOE_ROWS, d), lambda b, be, nu: (b, 0)),
    )
    return pl.pallas_call(
        _expert_kernel,
        grid_spec=grid_spec,
        out_shape=jax.ShapeDtypeStruct((cap, d), F32),
        compiler_params=_params(("arbitrary",)),
        name="moe_experts",
    )(block_e, n_used, xs, w1, w3, w2)


def _combine_kernel(slot_hbm, ys_hbm, x1_ref, h2_ref, gate_ref, sw1_ref, sw3_ref, sw2_ref,
                    g2_ref, lng_ref, lnb_ref, o_ref, slot_smem, gbuf, sem_tab, sem_rows):
    i = pl.program_id(0)
    tab = pltpu.make_async_copy(slot_hbm.at[i], slot_smem, sem_tab)
    tab.start()
    tab.wait()

    def issue(tt, carry):
        for kx in range(TOP_K):
            _row_copy(ys_hbm, slot_smem[kx, tt], gbuf.at[kx], tt, sem_rows).start()
        return carry

    lax.fori_loop(0, MOE_TILE, issue, 0)

    hb = h2_ref[...].astype(BF16)
    a = _dot(hb, sw1_ref[...])
    gte = _dot(hb, sw3_ref[...])
    y = _dot((a * _sigmoid(a) * gte).astype(BF16), sw2_ref[...])

    def drain(tt, carry):
        for kx in range(TOP_K):
            _row_copy(ys_hbm, 0, gbuf.at[kx], 0, sem_rows).wait()
        return carry

    lax.fori_loop(0, MOE_TILE, drain, 0)

    gates = gate_ref[...]
    for kx in range(TOP_K):
        y = y + gates[:, kx:kx + 1] * gbuf[kx]
    o_ref[...] = _layer_norm_rows(ALPHA * x1_ref[...] + g2_ref[...] * y, lng_ref[...], lnb_ref[...])


def _combine(slot_tiles, ys, x1, h2, gates_t, sw1, sw3, sw2, g2, ln_g, ln_b):
    t, d = x1.shape
    row = lambda i: (i, 0)
    const = lambda i: (0, 0)
    vec = pl.BlockSpec((1, d), const)
    return pl.pallas_call(
        _combine_kernel,
        grid=(t // MOE_TILE,),
        in_specs=[
            pl.BlockSpec(memory_space=pl.ANY),
            pl.BlockSpec(memory_space=pl.ANY),
            pl.BlockSpec((MOE_TILE, d), row),
            pl.BlockSpec((MOE_TILE, d), row),
            pl.BlockSpec((MOE_TILE, TOP_K), row),
            pl.BlockSpec((d, D_EXPERT), const),
            pl.BlockSpec((d, D_EXPERT), const),
            pl.BlockSpec((D_EXPERT, d), const),
            vec, vec, vec,
        ],
        out_specs=pl.BlockSpec((MOE_TILE, d), row),
        out_shape=jax.ShapeDtypeStruct((t, d), F32),
        scratch_shapes=[
            pltpu.SMEM((TOP_K, MOE_TILE), I32),
            pltpu.VMEM((TOP_K, MOE_TILE, d), F32),
            pltpu.SemaphoreType.DMA,
            pltpu.SemaphoreType.DMA,
        ],
        compiler_params=_params(("arbitrary",)),
        name="moe_combine",
    )(slot_tiles, ys, x1, h2, gates_t, sw1, sw3, sw2, g2, ln_g, ln_b)


def _pad_w_in(w_in_l):
    d = w_in_l.shape[0]
    pad = jnp.zeros((d, C_SQ[0] - N_ORIG_BEFORE_PAD), w_in_l.dtype)
    return jnp.concatenate([w_in_l[:, :N_ORIG_BEFORE_PAD], pad, w_in_l[:, N_ORIG_BEFORE_PAD:]], axis=1)


def _pad_lanes(v, width=LANES):
    v = v.reshape(1, -1)
    return jnp.pad(v, ((0, 0), (0, width - v.shape[1])))


def _moe_tables(eidx, rank, counts):
    t = eidx.shape[1]
    cnt = counts[:, 0].astype(I32)
    padded = (cnt + MOE_ROWS - 1) // MOE_ROWS * MOE_ROWS
    pad_end = jnp.cumsum(padded)
    pad_start = pad_end - padded
    slot = pad_start[eidx] + rank
    slot_tiles = slot.reshape(TOP_K, t // MOE_TILE, MOE_TILE).transpose(1, 0, 2)
    cap = t * TOP_K + N_EXPERTS * MOE_ROWS
    nb = cap // MOE_ROWS
    block_e = jnp.minimum(jnp.searchsorted(pad_end, jnp.arange(nb, dtype=I32) * MOE_ROWS, side='right'),
                          N_EXPERTS - 1).astype(I32)
    n_used = (pad_end[-1] // MOE_ROWS).astype(I32).reshape(1)
    return slot_tiles, block_e, n_used, cap


def kernel(x, c, w_mod, b_mod, w_in, rwkv_mu, rwkv_w0, rwkv_w2, rwkv_a0, rwkv_a2, rwkv_g2, rwkv_k_k, rwkv_k_a, rwkv_r_k, rwkv_ln_g, rwkv_ln_b, dsa_kv_norm, dsa_w_uk, dsa_w_uv, dsa_ik_g, dsa_ik_b, swa_sinks, w_out, ln_mix_g, ln_mix_b, router_w, router_bias, exp_w1, exp_w3, exp_w2, sh_w1, sh_w3, sh_w2, ln_ffn_g, ln_ffn_b):
    bsz, t, d = x.shape
    assert bsz == 1 and d == D_MODEL
    depth = w_mod.shape[0]
    n_sl = SWA_HEADS + DSA_HEADS
    slopes = [2.0 ** (-8.0 * (j + 1.0) / n_sl) for j in range(n_sl)]
    swa_slopes, dsa_slopes = slopes[:SWA_HEADS], slopes[SWA_HEADS:]

    mod = _modulation(c, w_mod, b_mod)
    xs_cur = x[0]
    row1 = lambda v: v.reshape(1, -1)
    for l in range(depth):
        sh1, sc1, g1, sh2, sc2, g2 = [mod[l, :, j * d:(j + 1) * d] for j in range(6)]
        wp = _pad_w_in(w_in[l])
        w_hi = wp.astype(BF16)
        w_idx = wp[:, C_IDX[0]:C_IDX[1]]
        w_idx_lo = (w_idx - w_idx.astype(BF16).astype(F32)).astype(BF16)
        rkv, lora, dq, ckv, iq, ikw, sq, skv = _input_proj(
            xs_cur, sc1, sh1, w_hi, w_idx_lo, row1(dsa_kv_norm[l]),
            _pad_lanes(dsa_ik_g[l]), _pad_lanes(dsa_ik_b[l]))
        o_rw = _rwkv_mix(rkv, lora, row1(rwkv_mu[l]), row1(rwkv_w0[l]), rwkv_w2[l], row1(rwkv_a0[l]),
                         rwkv_a2[l], rwkv_g2[l], row1(rwkv_k_k[l]), row1(rwkv_k_a[l]), row1(rwkv_r_k[l]),
                         row1(rwkv_ln_g[l]), row1(rwkv_ln_b[l]))
        o_ds = _dsa_mix(dq, iq, ikw, ckv, dsa_w_uk[l], dsa_w_uv[l], dsa_slopes)
        o_sw = _swa_mix(sq, skv, _pad_lanes(swa_sinks[l]), swa_slopes)
        x1, h2, eidx, rank, gates, counts = _post_mix(
            xs_cur, o_rw, o_ds, o_sw, w_out[l].astype(BF16), g1, row1(ln_mix_g[l]), row1(ln_mix_b[l]),
            sc2, sh2, router_w[l].T, router_bias[l].reshape(-1, 1))
        slot_tiles, block_e, n_used, cap = _moe_tables(eidx, rank, counts)
        xs_sorted = _dispatch(slot_tiles, h2, cap)
        ys = _experts(block_e, n_used, xs_sorted, exp_w1[l].astype(BF16), exp_w3[l].astype(BF16),
                      exp_w2[l].astype(BF16))
        xs_cur = _combine(slot_tiles, ys, x1, h2, gates.T, sh_w1[l].astype(BF16), sh_w3[l].astype(BF16),
                          sh_w2[l].astype(BF16), g2, row1(ln_ffn_g[l]), row1(ln_ffn_b[l]))
    return xs_cur[None]
```

```python
import functools
import math

import jax
import jax.numpy as jnp
import numpy as np
from jax import lax
from jax.experimental import pallas as pl
from jax.experimental.pallas import tpu as pltpu

F32 = jnp.float32
BF16 = jnp.bfloat16
I32 = jnp.int32
HI = lax.Precision.HIGHEST

D_MODEL = 1024
DEPTH = 4
HEAD_DIM = 64
RWKV_HEADS = 6
DSA_HEADS = 4
SWA_HEADS = 6
SWA_KV_HEADS = 2
RWKV_W = RWKV_HEADS * HEAD_DIM
DSA_W = DSA_HEADS * HEAD_DIM
SWA_W = SWA_HEADS * HEAD_DIM
DECAY_LORA = 64
AAA_LORA = 64
GATE_LORA = 128
GN_EPS = 64e-5
KV_LORA = 128
IDX_HEADS = 4
IDX_DIM = 64
TOPK_MAX = 256
WINDOW = 128
N_EXPERTS = 64
TOP_K = 8
N_GROUPS = 8
TOPK_GROUPS = 4
D_EXPERT = 256
ROUTED_SCALE = 2.5
ALPHA = (2 * DEPTH) ** 0.25
LN_EPS = 1e-5
NEG = -1e30
INT_MIN = -(2 ** 31)

LANES = 128
VMEM_LIMIT = 56 * 1024 * 1024

C_RKV = (0, 1152)
C_LORA = (1152, 1408)
C_DQ = (1408, 1664)
C_CKV = (1664, 1792)
C_IDX = (1792, 2176)
C_SQ = (2176, 2560)
C_SKV = (2560, 2816)
P_PAD = 2816
N_ORIG_BEFORE_PAD = 2116


def _dot(a, b, prec=None):
    return jnp.dot(a, b, preferred_element_type=F32, precision=prec)


def _dot_nt(a, b, prec=None):
    return lax.dot_general(a, b, (((1,), (1,)), ((), ())), preferred_element_type=F32, precision=prec)


def _split2(a):
    a_hi = a.astype(BF16)
    return a_hi, (a - a_hi.astype(F32)).astype(BF16)


def _bdot(a, b):
    return _dot(a.astype(BF16), b.astype(BF16))


def _bdot_nt(a, b):
    return _dot_nt(a.astype(BF16), b.astype(BF16))


def _dot2(a, b_exact):
    a_hi, a_lo = _split2(a)
    return _dot(a_hi, b_exact) + _dot(a_lo, b_exact)


def _dot2_l(a_exact, b):
    b_hi, b_lo = _split2(b)
    return _dot(a_exact, b_hi) + _dot(a_exact, b_lo)


def _dot3(a, b):
    a_hi, a_lo = _split2(a)
    b_hi, b_lo = _split2(b)
    return _dot(a_hi, b_hi) + (_dot(a_lo, b_hi) + _dot(a_hi, b_lo))


def _iota(shape, dim):
    return lax.broadcasted_iota(I32, shape, dim)


def _sigmoid(x):
    return 1.0 / (1.0 + jnp.exp(-x))


def _layer_norm_rows(v, g, b):
    mu = jnp.mean(v, axis=-1, keepdims=True)
    d = v - mu
    var = jnp.mean(d * d, axis=-1, keepdims=True)
    return d * lax.rsqrt(var + LN_EPS) * g + b


def _params(sem):
    return pltpu.CompilerParams(dimension_semantics=sem, vmem_limit_bytes=VMEM_LIMIT)


def _mod_kernel(c_ref, w_ref, b_ref, o_ref):
    c = c_ref[...]
    cond = c * _sigmoid(c)
    o_ref[0] = _dot(cond, w_ref[0], HI) + b_ref[0]


def _modulation(c, w_mod, b_mod):
    depth, d, d6 = w_mod.shape
    c8 = jnp.broadcast_to(c, (8, d))
    nj = d6 // d
    out = pl.pallas_call(
        _mod_kernel,
        grid=(depth, nj),
        in_specs=[
            pl.BlockSpec((8, d), lambda l, j: (0, 0)),
            pl.BlockSpec((1, d, d), lambda l, j: (l, 0, j)),
            pl.BlockSpec((1, 1, d), lambda l, j: (l, 0, j)),
        ],
        out_specs=pl.BlockSpec((1, 8, d), lambda l, j: (l, 0, j)),
        out_shape=jax.ShapeDtypeStruct((depth, 8, d6), F32),
        compiler_params=_params(("arbitrary", "arbitrary")),
        name="modulation",
    )(c8, w_mod, b_mod.reshape(depth, 1, d6))
    return out[:, 0:1, :]


def _proj_kernel(x_ref, sc_ref, sh_ref, w_ref, wlo_ref, kvn_ref, ikg_ref, ikb_ref,
                 rkv_ref, lora_ref, dq_ref, ckv_ref, iq_ref, ikw_ref, sq_ref, skv_ref):
    h = x_ref[...] * (1.0 + sc_ref[...]) + sh_ref[...]
    hb = h.astype(BF16)
    hl = (h - hb.astype(F32)).astype(BF16)

    def mm(c):
        return _dot(hb, w_ref[:, c[0]:c[1]])

    rkv_ref[...] = mm(C_RKV)
    lora_ref[...] = mm(C_LORA)
    dq_ref[...] = mm(C_DQ)
    sq_ref[...] = mm(C_SQ)
    skv_ref[...] = mm(C_SKV)
    ckv = mm(C_CKV)
    ckv_ref[...] = ckv * lax.rsqrt(jnp.mean(ckv * ckv, axis=-1, keepdims=True) + 1e-6) * kvn_ref[...]
    idx = mm(C_IDX) + _dot(hl, w_ref[:, C_IDX[0]:C_IDX[1]]) + _dot(hb, wlo_ref[...])
    iq_ref[...] = idx[:, 0:256]
    g3 = idx[:, 256:384]
    lane = _iota(g3.shape, 1)
    isk = lane < IDX_DIM
    mu = jnp.sum(jnp.where(isk, g3, 0.0), axis=-1, keepdims=True) * (1.0 / IDX_DIM)
    dk = jnp.where(isk, g3 - mu, 0.0)
    var = jnp.sum(dk * dk, axis=-1, keepdims=True) * (1.0 / IDX_DIM)
    ikn = dk * lax.rsqrt(var + LN_EPS) * ikg_ref[...] + ikb_ref[...]
    ikw_ref[...] = jnp.where(isk, ikn, g3 * (IDX_HEADS ** -0.5 * IDX_DIM ** -0.5))


def _input_proj(x, sc, sh, w_hi, w_idx_lo, kvn, ikg, ikb, tm=512):
    t, d = x.shape
    widths = [C_RKV, C_LORA, C_DQ, C_CKV, (0, 256), (0, 128), C_SQ, C_SKV]
    widths = [c[1] - c[0] for c in widths]
    const = lambda i: (0, 0)
    row = lambda i: (i, 0)
    return pl.pallas_call(
        _proj_kernel,
        grid=(t // tm,),
        in_specs=[
            pl.BlockSpec((tm, d), row),
            pl.BlockSpec((1, d), const),
            pl.BlockSpec((1, d), const),
            pl.BlockSpec((d, P_PAD), const),
            pl.BlockSpec((d, C_IDX[1] - C_IDX[0]), const),
            pl.BlockSpec((1, KV_LORA), const),
            pl.BlockSpec((1, LANES), const),
            pl.BlockSpec((1, LANES), const),
        ],
        out_specs=[pl.BlockSpec((tm, w), row) for w in widths],
        out_shape=[jax.ShapeDtypeStruct((t, w), F32) for w in widths],
        compiler_params=_params(("arbitrary",)),
        name="input_proj",
    )(x, sc, sh, w_hi, w_idx_lo, kvn, ikg, ikb)


RW_CHUNK = 64


def _rwkv_kernel(r_ref, k_ref, v_ref, lora_ref, rp_ref, kp_ref, vp_ref, lp_ref,
                 mur_ref, muk_ref, muv_ref, mul_ref, w0_ref, w2_ref, a0_ref, a2_ref, g2_ref,
                 kk_ref, ka_ref, rk_ref, lng_ref, lnb_ref, o_ref,
                 h_ref, y_ref, st_ref, wm_ref, ar_ref, rs_ref, vs_ref, lt_ref, zm_ref, y0_ref, gc_ref, *, tg):
    g = pl.program_id(1)
    c64 = RW_CHUNK
    lane = _iota((1, LANES), 1)
    first = g == 0

    @pl.when(first)
    def _():
        h_ref[...] = jnp.zeros_like(h_ref)

    rowid = _iota((tg, 1), 0)

    def shift_mix(cur_ref, prev_ref, mu_ref):
        cur = cur_ref[...]
        prev_row = jnp.where(first, 0.0, prev_ref[7:8, :])
        rolled = pltpu.roll(cur, 1, 0)
        shifted = jnp.where(rowid == 0, prev_row, rolled)
        return cur + (shifted - cur) * mu_ref[...]

    r = shift_mix(r_ref, rp_ref, mur_ref)
    k = shift_mix(k_ref, kp_ref, muk_ref)
    v = shift_mix(v_ref, vp_ref, muv_ref)
    lo = shift_mix(lora_ref, lp_ref, mul_ref)
    wl = lo[:, 0:DECAY_LORA]
    al = lo[:, DECAY_LORA:DECAY_LORA + AAA_LORA]
    gl = lo[:, 128:256]

    zw = -(w0_ref[...] + _dot3(jnp.tanh(wl), w2_ref[...]))
    softplus = jnp.maximum(zw, 0.0) + jnp.log(1.0 + jnp.exp(-jnp.abs(zw)))
    lw = -jnp.exp(-softplus - 0.5)
    a = _sigmoid(a0_ref[...] + _bdot(al, a2_ref[...]))
    gate = _bdot(_sigmoid(gl), g2_ref[...])

    ri = _iota((LANES, LANES), 0) // HEAD_DIM
    ci = _iota((LANES, LANES), 1) // HEAD_DIM
    bones = jnp.where(ri == ci, 1.0, 0.0).astype(BF16)
    kk = k * kk_ref[...]
    n2 = _dot2(kk * kk, bones)
    kk = kk / jnp.maximum(jnp.sqrt(n2), 1e-12)
    k2 = k * (1.0 + (a - 1.0) * ka_ref[...])
    bonus = _dot2(r * k2 * rk_ref[...], bones) * v
    bvec = a * kk

    st_ref[0] = r
    st_ref[1] = k2
    st_ref[2] = v
    st_ref[3] = lw
    st_ref[4] = kk
    st_ref[5] = bvec

    rr = _iota((LANES, LANES), 0)
    cc = _iota((LANES, LANES), 1)
    same = (rr // c64) == (cc // c64)
    strict = same & ((rr % c64) > (cc % c64))
    incl = same & ((rr % c64) >= (cc % c64))
    eye = jnp.where(rr == cc, 1.0, 0.0)
    tril = jnp.where(_iota((c64, c64), 0) >= _iota((c64, c64), 1), 1.0, 0.0).astype(BF16)
    lo_half = lane < HEAD_DIM

    def stack(xc):
        return jnp.concatenate([jnp.where(lo_half, xc, 0.0), jnp.where(lo_half, 0.0, xc)], axis=0)

    def prepare(c, carry):
        sl = pl.ds(pl.multiple_of(c * c64, c64), c64)
        rc = st_ref[0, sl, :]
        kc = st_ref[1, sl, :]
        vc = st_ref[2, sl, :]
        lwc = st_ref[3, sl, :]
        kkc = st_ref[4, sl, :]
        bc = st_ref[5, sl, :]
        cum = _dot2_l(tril, lwc)
        tot = cum[c64 - 1:c64, :]
        g_in = jnp.exp(cum)
        g_ex = jnp.exp(cum - lwc)
        g_inv = jnp.exp(-cum)
        g_rest = jnp.exp(tot - cum)
        a_s = stack(-kkc * g_ex).astype(BF16)
        b_s = stack(bc * g_inv).astype(BF16)
        k_s = stack(kc * g_inv).astype(BF16)
        r_s = stack(rc * g_in).astype(BF16)
        v_s = stack(vc).astype(BF16)
        nmat = jnp.where(strict, _dot_nt(a_s, b_s), 0.0)
        aak = jnp.where(strict, _dot_nt(a_s, k_s), 0.0)
        arb = jnp.where(incl, _dot_nt(r_s, b_s), 0.0)
        ark = jnp.where(incl, _dot_nt(r_s, k_s), 0.0)
        tinv = eye + nmat
        pw = nmat
        for _ in range(5):
            pw = _bdot(pw, pw)
            tinv = _bdot(tinv, eye + pw)
        tinv = tinv.astype(BF16)
        wm_ref[c] = _dot(tinv, a_s).astype(BF16)
        zm_ref[c] = _dot(tinv, _bdot(aak, v_s).astype(BF16))
        y0_ref[c] = _bdot(ark, v_s)
        ar_ref[c] = arb.astype(BF16)
        rs_ref[c] = r_s
        vs_ref[c] = v_s
        lt_ref[c] = jnp.concatenate([stack(bc * g_rest), stack(kc * g_rest)], axis=0).T.astype(BF16)
        gc_ref[c] = jnp.broadcast_to(jnp.sum(eye * jnp.exp(tot), axis=1, keepdims=True), (LANES, LANES))
        return carry

    lax.fori_loop(0, tg // c64, prepare, 0, unroll=2)

    def advance(c, carry):
        sl = pl.ds(pl.multiple_of(c * c64, c64), c64)
        hst = h_ref[...]
        hb = hst.astype(BF16)
        u = _dot(wm_ref[c], hb) + zm_ref[c]
        ub = u.astype(BF16)
        ys = _dot(rs_ref[c], hb) + _dot(ar_ref[c], ub) + y0_ref[c]
        h_ref[...] = gc_ref[c] * hst + _dot(lt_ref[c], jnp.concatenate([ub, vs_ref[c]], axis=0))
        y_ref[sl, :] = ys[0:c64, :] + ys[c64:2 * c64, :]
        return carry

    lax.fori_loop(0, tg // c64, advance, 0)

    y = y_ref[...]
    mean = _dot2(y, bones) * (1.0 / HEAD_DIM)
    dy = y - mean
    var = _dot2(dy * dy, bones) * (1.0 / HEAD_DIM)
    o = dy * lax.rsqrt(var + GN_EPS) * lng_ref[...] + lnb_ref[...]
    o_ref[...] = (o + bonus) * gate


def _rwkv_mix(rkv, lora, mu, w0, w2, a0, a2, g2, k_k, k_a, r_k, ln_g, ln_b, tg=512):
    t = rkv.shape[0]
    npair = RWKV_W // LANES
    nch = tg // RW_CHUNK
    mu_r, mu_k, mu_v, mu_l = mu[:, 0:384], mu[:, 384:768], mu[:, 768:1152], mu[:, 1152:1408]
    blk = lambda off: pl.BlockSpec((tg, LANES), lambda p, g: (g, off + p))
    prev = lambda off: pl.BlockSpec((8, LANES), lambda p, g: (jnp.maximum(g * (tg // 8) - 1, 0), off + p))
    vec = pl.BlockSpec((1, LANES), lambda p, g: (0, p))
    return pl.pallas_call(
        functools.partial(_rwkv_kernel, tg=tg),
        grid=(npair, t // tg),
        in_specs=[
            blk(0), blk(3), blk(6),
            pl.BlockSpec((tg, 256), lambda p, g: (g, 0)),
            prev(0), prev(3), prev(6),
            pl.BlockSpec((8, 256), lambda p, g: (jnp.maximum(g * (tg // 8) - 1, 0), 0)),
            vec, vec, vec,
            pl.BlockSpec((1, 256), lambda p, g: (0, 0)),
            vec,
            pl.BlockSpec((DECAY_LORA, LANES), lambda p, g: (0, p)),
            vec,
            pl.BlockSpec((AAA_LORA, LANES), lambda p, g: (0, p)),
            pl.BlockSpec((GATE_LORA, LANES), lambda p, g: (0, p)),
            vec, vec, vec, vec, vec,
        ],
        out_specs=pl.BlockSpec((tg, LANES), lambda p, g: (g, p)),
        out_shape=jax.ShapeDtypeStruct((t, RWKV_W), F32),
        scratch_shapes=[
            pltpu.VMEM((LANES, LANES), F32),
            pltpu.VMEM((tg, LANES), F32),
            pltpu.VMEM((6, tg, LANES), F32),
            pltpu.VMEM((nch, LANES, LANES), BF16),
            pltpu.VMEM((nch, LANES, LANES), BF16),
            pltpu.VMEM((nch, LANES, LANES), BF16),
            pltpu.VMEM((nch, LANES, LANES), BF16),
            pltpu.VMEM((nch, LANES, 2 * LANES), BF16),
            pltpu.VMEM((nch, LANES, LANES), F32),
            pltpu.VMEM((nch, LANES, LANES), F32),
            pltpu.VMEM((nch, LANES, LANES), F32),
        ],
        compiler_params=_params(("arbitrary", "arbitrary")),
        name="rwkv7_mix",
    )(rkv, rkv, rkv, lora, rkv, rkv, rkv, lora,
      mu_r, mu_k, mu_v, mu_l, w0, w2, a0, a2, g2, k_k, k_a, r_k, ln_g, ln_b)


DSA_QB = 128
DSA_KC = 512
CNT_ROWS = 64


def _float_key(v):
    bits = lax.bitcast_convert_type(v, I32)
    return bits ^ ((bits >> 31) & 0x7FFFFFFF)


def _dsa_kernel(dq_ref, iq_ref, ikw_ref, ikx_ref, kf_ref, vft_ref, wuk_ref, wuv_ref, tril_ref, slc_ref,
                o_ref, sc_ref, acc_ref):
    i = pl.program_id(0)
    qb, kc = DSA_QB, DSA_KC
    nh = DSA_HEADS
    t0 = i * qb
    nch = (t0 + qb + kc - 1) // kc
    tq = t0 + _iota((1, qb), 1)

    iq = iq_ref[...]
    iq_hi = iq.astype(BF16).astype(F32)
    iq_lo = iq - iq_hi
    lhs = []
    for h in range(IDX_HEADS):
        s = slice(h * IDX_DIM, (h + 1) * IDX_DIM)
        lhs.append(jnp.concatenate([iq_hi[:, s], iq_hi[:, s], iq_lo[:, s], iq_lo[:, s]], axis=1))
    lhs_t = jnp.concatenate(lhs, axis=0).T.astype(BF16)
    ikw_t = ikw_ref[...].T
    iw = [ikw_t[IDX_DIM + h:IDX_DIM + h + 1, :] for h in range(IDX_HEADS)]

    def score_body(ch, carry):
        m1, m2 = carry
        k0 = pl.multiple_of(ch * kc, kc)
        s_all = _dot(ikx_ref[pl.ds(k0, kc), :], lhs_t)
        acc = jnp.zeros((kc, qb), F32)
        for h in range(IDX_HEADS):
            acc = acc + jnp.maximum(s_all[:, h * qb:(h + 1) * qb], 0.0) * iw[h]
        acc = jnp.where(acc == 0.0, 0.0, acc)
        causal = (k0 + _iota((kc, 1), 0)) <= tq
        sc_ref[pl.ds(k0, kc), :] = jnp.where(causal, _float_key(acc), INT_MIN)
        accm = jnp.where(causal, acc, -jnp.inf)
        for j in range(kc // LANES):
            xj = accm[j * LANES:(j + 1) * LANES, :]
            m2 = jnp.maximum(m2, jnp.minimum(m1, xj))
            m1 = jnp.maximum(m1, xj)
        return m1, m2

    ninf = jnp.full((LANES, qb), -jnp.inf, F32)
    m1, m2 = lax.fori_loop(0, nch, score_body, (ninf, ninf))

    def count_ge(cand):
        def body(ch, acc):
            k0 = pl.multiple_of(ch * kc, kc)
            m = jnp.where(sc_ref[pl.ds(k0, kc), :] >= cand, 1.0, 0.0)
            for j in range(kc // CNT_ROWS):
                acc = acc + m[j * CNT_ROWS:(j + 1) * CNT_ROWS, :]
            return acc
        acc = lax.fori_loop(0, nch, body, jnp.zeros((CNT_ROWS, qb), F32))
        return jnp.sum(acc, axis=0, keepdims=True)

    k_row = jnp.minimum(tq + 1, TOPK_MAX).astype(F32)
    hi0 = _float_key(jnp.max(m1, axis=0, keepdims=True))
    lo0 = jnp.minimum(_float_key(jnp.min(m2, axis=0, keepdims=True)), hi0)

    def open_rows(lo, hi):
        return jnp.max(jnp.where(lo < hi, 1.0, 0.0))

    def bis_body(st):
        lo, hi, _ = st
        mid = (lo | hi) - ((lo ^ hi) >> 1)
        c = count_ge(mid)
        ge = c >= k_row
        lo_n = jnp.where(ge, mid, lo)
        hi_n = jnp.where(c == k_row, mid, jnp.where(ge, hi, mid - 1))
        return lo_n, hi_n, open_rows(lo_n, hi_n)

    thr, _, _ = lax.while_loop(lambda st: st[2] > 0.5, bis_body, (lo0, hi0, open_rows(lo0, hi0)))
    n_ge = count_ge(thr)
    has_tie = jnp.max(jnp.where(n_ge > k_row, 1.0, 0.0)) > 0.5

    dq = dq_ref[...]
    slc = slc_ref[...]
    qaug = []
    for h in range(nh):
        ql = _bdot(dq[:, h * HEAD_DIM:(h + 1) * HEAD_DIM], wuk_ref[h]) * HEAD_DIM ** -0.5
        qaug.append(jnp.concatenate([ql, jnp.broadcast_to(slc[h:h + 1, :], (qb, LANES))], axis=1))
    qaug_t = jnp.concatenate(qaug, axis=0).T.astype(BF16)
    acc_ref[...] = jnp.zeros_like(acc_ref)

    def attend(k0, sel, m_old):
        lg_all = _dot(kf_ref[pl.ds(k0, kc), :], qaug_t)
        ps, m_new = [], []
        for h in range(nh):
            cols = slice(h * qb, (h + 1) * qb)
            lg = jnp.where(sel, lg_all[:, cols], NEG)
            mh = jnp.maximum(m_old[:, cols], jnp.max(lg, axis=0, keepdims=True))
            ps.append(jnp.exp((lg - mh).astype(BF16)))
            m_new.append(mh)
        m_new = jnp.concatenate(m_new, axis=1)
        pv = _dot(vft_ref[:, pl.ds(k0, kc)], jnp.concatenate(ps, axis=1))
        acc_ref[...] = jnp.exp(m_old - m_new) * acc_ref[...] + pv
        return m_new

    m_init = jnp.full((1, nh * qb), NEG, F32)

    @pl.when(jnp.logical_not(has_tie))
    def _():
        def body(ch, m_old):
            k0 = pl.multiple_of(ch * kc, kc)
            return attend(k0, sc_ref[pl.ds(k0, kc), :] >= thr, m_old)
        lax.fori_loop(0, nch, body, m_init)

    @pl.when(has_tie)
    def _():
        need = k_row - count_ge(thr + 1)
        tril = tril_ref[...]

        def body(ch, carry):
            tie_run, m_old = carry
            k0 = pl.multiple_of(ch * kc, kc)
            key = sc_ref[pl.ds(k0, kc), :]
            eq = key == thr
            pref = _dot(tril, jnp.where(eq, 1.0, 0.0).astype(BF16))
            m_new = attend(k0, (key > thr) | (eq & (tie_run + pref <= need)), m_old)
            return tie_run + pref[kc - 1:kc, :], m_new
        lax.fori_loop(0, nch, body, (jnp.zeros((1, qb), F32), m_init))

    acc = acc_ref[...]
    o_lat = acc[0:KV_LORA, :] / acc[KV_LORA:KV_LORA + 1, :]
    outs = [_bdot(o_lat[:, h * qb:(h + 1) * qb].T, wuv_ref[h]) for h in range(nh)]
    o_ref[...] = jnp.concatenate(outs, axis=1)


DSA_VROWS = KV_LORA + 16


def _dsa_mix(dq, iq, ikw, ckv, w_uk, w_uv, slopes):
    t = dq.shape[0]
    assert t <= LANES * 256
    ikn = ikw[:, 0:IDX_DIM]
    ik_hi, ik_lo = _split2(ikn)
    ikx = jnp.concatenate([ik_hi, ik_lo, ik_hi, ik_lo], axis=1)
    ckv_b = ckv.astype(BF16)
    pos = jnp.arange(t, dtype=I32)
    pa = (pos // LANES).astype(BF16)[:, None]
    pb = (pos % LANES).astype(BF16)[:, None]
    kf = jnp.concatenate([ckv_b, pa, pa, pa, pb, pb, pb, jnp.zeros((t, LANES - 6), BF16)], axis=1)
    vft = jnp.concatenate([ckv_b.T, jnp.ones((1, t), BF16), jnp.zeros((DSA_VROWS - KV_LORA - 1, t), BF16)], axis=0)
    cols = []
    for sl in slopes:
        for coef in (sl * LANES, sl):
            c_hi = jnp.asarray(coef, F32).astype(BF16)
            r1 = jnp.asarray(coef, F32) - c_hi.astype(F32)
            c_mid = r1.astype(BF16)
            c_lo = (r1 - c_mid.astype(F32)).astype(BF16)
            cols += [c_hi.astype(F32), c_mid.astype(F32), c_lo.astype(F32)]
    slc = jnp.stack(cols).reshape(DSA_HEADS, 6)
    slc = jnp.pad(slc, ((0, 8 - DSA_HEADS), (0, LANES - 6)))
    kc = DSA_KC
    tril = jnp.asarray((np.arange(kc)[:, None] >= np.arange(kc)[None, :]).astype(np.float32), BF16)
    row = lambda i: (i, 0)
    const2 = lambda i: (0, 0)
    const3 = lambda i: (0, 0, 0)
    resident = lambda shape: pl.BlockSpec(shape, const2, pipeline_mode=pl.Buffered(1))
    return pl.pallas_call(
        _dsa_kernel,
        grid=(t // DSA_QB,),
        in_specs=[
            pl.BlockSpec((DSA_QB, DSA_W), row),
            pl.BlockSpec((DSA_QB, IDX_HEADS * IDX_DIM), row),
            pl.BlockSpec((DSA_QB, LANES), row),
            resident((t, 4 * IDX_DIM)),
            resident((t, 2 * LANES)),
            resident((DSA_VROWS, t)),
            pl.BlockSpec((DSA_HEADS, HEAD_DIM, KV_LORA), const3),
            pl.BlockSpec((DSA_HEADS, KV_LORA, HEAD_DIM), const3),
            resident((kc, kc)),
            pl.BlockSpec((8, LANES), const2),
        ],
        out_specs=pl.BlockSpec((DSA_QB, DSA_W), row),
        out_shape=jax.ShapeDtypeStruct((t, DSA_W), F32),
        scratch_shapes=[
            pltpu.VMEM((t, DSA_QB), I32),
            pltpu.VMEM((DSA_VROWS, DSA_HEADS * DSA_QB), F32),
        ],
        compiler_params=_params(("arbitrary",)),
        name="dsa_mix",
    )(dq, iq, ikw, ikx, kf, vft, w_uk, w_uv, tril, slc)


def _swa_kernel(q_ref, kv_ref, kvp_ref, sink_ref, o_ref, *, slopes):
    i = pl.program_id(0)
    w = WINDOW
    gsz = SWA_HEADS // SWA_KV_HEADS
    q = q_ref[...]
    kv = kv_ref[...]
    kvp = kvp_ref[...]
    qi = _iota((w, 2 * w), 0)
    kj = _iota((w, 2 * w), 1)
    dist = qi + w - kj
    valid = (dist >= 0) & (dist < w) & ((kj >= w) | (i > 0))
    distf = dist.astype(F32)
    sinks = sink_ref[...]
    outs = []
    for g in range(SWA_KV_HEADS):
        k2 = jnp.concatenate([kvp[:, g * HEAD_DIM:(g + 1) * HEAD_DIM],
                              kv[:, g * HEAD_DIM:(g + 1) * HEAD_DIM]], axis=0).astype(BF16)
        v2 = jnp.concatenate([kvp[:, w + g * HEAD_DIM:w + (g + 1) * HEAD_DIM],
                              kv[:, w + g * HEAD_DIM:w + (g + 1) * HEAD_DIM]], axis=0).astype(BF16)
        for r in range(gsz):
            hd = g * gsz + r
            qh = q[:, hd * HEAD_DIM:(hd + 1) * HEAD_DIM].astype(BF16)
            s = _dot_nt(qh, k2) * HEAD_DIM ** -0.5
            s = jnp.where(valid, s - slopes[hd] * distf, NEG)
            sink = sinks[0:1, hd:hd + 1]
            m = jnp.maximum(jnp.max(s, axis=1, keepdims=True), sink)
            e = jnp.exp(s - m)
            p = e / (jnp.sum(e, axis=1, keepdims=True) + jnp.exp(sink - m))
            outs.append(_dot(p.astype(BF16), v2))
    o_ref[...] = jnp.concatenate(outs, axis=1)


def _swa_mix(sq, skv, sinks, slopes):
    t = sq.shape[0]
    w = WINDOW
    return pl.pallas_call(
        functools.partial(_swa_kernel, slopes=slopes),
        grid=(t // w,),
        in_specs=[
            pl.BlockSpec((w, SWA_W), lambda i: (i, 0)),
            pl.BlockSpec((w, 2 * w), lambda i: (i, 0)),
            pl.BlockSpec((w, 2 * w), lambda i: (jnp.maximum(i - 1, 0), 0)),
            pl.BlockSpec((1, LANES), lambda i: (0, 0)),
        ],
        out_specs=pl.BlockSpec((w, SWA_W), lambda i: (i, 0)),
        out_shape=jax.ShapeDtypeStruct((t, SWA_W), F32),
        compiler_params=_params(("arbitrary",)),
        name="swa_mix",
    )(sq, skv, skv, sinks)


def _post_mix_kernel(x_ref, orw_ref, ods_ref, osw_ref, wout_ref, g1_ref, lng_ref, lnb_ref,
                     sc2_ref, sh2_ref, rwt_ref, rb_ref, tri_ref,
                     x1_ref, h2_ref, eidx_ref, rank_ref, gate_ref, cnt_ref, carry_ref):
    i = pl.program_id(0)

    @pl.when(i == 0)
    def _():
        carry_ref[...] = jnp.zeros_like(carry_ref)

    y = (_dot(orw_ref[...].astype(BF16), wout_ref[0:RWKV_W, :])
         + _dot(ods_ref[...].astype(BF16), wout_ref[RWKV_W:RWKV_W + DSA_W, :])
         + _dot(osw_ref[...].astype(BF16), wout_ref[RWKV_W + DSA_W:D_MODEL, :]))
    x1 = _layer_norm_rows(ALPHA * x_ref[...] + g1_ref[...] * y, lng_ref[...], lnb_ref[...])
    x1_ref[...] = x1
    h2 = x1 * (1.0 + sc2_ref[...]) + sh2_ref[...]
    h2_ref[...] = h2

    tm = h2.shape[0]
    ne = N_EXPERTS
    gs = ne // N_GROUPS
    scores = _sigmoid(_dot_nt(rwt_ref[...], h2, HI))
    sel = scores + rb_ref[...]
    sub = _iota((gs, tm), 0).astype(F32)
    gsc = []
    for j in range(N_GROUPS):
        gj = sel[j * gs:(j + 1) * gs, :]
        m1 = jnp.max(gj, axis=0, keepdims=True)
        f1 = jnp.min(jnp.where(gj == m1, sub, float(gs)), axis=0, keepdims=True)
        m2 = jnp.max(jnp.where(sub == f1, -jnp.inf, gj), axis=0, keepdims=True)
        gsc.append(m1 + m2)
    gsc = jnp.concatenate(gsc, axis=0)
    gid = _iota((N_GROUPS, tm), 0).astype(F32)
    gmask = jnp.zeros((N_GROUPS, tm), F32)
    for _ in range(TOPK_GROUPS):
        mx = jnp.max(gsc, axis=0, keepdims=True)
        fi = jnp.min(jnp.where(gsc == mx, gid, float(N_GROUPS)), axis=0, keepdims=True)
        pick = gid == fi
        gmask = jnp.where(pick, 1.0, gmask)
        gsc = jnp.where(pick, -jnp.inf, gsc)
    selm = jnp.concatenate(
        [jnp.where(gmask[j:j + 1, :] > 0.5, sel[j * gs:(j + 1) * gs, :], NEG) for j in range(N_GROUPS)], axis=0)
    eid = _iota((ne, tm), 0).astype(F32)
    gsel, eids = [], []
    chosen_f = jnp.zeros((ne, tm), F32)
    for _ in range(TOP_K):
        mx = jnp.max(selm, axis=0, keepdims=True)
        fi = jnp.min(jnp.where(selm == mx, eid, float(ne)), axis=0, keepdims=True)
        pick = eid == fi
        eids.append(fi)
        gsel.append(jnp.sum(jnp.where(pick, scores, 0.0), axis=0, keepdims=True))
        chosen_f = jnp.where(pick, 1.0, chosen_f)
        selm = jnp.where(pick, -jnp.inf, selm)
    gsum = gsel[0]
    for kx in range(1, TOP_K):
        gsum = gsum + gsel[kx]
    before = _dot(chosen_f.astype(BF16), tri_ref[...]) + carry_ref[:, 0:1]
    ranks = [jnp.sum(jnp.where(eid == eids[kx], before, 0.0), axis=0, keepdims=True) for kx in range(TOP_K)]
    eidx_ref[...] = jnp.concatenate(eids, axis=0).astype(I32)
    rank_ref[...] = jnp.concatenate(ranks, axis=0).astype(I32)
    gate_ref[...] = jnp.concatenate(gsel, axis=0) / gsum * ROUTED_SCALE
    carry_ref[...] = carry_ref[...] + jnp.sum(chosen_f, axis=1, keepdims=True)
    cnt_ref[...] = carry_ref[...]


def _post_mix(x, o_rw, o_ds, o_sw, w_out, g1, ln_g, ln_b, sc2, sh2, router_wt, router_b, tm=256):
    t, d = x.shape
    tri = (np.arange(tm)[:, None] < np.arange(tm)[None, :]).astype(np.float32)
    tri = jnp.asarray(tri, BF16)
    row = lambda i: (i, 0)
    const = lambda i: (0, 0)
    col = lambda i: (0, i)
    vec = pl.BlockSpec((1, d), const)
    return pl.pallas_call(
        _post_mix_kernel,
        grid=(t // tm,),
        in_specs=[
            pl.BlockSpec((tm, d), row),
            pl.BlockSpec((tm, RWKV_W), row),
            pl.BlockSpec((tm, DSA_W), row),
            pl.BlockSpec((tm, SWA_W), row),
            pl.BlockSpec((d, d), const),
            vec, vec, vec, vec, vec,
            pl.BlockSpec((N_EXPERTS, d), const),
            pl.BlockSpec((N_EXPERTS, 1), const),
            pl.BlockSpec((tm, tm), const),
        ],
        out_specs=[
            pl.BlockSpec((tm, d), row),
            pl.BlockSpec((tm, d), row),
            pl.BlockSpec((TOP_K, tm), col),
            pl.BlockSpec((TOP_K, tm), col),
            pl.BlockSpec((TOP_K, tm), col),
            pl.BlockSpec((N_EXPERTS, LANES), const),
        ],
        out_shape=[
            jax.ShapeDtypeStruct((t, d), F32),
            jax.ShapeDtypeStruct((t, d), F32),
            jax.ShapeDtypeStruct((TOP_K, t), I32),
            jax.ShapeDtypeStruct((TOP_K, t), I32),
            jax.ShapeDtypeStruct((TOP_K, t), F32),
            jax.ShapeDtypeStruct((N_EXPERTS, LANES), F32),
        ],
        scratch_shapes=[pltpu.VMEM((N_EXPERTS, LANES), F32)],
        compiler_params=_params(("arbitrary",)),
        name="post_mix_router",
    )(x, o_rw, o_ds, o_sw, w_out, g1, ln_g, ln_b, sc2, sh2, router_wt, router_b, tri)


MOE_ROWS = 256
MOE_TILE = 256


def _row_copy(src_ref, src_row, dst_ref, dst_row, sem):
    return pltpu.make_async_copy(src_ref.at[pl.ds(src_row, 1), :], dst_ref.at[pl.ds(dst_row, 1), :], sem)


def _dispatch_kernel(slot_hbm, h_ref, xs_in, xs_out, slot_smem, sem_tab, sem_rows):
    del xs_in
    i = pl.program_id(0)
    tab = pltpu.make_async_copy(slot_hbm.at[i], slot_smem, sem_tab)
    tab.start()
    tab.wait()

    def issue(tt, carry):
        for kx in range(TOP_K):
            _row_copy(h_ref, tt, xs_out, slot_smem[kx, tt], sem_rows).start()
        return carry

    lax.fori_loop(0, MOE_TILE, issue, 0)

    def drain(tt, carry):
        for kx in range(TOP_K):
            _row_copy(h_ref, 0, xs_out, 0, sem_rows).wait()
        return carry

    lax.fori_loop(0, MOE_TILE, drain, 0)


def _dispatch(slot_tiles, h2, cap):
    t, d = h2.shape
    xs0 = jnp.zeros((cap, d), F32)
    return pl.pallas_call(
        _dispatch_kernel,
        grid=(t // MOE_TILE,),
        in_specs=[
            pl.BlockSpec(memory_space=pl.ANY),
            pl.BlockSpec((MOE_TILE, d), lambda i: (i, 0)),
            pl.BlockSpec(memory_space=pl.ANY),
        ],
        out_specs=pl.BlockSpec(memory_space=pl.ANY),
        out_shape=jax.ShapeDtypeStruct((cap, d), F32),
        scratch_shapes=[
            pltpu.SMEM((TOP_K, MOE_TILE), I32),
            pltpu.SemaphoreType.DMA,
            pltpu.SemaphoreType.DMA,
        ],
        input_output_aliases={2: 0},
        compiler_params=_params(("arbitrary",)),
        name="moe_dispatch",
    )(slot_tiles, h2, xs0)


def _expert_kernel(be_ref, nb_ref, xs_ref, w1_ref, w3_ref, w2_ref, ys_ref, w1b, w3b, w2b):
    b = pl.program_id(0)
    changed = (b == 0) | (be_ref[b] != be_ref[jnp.maximum(b - 1, 0)])

    @pl.when(changed & (b < nb_ref[0]))
    def _():
        w1b[...] = w1_ref[0, 0].astype(BF16)
        w3b[...] = w3_ref[0, 0].astype(BF16)
        w2b[...] = w2_ref[0, 0].astype(BF16)

    @pl.when(b < nb_ref[0])
    def _():
        xb = xs_ref[...].astype(BF16)
        a = _dot(xb, w1b[...])
        gte = _dot(xb, w3b[...])
        hmid = (a * _sigmoid(a) * gte).astype(BF16)
        ys_ref[...] = _dot(hmid, w2b[...])

    @pl.when(b >= nb_ref[0])
    def _():
        ys_ref[...] = jnp.zeros_like(ys_ref)


def _experts(block_e, n_used, xs, w1, w3, w2, layer):
    cap, d = xs.shape
    nb = cap // MOE_ROWS
    grid_spec = pltpu.PrefetchScalarGridSpec(
        num_scalar_prefetch=2,
        grid=(nb,),
        in_specs=[
            pl.BlockSpec((MOE_ROWS, d), lambda b, be, nu: (b, 0)),
            pl.BlockSpec((1, 1, d, D_EXPERT), lambda b, be, nu: (layer, be[b], 0, 0)),
            pl.BlockSpec((1, 1, d, D_EXPERT), lambda b, be, nu: (layer, be[b], 0, 0)),
            pl.BlockSpec((1, 1, D_EXPERT, d), lambda b, be, nu: (layer, be[b], 0, 0)),
        ],
        out_specs=pl.BlockSpec((MOE_ROWS, d), lambda b, be, nu: (b, 0)),
        scratch_shapes=[
            pltpu.VMEM((d, D_EXPERT), BF16),
            pltpu.VMEM((d, D_EXPERT), BF16),
            pltpu.VMEM((D_EXPERT, d), BF16),
        ],
    )
    return pl.pallas_call(
        _expert_kernel,
        grid_spec=grid_spec,
        out_shape=jax.ShapeDtypeStruct((cap, d), F32),
        compiler_params=_params(("arbitrary",)),
        name="moe_experts",
    )(block_e, n_used, xs, w1, w3, w2)


def _combine_kernel(slot_hbm, ys_hbm, x1_ref, h2_ref, gate_ref, sw1_ref, sw3_ref, sw2_ref,
                    g2_ref, lng_ref, lnb_ref, o_ref, slot_smem, gbuf, sem_tab, sem_rows):
    i = pl.program_id(0)
    tab = pltpu.make_async_copy(slot_hbm.at[i], slot_smem, sem_tab)
    tab.start()
    tab.wait()

    def issue(tt, carry):
        for kx in range(TOP_K):
            _row_copy(ys_hbm, slot_smem[kx, tt], gbuf.at[kx], tt, sem_rows).start()
        return carry

    lax.fori_loop(0, MOE_TILE, issue, 0)

    hb = h2_ref[...].astype(BF16)
    a = _dot(hb, sw1_ref[...])
    gte = _dot(hb, sw3_ref[...])
    y = _dot((a * _sigmoid(a) * gte).astype(BF16), sw2_ref[...])

    def drain(tt, carry):
        for kx in range(TOP_K):
            _row_copy(ys_hbm, 0, gbuf.at[kx], 0, sem_rows).wait()
        return carry

    lax.fori_loop(0, MOE_TILE, drain, 0)

    gates = gate_ref[...]
    for kx in range(TOP_K):
        y = y + gates[:, kx:kx + 1] * gbuf[kx]
    o_ref[...] = _layer_norm_rows(ALPHA * x1_ref[...] + g2_ref[...] * y, lng_ref[...], lnb_ref[...])


def _combine(slot_tiles, ys, x1, h2, gates_t, sw1, sw3, sw2, g2, ln_g, ln_b):
    t, d = x1.shape
    row = lambda i: (i, 0)
    const = lambda i: (0, 0)
    vec = pl.BlockSpec((1, d), const)
    return pl.pallas_call(
        _combine_kernel,
        grid=(t // MOE_TILE,),
        in_specs=[
            pl.BlockSpec(memory_space=pl.ANY),
            pl.BlockSpec(memory_space=pl.ANY),
            pl.BlockSpec((MOE_TILE, d), row),
            pl.BlockSpec((MOE_TILE, d), row),
            pl.BlockSpec((MOE_TILE, TOP_K), row),
            pl.BlockSpec((d, D_EXPERT), const),
            pl.BlockSpec((d, D_EXPERT), const),
            pl.BlockSpec((D_EXPERT, d), const),
            vec, vec, vec,
        ],
        out_specs=pl.BlockSpec((MOE_TILE, d), row),
        out_shape=jax.ShapeDtypeStruct((t, d), F32),
        scratch_shapes=[
            pltpu.SMEM((TOP_K, MOE_TILE), I32),
            pltpu.VMEM((TOP_K, MOE_TILE, d), F32),
            pltpu.SemaphoreType.DMA,
            pltpu.SemaphoreType.DMA,
        ],
        compiler_params=_params(("arbitrary",)),
        name="moe_combine",
    )(slot_tiles, ys, x1, h2, gates_t, sw1, sw3, sw2, g2, ln_g, ln_b)


def _pad_w_in(w_in_l):
    d = w_in_l.shape[0]
    pad = jnp.zeros((d, C_SQ[0] - N_ORIG_BEFORE_PAD), w_in_l.dtype)
    return jnp.concatenate([w_in_l[:, :N_ORIG_BEFORE_PAD], pad, w_in_l[:, N_ORIG_BEFORE_PAD:]], axis=1)


def _pad_lanes(v, width=LANES):
    v = v.reshape(1, -1)
    return jnp.pad(v, ((0, 0), (0, width - v.shape[1])))


def _moe_tables(eidx, rank, counts):
    t = eidx.shape[1]
    cnt = counts[:, 0].astype(I32)
    padded = (cnt + MOE_ROWS - 1) // MOE_ROWS * MOE_ROWS
    pad_end = jnp.cumsum(padded)
    pad_start = pad_end - padded
    e_ids = jnp.arange(N_EXPERTS, dtype=I32)
    start_of = jnp.sum(jnp.where(eidx[..., None] == e_ids, pad_start, 0), axis=-1)
    slot = start_of + rank
    slot_tiles = slot.reshape(TOP_K, t // MOE_TILE, MOE_TILE).transpose(1, 0, 2)
    cap = t * TOP_K + N_EXPERTS * MOE_ROWS
    nb = cap // MOE_ROWS
    blk_row = jnp.arange(nb, dtype=I32)[:, None] * MOE_ROWS
    block_e = jnp.minimum(jnp.sum((pad_end[None, :] <= blk_row).astype(I32), axis=1), N_EXPERTS - 1)
    n_used = (pad_end[-1] // MOE_ROWS).astype(I32).reshape(1)
    return slot_tiles, block_e, n_used, cap


def kernel(x, c, w_mod, b_mod, w_in, rwkv_mu, rwkv_w0, rwkv_w2, rwkv_a0, rwkv_a2, rwkv_g2, rwkv_k_k, rwkv_k_a, rwkv_r_k, rwkv_ln_g, rwkv_ln_b, dsa_kv_norm, dsa_w_uk, dsa_w_uv, dsa_ik_g, dsa_ik_b, swa_sinks, w_out, ln_mix_g, ln_mix_b, router_w, router_bias, exp_w1, exp_w3, exp_w2, sh_w1, sh_w3, sh_w2, ln_ffn_g, ln_ffn_b):
    bsz, t, d = x.shape
    assert bsz == 1 and d == D_MODEL
    depth = w_mod.shape[0]
    n_sl = SWA_HEADS + DSA_HEADS
    slopes = [2.0 ** (-8.0 * (j + 1.0) / n_sl) for j in range(n_sl)]
    swa_slopes, dsa_slopes = slopes[:SWA_HEADS], slopes[SWA_HEADS:]

    mod = _modulation(c, w_mod, b_mod)
    xs_cur = x[0]
    row1 = lambda v: v.reshape(1, -1)
    for l in range(depth):
        sh1, sc1, g1, sh2, sc2, g2 = [mod[l, :, j * d:(j + 1) * d] for j in range(6)]
        wp = _pad_w_in(w_in[l])
        w_hi = wp.astype(BF16)
        w_idx = wp[:, C_IDX[0]:C_IDX[1]]
        w_idx_lo = (w_idx - w_idx.astype(BF16).astype(F32)).astype(BF16)
        rkv, lora, dq, ckv, iq, ikw, sq, skv = _input_proj(
            xs_cur, sc1, sh1, w_hi, w_idx_lo, row1(dsa_kv_norm[l]),
            _pad_lanes(dsa_ik_g[l]), _pad_lanes(dsa_ik_b[l]))
        o_rw = _rwkv_mix(rkv, lora, row1(rwkv_mu[l]), row1(rwkv_w0[l]), rwkv_w2[l], row1(rwkv_a0[l]),
                         rwkv_a2[l], rwkv_g2[l], row1(rwkv_k_k[l]), row1(rwkv_k_a[l]), row1(rwkv_r_k[l]),
                         row1(rwkv_ln_g[l]), row1(rwkv_ln_b[l]))
        o_ds = _dsa_mix(dq, iq, ikw, ckv, dsa_w_uk[l], dsa_w_uv[l], dsa_slopes)
        o_sw = _swa_mix(sq, skv, _pad_lanes(swa_sinks[l]), swa_slopes)
        x1, h2, eidx, rank, gates, counts = _post_mix(
            xs_cur, o_rw, o_ds, o_sw, w_out[l].astype(BF16), g1, row1(ln_mix_g[l]), row1(ln_mix_b[l]),
            sc2, sh2, router_w[l].T, router_bias[l].reshape(-1, 1))
        slot_tiles, block_e, n_used, cap = _moe_tables(eidx, rank, counts)
        xs_sorted = _dispatch(slot_tiles, h2, cap)
        ys = _experts(block_e, n_used, xs_sorted, exp_w1, exp_w3, exp_w2, l)
        xs_cur = _combine(slot_tiles, ys, x1, h2, gates.T, sh_w1[l].astype(BF16), sh_w3[l].astype(BF16),
                          sh_w2[l].astype(BF16), g2, row1(ln_ffn_g[l]), row1(ln_ffn_b[l]))
    return xs_cur[None]
```

```python
import functools
import math

import jax
import jax.numpy as jnp
import numpy as np
from jax import lax
from jax.experimental import pallas as pl
from jax.experimental.pallas import tpu as pltpu

F32 = jnp.float32
BF16 = jnp.bfloat16
I32 = jnp.int32
HI = lax.Precision.HIGHEST

D_MODEL = 1024
DEPTH = 4
HEAD_DIM = 64
RWKV_HEADS = 6
DSA_HEADS = 4
SWA_HEADS = 6
SWA_KV_HEADS = 2
RWKV_W = RWKV_HEADS * HEAD_DIM
DSA_W = DSA_HEADS * HEAD_DIM
SWA_W = SWA_HEADS * HEAD_DIM
DECAY_LORA = 64
AAA_LORA = 64
GATE_LORA = 128
GN_EPS = 64e-5
KV_LORA = 128
IDX_HEADS = 4
IDX_DIM = 64
TOPK_MAX = 256
WINDOW = 128
N_EXPERTS = 64
TOP_K = 8
N_GROUPS = 8
TOPK_GROUPS = 4
D_EXPERT = 256
ROUTED_SCALE = 2.5
ALPHA = (2 * DEPTH) ** 0.25
LN_EPS = 1e-5
NEG = -1e30
INT_MIN = -(2 ** 31)

LANES = 128
VMEM_LIMIT = 56 * 1024 * 1024

C_RKV = (0, 1152)
C_LORA = (1152, 1408)
C_DQ = (1408, 1664)
C_CKV = (1664, 1792)
C_IDX = (1792, 2176)
C_SQ = (2176, 2560)
C_SKV = (2560, 2816)
P_PAD = 2816
N_ORIG_BEFORE_PAD = 2116


def _dot(a, b, prec=None):
    return jnp.dot(a, b, preferred_element_type=F32, precision=prec)


def _dot_nt(a, b, prec=None):
    return lax.dot_general(a, b, (((1,), (1,)), ((), ())), preferred_element_type=F32, precision=prec)


def _split2(a):
    a_hi = a.astype(BF16)
    return a_hi, (a - a_hi.astype(F32)).astype(BF16)


def _bdot(a, b):
    return _dot(a.astype(BF16), b.astype(BF16))


def _bdot_nt(a, b):
    return _dot_nt(a.astype(BF16), b.astype(BF16))


def _dot2(a, b_exact):
    a_hi, a_lo = _split2(a)
    return _dot(a_hi, b_exact) + _dot(a_lo, b_exact)


def _dot2_l(a_exact, b):
    b_hi, b_lo = _split2(b)
    return _dot(a_exact, b_hi) + _dot(a_exact, b_lo)


def _dot3(a, b):
    a_hi, a_lo = _split2(a)
    b_hi, b_lo = _split2(b)
    return _dot(a_hi, b_hi) + (_dot(a_lo, b_hi) + _dot(a_hi, b_lo))


def _iota(shape, dim):
    return lax.broadcasted_iota(I32, shape, dim)


def _sigmoid(x):
    return 1.0 / (1.0 + jnp.exp(-x))


def _layer_norm_rows(v, g, b):
    mu = jnp.mean(v, axis=-1, keepdims=True)
    d = v - mu
    var = jnp.mean(d * d, axis=-1, keepdims=True)
    return d * lax.rsqrt(var + LN_EPS) * g + b


def _params(sem):
    return pltpu.CompilerParams(dimension_semantics=sem, vmem_limit_bytes=VMEM_LIMIT)


def _mod_kernel(c_ref, w_ref, b_ref, o_ref):
    c = c_ref[...]
    cond = c * _sigmoid(c)
    o_ref[0] = _dot(cond, w_ref[0], HI) + b_ref[0]


def _modulation(c, w_mod, b_mod):
    depth, d, d6 = w_mod.shape
    c8 = jnp.broadcast_to(c, (8, d))
    nj = d6 // d
    out = pl.pallas_call(
        _mod_kernel,
        grid=(depth, nj),
        in_specs=[
            pl.BlockSpec((8, d), lambda l, j: (0, 0)),
            pl.BlockSpec((1, d, d), lambda l, j: (l, 0, j)),
            pl.BlockSpec((1, 1, d), lambda l, j: (l, 0, j)),
        ],
        out_specs=pl.BlockSpec((1, 8, d), lambda l, j: (l, 0, j)),
        out_shape=jax.ShapeDtypeStruct((depth, 8, d6), F32),
        compiler_params=_params(("arbitrary", "arbitrary")),
        name="modulation",
    )(c8, w_mod, b_mod.reshape(depth, 1, d6))
    return out[:, 0:1, :]


def _proj_kernel(x_ref, sc_ref, sh_ref, w_ref, wlo_ref, kvn_ref, ikg_ref, ikb_ref,
                 rkv_ref, lora_ref, dq_ref, ckv_ref, iq_ref, ikw_ref, sq_ref, skv_ref):
    h = x_ref[...] * (1.0 + sc_ref[...]) + sh_ref[...]
    hb = h.astype(BF16)
    hl = (h - hb.astype(F32)).astype(BF16)

    def mm(c):
        return _dot(hb, w_ref[:, c[0]:c[1]])

    rkv_ref[...] = mm(C_RKV)
    lora_ref[...] = mm(C_LORA)
    dq_ref[...] = mm(C_DQ)
    sq_ref[...] = mm(C_SQ)
    skv_ref[...] = mm(C_SKV)
    ckv = mm(C_CKV)
    ckv_ref[...] = ckv * lax.rsqrt(jnp.mean(ckv * ckv, axis=-1, keepdims=True) + 1e-6) * kvn_ref[...]
    idx = mm(C_IDX) + _dot(hl, w_ref[:, C_IDX[0]:C_IDX[1]]) + _dot(hb, wlo_ref[...])
    iq_ref[...] = idx[:, 0:256]
    g3 = idx[:, 256:384]
    lane = _iota(g3.shape, 1)
    isk = lane < IDX_DIM
    mu = jnp.sum(jnp.where(isk, g3, 0.0), axis=-1, keepdims=True) * (1.0 / IDX_DIM)
    dk = jnp.where(isk, g3 - mu, 0.0)
    var = jnp.sum(dk * dk, axis=-1, keepdims=True) * (1.0 / IDX_DIM)
    ikn = dk * lax.rsqrt(var + LN_EPS) * ikg_ref[...] + ikb_ref[...]
    ikw_ref[...] = jnp.where(isk, ikn, g3 * (IDX_HEADS ** -0.5 * IDX_DIM ** -0.5))


def _input_proj(x, sc, sh, w_hi, w_idx_lo, kvn, ikg, ikb, tm=512):
    t, d = x.shape
    widths = [C_RKV, C_LORA, C_DQ, C_CKV, (0, 256), (0, 128), C_SQ, C_SKV]
    widths = [c[1] - c[0] for c in widths]
    const = lambda i: (0, 0)
    row = lambda i: (i, 0)
    return pl.pallas_call(
        _proj_kernel,
        grid=(t // tm,),
        in_specs=[
            pl.BlockSpec((tm, d), row),
            pl.BlockSpec((1, d), const),
            pl.BlockSpec((1, d), const),
            pl.BlockSpec((d, P_PAD), const),
            pl.BlockSpec((d, C_IDX[1] - C_IDX[0]), const),
            pl.BlockSpec((1, KV_LORA), const),
            pl.BlockSpec((1, LANES), const),
            pl.BlockSpec((1, LANES), const),
        ],
        out_specs=[pl.BlockSpec((tm, w), row) for w in widths],
        out_shape=[jax.ShapeDtypeStruct((t, w), F32) for w in widths],
        compiler_params=_params(("arbitrary",)),
        name="input_proj",
    )(x, sc, sh, w_hi, w_idx_lo, kvn, ikg, ikb)


RW_CHUNK = 64
RW_UNROLL = 2


def _rwkv_kernel(r_ref, k_ref, v_ref, lora_ref, rp_ref, kp_ref, vp_ref, lp_ref,
                 mur_ref, muk_ref, muv_ref, mul_ref, w0_ref, w2_ref, a0_ref, a2_ref, g2_ref,
                 kk_ref, ka_ref, rk_ref, lng_ref, lnb_ref, o_ref,
                 h_ref, y_ref, st_ref, wm_ref, ar_ref, rs_ref, vs_ref, lt_ref, zm_ref, y0_ref, gc_ref, *, tg):
    g = pl.program_id(0)
    c64 = RW_CHUNK
    nch = tg // c64
    npair = RWKV_W // LANES
    pair_lanes = [slice(p * LANES, (p + 1) * LANES) for p in range(npair)]
    lane = _iota((1, LANES), 1)
    first = g == 0

    @pl.when(first)
    def _():
        h_ref[...] = jnp.zeros_like(h_ref)

    rowid = _iota((tg, 1), 0)

    def shift_mix(cur_ref, prev_ref, mu_ref):
        cur = cur_ref[...]
        prev_row = jnp.where(first, 0.0, prev_ref[7:8, :])
        rolled = pltpu.roll(cur, 1, 0)
        shifted = jnp.where(rowid == 0, prev_row, rolled)
        return cur + (shifted - cur) * mu_ref[...]

    r = shift_mix(r_ref, rp_ref, mur_ref)
    k = shift_mix(k_ref, kp_ref, muk_ref)
    v = shift_mix(v_ref, vp_ref, muv_ref)
    lo = shift_mix(lora_ref, lp_ref, mul_ref)
    wl = lo[:, 0:DECAY_LORA]
    al = lo[:, DECAY_LORA:DECAY_LORA + AAA_LORA]
    gl = lo[:, 128:256]

    zw = -(w0_ref[...] + _dot3(jnp.tanh(wl), w2_ref[...]))
    softplus = jnp.maximum(zw, 0.0) + jnp.log(1.0 + jnp.exp(-jnp.abs(zw)))
    lw = -jnp.exp(-softplus - 0.5)
    a = _sigmoid(a0_ref[...] + _bdot(al, a2_ref[...]))
    gate = _bdot(_sigmoid(gl), g2_ref[...])

    ri = _iota((LANES, LANES), 0) // HEAD_DIM
    ci = _iota((LANES, LANES), 1) // HEAD_DIM
    bones = jnp.where(ri == ci, 1.0, 0.0).astype(BF16)

    def head_sum(xf):
        return jnp.concatenate([_dot2(xf[:, pl_], bones) for pl_ in pair_lanes], axis=1)

    kk = k * kk_ref[...]
    kk = kk / jnp.maximum(jnp.sqrt(head_sum(kk * kk)), 1e-12)
    k2 = k * (1.0 + (a - 1.0) * ka_ref[...])
    bonus = head_sum(r * k2 * rk_ref[...]) * v
    bvec = a * kk

    st_ref[0] = r
    st_ref[1] = k2
    st_ref[2] = v
    st_ref[3] = lw
    st_ref[4] = kk
    st_ref[5] = bvec

    rr = _iota((LANES, LANES), 0)
    cc = _iota((LANES, LANES), 1)
    same = (rr // c64) == (cc // c64)
    strict = same & ((rr % c64) > (cc % c64))
    incl = same & ((rr % c64) >= (cc % c64))
    eye = jnp.where(rr == cc, 1.0, 0.0)
    tril = jnp.where(_iota((c64, c64), 0) >= _iota((c64, c64), 1), 1.0, 0.0).astype(BF16)
    lo_half = lane < HEAD_DIM

    def stack(xc):
        return jnp.concatenate([jnp.where(lo_half, xc, 0.0), jnp.where(lo_half, 0.0, xc)], axis=0)

    def prepare(c, carry):
        chunks = [c * RW_UNROLL + j for j in range(RW_UNROLL)]
        sls = [pl.ds(pl.multiple_of(cj * c64, c64), c64) for cj in chunks]
        items = [(p, j) for j in range(RW_UNROLL) for p in range(npair)]
        pairs = range(len(items))
        idx = [p * nch + chunks[j] for p, j in items]
        ld = lambda q: [st_ref[q, sls[j], pair_lanes[p]] for p, j in items]
        rc, kc, vc, lwc, kkc, bc = ld(0), ld(1), ld(2), ld(3), ld(4), ld(5)
        cum = [_dot2_l(tril, lwc[p]) for p in pairs]
        tot = [cum[p][c64 - 1:c64, :] for p in pairs]
        g_in = [jnp.exp(cum[p]) for p in pairs]
        g_ex = [jnp.exp(cum[p] - lwc[p]) for p in pairs]
        g_inv = [jnp.exp(-cum[p]) for p in pairs]
        g_rest = [jnp.exp(tot[p] - cum[p]) for p in pairs]
        a_s = [stack(-kkc[p] * g_ex[p]).astype(BF16) for p in pairs]
        b_s = [stack(bc[p] * g_inv[p]).astype(BF16) for p in pairs]
        k_s = [stack(kc[p] * g_inv[p]).astype(BF16) for p in pairs]
        r_s = [stack(rc[p] * g_in[p]).astype(BF16) for p in pairs]
        v_s = [stack(vc[p]).astype(BF16) for p in pairs]
        nmat = [jnp.where(strict, _dot_nt(a_s[p], b_s[p]), 0.0) for p in pairs]
        aak = [jnp.where(strict, _dot_nt(a_s[p], k_s[p]), 0.0) for p in pairs]
        arb = [jnp.where(incl, _dot_nt(r_s[p], b_s[p]), 0.0) for p in pairs]
        ark = [jnp.where(incl, _dot_nt(r_s[p], k_s[p]), 0.0) for p in pairs]
        tinv = [eye + nmat[p] for p in pairs]
        pw = nmat
        for _ in range(5):
            pw = [_bdot(pw[p], pw[p]) for p in pairs]
            tinv = [_bdot(tinv[p], eye + pw[p]) for p in pairs]
        tinv = [tinv[p].astype(BF16) for p in pairs]
        akv = [_bdot(aak[p], v_s[p]).astype(BF16) for p in pairs]
        wmat = [_dot(tinv[p], a_s[p]) for p in pairs]
        zmat = [_dot(tinv[p], akv[p]) for p in pairs]
        y0 = [_bdot(ark[p], v_s[p]) for p in pairs]
        for p in pairs:
            wm_ref[idx[p]] = wmat[p].astype(BF16)
            zm_ref[idx[p]] = zmat[p]
            y0_ref[idx[p]] = y0[p]
            ar_ref[idx[p]] = arb[p].astype(BF16)
            rs_ref[idx[p]] = r_s[p]
            vs_ref[idx[p]] = v_s[p]
            lt_ref[idx[p]] = jnp.concatenate([stack(bc[p] * g_rest[p]), stack(kc[p] * g_rest[p])],
                                             axis=0).T.astype(BF16)
            gc_ref[idx[p]] = jnp.broadcast_to(jnp.sum(eye * jnp.exp(tot[p]), axis=1, keepdims=True),
                                              (LANES, LANES))
        return carry

    lax.fori_loop(0, nch // RW_UNROLL, prepare, 0)

    def advance(c, carry):
        sl = pl.ds(pl.multiple_of(c * c64, c64), c64)
        pairs = range(npair)
        idx = [p * nch + c for p in pairs]
        hst = [h_ref[p] for p in pairs]
        hb = [hst[p].astype(BF16) for p in pairs]
        u = [_dot(wm_ref[idx[p]], hb[p]) + zm_ref[idx[p]] for p in pairs]
        rh = [_dot(rs_ref[idx[p]], hb[p]) for p in pairs]
        ub = [u[p].astype(BF16) for p in pairs]
        hnew = [_dot(lt_ref[idx[p]], jnp.concatenate([ub[p], vs_ref[idx[p]]], axis=0)) for p in pairs]
        au = [_dot(ar_ref[idx[p]], ub[p]) for p in pairs]
        for p in pairs:
            h_ref[p] = gc_ref[idx[p]] * hst[p] + hnew[p]
            ys = rh[p] + au[p] + y0_ref[idx[p]]
            y_ref[sl, pair_lanes[p]] = ys[0:c64, :] + ys[c64:2 * c64, :]
        return carry

    lax.fori_loop(0, nch, advance, 0)

    y = y_ref[...]
    mean = head_sum(y) * (1.0 / HEAD_DIM)
    dy = y - mean
    var = head_sum(dy * dy) * (1.0 / HEAD_DIM)
    o = dy * lax.rsqrt(var + GN_EPS) * lng_ref[...] + lnb_ref[...]
    o_ref[...] = (o + bonus) * gate


def _rwkv_mix(rkv, lora, mu, w0, w2, a0, a2, g2, k_k, k_a, r_k, ln_g, ln_b, tg=512):
    t = rkv.shape[0]
    w = RWKV_W
    npair = w // LANES
    nmat = npair * (tg // RW_CHUNK)
    mu_r, mu_k, mu_v, mu_l = mu[:, 0:w], mu[:, w:2 * w], mu[:, 2 * w:3 * w], mu[:, 3 * w:3 * w + 256]
    blk = lambda off: pl.BlockSpec((tg, w), lambda g: (g, off))
    prev = lambda off: pl.BlockSpec((8, w), lambda g: (jnp.maximum(g * (tg // 8) - 1, 0), off))
    vec = pl.BlockSpec((1, w), lambda g: (0, 0))
    full = lambda rows: pl.BlockSpec((rows, w), lambda g: (0, 0))
    return pl.pallas_call(
        functools.partial(_rwkv_kernel, tg=tg),
        grid=(t // tg,),
        in_specs=[
            blk(0), blk(1), blk(2),
            pl.BlockSpec((tg, 256), lambda g: (g, 0)),
            prev(0), prev(1), prev(2),
            pl.BlockSpec((8, 256), lambda g: (jnp.maximum(g * (tg // 8) - 1, 0), 0)),
            vec, vec, vec,
            pl.BlockSpec((1, 256), lambda g: (0, 0)),
            vec, full(DECAY_LORA), vec, full(AAA_LORA), full(GATE_LORA),
            vec, vec, vec, vec, vec,
        ],
        out_specs=pl.BlockSpec((tg, w), lambda g: (g, 0)),
        out_shape=jax.ShapeDtypeStruct((t, w), F32),
        scratch_shapes=[
            pltpu.VMEM((npair, LANES, LANES), F32),
            pltpu.VMEM((tg, w), F32),
            pltpu.VMEM((6, tg, w), F32),
            pltpu.VMEM((nmat, LANES, LANES), BF16),
            pltpu.VMEM((nmat, LANES, LANES), BF16),
            pltpu.VMEM((nmat, LANES, LANES), BF16),
            pltpu.VMEM((nmat, LANES, LANES), BF16),
            pltpu.VMEM((nmat, LANES, 2 * LANES), BF16),
            pltpu.VMEM((nmat, LANES, LANES), F32),
            pltpu.VMEM((nmat, LANES, LANES), F32),
            pltpu.VMEM((nmat, LANES, LANES), F32),
        ],
        compiler_params=_params(("arbitrary",)),
        name="rwkv7_mix",
    )(rkv, rkv, rkv, lora, rkv, rkv, rkv, lora,
      mu_r, mu_k, mu_v, mu_l, w0, w2, a0, a2, g2, k_k, k_a, r_k, ln_g, ln_b)


DSA_QB = 128
DSA_KC = 1024
DSA_SUB = 512
CNT_ROWS = 64


def _float_key(v):
    bits = lax.bitcast_convert_type(v, I32)
    return bits ^ ((bits >> 31) & 0x7FFFFFFF)


def _dsa_kernel(dq_ref, iq_ref, ikw_ref, ikx_ref, kf_ref, vft_ref, wuk_ref, wuv_ref, tril_ref, slc_ref,
                o_ref, sc_ref, acc_ref):
    i = pl.program_id(0)
    qb, kc, sc_rows = DSA_QB, DSA_KC, DSA_SUB
    nh = DSA_HEADS
    t0 = i * qb
    nch = (t0 + qb + kc - 1) // kc
    tq = t0 + _iota((1, qb), 1)

    iq = iq_ref[...]
    iq_hi = iq.astype(BF16).astype(F32)
    iq_lo = iq - iq_hi
    lhs = []
    for h in range(IDX_HEADS):
        s = slice(h * IDX_DIM, (h + 1) * IDX_DIM)
        lhs.append(jnp.concatenate([iq_hi[:, s], iq_hi[:, s], iq_lo[:, s], iq_lo[:, s]], axis=1))
    lhs_t = jnp.concatenate(lhs, axis=0).T.astype(BF16)
    ikw_t = ikw_ref[...].T
    iw = [ikw_t[IDX_DIM + h:IDX_DIM + h + 1, :] for h in range(IDX_HEADS)]

    def score_body(ch, carry):
        m1, m2 = carry
        for sub in range(kc // sc_rows):
            k0 = pl.multiple_of(ch * kc + sub * sc_rows, sc_rows)
            s_all = _dot(ikx_ref[pl.ds(k0, sc_rows), :], lhs_t)
            acc = jnp.zeros((sc_rows, qb), F32)
            for h in range(IDX_HEADS):
                acc = acc + jnp.maximum(s_all[:, h * qb:(h + 1) * qb], 0.0) * iw[h]
            acc = jnp.where(acc == 0.0, 0.0, acc)
            causal = (k0 + _iota((sc_rows, 1), 0)) <= tq
            sc_ref[pl.ds(k0, sc_rows), :] = jnp.where(causal, _float_key(acc), INT_MIN)
            accm = jnp.where(causal, acc, -jnp.inf)
            for j in range(sc_rows // LANES):
                xj = accm[j * LANES:(j + 1) * LANES, :]
                m2 = jnp.maximum(m2, jnp.minimum(m1, xj))
                m1 = jnp.maximum(m1, xj)
        return m1, m2

    ninf = jnp.full((LANES, qb), -jnp.inf, F32)
    m1, m2 = lax.fori_loop(0, nch, score_body, (ninf, ninf))

    def count_ge(cand):
        def body(ch, acc):
            k0 = pl.multiple_of(ch * kc, kc)
            m = jnp.where(sc_ref[pl.ds(k0, kc), :] >= cand, 1.0, 0.0)
            for j in range(kc // CNT_ROWS):
                acc = acc + m[j * CNT_ROWS:(j + 1) * CNT_ROWS, :]
            return acc
        acc = lax.fori_loop(0, nch, body, jnp.zeros((CNT_ROWS, qb), F32))
        return jnp.sum(acc, axis=0, keepdims=True)

    k_row = jnp.minimum(tq + 1, TOPK_MAX).astype(F32)
    hi0 = _float_key(jnp.max(m1, axis=0, keepdims=True))
    lo0 = jnp.minimum(_float_key(jnp.min(m2, axis=0, keepdims=True)), hi0)
    c_pos = count_ge(jnp.ones((1, qb), I32))
    c_nonneg = count_ge(jnp.zeros((1, qb), I32))
    at_zero = (c_pos < k_row) & (c_nonneg >= k_row)
    above = c_pos >= k_row
    lo0 = jnp.where(at_zero, 0, jnp.where(above, jnp.maximum(lo0, 1), lo0))
    hi0 = jnp.where(at_zero, 0, jnp.where(above, hi0, jnp.minimum(hi0, -1)))
    lo0 = jnp.minimum(lo0, hi0)

    def open_rows(lo, hi):
        return jnp.max(jnp.where(lo < hi, 1.0, 0.0))

    def bis_body(st):
        lo, hi, _ = st
        mid = (lo | hi) - ((lo ^ hi) >> 1)
        c = count_ge(mid)
        ge = c >= k_row
        lo_n = jnp.where(ge, mid, lo)
        hi_n = jnp.where(c == k_row, mid, jnp.where(ge, hi, mid - 1))
        return lo_n, hi_n, open_rows(lo_n, hi_n)

    thr, _, _ = lax.while_loop(lambda st: st[2] > 0.5, bis_body, (lo0, hi0, open_rows(lo0, hi0)))
    n_ge = count_ge(thr)
    has_tie = jnp.max(jnp.where(n_ge > k_row, 1.0, 0.0)) > 0.5

    dq = dq_ref[...]
    slc = slc_ref[...]
    qaug = []
    for h in range(nh):
        ql = _bdot(dq[:, h * HEAD_DIM:(h + 1) * HEAD_DIM], wuk_ref[h]) * HEAD_DIM ** -0.5
        qaug.append(jnp.concatenate([ql, jnp.broadcast_to(slc[h:h + 1, :], (qb, LANES))], axis=1))
    qaug_t = jnp.concatenate(qaug, axis=0).T.astype(BF16)
    acc_ref[...] = jnp.zeros_like(acc_ref)

    nsub = kc // sc_rows

    def sub_starts(ch):
        return [pl.multiple_of(ch * kc + sub * sc_rows, sc_rows) for sub in range(nsub)]

    def logits(k0):
        return _dot(kf_ref[pl.ds(k0, sc_rows), :], qaug_t)

    def attend(k0, lg_all, sel, m_old):
        ps, m_new = [], []
        for h in range(nh):
            cols = slice(h * qb, (h + 1) * qb)
            lg = jnp.where(sel, lg_all[:, cols], NEG)
            mh = jnp.maximum(m_old[:, cols], jnp.max(lg, axis=0, keepdims=True))
            ps.append(jnp.exp((lg - mh).astype(BF16)))
            m_new.append(mh)
        m_new = jnp.concatenate(m_new, axis=1)
        pv = _dot(vft_ref[:, pl.ds(k0, sc_rows)], jnp.concatenate(ps, axis=1))
        acc_ref[...] = jnp.exp(m_old - m_new) * acc_ref[...] + pv
        return m_new

    m_init = jnp.full((1, nh * qb), NEG, F32)

    @pl.when(jnp.logical_not(has_tie))
    def _():
        def body(ch, m_old):
            ks = sub_starts(ch)
            lgs = [logits(k0) for k0 in ks]
            for k0, lg in zip(ks, lgs):
                m_old = attend(k0, lg, sc_ref[pl.ds(k0, sc_rows), :] >= thr, m_old)
            return m_old
        lax.fori_loop(0, nch, body, m_init)

    @pl.when(has_tie)
    def _():
        need = k_row - count_ge(thr + 1)
        tril = tril_ref[...]

        def body(ch, carry):
            tie_run, m_old = carry
            ks = sub_starts(ch)
            lgs = [logits(k0) for k0 in ks]
            keys = [sc_ref[pl.ds(k0, sc_rows), :] for k0 in ks]
            prefs = [_dot(tril, jnp.where(key == thr, 1.0, 0.0).astype(BF16)) for key in keys]
            for k0, lg, key, pref in zip(ks, lgs, keys, prefs):
                sel = (key > thr) | ((key == thr) & (tie_run + pref <= need))
                m_old = attend(k0, lg, sel, m_old)
                tie_run = tie_run + pref[sc_rows - 1:sc_rows, :]
            return tie_run, m_old
        lax.fori_loop(0, nch, body, (jnp.zeros((1, qb), F32), m_init))

    acc = acc_ref[...]
    o_lat = acc[0:KV_LORA, :] / acc[KV_LORA:KV_LORA + 1, :]
    outs = [_bdot(o_lat[:, h * qb:(h + 1) * qb].T, wuv_ref[h]) for h in range(nh)]
    o_ref[...] = jnp.concatenate(outs, axis=1)


DSA_VROWS = KV_LORA + 16


def _dsa_mix(dq, iq, ikw, ckv, w_uk, w_uv, slopes):
    t = dq.shape[0]
    assert t <= LANES * 256
    ikn = ikw[:, 0:IDX_DIM]
    ik_hi, ik_lo = _split2(ikn)
    ikx = jnp.concatenate([ik_hi, ik_lo, ik_hi, ik_lo], axis=1)
    ckv_b = ckv.astype(BF16)
    pos = jnp.arange(t, dtype=I32)
    pa = (pos // LANES).astype(BF16)[:, None]
    pb = (pos % LANES).astype(BF16)[:, None]
    kf = jnp.concatenate([ckv_b, pa, pa, pa, pb, pb, pb, jnp.zeros((t, LANES - 6), BF16)], axis=1)
    vft = jnp.concatenate([ckv_b.T, jnp.ones((1, t), BF16), jnp.zeros((DSA_VROWS - KV_LORA - 1, t), BF16)], axis=0)
    cols = []
    for sl in slopes:
        for coef in (sl * LANES, sl):
            c_hi = jnp.asarray(coef, F32).astype(BF16)
            r1 = jnp.asarray(coef, F32) - c_hi.astype(F32)
            c_mid = r1.astype(BF16)
            c_lo = (r1 - c_mid.astype(F32)).astype(BF16)
            cols += [c_hi.astype(F32), c_mid.astype(F32), c_lo.astype(F32)]
    slc = jnp.stack(cols).reshape(DSA_HEADS, 6)
    slc = jnp.pad(slc, ((0, 8 - DSA_HEADS), (0, LANES - 6)))
    assert t % DSA_KC == 0
    kc = DSA_SUB
    tril = jnp.asarray((np.arange(kc)[:, None] >= np.arange(kc)[None, :]).astype(np.float32), BF16)
    row = lambda i: (i, 0)
    const2 = lambda i: (0, 0)
    const3 = lambda i: (0, 0, 0)
    resident = lambda shape: pl.BlockSpec(shape, const2, pipeline_mode=pl.Buffered(1))
    return pl.pallas_call(
        _dsa_kernel,
        grid=(t // DSA_QB,),
        in_specs=[
            pl.BlockSpec((DSA_QB, DSA_W), row),
            pl.BlockSpec((DSA_QB, IDX_HEADS * IDX_DIM), row),
            pl.BlockSpec((DSA_QB, LANES), row),
            resident((t, 4 * IDX_DIM)),
            resident((t, 2 * LANES)),
            resident((DSA_VROWS, t)),
            pl.BlockSpec((DSA_HEADS, HEAD_DIM, KV_LORA), const3),
            pl.BlockSpec((DSA_HEADS, KV_LORA, HEAD_DIM), const3),
            resident((kc, kc)),
            pl.BlockSpec((8, LANES), const2),
        ],
        out_specs=pl.BlockSpec((DSA_QB, DSA_W), row),
        out_shape=jax.ShapeDtypeStruct((t, DSA_W), F32),
        scratch_shapes=[
            pltpu.VMEM((t, DSA_QB), I32),
            pltpu.VMEM((DSA_VROWS, DSA_HEADS * DSA_QB), F32),
        ],
        compiler_params=_params(("arbitrary",)),
        name="dsa_mix",
    )(dq, iq, ikw, ikx, kf, vft, w_uk, w_uv, tril, slc)


def _swa_kernel(q_ref, kv_ref, kvp_ref, sink_ref, o_ref, *, slopes):
    i = pl.program_id(0)
    w = WINDOW
    gsz = SWA_HEADS // SWA_KV_HEADS
    q = q_ref[...]
    kv = kv_ref[...]
    kvp = kvp_ref[...]
    qi = _iota((w, 2 * w), 0)
    kj = _iota((w, 2 * w), 1)
    dist = qi + w - kj
    valid = (dist >= 0) & (dist < w) & ((kj >= w) | (i > 0))
    distf = dist.astype(F32)
    sinks = sink_ref[...]
    k2, v2 = [], []
    for g in range(SWA_KV_HEADS):
        k2.append(jnp.concatenate([kvp[:, g * HEAD_DIM:(g + 1) * HEAD_DIM],
                                   kv[:, g * HEAD_DIM:(g + 1) * HEAD_DIM]], axis=0).astype(BF16))
        v2.append(jnp.concatenate([kvp[:, w + g * HEAD_DIM:w + (g + 1) * HEAD_DIM],
                                   kv[:, w + g * HEAD_DIM:w + (g + 1) * HEAD_DIM]], axis=0).astype(BF16))
    heads = range(SWA_HEADS)
    s = [_dot_nt(q[:, hd * HEAD_DIM:(hd + 1) * HEAD_DIM].astype(BF16), k2[hd // gsz]) * HEAD_DIM ** -0.5
         for hd in heads]
    s = [jnp.where(valid, s[hd] - slopes[hd] * distf, NEG) for hd in heads]
    sink = [sinks[0:1, hd:hd + 1] for hd in heads]
    m = [jnp.maximum(jnp.max(s[hd], axis=1, keepdims=True), sink[hd]) for hd in heads]
    e = [jnp.exp(s[hd] - m[hd]) for hd in heads]
    p = [e[hd] / (jnp.sum(e[hd], axis=1, keepdims=True) + jnp.exp(sink[hd] - m[hd])) for hd in heads]
    outs = [_dot(p[hd].astype(BF16), v2[hd // gsz]) for hd in heads]
    o_ref[...] = jnp.concatenate(outs, axis=1)


def _swa_mix(sq, skv, sinks, slopes):
    t = sq.shape[0]
    w = WINDOW
    return pl.pallas_call(
        functools.partial(_swa_kernel, slopes=slopes),
        grid=(t // w,),
        in_specs=[
            pl.BlockSpec((w, SWA_W), lambda i: (i, 0)),
            pl.BlockSpec((w, 2 * w), lambda i: (i, 0)),
            pl.BlockSpec((w, 2 * w), lambda i: (jnp.maximum(i - 1, 0), 0)),
            pl.BlockSpec((1, LANES), lambda i: (0, 0)),
        ],
        out_specs=pl.BlockSpec((w, SWA_W), lambda i: (i, 0)),
        out_shape=jax.ShapeDtypeStruct((t, SWA_W), F32),
        compiler_params=_params(("arbitrary",)),
        name="swa_mix",
    )(sq, skv, skv, sinks)


def _post_mix_kernel(x_ref, orw_ref, ods_ref, osw_ref, wout_ref, g1_ref, lng_ref, lnb_ref,
                     sc2_ref, sh2_ref, rwt_ref, rb_ref, tri_ref,
                     x1_ref, h2_ref, eidx_ref, rank_ref, gate_ref, cnt_ref, carry_ref):
    i = pl.program_id(0)

    @pl.when(i == 0)
    def _():
        carry_ref[...] = jnp.zeros_like(carry_ref)

    y = (_dot(orw_ref[...].astype(BF16), wout_ref[0:RWKV_W, :])
         + _dot(ods_ref[...].astype(BF16), wout_ref[RWKV_W:RWKV_W + DSA_W, :])
         + _dot(osw_ref[...].astype(BF16), wout_ref[RWKV_W + DSA_W:D_MODEL, :]))
    x1 = _layer_norm_rows(ALPHA * x_ref[...] + g1_ref[...] * y, lng_ref[...], lnb_ref[...])
    x1_ref[...] = x1
    h2 = x1 * (1.0 + sc2_ref[...]) + sh2_ref[...]
    h2_ref[...] = h2

    tm = h2.shape[0]
    ne = N_EXPERTS
    gs = ne // N_GROUPS
    scores = _sigmoid(_dot_nt(rwt_ref[...], h2, HI))
    sel = scores + rb_ref[...]
    sub = _iota((gs, tm), 0).astype(F32)
    gsc = []
    for j in range(N_GROUPS):
        gj = sel[j * gs:(j + 1) * gs, :]
        m1 = jnp.max(gj, axis=0, keepdims=True)
        f1 = jnp.min(jnp.where(gj == m1, sub, float(gs)), axis=0, keepdims=True)
        m2 = jnp.max(jnp.where(sub == f1, -jnp.inf, gj), axis=0, keepdims=True)
        gsc.append(m1 + m2)
    gsc = jnp.concatenate(gsc, axis=0)
    gid = _iota((N_GROUPS, tm), 0).astype(F32)
    gmask = jnp.zeros((N_GROUPS, tm), F32)
    for _ in range(TOPK_GROUPS):
        mx = jnp.max(gsc, axis=0, keepdims=True)
        fi = jnp.min(jnp.where(gsc == mx, gid, float(N_GROUPS)), axis=0, keepdims=True)
        pick = gid == fi
        gmask = jnp.where(pick, 1.0, gmask)
        gsc = jnp.where(pick, -jnp.inf, gsc)
    selm = jnp.concatenate(
        [jnp.where(gmask[j:j + 1, :] > 0.5, sel[j * gs:(j + 1) * gs, :], NEG) for j in range(N_GROUPS)], axis=0)
    eid = _iota((ne, tm), 0).astype(F32)
    gsel, eids = [], []
    chosen_f = jnp.zeros((ne, tm), F32)
    for _ in range(TOP_K):
        mx = jnp.max(selm, axis=0, keepdims=True)
        fi = jnp.min(jnp.where(selm == mx, eid, float(ne)), axis=0, keepdims=True)
        pick = eid == fi
        eids.append(fi)
        gsel.append(jnp.sum(jnp.where(pick, scores, 0.0), axis=0, keepdims=True))
        chosen_f = jnp.where(pick, 1.0, chosen_f)
        selm = jnp.where(pick, -jnp.inf, selm)
    gsum = gsel[0]
    for kx in range(1, TOP_K):
        gsum = gsum + gsel[kx]
    before = _dot(chosen_f.astype(BF16), tri_ref[...]) + carry_ref[:, 0:1]
    ranks = [jnp.sum(jnp.where(eid == eids[kx], before, 0.0), axis=0, keepdims=True) for kx in range(TOP_K)]
    eidx_ref[...] = jnp.concatenate(eids, axis=0).astype(I32)
    rank_ref[...] = jnp.concatenate(ranks, axis=0).astype(I32)
    gate_ref[...] = jnp.concatenate(gsel, axis=0) / gsum * ROUTED_SCALE
    carry_ref[...] = carry_ref[...] + jnp.sum(chosen_f, axis=1, keepdims=True)
    cnt_ref[...] = carry_ref[...]


def _post_mix(x, o_rw, o_ds, o_sw, w_out, g1, ln_g, ln_b, sc2, sh2, router_wt, router_b, tm=256):
    t, d = x.shape
    tri = (np.arange(tm)[:, None] < np.arange(tm)[None, :]).astype(np.float32)
    tri = jnp.asarray(tri, BF16)
    row = lambda i: (i, 0)
    const = lambda i: (0, 0)
    col = lambda i: (0, i)
    vec = pl.BlockSpec((1, d), const)
    return pl.pallas_call(
        _post_mix_kernel,
        grid=(t // tm,),
        in_specs=[
            pl.BlockSpec((tm, d), row),
            pl.BlockSpec((tm, RWKV_W), row),
            pl.BlockSpec((tm, DSA_W), row),
            pl.BlockSpec((tm, SWA_W), row),
            pl.BlockSpec((d, d), const),
            vec, vec, vec, vec, vec,
            pl.BlockSpec((N_EXPERTS, d), const),
            pl.BlockSpec((N_EXPERTS, 1), const),
            pl.BlockSpec((tm, tm), const),
        ],
        out_specs=[
            pl.BlockSpec((tm, d), row),
            pl.BlockSpec((tm, d), row),
            pl.BlockSpec((TOP_K, tm), col),
            pl.BlockSpec((TOP_K, tm), col),
            pl.BlockSpec((TOP_K, tm), col),
            pl.BlockSpec((N_EXPERTS, LANES), const),
        ],
        out_shape=[
            jax.ShapeDtypeStruct((t, d), F32),
            jax.ShapeDtypeStruct((t, d), F32),
            jax.ShapeDtypeStruct((TOP_K, t), I32),
            jax.ShapeDtypeStruct((TOP_K, t), I32),
            jax.ShapeDtypeStruct((TOP_K, t), F32),
            jax.ShapeDtypeStruct((N_EXPERTS, LANES), F32),
        ],
        scratch_shapes=[pltpu.VMEM((N_EXPERTS, LANES), F32)],
        compiler_params=_params(("arbitrary",)),
        name="post_mix_router",
    )(x, o_rw, o_ds, o_sw, w_out, g1, ln_g, ln_b, sc2, sh2, router_wt, router_b, tri)


MOE_ROWS = 256
MOE_TILE = 256


def _row_copy(src_ref, src_row, dst_ref, dst_row, sem):
    return pltpu.make_async_copy(src_ref.at[pl.ds(src_row, 1), :], dst_ref.at[pl.ds(dst_row, 1), :], sem)


def _dispatch_kernel(slot_hbm, h_ref, xs_in, xs_out, slot_smem, sem_tab, sem_rows):
    del xs_in
    i = pl.program_id(0)
    tab = pltpu.make_async_copy(slot_hbm.at[i], slot_smem, sem_tab)
    tab.start()
    tab.wait()

    def issue(tt, carry):
        for kx in range(TOP_K):
            _row_copy(h_ref, tt, xs_out, slot_smem[kx, tt], sem_rows).start(priority=kx % 2)
        return carry

    lax.fori_loop(0, MOE_TILE, issue, 0)

    def drain(tt, carry):
        for kx in range(TOP_K):
            _row_copy(h_ref, 0, xs_out, 0, sem_rows).wait()
        return carry

    lax.fori_loop(0, MOE_TILE, drain, 0)


def _dispatch(slot_tiles, h2, cap):
    t, d = h2.shape
    xs0 = jnp.zeros((cap, d), F32)
    return pl.pallas_call(
        _dispatch_kernel,
        grid=(t // MOE_TILE,),
        in_specs=[
            pl.BlockSpec(memory_space=pl.ANY),
            pl.BlockSpec((MOE_TILE, d), lambda i: (i, 0)),
            pl.BlockSpec(memory_space=pl.ANY),
        ],
        out_specs=pl.BlockSpec(memory_space=pl.ANY),
        out_shape=jax.ShapeDtypeStruct((cap, d), F32),
        scratch_shapes=[
            pltpu.SMEM((TOP_K, MOE_TILE), I32),
            pltpu.SemaphoreType.DMA,
            pltpu.SemaphoreType.DMA,
        ],
        input_output_aliases={2: 0},
        compiler_params=_params(("arbitrary",)),
        name="moe_dispatch",
    )(slot_tiles, h2, xs0)


def _expert_kernel(be_ref, nb_ref, xs_ref, w1_ref, w3_ref, w2_ref, ys_ref, w1b, w3b, w2b):
    b = pl.program_id(0)
    changed = (b == 0) | (be_ref[b] != be_ref[jnp.maximum(b - 1, 0)])

    @pl.when(changed & (b < nb_ref[0]))
    def _():
        w1b[...] = w1_ref[0, 0].astype(BF16)
        w3b[...] = w3_ref[0, 0].astype(BF16)
        w2b[...] = w2_ref[0, 0].astype(BF16)

    @pl.when(b < nb_ref[0])
    def _():
        xb = xs_ref[...].astype(BF16)
        a = _dot(xb, w1b[...])
        gte = _dot(xb, w3b[...])
        hmid = (a * _sigmoid(a) * gte).astype(BF16)
        ys_ref[...] = _dot(hmid, w2b[...])

    @pl.when(b >= nb_ref[0])
    def _():
        ys_ref[...] = jnp.zeros_like(ys_ref)


def _experts(block_e, n_used, xs, w1, w3, w2, layer):
    cap, d = xs.shape
    nb = cap // MOE_ROWS
    grid_spec = pltpu.PrefetchScalarGridSpec(
        num_scalar_prefetch=2,
        grid=(nb,),
        in_specs=[
            pl.BlockSpec((MOE_ROWS, d), lambda b, be, nu: (b, 0)),
            pl.BlockSpec((1, 1, d, D_EXPERT), lambda b, be, nu: (layer, be[b], 0, 0)),
            pl.BlockSpec((1, 1, d, D_EXPERT), lambda b, be, nu: (layer, be[b], 0, 0)),
            pl.BlockSpec((1, 1, D_EXPERT, d), lambda b, be, nu: (layer, be[b], 0, 0)),
        ],
        out_specs=pl.BlockSpec((MOE_ROWS, d), lambda b, be, nu: (b, 0)),
        scratch_shapes=[
            pltpu.VMEM((d, D_EXPERT), BF16),
            pltpu.VMEM((d, D_EXPERT), BF16),
            pltpu.VMEM((D_EXPERT, d), BF16),
        ],
    )
    return pl.pallas_call(
        _expert_kernel,
        grid_spec=grid_spec,
        out_shape=jax.ShapeDtypeStruct((cap, d), F32),
        compiler_params=_params(("arbitrary",)),
        name="moe_experts",
    )(block_e, n_used, xs, w1, w3, w2)


def _combine_kernel(slot_hbm, ys_hbm, x1_ref, h2_ref, gate_ref, sw1_ref, sw3_ref, sw2_ref,
                    g2_ref, lng_ref, lnb_ref, o_ref, slot_smem, gbuf, sem_tab, sem_rows):
    i = pl.program_id(0)
    tab = pltpu.make_async_copy(slot_hbm.at[i], slot_smem, sem_tab)
    tab.start()
    tab.wait()

    def issue(tt, carry):
        for kx in range(TOP_K):
            _row_copy(ys_hbm, slot_smem[kx, tt], gbuf.at[kx], tt, sem_rows).start(priority=kx % 2)
        return carry

    lax.fori_loop(0, MOE_TILE, issue, 0)

    hb = h2_ref[...].astype(BF16)
    a = _dot(hb, sw1_ref[...])
    gte = _dot(hb, sw3_ref[...])
    y = _dot((a * _sigmoid(a) * gte).astype(BF16), sw2_ref[...])

    def drain(tt, carry):
        for kx in range(TOP_K):
            _row_copy(ys_hbm, 0, gbuf.at[kx], 0, sem_rows).wait()
        return carry

    lax.fori_loop(0, MOE_TILE, drain, 0)

    gates = gate_ref[...]
    for kx in range(TOP_K):
        y = y + gates[:, kx:kx + 1] * gbuf[kx]
    o_ref[...] = _layer_norm_rows(ALPHA * x1_ref[...] + g2_ref[...] * y, lng_ref[...], lnb_ref[...])


def _combine(slot_tiles, ys, x1, h2, gates_t, sw1, sw3, sw2, g2, ln_g, ln_b):
    t, d = x1.shape
    row = lambda i: (i, 0)
    const = lambda i: (0, 0)
    vec = pl.BlockSpec((1, d), const)
    return pl.pallas_call(
        _combine_kernel,
        grid=(t // MOE_TILE,),
        in_specs=[
            pl.BlockSpec(memory_space=pl.ANY),
            pl.BlockSpec(memory_space=pl.ANY),
            pl.BlockSpec((MOE_TILE, d), row),
            pl.BlockSpec((MOE_TILE, d), row),
            pl.BlockSpec((MOE_TILE, TOP_K), row),
            pl.BlockSpec((d, D_EXPERT), const),
            pl.BlockSpec((d, D_EXPERT), const),
            pl.BlockSpec((D_EXPERT, d), const),
            vec, vec, vec,
        ],
        out_specs=pl.BlockSpec((MOE_TILE, d), row),
        out_shape=jax.ShapeDtypeStruct((t, d), F32),
        scratch_shapes=[
            pltpu.SMEM((TOP_K, MOE_TILE), I32),
            pltpu.VMEM((TOP_K, MOE_TILE, d), F32),
            pltpu.SemaphoreType.DMA,
            pltpu.SemaphoreType.DMA,
        ],
        compiler_params=_params(("arbitrary",)),
        name="moe_combine",
    )(slot_tiles, ys, x1, h2, gates_t, sw1, sw3, sw2, g2, ln_g, ln_b)


def _pad_w_in(w_in_l):
    d = w_in_l.shape[0]
    pad = jnp.zeros((d, C_SQ[0] - N_ORIG_BEFORE_PAD), w_in_l.dtype)
    return jnp.concatenate([w_in_l[:, :N_ORIG_BEFORE_PAD], pad, w_in_l[:, N_ORIG_BEFORE_PAD:]], axis=1)


def _pad_lanes(v, width=LANES):
    v = v.reshape(1, -1)
    return jnp.pad(v, ((0, 0), (0, width - v.shape[1])))


def _moe_tables(eidx, rank, counts):
    t = eidx.shape[1]
    cnt = counts[:, 0].astype(I32)
    padded = (cnt + MOE_ROWS - 1) // MOE_ROWS * MOE_ROWS
    pad_end = jnp.cumsum(padded)
    pad_start = pad_end - padded
    e_ids = jnp.arange(N_EXPERTS, dtype=I32)
    start_of = jnp.sum(jnp.where(eidx[..., None] == e_ids, pad_start, 0), axis=-1)
    slot = start_of + rank
    slot_tiles = slot.reshape(TOP_K, t // MOE_TILE, MOE_TILE).transpose(1, 0, 2)
    cap = t * TOP_K + N_EXPERTS * MOE_ROWS
    nb = cap // MOE_ROWS
    blk_row = jnp.arange(nb, dtype=I32)[:, None] * MOE_ROWS
    block_e = jnp.minimum(jnp.sum((pad_end[None, :] <= blk_row).astype(I32), axis=1), N_EXPERTS - 1)
    n_used = (pad_end[-1] // MOE_ROWS).astype(I32).reshape(1)
    return slot_tiles, block_e, n_used, cap


def kernel(x, c, w_mod, b_mod, w_in, rwkv_mu, rwkv_w0, rwkv_w2, rwkv_a0, rwkv_a2, rwkv_g2, rwkv_k_k, rwkv_k_a, rwkv_r_k, rwkv_ln_g, rwkv_ln_b, dsa_kv_norm, dsa_w_uk, dsa_w_uv, dsa_ik_g, dsa_ik_b, swa_sinks, w_out, ln_mix_g, ln_mix_b, router_w, router_bias, exp_w1, exp_w3, exp_w2, sh_w1, sh_w3, sh_w2, ln_ffn_g, ln_ffn_b):
    bsz, t, d = x.shape
    assert bsz == 1 and d == D_MODEL
    depth = w_mod.shape[0]
    n_sl = SWA_HEADS + DSA_HEADS
    slopes = [2.0 ** (-8.0 * (j + 1.0) / n_sl) for j in range(n_sl)]
    swa_slopes, dsa_slopes = slopes[:SWA_HEADS], slopes[SWA_HEADS:]

    mod = _modulation(c, w_mod, b_mod)
    xs_cur = x[0]
    row1 = lambda v: v.reshape(1, -1)
    for l in range(depth):
        sh1, sc1, g1, sh2, sc2, g2 = [mod[l, :, j * d:(j + 1) * d] for j in range(6)]
        wp = _pad_w_in(w_in[l])
        w_hi = wp.astype(BF16)
        w_idx = wp[:, C_IDX[0]:C_IDX[1]]
        w_idx_lo = (w_idx - w_idx.astype(BF16).astype(F32)).astype(BF16)
        rkv, lora, dq, ckv, iq, ikw, sq, skv = _input_proj(
            xs_cur, sc1, sh1, w_hi, w_idx_lo, row1(dsa_kv_norm[l]),
            _pad_lanes(dsa_ik_g[l]), _pad_lanes(dsa_ik_b[l]))
        o_rw = _rwkv_mix(rkv, lora, row1(rwkv_mu[l]), row1(rwkv_w0[l]), rwkv_w2[l], row1(rwkv_a0[l]),
                         rwkv_a2[l], rwkv_g2[l], row1(rwkv_k_k[l]), row1(rwkv_k_a[l]), row1(rwkv_r_k[l]),
                         row1(rwkv_ln_g[l]), row1(rwkv_ln_b[l]))
        o_ds = _dsa_mix(dq, iq, ikw, ckv, dsa_w_uk[l], dsa_w_uv[l], dsa_slopes)
        o_sw = _swa_mix(sq, skv, _pad_lanes(swa_sinks[l]), swa_slopes)
        x1, h2, eidx, rank, gates, counts = _post_mix(
            xs_cur, o_rw, o_ds, o_sw, w_out[l].astype(BF16), g1, row1(ln_mix_g[l]), row1(ln_mix_b[l]),
            sc2, sh2, router_w[l].T, router_bias[l].reshape(-1, 1))
        slot_tiles, block_e, n_used, cap = _moe_tables(eidx, rank, counts)
        xs_sorted = _dispatch(slot_tiles, h2, cap)
        ys = _experts(block_e, n_used, xs_sorted, exp_w1, exp_w3, exp_w2, l)
        xs_cur = _combine(slot_tiles, ys, x1, h2, gates.T, sh_w1[l].astype(BF16), sh_w3[l].astype(BF16),
                          sh_w2[l].astype(BF16), g2, row1(ln_ffn_g[l]), row1(ln_ffn_b[l]))
    return xs_cur[None]
```

```python
import functools
import math

import jax
import jax.numpy as jnp
import numpy as np
from jax import lax
from jax.experimental import pallas as pl
from jax.experimental.pallas import tpu as pltpu

F32 = jnp.float32
BF16 = jnp.bfloat16
I32 = jnp.int32
HI = lax.Precision.HIGHEST

D_MODEL = 1024
DEPTH = 4
HEAD_DIM = 64
RWKV_HEADS = 6
DSA_HEADS = 4
SWA_HEADS = 6
SWA_KV_HEADS = 2
RWKV_W = RWKV_HEADS * HEAD_DIM
DSA_W = DSA_HEADS * HEAD_DIM
SWA_W = SWA_HEADS * HEAD_DIM
DECAY_LORA = 64
AAA_LORA = 64
GATE_LORA = 128
GN_EPS = 64e-5
KV_LORA = 128
IDX_HEADS = 4
IDX_DIM = 64
TOPK_MAX = 256
WINDOW = 128
N_EXPERTS = 64
TOP_K = 8
N_GROUPS = 8
TOPK_GROUPS = 4
D_EXPERT = 256
ROUTED_SCALE = 2.5
ALPHA = (2 * DEPTH) ** 0.25
LN_EPS = 1e-5
NEG = -1e30
INT_MIN = -(2 ** 31)

LANES = 128
VMEM_LIMIT = 56 * 1024 * 1024

C_RKV = (0, 1152)
C_LORA = (1152, 1408)
C_DQ = (1408, 1664)
C_CKV = (1664, 1792)
C_IDX = (1792, 2176)
C_SQ = (2176, 2560)
C_SKV = (2560, 2816)
P_PAD = 2816
N_ORIG_BEFORE_PAD = 2116


def _dot(a, b, prec=None):
    return jnp.dot(a, b, preferred_element_type=F32, precision=prec)


def _dot_nt(a, b, prec=None):
    return lax.dot_general(a, b, (((1,), (1,)), ((), ())), preferred_element_type=F32, precision=prec)


def _split2(a):
    a_hi = a.astype(BF16)
    return a_hi, (a - a_hi.astype(F32)).astype(BF16)


def _bdot(a, b):
    return _dot(a.astype(BF16), b.astype(BF16))


def _bdot_nt(a, b):
    return _dot_nt(a.astype(BF16), b.astype(BF16))


def _dot2(a, b_exact):
    a_hi, a_lo = _split2(a)
    return _dot(a_hi, b_exact) + _dot(a_lo, b_exact)


def _dot2_l(a_exact, b):
    b_hi, b_lo = _split2(b)
    return _dot(a_exact, b_hi) + _dot(a_exact, b_lo)


def _dot3(a, b):
    a_hi, a_lo = _split2(a)
    b_hi, b_lo = _split2(b)
    return _dot(a_hi, b_hi) + (_dot(a_lo, b_hi) + _dot(a_hi, b_lo))


def _iota(shape, dim):
    return lax.broadcasted_iota(I32, shape, dim)


def _sigmoid(x):
    return 1.0 / (1.0 + jnp.exp(-x))


def _layer_norm_rows(v, g, b):
    mu = jnp.mean(v, axis=-1, keepdims=True)
    d = v - mu
    var = jnp.mean(d * d, axis=-1, keepdims=True)
    return d * lax.rsqrt(var + LN_EPS) * g + b


def _params(sem):
    return pltpu.CompilerParams(dimension_semantics=sem, vmem_limit_bytes=VMEM_LIMIT)


def _mod_kernel(c_ref, w_ref, b_ref, o_ref):
    c = c_ref[...]
    cond = c * _sigmoid(c)
    o_ref[0] = _dot(cond, w_ref[0], HI) + b_ref[0]


def _modulation(c, w_mod, b_mod):
    depth, d, d6 = w_mod.shape
    c8 = jnp.broadcast_to(c, (8, d))
    nj = d6 // d
    out = pl.pallas_call(
        _mod_kernel,
        grid=(depth, nj),
        in_specs=[
            pl.BlockSpec((8, d), lambda l, j: (0, 0)),
            pl.BlockSpec((1, d, d), lambda l, j: (l, 0, j)),
            pl.BlockSpec((1, 1, d), lambda l, j: (l, 0, j)),
        ],
        out_specs=pl.BlockSpec((1, 8, d), lambda l, j: (l, 0, j)),
        out_shape=jax.ShapeDtypeStruct((depth, 8, d6), F32),
        compiler_params=_params(("arbitrary", "arbitrary")),
        name="modulation",
    )(c8, w_mod, b_mod.reshape(depth, 1, d6))
    return out[:, 0:1, :]


def _proj_kernel(x_ref, sc_ref, sh_ref, w_ref, wlo_ref, kvn_ref, ikg_ref, ikb_ref,
                 rkv_ref, lora_ref, dq_ref, ckv_ref, iq_ref, ikw_ref, sq_ref, skv_ref):
    h = x_ref[...] * (1.0 + sc_ref[...]) + sh_ref[...]
    hb = h.astype(BF16)
    hl = (h - hb.astype(F32)).astype(BF16)

    def mm(c):
        return _dot(hb, w_ref[:, c[0]:c[1]])

    rkv_ref[...] = mm(C_RKV)
    lora_ref[...] = mm(C_LORA)
    dq_ref[...] = mm(C_DQ)
    sq_ref[...] = mm(C_SQ)
    skv_ref[...] = mm(C_SKV)
    ckv = mm(C_CKV)
    ckv_ref[...] = ckv * lax.rsqrt(jnp.mean(ckv * ckv, axis=-1, keepdims=True) + 1e-6) * kvn_ref[...]
    idx = mm(C_IDX) + _dot(hl, w_ref[:, C_IDX[0]:C_IDX[1]]) + _dot(hb, wlo_ref[...])
    iq_ref[...] = idx[:, 0:256]
    g3 = idx[:, 256:384]
    lane = _iota(g3.shape, 1)
    isk = lane < IDX_DIM
    mu = jnp.sum(jnp.where(isk, g3, 0.0), axis=-1, keepdims=True) * (1.0 / IDX_DIM)
    dk = jnp.where(isk, g3 - mu, 0.0)
    var = jnp.sum(dk * dk, axis=-1, keepdims=True) * (1.0 / IDX_DIM)
    ikn = dk * lax.rsqrt(var + LN_EPS) * ikg_ref[...] + ikb_ref[...]
    ikw_ref[...] = jnp.where(isk, ikn, g3 * (IDX_HEADS ** -0.5 * IDX_DIM ** -0.5))


def _input_proj(x, sc, sh, w_hi, w_idx_lo, kvn, ikg, ikb, tm=512):
    t, d = x.shape
    widths = [C_RKV, C_LORA, C_DQ, C_CKV, (0, 256), (0, 128), C_SQ, C_SKV]
    widths = [c[1] - c[0] for c in widths]
    const = lambda i: (0, 0)
    row = lambda i: (i, 0)
    return pl.pallas_call(
        _proj_kernel,
        grid=(t // tm,),
        in_specs=[
            pl.BlockSpec((tm, d), row),
            pl.BlockSpec((1, d), const),
            pl.BlockSpec((1, d), const),
            pl.BlockSpec((d, P_PAD), const),
            pl.BlockSpec((d, C_IDX[1] - C_IDX[0]), const),
            pl.BlockSpec((1, KV_LORA), const),
            pl.BlockSpec((1, LANES), const),
            pl.BlockSpec((1, LANES), const),
        ],
        out_specs=[pl.BlockSpec((tm, w), row) for w in widths],
        out_shape=[jax.ShapeDtypeStruct((t, w), F32) for w in widths],
        compiler_params=_params(("arbitrary",)),
        name="input_proj",
    )(x, sc, sh, w_hi, w_idx_lo, kvn, ikg, ikb)


RW_CHUNK = 64
RW_UNROLL = 2


def _rwkv_kernel(r_ref, k_ref, v_ref, lora_ref, rp_ref, kp_ref, vp_ref, lp_ref,
                 mur_ref, muk_ref, muv_ref, mul_ref, w0_ref, w2_ref, a0_ref, a2_ref, g2_ref,
                 kk_ref, ka_ref, rk_ref, lng_ref, lnb_ref, o_ref,
                 h_ref, y_ref, st_ref, wm_ref, ar_ref, rs_ref, vs_ref, lt_ref, zm_ref, y0_ref, gc_ref, *, tg):
    g = pl.program_id(0)
    c64 = RW_CHUNK
    nch = tg // c64
    npair = RWKV_W // LANES
    pair_lanes = [slice(p * LANES, (p + 1) * LANES) for p in range(npair)]
    lane = _iota((1, LANES), 1)
    first = g == 0

    @pl.when(first)
    def _():
        h_ref[...] = jnp.zeros_like(h_ref)

    rowid = _iota((tg, 1), 0)

    def shift_mix(cur_ref, prev_ref, mu_ref):
        cur = cur_ref[...]
        prev_row = jnp.where(first, 0.0, prev_ref[7:8, :])
        rolled = pltpu.roll(cur, 1, 0)
        shifted = jnp.where(rowid == 0, prev_row, rolled)
        return cur + (shifted - cur) * mu_ref[...]

    r = shift_mix(r_ref, rp_ref, mur_ref)
    k = shift_mix(k_ref, kp_ref, muk_ref)
    v = shift_mix(v_ref, vp_ref, muv_ref)
    lo = shift_mix(lora_ref, lp_ref, mul_ref)
    wl = lo[:, 0:DECAY_LORA]
    al = lo[:, DECAY_LORA:DECAY_LORA + AAA_LORA]
    gl = lo[:, 128:256]

    zw = -(w0_ref[...] + _dot3(jnp.tanh(wl), w2_ref[...]))
    softplus = jnp.maximum(zw, 0.0) + jnp.log(1.0 + jnp.exp(-jnp.abs(zw)))
    lw = -jnp.exp(-softplus - 0.5)
    a = _sigmoid(a0_ref[...] + _bdot(al, a2_ref[...]))
    gate = _bdot(_sigmoid(gl), g2_ref[...])

    ri = _iota((LANES, LANES), 0) // HEAD_DIM
    ci = _iota((LANES, LANES), 1) // HEAD_DIM
    bones = jnp.where(ri == ci, 1.0, 0.0).astype(BF16)

    def head_sum(xf):
        return jnp.concatenate([_dot2(xf[:, pl_], bones) for pl_ in pair_lanes], axis=1)

    kk = k * kk_ref[...]
    kk = kk / jnp.maximum(jnp.sqrt(head_sum(kk * kk)), 1e-12)
    k2 = k * (1.0 + (a - 1.0) * ka_ref[...])
    bonus = head_sum(r * k2 * rk_ref[...]) * v
    bvec = a * kk

    st_ref[0] = r
    st_ref[1] = k2
    st_ref[2] = v
    st_ref[3] = lw
    st_ref[4] = kk
    st_ref[5] = bvec

    rr = _iota((LANES, LANES), 0)
    cc = _iota((LANES, LANES), 1)
    same = (rr // c64) == (cc // c64)
    strict = same & ((rr % c64) > (cc % c64))
    incl = same & ((rr % c64) >= (cc % c64))
    eye = jnp.where(rr == cc, 1.0, 0.0)
    tril = jnp.where(_iota((c64, c64), 0) >= _iota((c64, c64), 1), 1.0, 0.0).astype(BF16)
    lo_half = lane < HEAD_DIM

    def stack(xc):
        return jnp.concatenate([jnp.where(lo_half, xc, 0.0), jnp.where(lo_half, 0.0, xc)], axis=0)

    def prepare(c, carry):
        chunks = [c * RW_UNROLL + j for j in range(RW_UNROLL)]
        sls = [pl.ds(pl.multiple_of(cj * c64, c64), c64) for cj in chunks]
        items = [(p, j) for j in range(RW_UNROLL) for p in range(npair)]
        pairs = range(len(items))
        idx = [p * nch + chunks[j] for p, j in items]
        ld = lambda q: [st_ref[q, sls[j], pair_lanes[p]] for p, j in items]
        rc, kc, vc, lwc, kkc, bc = ld(0), ld(1), ld(2), ld(3), ld(4), ld(5)
        cum = [_dot2_l(tril, lwc[p]) for p in pairs]
        tot = [cum[p][c64 - 1:c64, :] for p in pairs]
        g_in = [jnp.exp(cum[p]) for p in pairs]
        g_ex = [jnp.exp(cum[p] - lwc[p]) for p in pairs]
        g_inv = [jnp.exp(-cum[p]) for p in pairs]
        g_rest = [jnp.exp(tot[p] - cum[p]) for p in pairs]
        a_s = [stack(-kkc[p] * g_ex[p]).astype(BF16) for p in pairs]
        b_s = [stack(bc[p] * g_inv[p]).astype(BF16) for p in pairs]
        k_s = [stack(kc[p] * g_inv[p]).astype(BF16) for p in pairs]
        r_s = [stack(rc[p] * g_in[p]).astype(BF16) for p in pairs]
        v_s = [stack(vc[p]).astype(BF16) for p in pairs]
        nmat = [jnp.where(strict, _dot_nt(a_s[p], b_s[p]), 0.0) for p in pairs]
        aak = [jnp.where(strict, _dot_nt(a_s[p], k_s[p]), 0.0) for p in pairs]
        arb = [jnp.where(incl, _dot_nt(r_s[p], b_s[p]), 0.0) for p in pairs]
        ark = [jnp.where(incl, _dot_nt(r_s[p], k_s[p]), 0.0) for p in pairs]
        tinv = [eye + nmat[p] for p in pairs]
        pw = nmat
        for _ in range(5):
            pw = [_bdot(pw[p], pw[p]) for p in pairs]
            tinv = [_bdot(tinv[p], eye + pw[p]) for p in pairs]
        tinv = [tinv[p].astype(BF16) for p in pairs]
        akv = [_bdot(aak[p], v_s[p]).astype(BF16) for p in pairs]
        wmat = [_dot(tinv[p], a_s[p]) for p in pairs]
        zmat = [_dot(tinv[p], akv[p]) for p in pairs]
        y0 = [_bdot(ark[p], v_s[p]) for p in pairs]
        for p in pairs:
            wm_ref[idx[p]] = wmat[p].astype(BF16)
            zm_ref[idx[p]] = zmat[p]
            y0_ref[idx[p]] = y0[p]
            ar_ref[idx[p]] = arb[p].astype(BF16)
            rs_ref[idx[p]] = r_s[p]
            vs_ref[idx[p]] = v_s[p]
            lt_ref[idx[p]] = jnp.concatenate([stack(bc[p] * g_rest[p]), stack(kc[p] * g_rest[p])],
                                             axis=0).T.astype(BF16)
            gc_ref[idx[p]] = jnp.broadcast_to(jnp.sum(eye * jnp.exp(tot[p]), axis=1, keepdims=True),
                                              (LANES, LANES))
        return carry

    lax.fori_loop(0, nch // RW_UNROLL, prepare, 0)

    def advance(c, carry):
        sl = pl.ds(pl.multiple_of(c * c64, c64), c64)
        pairs = range(npair)
        idx = [p * nch + c for p in pairs]
        hst = [h_ref[p] for p in pairs]
        hb = [hst[p].astype(BF16) for p in pairs]
        u = [_dot(wm_ref[idx[p]], hb[p]) + zm_ref[idx[p]] for p in pairs]
        rh = [_dot(rs_ref[idx[p]], hb[p]) for p in pairs]
        ub = [u[p].astype(BF16) for p in pairs]
        hnew = [_dot(lt_ref[idx[p]], jnp.concatenate([ub[p], vs_ref[idx[p]]], axis=0)) for p in pairs]
        au = [_dot(ar_ref[idx[p]], ub[p]) for p in pairs]
        for p in pairs:
            h_ref[p] = gc_ref[idx[p]] * hst[p] + hnew[p]
            ys = rh[p] + au[p] + y0_ref[idx[p]]
            y_ref[sl, pair_lanes[p]] = ys[0:c64, :] + ys[c64:2 * c64, :]
        return carry

    lax.fori_loop(0, nch, advance, 0)

    y = y_ref[...]
    mean = head_sum(y) * (1.0 / HEAD_DIM)
    dy = y - mean
    var = head_sum(dy * dy) * (1.0 / HEAD_DIM)
    o = dy * lax.rsqrt(var + GN_EPS) * lng_ref[...] + lnb_ref[...]
    o_ref[...] = (o + bonus) * gate


def _rwkv_mix(rkv, lora, mu, w0, w2, a0, a2, g2, k_k, k_a, r_k, ln_g, ln_b, tg=512):
    t = rkv.shape[0]
    w = RWKV_W
    npair = w // LANES
    nmat = npair * (tg // RW_CHUNK)
    mu_r, mu_k, mu_v, mu_l = mu[:, 0:w], mu[:, w:2 * w], mu[:, 2 * w:3 * w], mu[:, 3 * w:3 * w + 256]
    blk = lambda off: pl.BlockSpec((tg, w), lambda g: (g, off))
    prev = lambda off: pl.BlockSpec((8, w), lambda g: (jnp.maximum(g * (tg // 8) - 1, 0), off))
    vec = pl.BlockSpec((1, w), lambda g: (0, 0))
    full = lambda rows: pl.BlockSpec((rows, w), lambda g: (0, 0))
    return pl.pallas_call(
        functools.partial(_rwkv_kernel, tg=tg),
        grid=(t // tg,),
        in_specs=[
            blk(0), blk(1), blk(2),
            pl.BlockSpec((tg, 256), lambda g: (g, 0)),
            prev(0), prev(1), prev(2),
            pl.BlockSpec((8, 256), lambda g: (jnp.maximum(g * (tg // 8) - 1, 0), 0)),
            vec, vec, vec,
            pl.BlockSpec((1, 256), lambda g: (0, 0)),
            vec, full(DECAY_LORA), vec, full(AAA_LORA), full(GATE_LORA),
            vec, vec, vec, vec, vec,
        ],
        out_specs=pl.BlockSpec((tg, w), lambda g: (g, 0)),
        out_shape=jax.ShapeDtypeStruct((t, w), F32),
        scratch_shapes=[
            pltpu.VMEM((npair, LANES, LANES), F32),
            pltpu.VMEM((tg, w), F32),
            pltpu.VMEM((6, tg, w), F32),
            pltpu.VMEM((nmat, LANES, LANES), BF16),
            pltpu.VMEM((nmat, LANES, LANES), BF16),
            pltpu.VMEM((nmat, LANES, LANES), BF16),
            pltpu.VMEM((nmat, LANES, LANES), BF16),
            pltpu.VMEM((nmat, LANES, 2 * LANES), BF16),
            pltpu.VMEM((nmat, LANES, LANES), F32),
            pltpu.VMEM((nmat, LANES, LANES), F32),
            pltpu.VMEM((nmat, LANES, LANES), F32),
        ],
        compiler_params=_params(("arbitrary",)),
        name="rwkv7_mix",
    )(rkv, rkv, rkv, lora, rkv, rkv, rkv, lora,
      mu_r, mu_k, mu_v, mu_l, w0, w2, a0, a2, g2, k_k, k_a, r_k, ln_g, ln_b)


DSA_QB = 128
DSA_KC = 1024
DSA_SUB = 512
CNT_ROWS = 64
CNT16_ROWS = 128
F32_MIN_NORMAL = 2.0 ** -126
KEY_MIN_NORMAL = 0x00800000


def _float_key(v):
    bits = lax.bitcast_convert_type(v, I32)
    return bits ^ ((bits >> 31) & 0x7FFFFFFF)


def _dsa_kernel(dq_ref, iq_ref, ikw_ref, ikx_ref, kf_ref, vft_ref, wuk_ref, wuv_ref, tril_ref, slc_ref,
                o_ref, sc_ref, acc_ref, s16_ref):
    i = pl.program_id(0)
    qb, kc, sc_rows = DSA_QB, DSA_KC, DSA_SUB
    nh = DSA_HEADS
    t0 = i * qb
    nch = (t0 + qb + kc - 1) // kc
    tq = t0 + _iota((1, qb), 1)

    iq = iq_ref[...]
    iq_hi = iq.astype(BF16).astype(F32)
    iq_lo = iq - iq_hi
    lhs = []
    for h in range(IDX_HEADS):
        s = slice(h * IDX_DIM, (h + 1) * IDX_DIM)
        lhs.append(jnp.concatenate([iq_hi[:, s], iq_hi[:, s], iq_lo[:, s], iq_lo[:, s]], axis=1))
    lhs_t = jnp.concatenate(lhs, axis=0).T.astype(BF16)
    ikw_t = ikw_ref[...].T
    iw = [ikw_t[IDX_DIM + h:IDX_DIM + h + 1, :] for h in range(IDX_HEADS)]

    def score_body(ch, carry):
        m1, m2 = carry
        for sub in range(kc // sc_rows):
            k0 = pl.multiple_of(ch * kc + sub * sc_rows, sc_rows)
            s_all = _dot(ikx_ref[pl.ds(k0, sc_rows), :], lhs_t)
            acc = jnp.zeros((sc_rows, qb), F32)
            for h in range(IDX_HEADS):
                acc = acc + jnp.maximum(s_all[:, h * qb:(h + 1) * qb], 0.0) * iw[h]
            acc = jnp.where(jnp.abs(acc) < F32_MIN_NORMAL, 0.0, acc)
            causal = (k0 + _iota((sc_rows, 1), 0)) <= tq
            sc_ref[pl.ds(k0, sc_rows), :] = jnp.where(causal, _float_key(acc), INT_MIN)
            accm = jnp.where(causal, acc, -jnp.inf)
            trunc = lax.bitcast_convert_type(lax.bitcast_convert_type(accm, I32) & -65536, F32)
            s16_ref[pl.ds(k0, sc_rows), :] = trunc.astype(BF16)
            for j in range(sc_rows // LANES):
                xj = accm[j * LANES:(j + 1) * LANES, :]
                m2 = jnp.maximum(m2, jnp.minimum(m1, xj))
                m1 = jnp.maximum(m1, xj)
        return m1, m2

    ninf = jnp.full((LANES, qb), -jnp.inf, F32)
    m1, m2 = lax.fori_loop(0, nch, score_body, (ninf, ninf))

    def count_ge(cand):
        def body(ch, acc):
            k0 = pl.multiple_of(ch * kc, kc)
            m = jnp.where(sc_ref[pl.ds(k0, kc), :] >= cand, 1.0, 0.0)
            for j in range(kc // CNT_ROWS):
                acc = acc + m[j * CNT_ROWS:(j + 1) * CNT_ROWS, :]
            return acc
        acc = lax.fori_loop(0, nch, body, jnp.zeros((CNT_ROWS, qb), F32))
        return jnp.sum(acc, axis=0, keepdims=True)

    def count16_ge(k16):
        rep = (k16 << 16) | jnp.where(k16 < 0, 0xFFFF, 0)
        v16 = lax.bitcast_convert_type(rep ^ ((rep >> 31) & 0x7FFFFFFF), F32).astype(BF16)
        one, zero = jnp.ones((), BF16), jnp.zeros((), BF16)

        def body(ch, acc):
            k0 = pl.multiple_of(ch * kc, kc)
            m = jnp.where(s16_ref[pl.ds(k0, kc), :] >= v16, one, zero)
            part = m[0:CNT16_ROWS, :]
            for j in range(1, kc // CNT16_ROWS):
                part = part + m[j * CNT16_ROWS:(j + 1) * CNT16_ROWS, :]
            return acc + part.astype(F32)
        acc = lax.fori_loop(0, nch, body, jnp.zeros((CNT16_ROWS, qb), F32))
        return jnp.sum(acc, axis=0, keepdims=True)

    k_row = jnp.minimum(tq + 1, TOPK_MAX).astype(F32)
    hi0 = _float_key(jnp.max(m1, axis=0, keepdims=True))
    lo0 = jnp.minimum(_float_key(jnp.min(m2, axis=0, keepdims=True)), hi0)
    c_pos = count_ge(jnp.full((1, qb), KEY_MIN_NORMAL, I32))
    c_nonneg = count_ge(jnp.zeros((1, qb), I32))
    at_zero = (c_pos < k_row) & (c_nonneg >= k_row)
    above = c_pos >= k_row
    lo0 = jnp.where(at_zero, 0, jnp.where(above, jnp.maximum(lo0, KEY_MIN_NORMAL), lo0))
    hi0 = jnp.where(at_zero, 0, jnp.where(above, hi0, jnp.minimum(hi0, -KEY_MIN_NORMAL - 1)))
    lo0 = jnp.minimum(lo0, hi0)

    def open_rows(lo, hi):
        return jnp.max(jnp.where(lo < hi, 1.0, 0.0))

    def bisect(count_fn, lo, hi, done):
        def body(st):
            lo, hi, done, _ = st
            mid = (lo | hi) - ((lo ^ hi) >> 1)
            c = count_fn(mid)
            ge = c >= k_row
            exact = (c == k_row) & (lo < hi)
            lo_n = jnp.where(ge, mid, lo)
            hi_n = jnp.where(exact, mid, jnp.where(ge, hi, mid - 1))
            return lo_n, hi_n, jnp.where(exact, 1.0, done), open_rows(lo_n, hi_n)
        lo, _, done, _ = lax.while_loop(lambda st: st[3] > 0.5, body, (lo, hi, done, open_rows(lo, hi)))
        return lo, done

    b16, done = bisect(count16_ge, lo0 >> 16, hi0 >> 16, jnp.zeros((1, qb), F32))
    floor16 = b16 << 16
    lo2 = jnp.where(done > 0.5, floor16, jnp.maximum(floor16, lo0))
    hi2 = jnp.where(done > 0.5, floor16, jnp.minimum(floor16 + 0xFFFF, hi0))
    thr, _ = bisect(count_ge, lo2, jnp.maximum(hi2, lo2), done)
    n_ge = count_ge(thr)
    has_tie = jnp.max(jnp.where(n_ge > k_row, 1.0, 0.0)) > 0.5

    dq = dq_ref[...]
    slc = slc_ref[...]
    qaug = []
    for h in range(nh):
        ql = _bdot(dq[:, h * HEAD_DIM:(h + 1) * HEAD_DIM], wuk_ref[h]) * HEAD_DIM ** -0.5
        qaug.append(jnp.concatenate([ql, jnp.broadcast_to(slc[h:h + 1, :], (qb, LANES))], axis=1))
    qaug_t = jnp.concatenate(qaug, axis=0).T.astype(BF16)
    acc_ref[...] = jnp.zeros_like(acc_ref)

    nsub = kc // sc_rows

    def sub_starts(ch):
        return [pl.multiple_of(ch * kc + sub * sc_rows, sc_rows) for sub in range(nsub)]

    def logits(k0):
        return _dot(kf_ref[pl.ds(k0, sc_rows), :], qaug_t)

    def partial_softmax(k0, lg_all, sel):
        ps, ms = [], []
        for h in range(nh):
            lg = jnp.where(sel, lg_all[:, h * qb:(h + 1) * qb], NEG)
            mh = jnp.max(lg, axis=0, keepdims=True)
            ps.append(jnp.exp((lg - mh).astype(BF16)))
            ms.append(mh)
        pv = _dot(vft_ref[:, pl.ds(k0, sc_rows)], jnp.concatenate(ps, axis=1))
        return jnp.concatenate(ms, axis=1), pv

    def merge(m_old, parts):
        m_new = m_old
        for m_loc, _ in parts:
            m_new = jnp.maximum(m_new, m_loc)
        acc = jnp.exp(m_old - m_new) * acc_ref[...]
        for m_loc, pv in parts:
            acc = acc + jnp.exp(m_loc - m_new) * pv
        acc_ref[...] = acc
        return m_new

    m_init = jnp.full((1, nh * qb), NEG, F32)

    @pl.when(jnp.logical_not(has_tie))
    def _():
        def body(ch, m_old):
            ks = sub_starts(ch)
            lgs = [logits(k0) for k0 in ks]
            parts = [partial_softmax(k0, lg, sc_ref[pl.ds(k0, sc_rows), :] >= thr) for k0, lg in zip(ks, lgs)]
            return merge(m_old, parts)
        lax.fori_loop(0, nch, body, m_init)

    @pl.when(has_tie)
    def _():
        need = k_row - count_ge(thr + 1)
        tril = tril_ref[...]

        def body(ch, carry):
            tie_run, m_old = carry
            ks = sub_starts(ch)
            lgs = [logits(k0) for k0 in ks]
            keys = [sc_ref[pl.ds(k0, sc_rows), :] for k0 in ks]
            prefs = [_dot(tril, jnp.where(key == thr, 1.0, 0.0).astype(BF16)) for key in keys]
            parts = []
            for k0, lg, key, pref in zip(ks, lgs, keys, prefs):
                sel = (key > thr) | ((key == thr) & (tie_run + pref <= need))
                parts.append(partial_softmax(k0, lg, sel))
                tie_run = tie_run + pref[sc_rows - 1:sc_rows, :]
            return tie_run, merge(m_old, parts)
        lax.fori_loop(0, nch, body, (jnp.zeros((1, qb), F32), m_init))

    acc = acc_ref[...]
    o_lat = acc[0:KV_LORA, :] / acc[KV_LORA:KV_LORA + 1, :]
    outs = [_bdot(o_lat[:, h * qb:(h + 1) * qb].T, wuv_ref[h]) for h in range(nh)]
    o_ref[...] = jnp.concatenate(outs, axis=1)


DSA_VROWS = KV_LORA + 16


def _dsa_mix(dq, iq, ikw, ckv, w_uk, w_uv, slopes):
    t = dq.shape[0]
    assert t <= LANES * 256
    ikn = ikw[:, 0:IDX_DIM]
    ik_hi, ik_lo = _split2(ikn)
    ikx = jnp.concatenate([ik_hi, ik_lo, ik_hi, ik_lo], axis=1)
    ckv_b = ckv.astype(BF16)
    pos = jnp.arange(t, dtype=I32)
    pa = (pos // LANES).astype(BF16)[:, None]
    pb = (pos % LANES).astype(BF16)[:, None]
    kf = jnp.concatenate([ckv_b, pa, pa, pa, pb, pb, pb, jnp.zeros((t, LANES - 6), BF16)], axis=1)
    vft = jnp.concatenate([ckv_b.T, jnp.ones((1, t), BF16), jnp.zeros((DSA_VROWS - KV_LORA - 1, t), BF16)], axis=0)
    cols = []
    for sl in slopes:
        for coef in (sl * LANES, sl):
            c_hi = jnp.asarray(coef, F32).astype(BF16)
            r1 = jnp.asarray(coef, F32) - c_hi.astype(F32)
            c_mid = r1.astype(BF16)
            c_lo = (r1 - c_mid.astype(F32)).astype(BF16)
            cols += [c_hi.astype(F32), c_mid.astype(F32), c_lo.astype(F32)]
    slc = jnp.stack(cols).reshape(DSA_HEADS, 6)
    slc = jnp.pad(slc, ((0, 8 - DSA_HEADS), (0, LANES - 6)))
    assert t % DSA_KC == 0
    kc = DSA_SUB
    tril = jnp.asarray((np.arange(kc)[:, None] >= np.arange(kc)[None, :]).astype(np.float32), BF16)
    row = lambda i: (i, 0)
    const2 = lambda i: (0, 0)
    const3 = lambda i: (0, 0, 0)
    resident = lambda shape: pl.BlockSpec(shape, const2, pipeline_mode=pl.Buffered(1))
    return pl.pallas_call(
        _dsa_kernel,
        grid=(t // DSA_QB,),
        in_specs=[
            pl.BlockSpec((DSA_QB, DSA_W), row),
            pl.BlockSpec((DSA_QB, IDX_HEADS * IDX_DIM), row),
            pl.BlockSpec((DSA_QB, LANES), row),
            resident((t, 4 * IDX_DIM)),
            resident((t, 2 * LANES)),
            resident((DSA_VROWS, t)),
            pl.BlockSpec((DSA_HEADS, HEAD_DIM, KV_LORA), const3),
            pl.BlockSpec((DSA_HEADS, KV_LORA, HEAD_DIM), const3),
            resident((kc, kc)),
            pl.BlockSpec((8, LANES), const2),
        ],
        out_specs=pl.BlockSpec((DSA_QB, DSA_W), row),
        out_shape=jax.ShapeDtypeStruct((t, DSA_W), F32),
        scratch_shapes=[
            pltpu.VMEM((t, DSA_QB), I32),
            pltpu.VMEM((DSA_VROWS, DSA_HEADS * DSA_QB), F32),
            pltpu.VMEM((t, DSA_QB), BF16),
        ],
        compiler_params=_params(("arbitrary",)),
        name="dsa_mix",
    )(dq, iq, ikw, ikx, kf, vft, w_uk, w_uv, tril, slc)


def _swa_kernel(q_ref, kv_ref, kvp_ref, sink_ref, o_ref, *, slopes):
    i = pl.program_id(0)
    w = WINDOW
    gsz = SWA_HEADS // SWA_KV_HEADS
    q = q_ref[...]
    kv = kv_ref[...]
    kvp = kvp_ref[...]
    qi = _iota((w, 2 * w), 0)
    kj = _iota((w, 2 * w), 1)
    dist = qi + w - kj
    valid = (dist >= 0) & (dist < w) & ((kj >= w) | (i > 0))
    distf = dist.astype(F32)
    sinks = sink_ref[...]
    k2, v2 = [], []
    for g in range(SWA_KV_HEADS):
        k2.append(jnp.concatenate([kvp[:, g * HEAD_DIM:(g + 1) * HEAD_DIM],
                                   kv[:, g * HEAD_DIM:(g + 1) * HEAD_DIM]], axis=0).astype(BF16))
        v2.append(jnp.concatenate([kvp[:, w + g * HEAD_DIM:w + (g + 1) * HEAD_DIM],
                                   kv[:, w + g * HEAD_DIM:w + (g + 1) * HEAD_DIM]], axis=0).astype(BF16))
    heads = range(SWA_HEADS)
    s = [_dot_nt(q[:, hd * HEAD_DIM:(hd + 1) * HEAD_DIM].astype(BF16), k2[hd // gsz]) * HEAD_DIM ** -0.5
         for hd in heads]
    s = [jnp.where(valid, s[hd] - slopes[hd] * distf, NEG) for hd in heads]
    sink = [sinks[0:1, hd:hd + 1] for hd in heads]
    m = [jnp.maximum(jnp.max(s[hd], axis=1, keepdims=True), sink[hd]) for hd in heads]
    e = [jnp.exp(s[hd] - m[hd]) for hd in heads]
    p = [e[hd] / (jnp.sum(e[hd], axis=1, keepdims=True) + jnp.exp(sink[hd] - m[hd])) for hd in heads]
    outs = [_dot(p[hd].astype(BF16), v2[hd // gsz]) for hd in heads]
    o_ref[...] = jnp.concatenate(outs, axis=1)


def _swa_mix(sq, skv, sinks, slopes):
    t = sq.shape[0]
    w = WINDOW
    return pl.pallas_call(
        functools.partial(_swa_kernel, slopes=slopes),
        grid=(t // w,),
        in_specs=[
            pl.BlockSpec((w, SWA_W), lambda i: (i, 0)),
            pl.BlockSpec((w, 2 * w), lambda i: (i, 0)),
            pl.BlockSpec((w, 2 * w), lambda i: (jnp.maximum(i - 1, 0), 0)),
            pl.BlockSpec((1, LANES), lambda i: (0, 0)),
        ],
        out_specs=pl.BlockSpec((w, SWA_W), lambda i: (i, 0)),
        out_shape=jax.ShapeDtypeStruct((t, SWA_W), F32),
        compiler_params=_params(("arbitrary",)),
        name="swa_mix",
    )(sq, skv, skv, sinks)


def _post_mix_kernel(x_ref, orw_ref, ods_ref, osw_ref, wout_ref, g1_ref, lng_ref, lnb_ref,
                     sc2_ref, sh2_ref, rwt_ref, rb_ref, tri_ref,
                     x1_ref, h2_ref, eidx_ref, rank_ref, gate_ref, cnt_ref, carry_ref):
    i = pl.program_id(0)

    @pl.when(i == 0)
    def _():
        carry_ref[...] = jnp.zeros_like(carry_ref)

    y = (_dot(orw_ref[...].astype(BF16), wout_ref[0:RWKV_W, :])
         + _dot(ods_ref[...].astype(BF16), wout_ref[RWKV_W:RWKV_W + DSA_W, :])
         + _dot(osw_ref[...].astype(BF16), wout_ref[RWKV_W + DSA_W:D_MODEL, :]))
    x1 = _layer_norm_rows(ALPHA * x_ref[...] + g1_ref[...] * y, lng_ref[...], lnb_ref[...])
    x1_ref[...] = x1
    h2 = x1 * (1.0 + sc2_ref[...]) + sh2_ref[...]
    h2_ref[...] = h2

    tm = h2.shape[0]
    ne = N_EXPERTS
    gs = ne // N_GROUPS
    scores = _sigmoid(_dot_nt(rwt_ref[...], h2, HI))
    sel = scores + rb_ref[...]
    sub = _iota((gs, tm), 0).astype(F32)
    gsc = []
    for j in range(N_GROUPS):
        gj = sel[j * gs:(j + 1) * gs, :]
        m1 = jnp.max(gj, axis=0, keepdims=True)
        f1 = jnp.min(jnp.where(gj == m1, sub, float(gs)), axis=0, keepdims=True)
        m2 = jnp.max(jnp.where(sub == f1, -jnp.inf, gj), axis=0, keepdims=True)
        gsc.append(m1 + m2)
    gsc = jnp.concatenate(gsc, axis=0)
    gid = _iota((N_GROUPS, tm), 0).astype(F32)
    gmask = jnp.zeros((N_GROUPS, tm), F32)
    for _ in range(TOPK_GROUPS):
        mx = jnp.max(gsc, axis=0, keepdims=True)
        fi = jnp.min(jnp.where(gsc == mx, gid, float(N_GROUPS)), axis=0, keepdims=True)
        pick = gid == fi
        gmask = jnp.where(pick, 1.0, gmask)
        gsc = jnp.where(pick, -jnp.inf, gsc)
    selm = jnp.concatenate(
        [jnp.where(gmask[j:j + 1, :] > 0.5, sel[j * gs:(j + 1) * gs, :], NEG) for j in range(N_GROUPS)], axis=0)
    eid = _iota((ne, tm), 0).astype(F32)
    gsel, eids = [], []
    chosen_f = jnp.zeros((ne, tm), F32)
    for _ in range(TOP_K):
        mx = jnp.max(selm, axis=0, keepdims=True)
        fi = jnp.min(jnp.where(selm == mx, eid, float(ne)), axis=0, keepdims=True)
        pick = eid == fi
        eids.append(fi)
        gsel.append(jnp.sum(jnp.where(pick, scores, 0.0), axis=0, keepdims=True))
        chosen_f = jnp.where(pick, 1.0, chosen_f)
        selm = jnp.where(pick, -jnp.inf, selm)
    gsum = gsel[0]
    for kx in range(1, TOP_K):
        gsum = gsum + gsel[kx]
    before = _dot(chosen_f.astype(BF16), tri_ref[...]) + carry_ref[:, 0:1]
    ranks = [jnp.sum(jnp.where(eid == eids[kx], before, 0.0), axis=0, keepdims=True) for kx in range(TOP_K)]
    eidx_ref[...] = jnp.concatenate(eids, axis=0).astype(I32)
    rank_ref[...] = jnp.concatenate(ranks, axis=0).astype(I32)
    gate_ref[...] = jnp.concatenate(gsel, axis=0) / gsum * ROUTED_SCALE
    carry_ref[...] = carry_ref[...] + jnp.sum(chosen_f, axis=1, keepdims=True)
    cnt_ref[...] = carry_ref[...]


def _post_mix(x, o_rw, o_ds, o_sw, w_out, g1, ln_g, ln_b, sc2, sh2, router_wt, router_b, tm=256):
    t, d = x.shape
    tri = (np.arange(tm)[:, None] < np.arange(tm)[None, :]).astype(np.float32)
    tri = jnp.asarray(tri, BF16)
    row = lambda i: (i, 0)
    const = lambda i: (0, 0)
    col = lambda i: (0, i)
    vec = pl.BlockSpec((1, d), const)
    return pl.pallas_call(
        _post_mix_kernel,
        grid=(t // tm,),
        in_specs=[
            pl.BlockSpec((tm, d), row),
            pl.BlockSpec((tm, RWKV_W), row),
            pl.BlockSpec((tm, DSA_W), row),
            pl.BlockSpec((tm, SWA_W), row),
            pl.BlockSpec((d, d), const),
            vec, vec, vec, vec, vec,
            pl.BlockSpec((N_EXPERTS, d), const),
            pl.BlockSpec((N_EXPERTS, 1), const),
            pl.BlockSpec((tm, tm), const),
        ],
        out_specs=[
            pl.BlockSpec((tm, d), row),
            pl.BlockSpec((tm, d), row),
            pl.BlockSpec((TOP_K, tm), col),
            pl.BlockSpec((TOP_K, tm), col),
            pl.BlockSpec((TOP_K, tm), col),
            pl.BlockSpec((N_EXPERTS, LANES), const),
        ],
        out_shape=[
            jax.ShapeDtypeStruct((t, d), F32),
            jax.ShapeDtypeStruct((t, d), F32),
            jax.ShapeDtypeStruct((TOP_K, t), I32),
            jax.ShapeDtypeStruct((TOP_K, t), I32),
            jax.ShapeDtypeStruct((TOP_K, t), F32),
            jax.ShapeDtypeStruct((N_EXPERTS, LANES), F32),
        ],
        scratch_shapes=[pltpu.VMEM((N_EXPERTS, LANES), F32)],
        compiler_params=_params(("arbitrary",)),
        name="post_mix_router",
    )(x, o_rw, o_ds, o_sw, w_out, g1, ln_g, ln_b, sc2, sh2, router_wt, router_b, tri)


MOE_ROWS = 512
MOE_TILE = 256


def _row_copy(src_ref, src_row, dst_ref, dst_row, sem):
    return pltpu.make_async_copy(src_ref.at[pl.ds(src_row, 1), :], dst_ref.at[pl.ds(dst_row, 1), :], sem)


def _dispatch_kernel(slot_hbm, h_ref, xs_in, xs_out, slot_smem, sem_tab, sem_rows):
    del xs_in
    i = pl.program_id(0)
    tab = pltpu.make_async_copy(slot_hbm.at[i], slot_smem, sem_tab)
    tab.start()
    tab.wait()

    def issue(tt, carry):
        for kx in range(TOP_K):
            _row_copy(h_ref, tt, xs_out, slot_smem[kx, tt], sem_rows).start(priority=kx % 2)
        return carry

    lax.fori_loop(0, MOE_TILE, issue, 0)

    def drain(tt, carry):
        for kx in range(TOP_K):
            _row_copy(h_ref, 0, xs_out, 0, sem_rows).wait()
        return carry

    lax.fori_loop(0, MOE_TILE, drain, 0)


def _dispatch(slot_tiles, h2, cap):
    t, d = h2.shape
    xs0 = jnp.zeros((cap, d), F32)
    return pl.pallas_call(
        _dispatch_kernel,
        grid=(t // MOE_TILE,),
        in_specs=[
            pl.BlockSpec(memory_space=pl.ANY),
            pl.BlockSpec((MOE_TILE, d), lambda i: (i, 0)),
            pl.BlockSpec(memory_space=pl.ANY),
        ],
        out_specs=pl.BlockSpec(memory_space=pl.ANY),
        out_shape=jax.ShapeDtypeStruct((cap, d), F32),
        scratch_shapes=[
            pltpu.SMEM((TOP_K, MOE_TILE), I32),
            pltpu.SemaphoreType.DMA,
            pltpu.SemaphoreType.DMA,
        ],
        input_output_aliases={2: 0},
        compiler_params=_params(("arbitrary",)),
        name="moe_dispatch",
    )(slot_tiles, h2, xs0)


def _expert_kernel(be_ref, nb_ref, xs_ref, w1_ref, w3_ref, w2_ref, ys_ref, w1b, w3b, w2b):
    b = pl.program_id(0)
    changed = (b == 0) | (be_ref[b] != be_ref[jnp.maximum(b - 1, 0)])

    @pl.when(changed & (b < nb_ref[0]))
    def _():
        w1b[...] = w1_ref[0, 0].astype(BF16)
        w3b[...] = w3_ref[0, 0].astype(BF16)
        w2b[...] = w2_ref[0, 0].astype(BF16)

    @pl.when(b < nb_ref[0])
    def _():
        xb = xs_ref[...].astype(BF16)
        a = _dot(xb, w1b[...])
        gte = _dot(xb, w3b[...])
        hmid = (a * _sigmoid(a) * gte).astype(BF16)
        ys_ref[...] = _dot(hmid, w2b[...])

    @pl.when(b >= nb_ref[0])
    def _():
        ys_ref[...] = jnp.zeros_like(ys_ref)


def _experts(block_e, n_used, xs, w1, w3, w2, layer):
    cap, d = xs.shape
    nb = cap // MOE_ROWS
    grid_spec = pltpu.PrefetchScalarGridSpec(
        num_scalar_prefetch=2,
        grid=(nb,),
        in_specs=[
            pl.BlockSpec((MOE_ROWS, d), lambda b, be, nu: (b, 0)),
            pl.BlockSpec((1, 1, d, D_EXPERT), lambda b, be, nu: (layer, be[b], 0, 0)),
            pl.BlockSpec((1, 1, d, D_EXPERT), lambda b, be, nu: (layer, be[b], 0, 0)),
            pl.BlockSpec((1, 1, D_EXPERT, d), lambda b, be, nu: (layer, be[b], 0, 0)),
        ],
        out_specs=pl.BlockSpec((MOE_ROWS, d), lambda b, be, nu: (b, 0)),
        scratch_shapes=[
            pltpu.VMEM((d, D_EXPERT), BF16),
            pltpu.VMEM((d, D_EXPERT), BF16),
            pltpu.VMEM((D_EXPERT, d), BF16),
        ],
    )
    return pl.pallas_call(
        _expert_kernel,
        grid_spec=grid_spec,
        out_shape=jax.ShapeDtypeStruct((cap, d), F32),
        compiler_params=_params(("arbitrary",)),
        name="moe_experts",
    )(block_e, n_used, xs, w1, w3, w2)


def _combine_kernel(slot_hbm, ys_hbm, x1_ref, h2_ref, gate_ref, sw1_ref, sw3_ref, sw2_ref,
                    g2_ref, lng_ref, lnb_ref, o_ref, slot_smem, gbuf, sem_tab, sem_rows):
    i = pl.program_id(0)
    tab = pltpu.make_async_copy(slot_hbm.at[i], slot_smem, sem_tab)
    tab.start()
    tab.wait()

    def issue(tt, carry):
        for kx in range(TOP_K):
            _row_copy(ys_hbm, slot_smem[kx, tt], gbuf.at[kx], tt, sem_rows).start(priority=kx % 2)
        return carry

    lax.fori_loop(0, MOE_TILE, issue, 0)

    hb = h2_ref[...].astype(BF16)
    a = _dot(hb, sw1_ref[...])
    gte = _dot(hb, sw3_ref[...])
    y = _dot((a * _sigmoid(a) * gte).astype(BF16), sw2_ref[...])

    def drain(tt, carry):
        for kx in range(TOP_K):
            _row_copy(ys_hbm, 0, gbuf.at[kx], 0, sem_rows).wait()
        return carry

    lax.fori_loop(0, MOE_TILE, drain, 0)

    gates = gate_ref[...]
    for kx in range(TOP_K):
        y = y + gates[:, kx:kx + 1] * gbuf[kx]
    o_ref[...] = _layer_norm_rows(ALPHA * x1_ref[...] + g2_ref[...] * y, lng_ref[...], lnb_ref[...])


def _combine(slot_tiles, ys, x1, h2, gates_t, sw1, sw3, sw2, g2, ln_g, ln_b):
    t, d = x1.shape
    row = lambda i: (i, 0)
    const = lambda i: (0, 0)
    vec = pl.BlockSpec((1, d), const)
    return pl.pallas_call(
        _combine_kernel,
        grid=(t // MOE_TILE,),
        in_specs=[
            pl.BlockSpec(memory_space=pl.ANY),
            pl.BlockSpec(memory_space=pl.ANY),
            pl.BlockSpec((MOE_TILE, d), row),
            pl.BlockSpec((MOE_TILE, d), row),
            pl.BlockSpec((MOE_TILE, TOP_K), row),
            pl.BlockSpec((d, D_EXPERT), const),
            pl.BlockSpec((d, D_EXPERT), const),
            pl.BlockSpec((D_EXPERT, d), const),
            vec, vec, vec,
        ],
        out_specs=pl.BlockSpec((MOE_TILE, d), row),
        out_shape=jax.ShapeDtypeStruct((t, d), F32),
        scratch_shapes=[
            pltpu.SMEM((TOP_K, MOE_TILE), I32),
            pltpu.VMEM((TOP_K, MOE_TILE, d), F32),
            pltpu.SemaphoreType.DMA,
            pltpu.SemaphoreType.DMA,
        ],
        compiler_params=_params(("arbitrary",)),
        name="moe_combine",
    )(slot_tiles, ys, x1, h2, gates_t, sw1, sw3, sw2, g2, ln_g, ln_b)


def _pad_w_in(w_in_l):
    d = w_in_l.shape[0]
    pad = jnp.zeros((d, C_SQ[0] - N_ORIG_BEFORE_PAD), w_in_l.dtype)
    return jnp.concatenate([w_in_l[:, :N_ORIG_BEFORE_PAD], pad, w_in_l[:, N_ORIG_BEFORE_PAD:]], axis=1)


def _pad_lanes(v, width=LANES):
    v = v.reshape(1, -1)
    return jnp.pad(v, ((0, 0), (0, width - v.shape[1])))


def _moe_tables(eidx, rank, counts):
    t = eidx.shape[1]
    cnt = counts[:, 0].astype(I32)
    padded = (cnt + MOE_ROWS - 1) // MOE_ROWS * MOE_ROWS
    pad_end = jnp.cumsum(padded)
    pad_start = pad_end - padded
    e_ids = jnp.arange(N_EXPERTS, dtype=I32)
    start_of = jnp.sum(jnp.where(eidx[..., None] == e_ids, pad_start, 0), axis=-1)
    slot = start_of + rank
    slot_tiles = slot.reshape(TOP_K, t // MOE_TILE, MOE_TILE).transpose(1, 0, 2)
    cap = t * TOP_K + N_EXPERTS * MOE_ROWS
    nb = cap // MOE_ROWS
    blk_row = jnp.arange(nb, dtype=I32)[:, None] * MOE_ROWS
    block_e = jnp.minimum(jnp.sum((pad_end[None, :] <= blk_row).astype(I32), axis=1), N_EXPERTS - 1)
    n_used = (pad_end[-1] // MOE_ROWS).astype(I32).reshape(1)
    return slot_tiles, block_e, n_used, cap


def kernel(x, c, w_mod, b_mod, w_in, rwkv_mu, rwkv_w0, rwkv_w2, rwkv_a0, rwkv_a2, rwkv_g2, rwkv_k_k, rwkv_k_a, rwkv_r_k, rwkv_ln_g, rwkv_ln_b, dsa_kv_norm, dsa_w_uk, dsa_w_uv, dsa_ik_g, dsa_ik_b, swa_sinks, w_out, ln_mix_g, ln_mix_b, router_w, router_bias, exp_w1, exp_w3, exp_w2, sh_w1, sh_w3, sh_w2, ln_ffn_g, ln_ffn_b):
    bsz, t, d = x.shape
    assert bsz == 1 and d == D_MODEL
    depth = w_mod.shape[0]
    n_sl = SWA_HEADS + DSA_HEADS
    slopes = [2.0 ** (-8.0 * (j + 1.0) / n_sl) for j in range(n_sl)]
    swa_slopes, dsa_slopes = slopes[:SWA_HEADS], slopes[SWA_HEADS:]

    mod = _modulation(c, w_mod, b_mod)
    xs_cur = x[0]
    row1 = lambda v: v.reshape(1, -1)
    for l in range(depth):
        sh1, sc1, g1, sh2, sc2, g2 = [mod[l, :, j * d:(j + 1) * d] for j in range(6)]
        wp = _pad_w_in(w_in[l])
        w_hi = wp.astype(BF16)
        w_idx = wp[:, C_IDX[0]:C_IDX[1]]
        w_idx_lo = (w_idx - w_idx.astype(BF16).astype(F32)).astype(BF16)
        rkv, lora, dq, ckv, iq, ikw, sq, skv = _input_proj(
            xs_cur, sc1, sh1, w_hi, w_idx_lo, row1(dsa_kv_norm[l]),
            _pad_lanes(dsa_ik_g[l]), _pad_lanes(dsa_ik_b[l]))
        o_rw = _rwkv_mix(rkv, lora, row1(rwkv_mu[l]), row1(rwkv_w0[l]), rwkv_w2[l], row1(rwkv_a0[l]),
                         rwkv_a2[l], rwkv_g2[l], row1(rwkv_k_k[l]), row1(rwkv_k_a[l]), row1(rwkv_r_k[l]),
                         row1(rwkv_ln_g[l]), row1(rwkv_ln_b[l]))
        o_ds = _dsa_mix(dq, iq, ikw, ckv, dsa_w_uk[l], dsa_w_uv[l], dsa_slopes)
        o_sw = _swa_mix(sq, skv, _pad_lanes(swa_sinks[l]), swa_slopes)
        x1, h2, eidx, rank, gates, counts = _post_mix(
            xs_cur, o_rw, o_ds, o_sw, w_out[l].astype(BF16), g1, row1(ln_mix_g[l]), row1(ln_mix_b[l]),
            sc2, sh2, router_w[l].T, router_bias[l].reshape(-1, 1))
        slot_tiles, block_e, n_used, cap = _moe_tables(eidx, rank, counts)
        xs_sorted = _dispatch(slot_tiles, h2, cap)
        ys = _experts(block_e, n_used, xs_sorted, exp_w1, exp_w3, exp_w2, l)
        xs_cur = _combine(slot_tiles, ys, x1, h2, gates.T, sh_w1[l].astype(BF16), sh_w3[l].astype(BF16),
                          sh_w2[l].astype(BF16), g2, row1(ln_ffn_g[l]), row1(ln_ffn_b[l]))
    return xs_cur[None]
```

```python
import functools
import math

import jax
import jax.numpy as jnp
import numpy as np
from jax import lax
from jax.experimental import pallas as pl
from jax.experimental.pallas import tpu as pltpu

F32 = jnp.float32
BF16 = jnp.bfloat16
I32 = jnp.int32
HI = lax.Precision.HIGHEST

D_MODEL = 1024
DEPTH = 4
HEAD_DIM = 64
RWKV_HEADS = 6
DSA_HEADS = 4
SWA_HEADS = 6
SWA_KV_HEADS = 2
RWKV_W = RWKV_HEADS * HEAD_DIM
DSA_W = DSA_HEADS * HEAD_DIM
SWA_W = SWA_HEADS * HEAD_DIM
DECAY_LORA = 64
AAA_LORA = 64
GATE_LORA = 128
GN_EPS = 64e-5
KV_LORA = 128
IDX_HEADS = 4
IDX_DIM = 64
TOPK_MAX = 256
WINDOW = 128
N_EXPERTS = 64
TOP_K = 8
N_GROUPS = 8
TOPK_GROUPS = 4
D_EXPERT = 256
ROUTED_SCALE = 2.5
ALPHA = (2 * DEPTH) ** 0.25
LN_EPS = 1e-5
NEG = -1e30
INT_MIN = -(2 ** 31)

LANES = 128
VMEM_LIMIT = 56 * 1024 * 1024

C_RKV = (0, 1152)
C_LORA = (1152, 1408)
C_DQ = (1408, 1664)
C_CKV = (1664, 1792)
C_IDX = (1792, 2176)
C_SQ = (2176, 2560)
C_SKV = (2560, 2816)
P_PAD = 2816
N_ORIG_BEFORE_PAD = 2116


def _dot(a, b, prec=None):
    return jnp.dot(a, b, preferred_element_type=F32, precision=prec)


def _dot_nt(a, b, prec=None):
    return lax.dot_general(a, b, (((1,), (1,)), ((), ())), preferred_element_type=F32, precision=prec)


def _split2(a):
    a_hi = a.astype(BF16)
    return a_hi, (a - a_hi.astype(F32)).astype(BF16)


def _bdot(a, b):
    return _dot(a.astype(BF16), b.astype(BF16))


def _bdot_nt(a, b):
    return _dot_nt(a.astype(BF16), b.astype(BF16))


def _dot2(a, b_exact):
    a_hi, a_lo = _split2(a)
    return _dot(a_hi, b_exact) + _dot(a_lo, b_exact)


def _dot2_l(a_exact, b):
    b_hi, b_lo = _split2(b)
    return _dot(a_exact, b_hi) + _dot(a_exact, b_lo)


def _dot3(a, b):
    a_hi, a_lo = _split2(a)
    b_hi, b_lo = _split2(b)
    return _dot(a_hi, b_hi) + (_dot(a_lo, b_hi) + _dot(a_hi, b_lo))


def _iota(shape, dim):
    return lax.broadcasted_iota(I32, shape, dim)


def _sigmoid(x):
    return 1.0 / (1.0 + jnp.exp(-x))


def _layer_norm_rows(v, g, b):
    mu = jnp.mean(v, axis=-1, keepdims=True)
    d = v - mu
    var = jnp.mean(d * d, axis=-1, keepdims=True)
    return d * lax.rsqrt(var + LN_EPS) * g + b


def _params(sem):
    return pltpu.CompilerParams(dimension_semantics=sem, vmem_limit_bytes=VMEM_LIMIT)


def _mod_kernel(c_ref, w_ref, b_ref, o_ref):
    c = c_ref[...]
    cond = c * _sigmoid(c)
    o_ref[0] = _dot(cond, w_ref[0], HI) + b_ref[0]


def _modulation(c, w_mod, b_mod):
    depth, d, d6 = w_mod.shape
    c8 = jnp.broadcast_to(c, (8, d))
    nj = d6 // d
    out = pl.pallas_call(
        _mod_kernel,
        grid=(depth, nj),
        in_specs=[
            pl.BlockSpec((8, d), lambda l, j: (0, 0)),
            pl.BlockSpec((1, d, d), lambda l, j: (l, 0, j)),
            pl.BlockSpec((1, 1, d), lambda l, j: (l, 0, j)),
        ],
        out_specs=pl.BlockSpec((1, 8, d), lambda l, j: (l, 0, j)),
        out_shape=jax.ShapeDtypeStruct((depth, 8, d6), F32),
        compiler_params=_params(("arbitrary", "arbitrary")),
        name="modulation",
    )(c8, w_mod, b_mod.reshape(depth, 1, d6))
    return out[:, 0:1, :]


def _proj_kernel(x_ref, sc_ref, sh_ref, w_ref, wlo_ref, kvn_ref, ikg_ref, ikb_ref,
                 rkv_ref, lora_ref, dq_ref, ckv_ref, iq_ref, ikw_ref, sq_ref, skv_ref):
    h = x_ref[...] * (1.0 + sc_ref[...]) + sh_ref[...]
    hb = h.astype(BF16)
    hl = (h - hb.astype(F32)).astype(BF16)

    def mm(c):
        return _dot(hb, w_ref[:, c[0]:c[1]])

    rkv_ref[...] = mm(C_RKV)
    lora_ref[...] = mm(C_LORA)
    dq_ref[...] = mm(C_DQ)
    sq_ref[...] = mm(C_SQ)
    skv_ref[...] = mm(C_SKV)
    ckv = mm(C_CKV)
    ckv_ref[...] = ckv * lax.rsqrt(jnp.mean(ckv * ckv, axis=-1, keepdims=True) + 1e-6) * kvn_ref[...]
    idx = mm(C_IDX) + _dot(hl, w_ref[:, C_IDX[0]:C_IDX[1]]) + _dot(hb, wlo_ref[...])
    iq_ref[...] = idx[:, 0:256]
    g3 = idx[:, 256:384]
    lane = _iota(g3.shape, 1)
    isk = lane < IDX_DIM
    mu = jnp.sum(jnp.where(isk, g3, 0.0), axis=-1, keepdims=True) * (1.0 / IDX_DIM)
    dk = jnp.where(isk, g3 - mu, 0.0)
    var = jnp.sum(dk * dk, axis=-1, keepdims=True) * (1.0 / IDX_DIM)
    ikn = dk * lax.rsqrt(var + LN_EPS) * ikg_ref[...] + ikb_ref[...]
    ikw_ref[...] = jnp.where(isk, ikn, g3 * (IDX_HEADS ** -0.5 * IDX_DIM ** -0.5))


def _input_proj(x, sc, sh, w_hi, w_idx_lo, kvn, ikg, ikb, tm=512):
    t, d = x.shape
    widths = [C_RKV, C_LORA, C_DQ, C_CKV, (0, 256), (0, 128), C_SQ, C_SKV]
    widths = [c[1] - c[0] for c in widths]
    const = lambda i: (0, 0)
    row = lambda i: (i, 0)
    return pl.pallas_call(
        _proj_kernel,
        grid=(t // tm,),
        in_specs=[
            pl.BlockSpec((tm, d), row),
            pl.BlockSpec((1, d), const),
            pl.BlockSpec((1, d), const),
            pl.BlockSpec((d, P_PAD), const),
            pl.BlockSpec((d, C_IDX[1] - C_IDX[0]), const),
            pl.BlockSpec((1, KV_LORA), const),
            pl.BlockSpec((1, LANES), const),
            pl.BlockSpec((1, LANES), const),
        ],
        out_specs=[pl.BlockSpec((tm, w), row) for w in widths],
        out_shape=[jax.ShapeDtypeStruct((t, w), F32) for w in widths],
        compiler_params=_params(("arbitrary",)),
        name="input_proj",
    )(x, sc, sh, w_hi, w_idx_lo, kvn, ikg, ikb)


RW_CHUNK = 64
RW_UNROLL = 2


def _rwkv_kernel(r_ref, k_ref, v_ref, lora_ref, rp_ref, kp_ref, vp_ref, lp_ref,
                 mur_ref, muk_ref, muv_ref, mul_ref, w0_ref, w2_ref, a0_ref, a2_ref, g2_ref,
                 kk_ref, ka_ref, rk_ref, lng_ref, lnb_ref, o_ref,
                 h_ref, y_ref, st_ref, wm_ref, ar_ref, rs_ref, vs_ref, lt_ref, zm_ref, y0_ref, gc_ref, *, tg):
    g = pl.program_id(0)
    c64 = RW_CHUNK
    nch = tg // c64
    npair = RWKV_W // LANES
    pair_lanes = [slice(p * LANES, (p + 1) * LANES) for p in range(npair)]
    lane = _iota((1, LANES), 1)
    first = g == 0

    @pl.when(first)
    def _():
        h_ref[...] = jnp.zeros_like(h_ref)

    rowid = _iota((tg, 1), 0)

    def shift_mix(cur_ref, prev_ref, mu_ref):
        cur = cur_ref[...]
        prev_row = jnp.where(first, 0.0, prev_ref[7:8, :])
        rolled = pltpu.roll(cur, 1, 0)
        shifted = jnp.where(rowid == 0, prev_row, rolled)
        return cur + (shifted - cur) * mu_ref[...]

    r = shift_mix(r_ref, rp_ref, mur_ref)
    k = shift_mix(k_ref, kp_ref, muk_ref)
    v = shift_mix(v_ref, vp_ref, muv_ref)
    lo = shift_mix(lora_ref, lp_ref, mul_ref)
    wl = lo[:, 0:DECAY_LORA]
    al = lo[:, DECAY_LORA:DECAY_LORA + AAA_LORA]
    gl = lo[:, 128:256]

    zw = -(w0_ref[...] + _dot3(jnp.tanh(wl), w2_ref[...]))
    softplus = jnp.maximum(zw, 0.0) + jnp.log(1.0 + jnp.exp(-jnp.abs(zw)))
    lw = -jnp.exp(-softplus - 0.5)
    a = _sigmoid(a0_ref[...] + _bdot(al, a2_ref[...]))
    gate = _bdot(_sigmoid(gl), g2_ref[...])

    ri = _iota((LANES, LANES), 0) // HEAD_DIM
    ci = _iota((LANES, LANES), 1) // HEAD_DIM
    bones = jnp.where(ri == ci, 1.0, 0.0).astype(BF16)

    def head_sum(xf):
        return jnp.concatenate([_dot2(xf[:, pl_], bones) for pl_ in pair_lanes], axis=1)

    kk = k * kk_ref[...]
    kk = kk / jnp.maximum(jnp.sqrt(head_sum(kk * kk)), 1e-12)
    k2 = k * (1.0 + (a - 1.0) * ka_ref[...])
    bonus = head_sum(r * k2 * rk_ref[...]) * v
    bvec = a * kk

    st_ref[0] = r
    st_ref[1] = k2
    st_ref[2] = v
    st_ref[3] = lw
    st_ref[4] = kk
    st_ref[5] = bvec

    rr = _iota((LANES, LANES), 0)
    cc = _iota((LANES, LANES), 1)
    same = (rr // c64) == (cc // c64)
    strict = same & ((rr % c64) > (cc % c64))
    incl = same & ((rr % c64) >= (cc % c64))
    eye = jnp.where(rr == cc, 1.0, 0.0)
    tril = jnp.where(_iota((c64, c64), 0) >= _iota((c64, c64), 1), 1.0, 0.0).astype(BF16)
    lo_half = lane < HEAD_DIM

    def stack(xc):
        return jnp.concatenate([jnp.where(lo_half, xc, 0.0), jnp.where(lo_half, 0.0, xc)], axis=0)

    def prepare(c, carry):
        chunks = [c * RW_UNROLL + j for j in range(RW_UNROLL)]
        sls = [pl.ds(pl.multiple_of(cj * c64, c64), c64) for cj in chunks]
        items = [(p, j) for j in range(RW_UNROLL) for p in range(npair)]
        pairs = range(len(items))
        idx = [p * nch + chunks[j] for p, j in items]
        ld = lambda q: [st_ref[q, sls[j], pair_lanes[p]] for p, j in items]
        rc, kc, vc, lwc, kkc, bc = ld(0), ld(1), ld(2), ld(3), ld(4), ld(5)
        cum = [_dot2_l(tril, lwc[p]) for p in pairs]
        tot = [cum[p][c64 - 1:c64, :] for p in pairs]
        g_in = [jnp.exp(cum[p]) for p in pairs]
        g_ex = [jnp.exp(cum[p] - lwc[p]) for p in pairs]
        g_inv = [jnp.exp(-cum[p]) for p in pairs]
        g_rest = [jnp.exp(tot[p] - cum[p]) for p in pairs]
        a_s = [stack(-kkc[p] * g_ex[p]).astype(BF16) for p in pairs]
        b_s = [stack(bc[p] * g_inv[p]).astype(BF16) for p in pairs]
        k_s = [stack(kc[p] * g_inv[p]).astype(BF16) for p in pairs]
        r_s = [stack(rc[p] * g_in[p]).astype(BF16) for p in pairs]
        v_s = [stack(vc[p]).astype(BF16) for p in pairs]
        nmat = [jnp.where(strict, _dot_nt(a_s[p], b_s[p]), 0.0) for p in pairs]
        aak = [jnp.where(strict, _dot_nt(a_s[p], k_s[p]), 0.0) for p in pairs]
        arb = [jnp.where(incl, _dot_nt(r_s[p], b_s[p]), 0.0) for p in pairs]
        ark = [jnp.where(incl, _dot_nt(r_s[p], k_s[p]), 0.0) for p in pairs]
        tinv = [eye + nmat[p] for p in pairs]
        pw = nmat
        for _ in range(5):
            pw = [_bdot(pw[p], pw[p]) for p in pairs]
            tinv = [_bdot(tinv[p], eye + pw[p]) for p in pairs]
        tinv = [tinv[p].astype(BF16) for p in pairs]
        akv = [_bdot(aak[p], v_s[p]).astype(BF16) for p in pairs]
        wmat = [_dot(tinv[p], a_s[p]) for p in pairs]
        zmat = [_dot(tinv[p], akv[p]) for p in pairs]
        y0 = [_bdot(ark[p], v_s[p]) for p in pairs]
        for p in pairs:
            wm_ref[idx[p]] = wmat[p].astype(BF16)
            zm_ref[idx[p]] = zmat[p]
            y0_ref[idx[p]] = y0[p]
            ar_ref[idx[p]] = arb[p].astype(BF16)
            rs_ref[idx[p]] = r_s[p]
            vs_ref[idx[p]] = v_s[p]
            lt_ref[idx[p]] = jnp.concatenate([stack(bc[p] * g_rest[p]), stack(kc[p] * g_rest[p])],
                                             axis=0).T.astype(BF16)
            gc_ref[idx[p]] = jnp.broadcast_to(jnp.sum(eye * jnp.exp(tot[p]), axis=1, keepdims=True),
                                              (LANES, LANES))
        return carry

    lax.fori_loop(0, nch // RW_UNROLL, prepare, 0)

    def advance(c, carry):
        sl = pl.ds(pl.multiple_of(c * c64, c64), c64)
        pairs = range(npair)
        idx = [p * nch + c for p in pairs]
        hst = [h_ref[p] for p in pairs]
        hb = [hst[p].astype(BF16) for p in pairs]
        u = [_dot(wm_ref[idx[p]], hb[p]) + zm_ref[idx[p]] for p in pairs]
        rh = [_dot(rs_ref[idx[p]], hb[p]) for p in pairs]
        ub = [u[p].astype(BF16) for p in pairs]
        hnew = [_dot(lt_ref[idx[p]], jnp.concatenate([ub[p], vs_ref[idx[p]]], axis=0)) for p in pairs]
        au = [_dot(ar_ref[idx[p]], ub[p]) for p in pairs]
        for p in pairs:
            h_ref[p] = gc_ref[idx[p]] * hst[p] + hnew[p]
            ys = rh[p] + au[p] + y0_ref[idx[p]]
            y_ref[sl, pair_lanes[p]] = ys[0:c64, :] + ys[c64:2 * c64, :]
        return carry

    lax.fori_loop(0, nch, advance, 0)

    y = y_ref[...]
    mean = head_sum(y) * (1.0 / HEAD_DIM)
    dy = y - mean
    var = head_sum(dy * dy) * (1.0 / HEAD_DIM)
    o = dy * lax.rsqrt(var + GN_EPS) * lng_ref[...] + lnb_ref[...]
    o_ref[...] = (o + bonus) * gate


def _rwkv_mix(rkv, lora, mu, w0, w2, a0, a2, g2, k_k, k_a, r_k, ln_g, ln_b, tg=512):
    t = rkv.shape[0]
    w = RWKV_W
    npair = w // LANES
    nmat = npair * (tg // RW_CHUNK)
    mu_r, mu_k, mu_v, mu_l = mu[:, 0:w], mu[:, w:2 * w], mu[:, 2 * w:3 * w], mu[:, 3 * w:3 * w + 256]
    blk = lambda off: pl.BlockSpec((tg, w), lambda g: (g, off))
    prev = lambda off: pl.BlockSpec((8, w), lambda g: (jnp.maximum(g * (tg // 8) - 1, 0), off))
    vec = pl.BlockSpec((1, w), lambda g: (0, 0))
    full = lambda rows: pl.BlockSpec((rows, w), lambda g: (0, 0))
    return pl.pallas_call(
        functools.partial(_rwkv_kernel, tg=tg),
        grid=(t // tg,),
        in_specs=[
            blk(0), blk(1), blk(2),
            pl.BlockSpec((tg, 256), lambda g: (g, 0)),
            prev(0), prev(1), prev(2),
            pl.BlockSpec((8, 256), lambda g: (jnp.maximum(g * (tg // 8) - 1, 0), 0)),
            vec, vec, vec,
            pl.BlockSpec((1, 256), lambda g: (0, 0)),
            vec, full(DECAY_LORA), vec, full(AAA_LORA), full(GATE_LORA),
            vec, vec, vec, vec, vec,
        ],
        out_specs=pl.BlockSpec((tg, w), lambda g: (g, 0)),
        out_shape=jax.ShapeDtypeStruct((t, w), F32),
        scratch_shapes=[
            pltpu.VMEM((npair, LANES, LANES), F32),
            pltpu.VMEM((tg, w), F32),
            pltpu.VMEM((6, tg, w), F32),
            pltpu.VMEM((nmat, LANES, LANES), BF16),
            pltpu.VMEM((nmat, LANES, LANES), BF16),
            pltpu.VMEM((nmat, LANES, LANES), BF16),
            pltpu.VMEM((nmat, LANES, LANES), BF16),
            pltpu.VMEM((nmat, LANES, 2 * LANES), BF16),
            pltpu.VMEM((nmat, LANES, LANES), F32),
            pltpu.VMEM((nmat, LANES, LANES), F32),
            pltpu.VMEM((nmat, LANES, LANES), F32),
        ],
        compiler_params=_params(("arbitrary",)),
        name="rwkv7_mix",
    )(rkv, rkv, rkv, lora, rkv, rkv, rkv, lora,
      mu_r, mu_k, mu_v, mu_l, w0, w2, a0, a2, g2, k_k, k_a, r_k, ln_g, ln_b)


DSA_QB = 128
DSA_KC = 1024
DSA_SUB = 512
CNT_ROWS = 64


def _float_key(v):
    bits = lax.bitcast_convert_type(v, I32)
    return bits ^ ((bits >> 31) & 0x7FFFFFFF)


def _dsa_kernel(dq_ref, iq_ref, ikw_ref, ikx_ref, kf_ref, vft_ref, wuk_ref, wuv_ref, tril_ref, slc_ref,
                o_ref, sc_ref, acc_ref):
    i = pl.program_id(0)
    qb, kc, sc_rows = DSA_QB, DSA_KC, DSA_SUB
    nh = DSA_HEADS
    t0 = i * qb
    nch = (t0 + qb + kc - 1) // kc
    tq = t0 + _iota((1, qb), 1)

    iq = iq_ref[...]
    iq_hi = iq.astype(BF16).astype(F32)
    iq_lo = iq - iq_hi
    lhs = []
    for h in range(IDX_HEADS):
        s = slice(h * IDX_DIM, (h + 1) * IDX_DIM)
        lhs.append(jnp.concatenate([iq_hi[:, s], iq_hi[:, s], iq_lo[:, s], iq_lo[:, s]], axis=1))
    lhs_t = jnp.concatenate(lhs, axis=0).T.astype(BF16)
    ikw_t = ikw_ref[...].T
    iw = [ikw_t[IDX_DIM + h:IDX_DIM + h + 1, :] for h in range(IDX_HEADS)]

    def score_body(ch, carry):
        m1, m2 = carry
        for sub in range(kc // sc_rows):
            k0 = pl.multiple_of(ch * kc + sub * sc_rows, sc_rows)
            s_all = _dot(ikx_ref[pl.ds(k0, sc_rows), :], lhs_t)
            acc = jnp.zeros((sc_rows, qb), F32)
            for h in range(IDX_HEADS):
                acc = acc + jnp.maximum(s_all[:, h * qb:(h + 1) * qb], 0.0) * iw[h]
            acc = jnp.where(acc == 0.0, 0.0, acc)
            causal = (k0 + _iota((sc_rows, 1), 0)) <= tq
            sc_ref[pl.ds(k0, sc_rows), :] = jnp.where(causal, _float_key(acc), INT_MIN)
            accm = jnp.where(causal, acc, -jnp.inf)
            for j in range(sc_rows // LANES):
                xj = accm[j * LANES:(j + 1) * LANES, :]
                m2 = jnp.maximum(m2, jnp.minimum(m1, xj))
                m1 = jnp.maximum(m1, xj)
        return m1, m2

    ninf = jnp.full((LANES, qb), -jnp.inf, F32)
    m1, m2 = lax.fori_loop(0, nch, score_body, (ninf, ninf))

    def count_ge(cand):
        def body(ch, acc):
            k0 = pl.multiple_of(ch * kc, kc)
            m = jnp.where(sc_ref[pl.ds(k0, kc), :] >= cand, 1.0, 0.0)
            for j in range(kc // CNT_ROWS):
                acc = acc + m[j * CNT_ROWS:(j + 1) * CNT_ROWS, :]
            return acc
        acc = lax.fori_loop(0, nch, body, jnp.zeros((CNT_ROWS, qb), F32))
        return jnp.sum(acc, axis=0, keepdims=True)

    k_row = jnp.minimum(tq + 1, TOPK_MAX).astype(F32)
    hi0 = _float_key(jnp.max(m1, axis=0, keepdims=True))
    lo0 = jnp.minimum(_float_key(jnp.min(m2, axis=0, keepdims=True)), hi0)
    c_pos = count_ge(jnp.ones((1, qb), I32))
    c_nonneg = count_ge(jnp.zeros((1, qb), I32))
    at_zero = (c_pos < k_row) & (c_nonneg >= k_row)
    above = c_pos >= k_row
    lo0 = jnp.where(at_zero, 0, jnp.where(above, jnp.maximum(lo0, 1), lo0))
    hi0 = jnp.where(at_zero, 0, jnp.where(above, hi0, jnp.minimum(hi0, -1)))
    lo0 = jnp.minimum(lo0, hi0)

    def open_rows(lo, hi):
        return jnp.max(jnp.where(lo < hi, 1.0, 0.0))

    def bis_body(st):
        lo, hi, _ = st
        mid = (lo | hi) - ((lo ^ hi) >> 1)
        c = count_ge(mid)
        ge = c >= k_row
        lo_n = jnp.where(ge, mid, lo)
        hi_n = jnp.where(c == k_row, mid, jnp.where(ge, hi, mid - 1))
        return lo_n, hi_n, open_rows(lo_n, hi_n)

    thr, _, _ = lax.while_loop(lambda st: st[2] > 0.5, bis_body, (lo0, hi0, open_rows(lo0, hi0)))
    n_ge = count_ge(thr)
    has_tie = jnp.max(jnp.where(n_ge > k_row, 1.0, 0.0)) > 0.5

    dq = dq_ref[...]
    slc = slc_ref[...]
    qaug = []
    for h in range(nh):
        ql = _bdot(dq[:, h * HEAD_DIM:(h + 1) * HEAD_DIM], wuk_ref[h]) * HEAD_DIM ** -0.5
        qaug.append(jnp.concatenate([ql, jnp.broadcast_to(slc[h:h + 1, :], (qb, LANES))], axis=1))
    qaug_t = jnp.concatenate(qaug, axis=0).T.astype(BF16)
    acc_ref[...] = jnp.zeros_like(acc_ref)

    nsub = kc // sc_rows

    def sub_starts(ch):
        return [pl.multiple_of(ch * kc + sub * sc_rows, sc_rows) for sub in range(nsub)]

    def logits(k0):
        return _dot(kf_ref[pl.ds(k0, sc_rows), :], qaug_t)

    def attend(k0, lg_all, sel, m_old):
        ps, m_new = [], []
        for h in range(nh):
            cols = slice(h * qb, (h + 1) * qb)
            lg = jnp.where(sel, lg_all[:, cols], NEG)
            mh = jnp.maximum(m_old[:, cols], jnp.max(lg, axis=0, keepdims=True))
            ps.append(jnp.exp((lg - mh).astype(BF16)))
            m_new.append(mh)
        m_new = jnp.concatenate(m_new, axis=1)
        pv = _dot(vft_ref[:, pl.ds(k0, sc_rows)], jnp.concatenate(ps, axis=1))
        acc_ref[...] = jnp.exp(m_old - m_new) * acc_ref[...] + pv
        return m_new

    m_init = jnp.full((1, nh * qb), NEG, F32)

    @pl.when(jnp.logical_not(has_tie))
    def _():
        def body(ch, m_old):
            ks = sub_starts(ch)
            lgs = [logits(k0) for k0 in ks]
            for k0, lg in zip(ks, lgs):
                m_old = attend(k0, lg, sc_ref[pl.ds(k0, sc_rows), :] >= thr, m_old)
            return m_old
        lax.fori_loop(0, nch, body, m_init)

    @pl.when(has_tie)
    def _():
        need = k_row - count_ge(thr + 1)
        tril = tril_ref[...]

        def body(ch, carry):
            tie_run, m_old = carry
            ks = sub_starts(ch)
            lgs = [logits(k0) for k0 in ks]
            keys = [sc_ref[pl.ds(k0, sc_rows), :] for k0 in ks]
            prefs = [_dot(tril, jnp.where(key == thr, 1.0, 0.0).astype(BF16)) for key in keys]
            for k0, lg, key, pref in zip(ks, lgs, keys, prefs):
                sel = (key > thr) | ((key == thr) & (tie_run + pref <= need))
                m_old = attend(k0, lg, sel, m_old)
                tie_run = tie_run + pref[sc_rows - 1:sc_rows, :]
            return tie_run, m_old
        lax.fori_loop(0, nch, body, (jnp.zeros((1, qb), F32), m_init))

    acc = acc_ref[...]
    o_lat = acc[0:KV_LORA, :] / acc[KV_LORA:KV_LORA + 1, :]
    outs = [_bdot(o_lat[:, h * qb:(h + 1) * qb].T, wuv_ref[h]) for h in range(nh)]
    o_ref[...] = jnp.concatenate(outs, axis=1)


DSA_VROWS = KV_LORA + 16


def _dsa_mix(dq, iq, ikw, ckv, w_uk, w_uv, slopes):
    t = dq.shape[0]
    assert t <= LANES * 256
    ikn = ikw[:, 0:IDX_DIM]
    ik_hi, ik_lo = _split2(ikn)
    ikx = jnp.concatenate([ik_hi, ik_lo, ik_hi, ik_lo], axis=1)
    ckv_b = ckv.astype(BF16)
    pos = jnp.arange(t, dtype=I32)
    pa = (pos // LANES).astype(BF16)[:, None]
    pb = (pos % LANES).astype(BF16)[:, None]
    kf = jnp.concatenate([ckv_b, pa, pa, pa, pb, pb, pb, jnp.zeros((t, LANES - 6), BF16)], axis=1)
    vft = jnp.concatenate([ckv_b.T, jnp.ones((1, t), BF16), jnp.zeros((DSA_VROWS - KV_LORA - 1, t), BF16)], axis=0)
    cols = []
    for sl in slopes:
        for coef in (sl * LANES, sl):
            c_hi = jnp.asarray(coef, F32).astype(BF16)
            r1 = jnp.asarray(coef, F32) - c_hi.astype(F32)
            c_mid = r1.astype(BF16)
            c_lo = (r1 - c_mid.astype(F32)).astype(BF16)
            cols += [c_hi.astype(F32), c_mid.astype(F32), c_lo.astype(F32)]
    slc = jnp.stack(cols).reshape(DSA_HEADS, 6)
    slc = jnp.pad(slc, ((0, 8 - DSA_HEADS), (0, LANES - 6)))
    assert t % DSA_KC == 0
    kc = DSA_SUB
    tril = jnp.asarray((np.arange(kc)[:, None] >= np.arange(kc)[None, :]).astype(np.float32), BF16)
    row = lambda i: (i, 0)
    const2 = lambda i: (0, 0)
    const3 = lambda i: (0, 0, 0)
    resident = lambda shape: pl.BlockSpec(shape, const2, pipeline_mode=pl.Buffered(1))
    return pl.pallas_call(
        _dsa_kernel,
        grid=(t // DSA_QB,),
        in_specs=[
            pl.BlockSpec((DSA_QB, DSA_W), row),
            pl.BlockSpec((DSA_QB, IDX_HEADS * IDX_DIM), row),
            pl.BlockSpec((DSA_QB, LANES), row),
            resident((t, 4 * IDX_DIM)),
            resident((t, 2 * LANES)),
            resident((DSA_VROWS, t)),
            pl.BlockSpec((DSA_HEADS, HEAD_DIM, KV_LORA), const3),
            pl.BlockSpec((DSA_HEADS, KV_LORA, HEAD_DIM), const3),
            resident((kc, kc)),
            pl.BlockSpec((8, LANES), const2),
        ],
        out_specs=pl.BlockSpec((DSA_QB, DSA_W), row),
        out_shape=jax.ShapeDtypeStruct((t, DSA_W), F32),
        scratch_shapes=[
            pltpu.VMEM((t, DSA_QB), I32),
            pltpu.VMEM((DSA_VROWS, DSA_HEADS * DSA_QB), F32),
        ],
        compiler_params=_params(("arbitrary",)),
        name="dsa_mix",
    )(dq, iq, ikw, ikx, kf, vft, w_uk, w_uv, tril, slc)


def _swa_kernel(q_ref, kv_ref, kvp_ref, sink_ref, o_ref, *, slopes):
    i = pl.program_id(0)
    w = WINDOW
    gsz = SWA_HEADS // SWA_KV_HEADS
    q = q_ref[...]
    kv = kv_ref[...]
    kvp = kvp_ref[...]
    qi = _iota((w, 2 * w), 0)
    kj = _iota((w, 2 * w), 1)
    dist = qi + w - kj
    valid = (dist >= 0) & (dist < w) & ((kj >= w) | (i > 0))
    distf = dist.astype(F32)
    sinks = sink_ref[...]
    k2, v2 = [], []
    for g in range(SWA_KV_HEADS):
        k2.append(jnp.concatenate([kvp[:, g * HEAD_DIM:(g + 1) * HEAD_DIM],
                                   kv[:, g * HEAD_DIM:(g + 1) * HEAD_DIM]], axis=0).astype(BF16))
        v2.append(jnp.concatenate([kvp[:, w + g * HEAD_DIM:w + (g + 1) * HEAD_DIM],
                                   kv[:, w + g * HEAD_DIM:w + (g + 1) * HEAD_DIM]], axis=0).astype(BF16))
    heads = range(SWA_HEADS)
    s = [_dot_nt(q[:, hd * HEAD_DIM:(hd + 1) * HEAD_DIM].astype(BF16), k2[hd // gsz]) * HEAD_DIM ** -0.5
         for hd in heads]
    s = [jnp.where(valid, s[hd] - slopes[hd] * distf, NEG) for hd in heads]
    sink = [sinks[0:1, hd:hd + 1] for hd in heads]
    m = [jnp.maximum(jnp.max(s[hd], axis=1, keepdims=True), sink[hd]) for hd in heads]
    e = [jnp.exp(s[hd] - m[hd]) for hd in heads]
    p = [e[hd] / (jnp.sum(e[hd], axis=1, keepdims=True) + jnp.exp(sink[hd] - m[hd])) for hd in heads]
    outs = [_dot(p[hd].astype(BF16), v2[hd // gsz]) for hd in heads]
    o_ref[...] = jnp.concatenate(outs, axis=1)


def _swa_mix(sq, skv, sinks, slopes):
    t = sq.shape[0]
    w = WINDOW
    return pl.pallas_call(
        functools.partial(_swa_kernel, slopes=slopes),
        grid=(t // w,),
        in_specs=[
            pl.BlockSpec((w, SWA_W), lambda i: (i, 0)),
            pl.BlockSpec((w, 2 * w), lambda i: (i, 0)),
            pl.BlockSpec((w, 2 * w), lambda i: (jnp.maximum(i - 1, 0), 0)),
            pl.BlockSpec((1, LANES), lambda i: (0, 0)),
        ],
        out_specs=pl.BlockSpec((w, SWA_W), lambda i: (i, 0)),
        out_shape=jax.ShapeDtypeStruct((t, SWA_W), F32),
        compiler_params=_params(("arbitrary",)),
        name="swa_mix",
    )(sq, skv, skv, sinks)


def _post_mix_kernel(x_ref, orw_ref, ods_ref, osw_ref, wout_ref, g1_ref, lng_ref, lnb_ref,
                     sc2_ref, sh2_ref, rwt_ref, rb_ref, tri_ref,
                     x1_ref, h2_ref, eidx_ref, rank_ref, gate_ref, cnt_ref, carry_ref):
    i = pl.program_id(0)

    @pl.when(i == 0)
    def _():
        carry_ref[...] = jnp.zeros_like(carry_ref)

    y = (_dot(orw_ref[...].astype(BF16), wout_ref[0:RWKV_W, :])
         + _dot(ods_ref[...].astype(BF16), wout_ref[RWKV_W:RWKV_W + DSA_W, :])
         + _dot(osw_ref[...].astype(BF16), wout_ref[RWKV_W + DSA_W:D_MODEL, :]))
    x1 = _layer_norm_rows(ALPHA * x_ref[...] + g1_ref[...] * y, lng_ref[...], lnb_ref[...])
    x1_ref[...] = x1
    h2 = x1 * (1.0 + sc2_ref[...]) + sh2_ref[...]
    h2_ref[...] = h2

    tm = h2.shape[0]
    ne = N_EXPERTS
    gs = ne // N_GROUPS
    scores = _sigmoid(_dot_nt(rwt_ref[...], h2, HI))
    sel = scores + rb_ref[...]
    sub = _iota((gs, tm), 0).astype(F32)
    gsc = []
    for j in range(N_GROUPS):
        gj = sel[j * gs:(j + 1) * gs, :]
        m1 = jnp.max(gj, axis=0, keepdims=True)
        f1 = jnp.min(jnp.where(gj == m1, sub, float(gs)), axis=0, keepdims=True)
        m2 = jnp.max(jnp.where(sub == f1, -jnp.inf, gj), axis=0, keepdims=True)
        gsc.append(m1 + m2)
    gsc = jnp.concatenate(gsc, axis=0)
    gid = _iota((N_GROUPS, tm), 0).astype(F32)
    gmask = jnp.zeros((N_GROUPS, tm), F32)
    for _ in range(TOPK_GROUPS):
        mx = jnp.max(gsc, axis=0, keepdims=True)
        fi = jnp.min(jnp.where(gsc == mx, gid, float(N_GROUPS)), axis=0, keepdims=True)
        pick = gid == fi
        gmask = jnp.where(pick, 1.0, gmask)
        gsc = jnp.where(pick, -jnp.inf, gsc)
    selm = jnp.concatenate(
        [jnp.where(gmask[j:j + 1, :] > 0.5, sel[j * gs:(j + 1) * gs, :], NEG) for j in range(N_GROUPS)], axis=0)
    eid = _iota((ne, tm), 0).astype(F32)
    gsel, eids = [], []
    chosen_f = jnp.zeros((ne, tm), F32)
    for _ in range(TOP_K):
        mx = jnp.max(selm, axis=0, keepdims=True)
        fi = jnp.min(jnp.where(selm == mx, eid, float(ne)), axis=0, keepdims=True)
        pick = eid == fi
        eids.append(fi)
        gsel.append(jnp.sum(jnp.where(pick, scores, 0.0), axis=0, keepdims=True))
        chosen_f = jnp.where(pick, 1.0, chosen_f)
        selm = jnp.where(pick, -jnp.inf, selm)
    gsum = gsel[0]
    for kx in range(1, TOP_K):
        gsum = gsum + gsel[kx]
    before = _dot(chosen_f.astype(BF16), tri_ref[...]) + carry_ref[:, 0:1]
    ranks = [jnp.sum(jnp.where(eid == eids[kx], before, 0.0), axis=0, keepdims=True) for kx in range(TOP_K)]
    eidx_ref[...] = jnp.concatenate(eids, axis=0).astype(I32)
    rank_ref[...] = jnp.concatenate(ranks, axis=0).astype(I32)
    gate_ref[...] = jnp.concatenate(gsel, axis=0) / gsum * ROUTED_SCALE
    carry_ref[...] = carry_ref[...] + jnp.sum(chosen_f, axis=1, keepdims=True)
    cnt_ref[...] = carry_ref[...]


def _post_mix(x, o_rw, o_ds, o_sw, w_out, g1, ln_g, ln_b, sc2, sh2, router_wt, router_b, tm=256):
    t, d = x.shape
    tri = (np.arange(tm)[:, None] < np.arange(tm)[None, :]).astype(np.float32)
    tri = jnp.asarray(tri, BF16)
    row = lambda i: (i, 0)
    const = lambda i: (0, 0)
    col = lambda i: (0, i)
    vec = pl.BlockSpec((1, d), const)
    return pl.pallas_call(
        _post_mix_kernel,
        grid=(t // tm,),
        in_specs=[
            pl.BlockSpec((tm, d), row),
            pl.BlockSpec((tm, RWKV_W), row),
            pl.BlockSpec((tm, DSA_W), row),
            pl.BlockSpec((tm, SWA_W), row),
            pl.BlockSpec((d, d), const),
            vec, vec, vec, vec, vec,
            pl.BlockSpec((N_EXPERTS, d), const),
            pl.BlockSpec((N_EXPERTS, 1), const),
            pl.BlockSpec((tm, tm), const),
        ],
        out_specs=[
            pl.BlockSpec((tm, d), row),
            pl.BlockSpec((tm, d), row),
            pl.BlockSpec((TOP_K, tm), col),
            pl.BlockSpec((TOP_K, tm), col),
            pl.BlockSpec((TOP_K, tm), col),
            pl.BlockSpec((N_EXPERTS, LANES), const),
        ],
        out_shape=[
            jax.ShapeDtypeStruct((t, d), F32),
            jax.ShapeDtypeStruct((t, d), F32),
            jax.ShapeDtypeStruct((TOP_K, t), I32),
            jax.ShapeDtypeStruct((TOP_K, t), I32),
            jax.ShapeDtypeStruct((TOP_K, t), F32),
            jax.ShapeDtypeStruct((N_EXPERTS, LANES), F32),
        ],
        scratch_shapes=[pltpu.VMEM((N_EXPERTS, LANES), F32)],
        compiler_params=_params(("arbitrary",)),
        name="post_mix_router",
    )(x, o_rw, o_ds, o_sw, w_out, g1, ln_g, ln_b, sc2, sh2, router_wt, router_b, tri)


MOE_ROWS = 512
MOE_TILE = 256


def _row_copy(src_ref, src_row, dst_ref, dst_row, sem):
    return pltpu.make_async_copy(src_ref.at[pl.ds(src_row, 1), :], dst_ref.at[pl.ds(dst_row, 1), :], sem)


def _dispatch_kernel(slot_hbm, h_ref, xs_in, xs_out, slot_smem, sem_tab, sem_rows):
    del xs_in
    i = pl.program_id(0)
    tab = pltpu.make_async_copy(slot_hbm.at[i], slot_smem, sem_tab)
    tab.start()
    tab.wait()

    def issue(tt, carry):
        for kx in range(TOP_K):
            _row_copy(h_ref, tt, xs_out, slot_smem[kx, tt], sem_rows).start(priority=kx % 2)
        return carry

    lax.fori_loop(0, MOE_TILE, issue, 0)

    def drain(tt, carry):
        for kx in range(TOP_K):
            _row_copy(h_ref, 0, xs_out, 0, sem_rows).wait()
        return carry

    lax.fori_loop(0, MOE_TILE, drain, 0)


def _dispatch(slot_tiles, h2, cap):
    t, d = h2.shape
    xs0 = jnp.zeros((cap, d), F32)
    return pl.pallas_call(
        _dispatch_kernel,
        grid=(t // MOE_TILE,),
        in_specs=[
            pl.BlockSpec(memory_space=pl.ANY),
            pl.BlockSpec((MOE_TILE, d), lambda i: (i, 0)),
            pl.BlockSpec(memory_space=pl.ANY),
        ],
        out_specs=pl.BlockSpec(memory_space=pl.ANY),
        out_shape=jax.ShapeDtypeStruct((cap, d), F32),
        scratch_shapes=[
            pltpu.SMEM((TOP_K, MOE_TILE), I32),
            pltpu.SemaphoreType.DMA,
            pltpu.SemaphoreType.DMA,
        ],
        input_output_aliases={2: 0},
        compiler_params=_params(("arbitrary",)),
        name="moe_dispatch",
    )(slot_tiles, h2, xs0)


def _expert_kernel(be_ref, nb_ref, xs_ref, w1_ref, w3_ref, w2_ref, ys_ref, w1b, w3b, w2b):
    b = pl.program_id(0)
    changed = (b == 0) | (be_ref[b] != be_ref[jnp.maximum(b - 1, 0)])

    @pl.when(changed & (b < nb_ref[0]))
    def _():
        w1b[...] = w1_ref[0, 0].astype(BF16)
        w3b[...] = w3_ref[0, 0].astype(BF16)
        w2b[...] = w2_ref[0, 0].astype(BF16)

    @pl.when(b < nb_ref[0])
    def _():
        xb = xs_ref[...].astype(BF16)
        a = _dot(xb, w1b[...])
        gte = _dot(xb, w3b[...])
        hmid = (a * _sigmoid(a) * gte).astype(BF16)
        ys_ref[...] = _dot(hmid, w2b[...])

    @pl.when(b >= nb_ref[0])
    def _():
        ys_ref[...] = jnp.zeros_like(ys_ref)


def _experts(block_e, n_used, xs, w1, w3, w2, layer):
    cap, d = xs.shape
    nb = cap // MOE_ROWS
    grid_spec = pltpu.PrefetchScalarGridSpec(
        num_scalar_prefetch=2,
        grid=(nb,),
        in_specs=[
            pl.BlockSpec((MOE_ROWS, d), lambda b, be, nu: (b, 0)),
            pl.BlockSpec((1, 1, d, D_EXPERT), lambda b, be, nu: (layer, be[b], 0, 0)),
            pl.BlockSpec((1, 1, d, D_EXPERT), lambda b, be, nu: (layer, be[b], 0, 0)),
            pl.BlockSpec((1, 1, D_EXPERT, d), lambda b, be, nu: (layer, be[b], 0, 0)),
        ],
        out_specs=pl.BlockSpec((MOE_ROWS, d), lambda b, be, nu: (b, 0)),
        scratch_shapes=[
            pltpu.VMEM((d, D_EXPERT), BF16),
            pltpu.VMEM((d, D_EXPERT), BF16),
            pltpu.VMEM((D_EXPERT, d), BF16),
        ],
    )
    return pl.pallas_call(
        _expert_kernel,
        grid_spec=grid_spec,
        out_shape=jax.ShapeDtypeStruct((cap, d), F32),
        compiler_params=_params(("arbitrary",)),
        name="moe_experts",
    )(block_e, n_used, xs, w1, w3, w2)


def _combine_kernel(slot_hbm, ys_hbm, x1_ref, h2_ref, gate_ref, sw1_ref, sw3_ref, sw2_ref,
                    g2_ref, lng_ref, lnb_ref, o_ref, slot_smem, gbuf, sem_tab, sem_rows):
    i = pl.program_id(0)
    tab = pltpu.make_async_copy(slot_hbm.at[i], slot_smem, sem_tab)
    tab.start()
    tab.wait()

    def issue(tt, carry):
        for kx in range(TOP_K):
            _row_copy(ys_hbm, slot_smem[kx, tt], gbuf.at[kx], tt, sem_rows).start(priority=kx % 2)
        return carry

    lax.fori_loop(0, MOE_TILE, issue, 0)

    hb = h2_ref[...].astype(BF16)
    a = _dot(hb, sw1_ref[...])
    gte = _dot(hb, sw3_ref[...])
    y = _dot((a * _sigmoid(a) * gte).astype(BF16), sw2_ref[...])

    def drain(tt, carry):
        for kx in range(TOP_K):
            _row_copy(ys_hbm, 0, gbuf.at[kx], 0, sem_rows).wait()
        return carry

    lax.fori_loop(0, MOE_TILE, drain, 0)

    gates = gate_ref[...]
    for kx in range(TOP_K):
        y = y + gates[:, kx:kx + 1] * gbuf[kx]
    o_ref[...] = _layer_norm_rows(ALPHA * x1_ref[...] + g2_ref[...] * y, lng_ref[...], lnb_ref[...])


def _combine(slot_tiles, ys, x1, h2, gates_t, sw1, sw3, sw2, g2, ln_g, ln_b):
    t, d = x1.shape
    row = lambda i: (i, 0)
    const = lambda i: (0, 0)
    vec = pl.BlockSpec((1, d), const)
    return pl.pallas_call(
        _combine_kernel,
        grid=(t // MOE_TILE,),
        in_specs=[
            pl.BlockSpec(memory_space=pl.ANY),
            pl.BlockSpec(memory_space=pl.ANY),
            pl.BlockSpec((MOE_TILE, d), row),
            pl.BlockSpec((MOE_TILE, d), row),
            pl.BlockSpec((MOE_TILE, TOP_K), row),
            pl.BlockSpec((d, D_EXPERT), const),
            pl.BlockSpec((d, D_EXPERT), const),
            pl.BlockSpec((D_EXPERT, d), const),
            vec, vec, vec,
        ],
        out_specs=pl.BlockSpec((MOE_TILE, d), row),
        out_shape=jax.ShapeDtypeStruct((t, d), F32),
        scratch_shapes=[
            pltpu.SMEM((TOP_K, MOE_TILE), I32),
            pltpu.VMEM((TOP_K, MOE_TILE, d), F32),
            pltpu.SemaphoreType.DMA,
            pltpu.SemaphoreType.DMA,
        ],
        compiler_params=_params(("arbitrary",)),
        name="moe_combine",
    )(slot_tiles, ys, x1, h2, gates_t, sw1, sw3, sw2, g2, ln_g, ln_b)


def _pad_w_in(w_in_l):
    d = w_in_l.shape[0]
    pad = jnp.zeros((d, C_SQ[0] - N_ORIG_BEFORE_PAD), w_in_l.dtype)
    return jnp.concatenate([w_in_l[:, :N_ORIG_BEFORE_PAD], pad, w_in_l[:, N_ORIG_BEFORE_PAD:]], axis=1)


def _pad_lanes(v, width=LANES):
    v = v.reshape(1, -1)
    return jnp.pad(v, ((0, 0), (0, width - v.shape[1])))


def _moe_tables(eidx, rank, counts):
    t = eidx.shape[1]
    cnt = counts[:, 0].astype(I32)
    padded = (cnt + MOE_ROWS - 1) // MOE_ROWS * MOE_ROWS
    pad_end = jnp.cumsum(padded)
    pad_start = pad_end - padded
    e_ids = jnp.arange(N_EXPERTS, dtype=I32)
    start_of = jnp.sum(jnp.where(eidx[..., None] == e_ids, pad_start, 0), axis=-1)
    slot = start_of + rank
    slot_tiles = slot.reshape(TOP_K, t // MOE_TILE, MOE_TILE).transpose(1, 0, 2)
    cap = t * TOP_K + N_EXPERTS * MOE_ROWS
    nb = cap // MOE_ROWS
    blk_row = jnp.arange(nb, dtype=I32)[:, None] * MOE_ROWS
    block_e = jnp.minimum(jnp.sum((pad_end[None, :] <= blk_row).astype(I32), axis=1), N_EXPERTS - 1)
    n_used = (pad_end[-1] // MOE_ROWS).astype(I32).reshape(1)
    return slot_tiles, block_e, n_used, cap


def kernel(x, c, w_mod, b_mod, w_in, rwkv_mu, rwkv_w0, rwkv_w2, rwkv_a0, rwkv_a2, rwkv_g2, rwkv_k_k, rwkv_k_a, rwkv_r_k, rwkv_ln_g, rwkv_ln_b, dsa_kv_norm, dsa_w_uk, dsa_w_uv, dsa_ik_g, dsa_ik_b, swa_sinks, w_out, ln_mix_g, ln_mix_b, router_w, router_bias, exp_w1, exp_w3, exp_w2, sh_w1, sh_w3, sh_w2, ln_ffn_g, ln_ffn_b):
    bsz, t, d = x.shape
    assert bsz == 1 and d == D_MODEL
    depth = w_mod.shape[0]
    n_sl = SWA_HEADS + DSA_HEADS
    slopes = [2.0 ** (-8.0 * (j + 1.0) / n_sl) for j in range(n_sl)]
    swa_slopes, dsa_slopes = slopes[:SWA_HEADS], slopes[SWA_HEADS:]

    mod = _modulation(c, w_mod, b_mod)
    xs_cur = x[0]
    row1 = lambda v: v.reshape(1, -1)
    for l in range(depth):
        sh1, sc1, g1, sh2, sc2, g2 = [mod[l, :, j * d:(j + 1) * d] for j in range(6)]
        wp = _pad_w_in(w_in[l])
        w_hi = wp.astype(BF16)
        w_idx = wp[:, C_IDX[0]:C_IDX[1]]
        w_idx_lo = (w_idx - w_idx.astype(BF16).astype(F32)).astype(BF16)
        rkv, lora, dq, ckv, iq, ikw, sq, skv = _input_proj(
            xs_cur, sc1, sh1, w_hi, w_idx_lo, row1(dsa_kv_norm[l]),
            _pad_lanes(dsa_ik_g[l]), _pad_lanes(dsa_ik_b[l]))
        o_rw = _rwkv_mix(rkv, lora, row1(rwkv_mu[l]), row1(rwkv_w0[l]), rwkv_w2[l], row1(rwkv_a0[l]),
                         rwkv_a2[l], rwkv_g2[l], row1(rwkv_k_k[l]), row1(rwkv_k_a[l]), row1(rwkv_r_k[l]),
                         row1(rwkv_ln_g[l]), row1(rwkv_ln_b[l]))
        o_ds = _dsa_mix(dq, iq, ikw, ckv, dsa_w_uk[l], dsa_w_uv[l], dsa_slopes)
        o_sw = _swa_mix(sq, skv, _pad_lanes(swa_sinks[l]), swa_slopes)
        x1, h2, eidx, rank, gates, counts = _post_mix(
            xs_cur, o_rw, o_ds, o_sw, w_out[l].astype(BF16), g1, row1(ln_mix_g[l]), row1(ln_mix_b[l]),
            sc2, sh2, router_w[l].T, router_bias[l].reshape(-1, 1))
        slot_tiles, block_e, n_used, cap = _moe_tables(eidx, rank, counts)
        xs_sorted = _dispatch(slot_tiles, h2, cap)
        ys = _experts(block_e, n_used, xs_sorted, exp_w1, exp_w3, exp_w2, l)
        xs_cur = _combine(slot_tiles, ys, x1, h2, gates.T, sh_w1[l].astype(BF16), sh_w3[l].astype(BF16),
                          sh_w2[l].astype(BF16), g2, row1(ln_ffn_g[l]), row1(ln_ffn_b[l]))
    return xs_cur[None]
```

```python
import functools
import math

import jax
import jax.numpy as jnp
import numpy as np
from jax import lax
from jax.experimental import pallas as pl
from jax.experimental.pallas import tpu as pltpu

F32 = jnp.float32
BF16 = jnp.bfloat16
I32 = jnp.int32
HI = lax.Precision.HIGHEST

D_MODEL = 1024
DEPTH = 4
HEAD_DIM = 64
RWKV_HEADS = 6
DSA_HEADS = 4
SWA_HEADS = 6
SWA_KV_HEADS = 2
RWKV_W = RWKV_HEADS * HEAD_DIM
DSA_W = DSA_HEADS * HEAD_DIM
SWA_W = SWA_HEADS * HEAD_DIM
DECAY_LORA = 64
AAA_LORA = 64
GATE_LORA = 128
GN_EPS = 64e-5
KV_LORA = 128
IDX_HEADS = 4
IDX_DIM = 64
TOPK_MAX = 256
WINDOW = 128
N_EXPERTS = 64
TOP_K = 8
N_GROUPS = 8
TOPK_GROUPS = 4
D_EXPERT = 256
ROUTED_SCALE = 2.5
ALPHA = (2 * DEPTH) ** 0.25
LN_EPS = 1e-5
NEG = -1e30
INT_MIN = -(2 ** 31)

LANES = 128
VMEM_LIMIT = 56 * 1024 * 1024

C_RKV = (0, 1152)
C_LORA = (1152, 1408)
C_DQ = (1408, 1664)
C_CKV = (1664, 1792)
C_IDX = (1792, 2176)
C_SQ = (2176, 2560)
C_SKV = (2560, 2816)
P_PAD = 2816
N_ORIG_BEFORE_PAD = 2116


def _dot(a, b, prec=None):
    return jnp.dot(a, b, preferred_element_type=F32, precision=prec)


def _dot_nt(a, b, prec=None):
    return lax.dot_general(a, b, (((1,), (1,)), ((), ())), preferred_element_type=F32, precision=prec)


def _split2(a):
    a_hi = a.astype(BF16)
    return a_hi, (a - a_hi.astype(F32)).astype(BF16)


def _bdot(a, b):
    return _dot(a.astype(BF16), b.astype(BF16))


def _bdot_nt(a, b):
    return _dot_nt(a.astype(BF16), b.astype(BF16))


def _dot2(a, b_exact):
    a_hi, a_lo = _split2(a)
    return _dot(a_hi, b_exact) + _dot(a_lo, b_exact)


def _dot2_l(a_exact, b):
    b_hi, b_lo = _split2(b)
    return _dot(a_exact, b_hi) + _dot(a_exact, b_lo)


def _dot3(a, b):
    a_hi, a_lo = _split2(a)
    b_hi, b_lo = _split2(b)
    return _dot(a_hi, b_hi) + (_dot(a_lo, b_hi) + _dot(a_hi, b_lo))


def _iota(shape, dim):
    return lax.broadcasted_iota(I32, shape, dim)


def _sigmoid(x):
    return 1.0 / (1.0 + jnp.exp(-x))


def _layer_norm_rows(v, g, b):
    mu = jnp.mean(v, axis=-1, keepdims=True)
    d = v - mu
    var = jnp.mean(d * d, axis=-1, keepdims=True)
    return d * lax.rsqrt(var + LN_EPS) * g + b


def _params(sem):
    return pltpu.CompilerParams(dimension_semantics=sem, vmem_limit_bytes=VMEM_LIMIT)


def _mod_kernel(c_ref, w_ref, b_ref, o_ref):
    c = c_ref[...]
    cond = c * _sigmoid(c)
    o_ref[0] = _dot(cond, w_ref[0], HI) + b_ref[0]


def _modulation(c, w_mod, b_mod):
    depth, d, d6 = w_mod.shape
    c8 = jnp.broadcast_to(c, (8, d))
    nj = d6 // d
    out = pl.pallas_call(
        _mod_kernel,
        grid=(depth, nj),
        in_specs=[
            pl.BlockSpec((8, d), lambda l, j: (0, 0)),
            pl.BlockSpec((1, d, d), lambda l, j: (l, 0, j)),
            pl.BlockSpec((1, 1, d), lambda l, j: (l, 0, j)),
        ],
        out_specs=pl.BlockSpec((1, 8, d), lambda l, j: (l, 0, j)),
        out_shape=jax.ShapeDtypeStruct((depth, 8, d6), F32),
        compiler_params=_params(("arbitrary", "arbitrary")),
        name="modulation",
    )(c8, w_mod, b_mod.reshape(depth, 1, d6))
    return out[:, 0:1, :]


def _proj_kernel(x_ref, sc_ref, sh_ref, w_ref, wlo_ref, kvn_ref, ikg_ref, ikb_ref,
                 rkv_ref, lora_ref, dq_ref, ckv_ref, iq_ref, ikw_ref, sq_ref, skv_ref):
    h = x_ref[...] * (1.0 + sc_ref[...]) + sh_ref[...]
    hb = h.astype(BF16)
    hl = (h - hb.astype(F32)).astype(BF16)

    def mm(c):
        return _dot(hb, w_ref[:, c[0]:c[1]])

    rkv_ref[...] = mm(C_RKV)
    lora_ref[...] = mm(C_LORA)
    dq_ref[...] = mm(C_DQ)
    sq_ref[...] = mm(C_SQ)
    skv_ref[...] = mm(C_SKV)
    ckv = mm(C_CKV)
    ckv_ref[...] = ckv * lax.rsqrt(jnp.mean(ckv * ckv, axis=-1, keepdims=True) + 1e-6) * kvn_ref[...]
    idx = mm(C_IDX) + _dot(hl, w_ref[:, C_IDX[0]:C_IDX[1]]) + _dot(hb, wlo_ref[...])
    iq_ref[...] = idx[:, 0:256]
    g3 = idx[:, 256:384]
    lane = _iota(g3.shape, 1)
    isk = lane < IDX_DIM
    mu = jnp.sum(jnp.where(isk, g3, 0.0), axis=-1, keepdims=True) * (1.0 / IDX_DIM)
    dk = jnp.where(isk, g3 - mu, 0.0)
    var = jnp.sum(dk * dk, axis=-1, keepdims=True) * (1.0 / IDX_DIM)
    ikn = dk * lax.rsqrt(var + LN_EPS) * ikg_ref[...] + ikb_ref[...]
    ikw_ref[...] = jnp.where(isk, ikn, g3 * (IDX_HEADS ** -0.5 * IDX_DIM ** -0.5))


def _input_proj(x, sc, sh, w_hi, w_idx_lo, kvn, ikg, ikb, tm=512):
    t, d = x.shape
    widths = [C_RKV, C_LORA, C_DQ, C_CKV, (0, 256), (0, 128), C_SQ, C_SKV]
    widths = [c[1] - c[0] for c in widths]
    const = lambda i: (0, 0)
    row = lambda i: (i, 0)
    return pl.pallas_call(
        _proj_kernel,
        grid=(t // tm,),
        in_specs=[
            pl.BlockSpec((tm, d), row),
            pl.BlockSpec((1, d), const),
            pl.BlockSpec((1, d), const),
            pl.BlockSpec((d, P_PAD), const),
            pl.BlockSpec((d, C_IDX[1] - C_IDX[0]), const),
            pl.BlockSpec((1, KV_LORA), const),
            pl.BlockSpec((1, LANES), const),
            pl.BlockSpec((1, LANES), const),
        ],
        out_specs=[pl.BlockSpec((tm, w), row) for w in widths],
        out_shape=[jax.ShapeDtypeStruct((t, w), F32) for w in widths],
        compiler_params=_params(("arbitrary",)),
        name="input_proj",
    )(x, sc, sh, w_hi, w_idx_lo, kvn, ikg, ikb)


RW_CHUNK = 64
RW_UNROLL = 2


def _rwkv_kernel(r_ref, k_ref, v_ref, lora_ref, rp_ref, kp_ref, vp_ref, lp_ref,
                 mur_ref, muk_ref, muv_ref, mul_ref, w0_ref, w2_ref, a0_ref, a2_ref, g2_ref,
                 kk_ref, ka_ref, rk_ref, lng_ref, lnb_ref, o_ref,
                 h_ref, y_ref, st_ref, wm_ref, ar_ref, rs_ref, vs_ref, lt_ref, zm_ref, y0_ref, gc_ref, *, tg):
    g = pl.program_id(0)
    c64 = RW_CHUNK
    nch = tg // c64
    npair = RWKV_W // LANES
    pair_lanes = [slice(p * LANES, (p + 1) * LANES) for p in range(npair)]
    lane = _iota((1, LANES), 1)
    first = g == 0

    @pl.when(first)
    def _():
        h_ref[...] = jnp.zeros_like(h_ref)

    rowid = _iota((tg, 1), 0)

    def shift_mix(cur_ref, prev_ref, mu_ref):
        cur = cur_ref[...]
        prev_row = jnp.where(first, 0.0, prev_ref[7:8, :])
        rolled = pltpu.roll(cur, 1, 0)
        shifted = jnp.where(rowid == 0, prev_row, rolled)
        return cur + (shifted - cur) * mu_ref[...]

    r = shift_mix(r_ref, rp_ref, mur_ref)
    k = shift_mix(k_ref, kp_ref, muk_ref)
    v = shift_mix(v_ref, vp_ref, muv_ref)
    lo = shift_mix(lora_ref, lp_ref, mul_ref)
    wl = lo[:, 0:DECAY_LORA]
    al = lo[:, DECAY_LORA:DECAY_LORA + AAA_LORA]
    gl = lo[:, 128:256]

    zw = -(w0_ref[...] + _dot3(jnp.tanh(wl), w2_ref[...]))
    softplus = jnp.maximum(zw, 0.0) + jnp.log(1.0 + jnp.exp(-jnp.abs(zw)))
    lw = -jnp.exp(-softplus - 0.5)
    a = _sigmoid(a0_ref[...] + _bdot(al, a2_ref[...]))
    gate = _bdot(_sigmoid(gl), g2_ref[...])

    ri = _iota((LANES, LANES), 0) // HEAD_DIM
    ci = _iota((LANES, LANES), 1) // HEAD_DIM
    bones = jnp.where(ri == ci, 1.0, 0.0).astype(BF16)

    def head_sum(xf):
        return jnp.concatenate([_dot2(xf[:, pl_], bones) for pl_ in pair_lanes], axis=1)

    kk = k * kk_ref[...]
    kk = kk / jnp.maximum(jnp.sqrt(head_sum(kk * kk)), 1e-12)
    k2 = k * (1.0 + (a - 1.0) * ka_ref[...])
    bonus = head_sum(r * k2 * rk_ref[...]) * v
    bvec = a * kk

    st_ref[0] = r
    st_ref[1] = k2
    st_ref[2] = v
    st_ref[3] = lw
    st_ref[4] = kk
    st_ref[5] = bvec

    rr = _iota((LANES, LANES), 0)
    cc = _iota((LANES, LANES), 1)
    same = (rr // c64) == (cc // c64)
    strict = same & ((rr % c64) > (cc % c64))
    incl = same & ((rr % c64) >= (cc % c64))
    eye = jnp.where(rr == cc, 1.0, 0.0)
    tril = jnp.where(_iota((c64, c64), 0) >= _iota((c64, c64), 1), 1.0, 0.0).astype(BF16)
    lo_half = lane < HEAD_DIM

    def stack(xc):
        return jnp.concatenate([jnp.where(lo_half, xc, 0.0), jnp.where(lo_half, 0.0, xc)], axis=0)

    def prepare(c, carry):
        chunks = [c * RW_UNROLL + j for j in range(RW_UNROLL)]
        sls = [pl.ds(pl.multiple_of(cj * c64, c64), c64) for cj in chunks]
        items = [(p, j) for j in range(RW_UNROLL) for p in range(npair)]
        pairs = range(len(items))
        idx = [p * nch + chunks[j] for p, j in items]
        ld = lambda q: [st_ref[q, sls[j], pair_lanes[p]] for p, j in items]
        rc, kc, vc, lwc, kkc, bc = ld(0), ld(1), ld(2), ld(3), ld(4), ld(5)
        cum = [_dot2_l(tril, lwc[p]) for p in pairs]
        tot = [cum[p][c64 - 1:c64, :] for p in pairs]
        g_in = [jnp.exp(cum[p]) for p in pairs]
        g_ex = [jnp.exp(cum[p] - lwc[p]) for p in pairs]
        g_inv = [jnp.exp(-cum[p]) for p in pairs]
        g_rest = [jnp.exp(tot[p] - cum[p]) for p in pairs]
        a_s = [stack(-kkc[p] * g_ex[p]).astype(BF16) for p in pairs]
        b_s = [stack(bc[p] * g_inv[p]).astype(BF16) for p in pairs]
        k_s = [stack(kc[p] * g_inv[p]).astype(BF16) for p in pairs]
        r_s = [stack(rc[p] * g_in[p]).astype(BF16) for p in pairs]
        v_s = [stack(vc[p]).astype(BF16) for p in pairs]
        nmat = [jnp.where(strict, _dot_nt(a_s[p], b_s[p]), 0.0) for p in pairs]
        aak = [jnp.where(strict, _dot_nt(a_s[p], k_s[p]), 0.0) for p in pairs]
        arb = [jnp.where(incl, _dot_nt(r_s[p], b_s[p]), 0.0) for p in pairs]
        ark = [jnp.where(incl, _dot_nt(r_s[p], k_s[p]), 0.0) for p in pairs]
        tinv = [eye + nmat[p] for p in pairs]
        pw = nmat
        for _ in range(5):
            pw = [_bdot(pw[p], pw[p]) for p in pairs]
            tinv = [_bdot(tinv[p], eye + pw[p]) for p in pairs]
        tinv = [tinv[p].astype(BF16) for p in pairs]
        akv = [_bdot(aak[p], v_s[p]).astype(BF16) for p in pairs]
        wmat = [_dot(tinv[p], a_s[p]) for p in pairs]
        zmat = [_dot(tinv[p], akv[p]) for p in pairs]
        y0 = [_bdot(ark[p], v_s[p]) for p in pairs]
        for p in pairs:
            wm_ref[idx[p]] = wmat[p].astype(BF16)
            zm_ref[idx[p]] = zmat[p]
            y0_ref[idx[p]] = y0[p]
            ar_ref[idx[p]] = arb[p].astype(BF16)
            rs_ref[idx[p]] = r_s[p]
            vs_ref[idx[p]] = v_s[p]
            lt_ref[idx[p]] = jnp.concatenate([stack(bc[p] * g_rest[p]), stack(kc[p] * g_rest[p])],
                                             axis=0).T.astype(BF16)
            gc_ref[idx[p]] = jnp.broadcast_to(jnp.sum(eye * jnp.exp(tot[p]), axis=1, keepdims=True),
                                              (LANES, LANES))
        return carry

    lax.fori_loop(0, nch // RW_UNROLL, prepare, 0)

    def advance(c, carry):
        sl = pl.ds(pl.multiple_of(c * c64, c64), c64)
        pairs = range(npair)
        idx = [p * nch + c for p in pairs]
        hst = [h_ref[p] for p in pairs]
        hb = [hst[p].astype(BF16) for p in pairs]
        u = [_dot(wm_ref[idx[p]], hb[p]) + zm_ref[idx[p]] for p in pairs]
        rh = [_dot(rs_ref[idx[p]], hb[p]) for p in pairs]
        ub = [u[p].astype(BF16) for p in pairs]
        hnew = [_dot(lt_ref[idx[p]], jnp.concatenate([ub[p], vs_ref[idx[p]]], axis=0)) for p in pairs]
        au = [_dot(ar_ref[idx[p]], ub[p]) for p in pairs]
        for p in pairs:
            h_ref[p] = gc_ref[idx[p]] * hst[p] + hnew[p]
            ys = rh[p] + au[p] + y0_ref[idx[p]]
            y_ref[sl, pair_lanes[p]] = ys[0:c64, :] + ys[c64:2 * c64, :]
        return carry

    lax.fori_loop(0, nch, advance, 0)

    y = y_ref[...]
    mean = head_sum(y) * (1.0 / HEAD_DIM)
    dy = y - mean
    var = head_sum(dy * dy) * (1.0 / HEAD_DIM)
    o = dy * lax.rsqrt(var + GN_EPS) * lng_ref[...] + lnb_ref[...]
    o_ref[...] = (o + bonus) * gate


def _rwkv_mix(rkv, lora, mu, w0, w2, a0, a2, g2, k_k, k_a, r_k, ln_g, ln_b, tg=512):
    t = rkv.shape[0]
    w = RWKV_W
    npair = w // LANES
    nmat = npair * (tg // RW_CHUNK)
    mu_r, mu_k, mu_v, mu_l = mu[:, 0:w], mu[:, w:2 * w], mu[:, 2 * w:3 * w], mu[:, 3 * w:3 * w + 256]
    blk = lambda off: pl.BlockSpec((tg, w), lambda g: (g, off))
    prev = lambda off: pl.BlockSpec((8, w), lambda g: (jnp.maximum(g * (tg // 8) - 1, 0), off))
    vec = pl.BlockSpec((1, w), lambda g: (0, 0))
    full = lambda rows: pl.BlockSpec((rows, w), lambda g: (0, 0))
    return pl.pallas_call(
        functools.partial(_rwkv_kernel, tg=tg),
        grid=(t // tg,),
        in_specs=[
            blk(0), blk(1), blk(2),
            pl.BlockSpec((tg, 256), lambda g: (g, 0)),
            prev(0), prev(1), prev(2),
            pl.BlockSpec((8, 256), lambda g: (jnp.maximum(g * (tg // 8) - 1, 0), 0)),
            vec, vec, vec,
            pl.BlockSpec((1, 256), lambda g: (0, 0)),
            vec, full(DECAY_LORA), vec, full(AAA_LORA), full(GATE_LORA),
            vec, vec, vec, vec, vec,
        ],
        out_specs=pl.BlockSpec((tg, w), lambda g: (g, 0)),
        out_shape=jax.ShapeDtypeStruct((t, w), F32),
        scratch_shapes=[
            pltpu.VMEM((npair, LANES, LANES), F32),
            pltpu.VMEM((tg, w), F32),
            pltpu.VMEM((6, tg, w), F32),
            pltpu.VMEM((nmat, LANES, LANES), BF16),
            pltpu.VMEM((nmat, LANES, LANES), BF16),
            pltpu.VMEM((nmat, LANES, LANES), BF16),
            pltpu.VMEM((nmat, LANES, LANES), BF16),
            pltpu.VMEM((nmat, LANES, 2 * LANES), BF16),
            pltpu.VMEM((nmat, LANES, LANES), F32),
            pltpu.VMEM((nmat, LANES, LANES), F32),
            pltpu.VMEM((nmat, LANES, LANES), F32),
        ],
        compiler_params=_params(("arbitrary",)),
        name="rwkv7_mix",
    )(rkv, rkv, rkv, lora, rkv, rkv, rkv, lora,
      mu_r, mu_k, mu_v, mu_l, w0, w2, a0, a2, g2, k_k, k_a, r_k, ln_g, ln_b)


DSA_QB = 128
DSA_KC = 1024
DSA_SUB = 512
CNT_ROWS = 64


def _float_key(v):
    bits = lax.bitcast_convert_type(v, I32)
    return bits ^ ((bits >> 31) & 0x7FFFFFFF)


def _dsa_kernel(dq_ref, iq_ref, ikw_ref, ikx_ref, kf_ref, vft_ref, wuk_ref, wuv_ref, tril_ref, slc_ref,
                o_ref, sc_ref, acc_ref):
    i = pl.program_id(0)
    qb, kc, sc_rows = DSA_QB, DSA_KC, DSA_SUB
    nh = DSA_HEADS
    t0 = i * qb
    nch = (t0 + qb + kc - 1) // kc
    tq = t0 + _iota((1, qb), 1)

    iq = iq_ref[...]
    iq_hi = iq.astype(BF16).astype(F32)
    iq_lo = iq - iq_hi
    lhs = []
    for h in range(IDX_HEADS):
        s = slice(h * IDX_DIM, (h + 1) * IDX_DIM)
        lhs.append(jnp.concatenate([iq_hi[:, s], iq_hi[:, s], iq_lo[:, s], iq_lo[:, s]], axis=1))
    lhs_t = jnp.concatenate(lhs, axis=0).T.astype(BF16)
    ikw_t = ikw_ref[...].T
    iw = [ikw_t[IDX_DIM + h:IDX_DIM + h + 1, :] for h in range(IDX_HEADS)]

    def score_body(ch, carry):
        m1, m2 = carry
        for sub in range(kc // sc_rows):
            k0 = pl.multiple_of(ch * kc + sub * sc_rows, sc_rows)
            s_all = _dot(ikx_ref[pl.ds(k0, sc_rows), :], lhs_t)
            acc = jnp.zeros((sc_rows, qb), F32)
            for h in range(IDX_HEADS):
                acc = acc + jnp.maximum(s_all[:, h * qb:(h + 1) * qb], 0.0) * iw[h]
            acc = jnp.where(acc == 0.0, 0.0, acc)
            causal = (k0 + _iota((sc_rows, 1), 0)) <= tq
            sc_ref[pl.ds(k0, sc_rows), :] = jnp.where(causal, _float_key(acc), INT_MIN)
            accm = jnp.where(causal, acc, -jnp.inf)
            for j in range(sc_rows // LANES):
                xj = accm[j * LANES:(j + 1) * LANES, :]
                m2 = jnp.maximum(m2, jnp.minimum(m1, xj))
                m1 = jnp.maximum(m1, xj)
        return m1, m2

    ninf = jnp.full((LANES, qb), -jnp.inf, F32)
    m1, m2 = lax.fori_loop(0, nch, score_body, (ninf, ninf))

    def count_ge(cand):
        def body(ch, acc):
            k0 = pl.multiple_of(ch * kc, kc)
            m = jnp.where(sc_ref[pl.ds(k0, kc), :] >= cand, 1.0, 0.0)
            for j in range(kc // CNT_ROWS):
                acc = acc + m[j * CNT_ROWS:(j + 1) * CNT_ROWS, :]
            return acc
        acc = lax.fori_loop(0, nch, body, jnp.zeros((CNT_ROWS, qb), F32))
        return jnp.sum(acc, axis=0, keepdims=True)

    k_row = jnp.minimum(tq + 1, TOPK_MAX).astype(F32)
    hi0 = _float_key(jnp.max(m1, axis=0, keepdims=True))
    lo0 = jnp.minimum(_float_key(jnp.min(m2, axis=0, keepdims=True)), hi0)
    c_pos = count_ge(jnp.ones((1, qb), I32))
    c_nonneg = count_ge(jnp.zeros((1, qb), I32))
    at_zero = (c_pos < k_row) & (c_nonneg >= k_row)
    above = c_pos >= k_row
    lo0 = jnp.where(at_zero, 0, jnp.where(above, jnp.maximum(lo0, 1), lo0))
    hi0 = jnp.where(at_zero, 0, jnp.where(above, hi0, jnp.minimum(hi0, -1)))
    lo0 = jnp.minimum(lo0, hi0)

    def open_rows(lo, hi):
        return jnp.max(jnp.where(lo < hi, 1.0, 0.0))

    def bis_body(st):
        lo, hi, _ = st
        mid = (lo | hi) - ((lo ^ hi) >> 1)
        c = count_ge(mid)
        ge = c >= k_row
        lo_n = jnp.where(ge, mid, lo)
        hi_n = jnp.where(c == k_row, mid, jnp.where(ge, hi, mid - 1))
        return lo_n, hi_n, open_rows(lo_n, hi_n)

    thr, _, _ = lax.while_loop(lambda st: st[2] > 0.5, bis_body, (lo0, hi0, open_rows(lo0, hi0)))
    n_ge = count_ge(thr)
    has_tie = jnp.max(jnp.where(n_ge > k_row, 1.0, 0.0)) > 0.5

    dq = dq_ref[...]
    slc = slc_ref[...]
    qaug = []
    for h in range(nh):
        ql = _bdot(dq[:, h * HEAD_DIM:(h + 1) * HEAD_DIM], wuk_ref[h]) * HEAD_DIM ** -0.5
        qaug.append(jnp.concatenate([ql, jnp.broadcast_to(slc[h:h + 1, :], (qb, LANES))], axis=1))
    qaug_t = jnp.concatenate(qaug, axis=0).T.astype(BF16)
    acc_ref[...] = jnp.zeros_like(acc_ref)

    nsub = kc // sc_rows

    def sub_starts(ch):
        return [pl.multiple_of(ch * kc + sub * sc_rows, sc_rows) for sub in range(nsub)]

    def logits(k0):
        return _dot(kf_ref[pl.ds(k0, sc_rows), :], qaug_t)

    def attend(k0, lg_all, sel, m_old):
        ps, m_new = [], []
        for h in range(nh):
            cols = slice(h * qb, (h + 1) * qb)
            lg = jnp.where(sel, lg_all[:, cols], NEG)
            mh = jnp.maximum(m_old[:, cols], jnp.max(lg, axis=0, keepdims=True))
            ps.append(jnp.exp((lg - mh).astype(BF16)))
            m_new.append(mh)
        m_new = jnp.concatenate(m_new, axis=1)
        pv = _dot(vft_ref[:, pl.ds(k0, sc_rows)], jnp.concatenate(ps, axis=1))
        acc_ref[...] = jnp.exp(m_old - m_new) * acc_ref[...] + pv
        return m_new

    m_init = jnp.full((1, nh * qb), NEG, F32)

    @pl.when(jnp.logical_not(has_tie))
    def _():
        def body(ch, m_old):
            ks = sub_starts(ch)
            lgs = [logits(k0) for k0 in ks]
            for k0, lg in zip(ks, lgs):
                m_old = attend(k0, lg, sc_ref[pl.ds(k0, sc_rows), :] >= thr, m_old)
            return m_old
        lax.fori_loop(0, nch, body, m_init)

    @pl.when(has_tie)
    def _():
        need = k_row - count_ge(thr + 1)
        tril = tril_ref[...]

        def body(ch, carry):
            tie_run, m_old = carry
            ks = sub_starts(ch)
            lgs = [logits(k0) for k0 in ks]
            keys = [sc_ref[pl.ds(k0, sc_rows), :] for k0 in ks]
            prefs = [_dot(tril, jnp.where(key == thr, 1.0, 0.0).astype(BF16)) for key in keys]
            for k0, lg, key, pref in zip(ks, lgs, keys, prefs):
                sel = (key > thr) | ((key == thr) & (tie_run + pref <= need))
                m_old = attend(k0, lg, sel, m_old)
                tie_run = tie_run + pref[sc_rows - 1:sc_rows, :]
            return tie_run, m_old
        lax.fori_loop(0, nch, body, (jnp.zeros((1, qb), F32), m_init))

    acc = acc_ref[...]
    o_lat = acc[0:KV_LORA, :] / acc[KV_LORA:KV_LORA + 1, :]
    outs = [_bdot(o_lat[:, h * qb:(h + 1) * qb].T, wuv_ref[h]) for h in range(nh)]
    o_ref[...] = jnp.concatenate(outs, axis=1)


DSA_VROWS = KV_LORA + 16


def _dsa_mix(dq, iq, ikw, ckv, w_uk, w_uv, slopes):
    t = dq.shape[0]
    assert t <= LANES * 256
    ikn = ikw[:, 0:IDX_DIM]
    ik_hi, ik_lo = _split2(ikn)
    ikx = jnp.concatenate([ik_hi, ik_lo, ik_hi, ik_lo], axis=1)
    ckv_b = ckv.astype(BF16)
    pos = jnp.arange(t, dtype=I32)
    pa = (pos // LANES).astype(BF16)[:, None]
    pb = (pos % LANES).astype(BF16)[:, None]
    kf = jnp.concatenate([ckv_b, pa, pa, pa, pb, pb, pb, jnp.zeros((t, LANES - 6), BF16)], axis=1)
    vft = jnp.concatenate([ckv_b.T, jnp.ones((1, t), BF16), jnp.zeros((DSA_VROWS - KV_LORA - 1, t), BF16)], axis=0)
    cols = []
    for sl in slopes:
        for coef in (sl * LANES, sl):
            c_hi = jnp.asarray(coef, F32).astype(BF16)
            r1 = jnp.asarray(coef, F32) - c_hi.astype(F32)
            c_mid = r1.astype(BF16)
            c_lo = (r1 - c_mid.astype(F32)).astype(BF16)
            cols += [c_hi.astype(F32), c_mid.astype(F32), c_lo.astype(F32)]
    slc = jnp.stack(cols).reshape(DSA_HEADS, 6)
    slc = jnp.pad(slc, ((0, 8 - DSA_HEADS), (0, LANES - 6)))
    assert t % DSA_KC == 0
    kc = DSA_SUB
    tril = jnp.asarray((np.arange(kc)[:, None] >= np.arange(kc)[None, :]).astype(np.float32), BF16)
    row = lambda i: (i, 0)
    const2 = lambda i: (0, 0)
    const3 = lambda i: (0, 0, 0)
    resident = lambda shape: pl.BlockSpec(shape, const2, pipeline_mode=pl.Buffered(1))
    return pl.pallas_call(
        _dsa_kernel,
        grid=(t // DSA_QB,),
        in_specs=[
            pl.BlockSpec((DSA_QB, DSA_W), row),
            pl.BlockSpec((DSA_QB, IDX_HEADS * IDX_DIM), row),
            pl.BlockSpec((DSA_QB, LANES), row),
            resident((t, 4 * IDX_DIM)),
            resident((t, 2 * LANES)),
            resident((DSA_VROWS, t)),
            pl.BlockSpec((DSA_HEADS, HEAD_DIM, KV_LORA), const3),
            pl.BlockSpec((DSA_HEADS, KV_LORA, HEAD_DIM), const3),
            resident((kc, kc)),
            pl.BlockSpec((8, LANES), const2),
        ],
        out_specs=pl.BlockSpec((DSA_QB, DSA_W), row),
        out_shape=jax.ShapeDtypeStruct((t, DSA_W), F32),
        scratch_shapes=[
            pltpu.VMEM((t, DSA_QB), I32),
            pltpu.VMEM((DSA_VROWS, DSA_HEADS * DSA_QB), F32),
        ],
        compiler_params=_params(("arbitrary",)),
        name="dsa_mix",
    )(dq, iq, ikw, ikx, kf, vft, w_uk, w_uv, tril, slc)


def _swa_kernel(q_ref, kv_ref, kvp_ref, sink_ref, o_ref, *, slopes):
    i = pl.program_id(0)
    w = WINDOW
    gsz = SWA_HEADS // SWA_KV_HEADS
    q = q_ref[...]
    kv = kv_ref[...]
    kvp = kvp_ref[...]
    qi = _iota((w, 2 * w), 0)
    kj = _iota((w, 2 * w), 1)
    dist = qi + w - kj
    valid = (dist >= 0) & (dist < w) & ((kj >= w) | (i > 0))
    distf = dist.astype(F32)
    sinks = sink_ref[...]
    k2, v2 = [], []
    for g in range(SWA_KV_HEADS):
        k2.append(jnp.concatenate([kvp[:, g * HEAD_DIM:(g + 1) * HEAD_DIM],
                                   kv[:, g * HEAD_DIM:(g + 1) * HEAD_DIM]], axis=0).astype(BF16))
        v2.append(jnp.concatenate([kvp[:, w + g * HEAD_DIM:w + (g + 1) * HEAD_DIM],
                                   kv[:, w + g * HEAD_DIM:w + (g + 1) * HEAD_DIM]], axis=0).astype(BF16))
    heads = range(SWA_HEADS)
    s = [_dot_nt(q[:, hd * HEAD_DIM:(hd + 1) * HEAD_DIM].astype(BF16), k2[hd // gsz]) * HEAD_DIM ** -0.5
         for hd in heads]
    s = [jnp.where(valid, s[hd] - slopes[hd] * distf, NEG) for hd in heads]
    sink = [sinks[0:1, hd:hd + 1] for hd in heads]
    m = [jnp.maximum(jnp.max(s[hd], axis=1, keepdims=True), sink[hd]) for hd in heads]
    e = [jnp.exp(s[hd] - m[hd]) for hd in heads]
    p = [e[hd] / (jnp.sum(e[hd], axis=1, keepdims=True) + jnp.exp(sink[hd] - m[hd])) for hd in heads]
    outs = [_dot(p[hd].astype(BF16), v2[hd // gsz]) for hd in heads]
    o_ref[...] = jnp.concatenate(outs, axis=1)


def _swa_mix(sq, skv, sinks, slopes):
    t = sq.shape[0]
    w = WINDOW
    return pl.pallas_call(
        functools.partial(_swa_kernel, slopes=slopes),
        grid=(t // w,),
        in_specs=[
            pl.BlockSpec((w, SWA_W), lambda i: (i, 0)),
            pl.BlockSpec((w, 2 * w), lambda i: (i, 0)),
            pl.BlockSpec((w, 2 * w), lambda i: (jnp.maximum(i - 1, 0), 0)),
            pl.BlockSpec((1, LANES), lambda i: (0, 0)),
        ],
        out_specs=pl.BlockSpec((w, SWA_W), lambda i: (i, 0)),
        out_shape=jax.ShapeDtypeStruct((t, SWA_W), F32),
        compiler_params=_params(("arbitrary",)),
        name="swa_mix",
    )(sq, skv, skv, sinks)


def _post_mix_kernel(x_ref, orw_ref, ods_ref, osw_ref, wout_ref, g1_ref, lng_ref, lnb_ref,
                     sc2_ref, sh2_ref, rwt_ref, rb_ref, tri_ref,
                     x1_ref, h2_ref, eidx_ref, rank_ref, gate_ref, cnt_ref, carry_ref):
    i = pl.program_id(0)

    @pl.when(i == 0)
    def _():
        carry_ref[...] = jnp.zeros_like(carry_ref)

    y = (_dot(orw_ref[...].astype(BF16), wout_ref[0:RWKV_W, :])
         + _dot(ods_ref[...].astype(BF16), wout_ref[RWKV_W:RWKV_W + DSA_W, :])
         + _dot(osw_ref[...].astype(BF16), wout_ref[RWKV_W + DSA_W:D_MODEL, :]))
    x1 = _layer_norm_rows(ALPHA * x_ref[...] + g1_ref[...] * y, lng_ref[...], lnb_ref[...])
    x1_ref[...] = x1
    h2 = x1 * (1.0 + sc2_ref[...]) + sh2_ref[...]
    h2_ref[...] = _pack_halves(h2)

    tm = h2.shape[0]
    ne = N_EXPERTS
    gs = ne // N_GROUPS
    scores = _sigmoid(_dot_nt(rwt_ref[...], h2, HI))
    sel = scores + rb_ref[...]
    sub = _iota((gs, tm), 0).astype(F32)
    gsc = []
    for j in range(N_GROUPS):
        gj = sel[j * gs:(j + 1) * gs, :]
        m1 = jnp.max(gj, axis=0, keepdims=True)
        f1 = jnp.min(jnp.where(gj == m1, sub, float(gs)), axis=0, keepdims=True)
        m2 = jnp.max(jnp.where(sub == f1, -jnp.inf, gj), axis=0, keepdims=True)
        gsc.append(m1 + m2)
    gsc = jnp.concatenate(gsc, axis=0)
    gid = _iota((N_GROUPS, tm), 0).astype(F32)
    gmask = jnp.zeros((N_GROUPS, tm), F32)
    for _ in range(TOPK_GROUPS):
        mx = jnp.max(gsc, axis=0, keepdims=True)
        fi = jnp.min(jnp.where(gsc == mx, gid, float(N_GROUPS)), axis=0, keepdims=True)
        pick = gid == fi
        gmask = jnp.where(pick, 1.0, gmask)
        gsc = jnp.where(pick, -jnp.inf, gsc)
    selm = jnp.concatenate(
        [jnp.where(gmask[j:j + 1, :] > 0.5, sel[j * gs:(j + 1) * gs, :], NEG) for j in range(N_GROUPS)], axis=0)
    eid = _iota((ne, tm), 0).astype(F32)
    gsel, eids = [], []
    chosen_f = jnp.zeros((ne, tm), F32)
    for _ in range(TOP_K):
        mx = jnp.max(selm, axis=0, keepdims=True)
        fi = jnp.min(jnp.where(selm == mx, eid, float(ne)), axis=0, keepdims=True)
        pick = eid == fi
        eids.append(fi)
        gsel.append(jnp.sum(jnp.where(pick, scores, 0.0), axis=0, keepdims=True))
        chosen_f = jnp.where(pick, 1.0, chosen_f)
        selm = jnp.where(pick, -jnp.inf, selm)
    gsum = gsel[0]
    for kx in range(1, TOP_K):
        gsum = gsum + gsel[kx]
    before = _dot(chosen_f.astype(BF16), tri_ref[...]) + carry_ref[:, 0:1]
    ranks = [jnp.sum(jnp.where(eid == eids[kx], before, 0.0), axis=0, keepdims=True) for kx in range(TOP_K)]
    eidx_ref[...] = jnp.concatenate(eids, axis=0).astype(I32)
    rank_ref[...] = jnp.concatenate(ranks, axis=0).astype(I32)
    gate_ref[...] = jnp.concatenate(gsel, axis=0) / gsum * ROUTED_SCALE
    carry_ref[...] = carry_ref[...] + jnp.sum(chosen_f, axis=1, keepdims=True)
    cnt_ref[...] = carry_ref[...]


def _post_mix(x, o_rw, o_ds, o_sw, w_out, g1, ln_g, ln_b, sc2, sh2, router_wt, router_b, tm=256):
    t, d = x.shape
    tri = (np.arange(tm)[:, None] < np.arange(tm)[None, :]).astype(np.float32)
    tri = jnp.asarray(tri, BF16)
    row = lambda i: (i, 0)
    const = lambda i: (0, 0)
    col = lambda i: (0, i)
    vec = pl.BlockSpec((1, d), const)
    return pl.pallas_call(
        _post_mix_kernel,
        grid=(t // tm,),
        in_specs=[
            pl.BlockSpec((tm, d), row),
            pl.BlockSpec((tm, RWKV_W), row),
            pl.BlockSpec((tm, DSA_W), row),
            pl.BlockSpec((tm, SWA_W), row),
            pl.BlockSpec((d, d), const),
            vec, vec, vec, vec, vec,
            pl.BlockSpec((N_EXPERTS, d), const),
            pl.BlockSpec((N_EXPERTS, 1), const),
            pl.BlockSpec((tm, tm), const),
        ],
        out_specs=[
            pl.BlockSpec((tm, d), row),
            pl.BlockSpec((tm, d // 2), row),
            pl.BlockSpec((TOP_K, tm), col),
            pl.BlockSpec((TOP_K, tm), col),
            pl.BlockSpec((TOP_K, tm), col),
            pl.BlockSpec((N_EXPERTS, LANES), const),
        ],
        out_shape=[
            jax.ShapeDtypeStruct((t, d), F32),
            jax.ShapeDtypeStruct((t, d // 2), I32),
            jax.ShapeDtypeStruct((TOP_K, t), I32),
            jax.ShapeDtypeStruct((TOP_K, t), I32),
            jax.ShapeDtypeStruct((TOP_K, t), F32),
            jax.ShapeDtypeStruct((N_EXPERTS, LANES), F32),
        ],
        scratch_shapes=[pltpu.VMEM((N_EXPERTS, LANES), F32)],
        compiler_params=_params(("arbitrary",)),
        name="post_mix_router",
    )(x, o_rw, o_ds, o_sw, w_out, g1, ln_g, ln_b, sc2, sh2, router_wt, router_b, tri)


MOE_ROWS = 512
MOE_TILE = 256


def _pack_halves(v):
    w = v.shape[1] // 2
    bits = lax.bitcast_convert_type(v.astype(BF16).astype(F32), I32)
    return bits[:, :w] | lax.shift_right_logical(bits[:, w:], 16)


def _unpack_halves(p):
    return lax.bitcast_convert_type(p & -65536, F32), lax.bitcast_convert_type(p << 16, F32)


def _row_copy(src_ref, src_row, dst_ref, dst_row, sem):
    return pltpu.make_async_copy(src_ref.at[pl.ds(src_row, 1), :], dst_ref.at[pl.ds(dst_row, 1), :], sem)


def _dispatch_kernel(slot_hbm, h_ref, xs_in, xs_out, slot_smem, sem_tab, sem_rows):
    del xs_in
    i = pl.program_id(0)
    tab = pltpu.make_async_copy(slot_hbm.at[i], slot_smem, sem_tab)
    tab.start()
    tab.wait()

    def issue(tt, carry):
        for kx in range(TOP_K):
            _row_copy(h_ref, tt, xs_out, slot_smem[kx, tt], sem_rows).start(priority=kx % 2)
        return carry

    lax.fori_loop(0, MOE_TILE, issue, 0)

    def drain(tt, carry):
        for kx in range(TOP_K):
            _row_copy(h_ref, 0, xs_out, 0, sem_rows).wait()
        return carry

    lax.fori_loop(0, MOE_TILE, drain, 0)


def _dispatch(slot_tiles, rows, cap):
    t, d = rows.shape
    xs0 = jnp.zeros((cap, d), rows.dtype)
    return pl.pallas_call(
        _dispatch_kernel,
        grid=(t // MOE_TILE,),
        in_specs=[
            pl.BlockSpec(memory_space=pl.ANY),
            pl.BlockSpec((MOE_TILE, d), lambda i: (i, 0)),
            pl.BlockSpec(memory_space=pl.ANY),
        ],
        out_specs=pl.BlockSpec(memory_space=pl.ANY),
        out_shape=jax.ShapeDtypeStruct((cap, d), rows.dtype),
        scratch_shapes=[
            pltpu.SMEM((TOP_K, MOE_TILE), I32),
            pltpu.SemaphoreType.DMA,
            pltpu.SemaphoreType.DMA,
        ],
        input_output_aliases={2: 0},
        compiler_params=_params(("arbitrary",)),
        name="moe_dispatch",
    )(slot_tiles, rows, xs0)


def _expert_kernel(be_ref, nb_ref, xs_ref, w1_ref, w3_ref, w2_ref, ys_ref, w1b, w3b, w2b):
    b = pl.program_id(0)
    changed = (b == 0) | (be_ref[b] != be_ref[jnp.maximum(b - 1, 0)])

    @pl.when(changed & (b < nb_ref[0]))
    def _():
        w1b[...] = w1_ref[0, 0].astype(BF16)
        w3b[...] = w3_ref[0, 0].astype(BF16)
        w2b[...] = w2_ref[0, 0].astype(BF16)

    @pl.when(b < nb_ref[0])
    def _():
        x_hi, x_lo = _unpack_halves(xs_ref[...])
        x_hi, x_lo = x_hi.astype(BF16), x_lo.astype(BF16)
        half = x_hi.shape[1]
        a = _dot(x_hi, w1b[0:half, :]) + _dot(x_lo, w1b[half:2 * half, :])
        gte = _dot(x_hi, w3b[0:half, :]) + _dot(x_lo, w3b[half:2 * half, :])
        hmid = (a * _sigmoid(a) * gte).astype(BF16)
        ys_ref[...] = _pack_halves(_dot(hmid, w2b[...]))

    @pl.when(b >= nb_ref[0])
    def _():
        ys_ref[...] = jnp.zeros_like(ys_ref)


def _experts(block_e, n_used, xs, w1, w3, w2, layer):
    cap, dp = xs.shape
    d = 2 * dp
    nb = cap // MOE_ROWS
    grid_spec = pltpu.PrefetchScalarGridSpec(
        num_scalar_prefetch=2,
        grid=(nb,),
        in_specs=[
            pl.BlockSpec((MOE_ROWS, dp), lambda b, be, nu: (b, 0)),
            pl.BlockSpec((1, 1, d, D_EXPERT), lambda b, be, nu: (layer, be[b], 0, 0)),
            pl.BlockSpec((1, 1, d, D_EXPERT), lambda b, be, nu: (layer, be[b], 0, 0)),
            pl.BlockSpec((1, 1, D_EXPERT, d), lambda b, be, nu: (layer, be[b], 0, 0)),
        ],
        out_specs=pl.BlockSpec((MOE_ROWS, dp), lambda b, be, nu: (b, 0)),
        scratch_shapes=[
            pltpu.VMEM((d, D_EXPERT), BF16),
            pltpu.VMEM((d, D_EXPERT), BF16),
            pltpu.VMEM((D_EXPERT, d), BF16),
        ],
    )
    return pl.pallas_call(
        _expert_kernel,
        grid_spec=grid_spec,
        out_shape=jax.ShapeDtypeStruct((cap, dp), I32),
        compiler_params=_params(("arbitrary",)),
        name="moe_experts",
    )(block_e, n_used, xs, w1, w3, w2)


def _combine_kernel(slot_hbm, ys_hbm, x1_ref, h2_ref, gate_ref, sw1_ref, sw3_ref, sw2_ref,
                    g2_ref, lng_ref, lnb_ref, o_ref, slot_smem, gbuf, sem_tab, sem_rows):
    i = pl.program_id(0)
    tab = pltpu.make_async_copy(slot_hbm.at[i], slot_smem, sem_tab)
    tab.start()
    tab.wait()

    def issue(tt, carry):
        for kx in range(TOP_K):
            _row_copy(ys_hbm, slot_smem[kx, tt], gbuf.at[kx], tt, sem_rows).start(priority=kx % 2)
        return carry

    lax.fori_loop(0, MOE_TILE, issue, 0)

    h_hi, h_lo = _unpack_halves(h2_ref[...])
    h_hi, h_lo = h_hi.astype(BF16), h_lo.astype(BF16)
    half = h_hi.shape[1]
    a = _dot(h_hi, sw1_ref[0:half, :]) + _dot(h_lo, sw1_ref[half:2 * half, :])
    gte = _dot(h_hi, sw3_ref[0:half, :]) + _dot(h_lo, sw3_ref[half:2 * half, :])
    y = _dot((a * _sigmoid(a) * gte).astype(BF16), sw2_ref[...])

    def drain(tt, carry):
        for kx in range(TOP_K):
            _row_copy(ys_hbm, 0, gbuf.at[kx], 0, sem_rows).wait()
        return carry

    lax.fori_loop(0, MOE_TILE, drain, 0)

    gates = gate_ref[...]
    r_hi = jnp.zeros((MOE_TILE, half), F32)
    r_lo = jnp.zeros((MOE_TILE, half), F32)
    for kx in range(TOP_K):
        e_hi, e_lo = _unpack_halves(gbuf[kx])
        r_hi = r_hi + gates[:, kx:kx + 1] * e_hi
        r_lo = r_lo + gates[:, kx:kx + 1] * e_lo
    y = y + jnp.concatenate([r_hi, r_lo], axis=1)
    o_ref[...] = _layer_norm_rows(ALPHA * x1_ref[...] + g2_ref[...] * y, lng_ref[...], lnb_ref[...])


def _combine(slot_tiles, ys, x1, h2, gates_t, sw1, sw3, sw2, g2, ln_g, ln_b):
    t, d = x1.shape
    row = lambda i: (i, 0)
    const = lambda i: (0, 0)
    vec = pl.BlockSpec((1, d), const)
    return pl.pallas_call(
        _combine_kernel,
        grid=(t // MOE_TILE,),
        in_specs=[
            pl.BlockSpec(memory_space=pl.ANY),
            pl.BlockSpec(memory_space=pl.ANY),
            pl.BlockSpec((MOE_TILE, d), row),
            pl.BlockSpec((MOE_TILE, d // 2), row),
            pl.BlockSpec((MOE_TILE, TOP_K), row),
            pl.BlockSpec((d, D_EXPERT), const),
            pl.BlockSpec((d, D_EXPERT), const),
            pl.BlockSpec((D_EXPERT, d), const),
            vec, vec, vec,
        ],
        out_specs=pl.BlockSpec((MOE_TILE, d), row),
        out_shape=jax.ShapeDtypeStruct((t, d), F32),
        scratch_shapes=[
            pltpu.SMEM((TOP_K, MOE_TILE), I32),
            pltpu.VMEM((TOP_K, MOE_TILE, d // 2), I32),
            pltpu.SemaphoreType.DMA,
            pltpu.SemaphoreType.DMA,
        ],
        compiler_params=_params(("arbitrary",)),
        name="moe_combine",
    )(slot_tiles, ys, x1, h2, gates_t, sw1, sw3, sw2, g2, ln_g, ln_b)


def _pad_w_in(w_in_l):
    d = w_in_l.shape[0]
    pad = jnp.zeros((d, C_SQ[0] - N_ORIG_BEFORE_PAD), w_in_l.dtype)
    return jnp.concatenate([w_in_l[:, :N_ORIG_BEFORE_PAD], pad, w_in_l[:, N_ORIG_BEFORE_PAD:]], axis=1)


def _pad_lanes(v, width=LANES):
    v = v.reshape(1, -1)
    return jnp.pad(v, ((0, 0), (0, width - v.shape[1])))


def _moe_tables(eidx, rank, counts):
    t = eidx.shape[1]
    cnt = counts[:, 0].astype(I32)
    padded = (cnt + MOE_ROWS - 1) // MOE_ROWS * MOE_ROWS
    pad_end = jnp.cumsum(padded)
    pad_start = pad_end - padded
    e_ids = jnp.arange(N_EXPERTS, dtype=I32)
    start_of = jnp.sum(jnp.where(eidx[..., None] == e_ids, pad_start, 0), axis=-1)
    slot = start_of + rank
    slot_tiles = slot.reshape(TOP_K, t // MOE_TILE, MOE_TILE).transpose(1, 0, 2)
    cap = t * TOP_K + N_EXPERTS * MOE_ROWS
    nb = cap // MOE_ROWS
    blk_row = jnp.arange(nb, dtype=I32)[:, None] * MOE_ROWS
    block_e = jnp.minimum(jnp.sum((pad_end[None, :] <= blk_row).astype(I32), axis=1), N_EXPERTS - 1)
    n_used = (pad_end[-1] // MOE_ROWS).astype(I32).reshape(1)
    return slot_tiles, block_e, n_used, cap


def kernel(x, c, w_mod, b_mod, w_in, rwkv_mu, rwkv_w0, rwkv_w2, rwkv_a0, rwkv_a2, rwkv_g2, rwkv_k_k, rwkv_k_a, rwkv_r_k, rwkv_ln_g, rwkv_ln_b, dsa_kv_norm, dsa_w_uk, dsa_w_uv, dsa_ik_g, dsa_ik_b, swa_sinks, w_out, ln_mix_g, ln_mix_b, router_w, router_bias, exp_w1, exp_w3, exp_w2, sh_w1, sh_w3, sh_w2, ln_ffn_g, ln_ffn_b):
    bsz, t, d = x.shape
    assert bsz == 1 and d == D_MODEL
    depth = w_mod.shape[0]
    n_sl = SWA_HEADS + DSA_HEADS
    slopes = [2.0 ** (-8.0 * (j + 1.0) / n_sl) for j in range(n_sl)]
    swa_slopes, dsa_slopes = slopes[:SWA_HEADS], slopes[SWA_HEADS:]

    mod = _modulation(c, w_mod, b_mod)
    xs_cur = x[0]
    row1 = lambda v: v.reshape(1, -1)
    for l in range(depth):
        sh1, sc1, g1, sh2, sc2, g2 = [mod[l, :, j * d:(j + 1) * d] for j in range(6)]
        wp = _pad_w_in(w_in[l])
        w_hi = wp.astype(BF16)
        w_idx = wp[:, C_IDX[0]:C_IDX[1]]
        w_idx_lo = (w_idx - w_idx.astype(BF16).astype(F32)).astype(BF16)
        rkv, lora, dq, ckv, iq, ikw, sq, skv = _input_proj(
            xs_cur, sc1, sh1, w_hi, w_idx_lo, row1(dsa_kv_norm[l]),
            _pad_lanes(dsa_ik_g[l]), _pad_lanes(dsa_ik_b[l]))
        o_rw = _rwkv_mix(rkv, lora, row1(rwkv_mu[l]), row1(rwkv_w0[l]), rwkv_w2[l], row1(rwkv_a0[l]),
                         rwkv_a2[l], rwkv_g2[l], row1(rwkv_k_k[l]), row1(rwkv_k_a[l]), row1(rwkv_r_k[l]),
                         row1(rwkv_ln_g[l]), row1(rwkv_ln_b[l]))
        o_ds = _dsa_mix(dq, iq, ikw, ckv, dsa_w_uk[l], dsa_w_uv[l], dsa_slopes)
        o_sw = _swa_mix(sq, skv, _pad_lanes(swa_sinks[l]), swa_slopes)
        x1, h2, eidx, rank, gates, counts = _post_mix(
            xs_cur, o_rw, o_ds, o_sw, w_out[l].astype(BF16), g1, row1(ln_mix_g[l]), row1(ln_mix_b[l]),
            sc2, sh2, router_w[l].T, router_bias[l].reshape(-1, 1))
        slot_tiles, block_e, n_used, cap = _moe_tables(eidx, rank, counts)
        xs_sorted = _dispatch(slot_tiles, h2, cap)
        ys = _experts(block_e, n_used, xs_sorted, exp_w1, exp_w3, exp_w2, l)
        xs_cur = _combine(slot_tiles, ys, x1, h2, gates.T, sh_w1[l].astype(BF16), sh_w3[l].astype(BF16),
                          sh_w2[l].astype(BF16), g2, row1(ln_ffn_g[l]), row1(ln_ffn_b[l]))
    return xs_cur[None]
```

```python
import functools
import math

import jax
import jax.numpy as jnp
import numpy as np
from jax import lax
from jax.experimental import pallas as pl
from jax.experimental.pallas import tpu as pltpu

F32 = jnp.float32
BF16 = jnp.bfloat16
I32 = jnp.int32
HI = lax.Precision.HIGHEST

D_MODEL = 1024
DEPTH = 4
HEAD_DIM = 64
RWKV_HEADS = 6
DSA_HEADS = 4
SWA_HEADS = 6
SWA_KV_HEADS = 2
RWKV_W = RWKV_HEADS * HEAD_DIM
DSA_W = DSA_HEADS * HEAD_DIM
SWA_W = SWA_HEADS * HEAD_DIM
DECAY_LORA = 64
AAA_LORA = 64
GATE_LORA = 128
GN_EPS = 64e-5
KV_LORA = 128
IDX_HEADS = 4
IDX_DIM = 64
TOPK_MAX = 256
WINDOW = 128
N_EXPERTS = 64
TOP_K = 8
N_GROUPS = 8
TOPK_GROUPS = 4
D_EXPERT = 256
ROUTED_SCALE = 2.5
ALPHA = (2 * DEPTH) ** 0.25
LN_EPS = 1e-5
NEG = -1e30
INT_MIN = -(2 ** 31)

LANES = 128
VMEM_LIMIT = 56 * 1024 * 1024

C_RKV = (0, 1152)
C_LORA = (1152, 1408)
C_DQ = (1408, 1664)
C_CKV = (1664, 1792)
C_IDX = (1792, 2176)
C_SQ = (2176, 2560)
C_SKV = (2560, 2816)
P_PAD = 2816
N_ORIG_BEFORE_PAD = 2116


def _dot(a, b, prec=None):
    return jnp.dot(a, b, preferred_element_type=F32, precision=prec)


def _dot_nt(a, b, prec=None):
    return lax.dot_general(a, b, (((1,), (1,)), ((), ())), preferred_element_type=F32, precision=prec)


def _split2(a):
    a_hi = a.astype(BF16)
    return a_hi, (a - a_hi.astype(F32)).astype(BF16)


def _bdot(a, b):
    return _dot(a.astype(BF16), b.astype(BF16))


def _bdot_nt(a, b):
    return _dot_nt(a.astype(BF16), b.astype(BF16))


def _dot2(a, b_exact):
    a_hi, a_lo = _split2(a)
    return _dot(a_hi, b_exact) + _dot(a_lo, b_exact)


def _dot2_l(a_exact, b):
    b_hi, b_lo = _split2(b)
    return _dot(a_exact, b_hi) + _dot(a_exact, b_lo)


def _dot3(a, b):
    a_hi, a_lo = _split2(a)
    b_hi, b_lo = _split2(b)
    return _dot(a_hi, b_hi) + (_dot(a_lo, b_hi) + _dot(a_hi, b_lo))


def _iota(shape, dim):
    return lax.broadcasted_iota(I32, shape, dim)


def _sigmoid(x):
    return 1.0 / (1.0 + jnp.exp(-x))


def _layer_norm_rows(v, g, b):
    mu = jnp.mean(v, axis=-1, keepdims=True)
    d = v - mu
    var = jnp.mean(d * d, axis=-1, keepdims=True)
    return d * lax.rsqrt(var + LN_EPS) * g + b


def _params(sem):
    return pltpu.CompilerParams(dimension_semantics=sem, vmem_limit_bytes=VMEM_LIMIT)


def _mod_kernel(c_ref, w_ref, b_ref, o_ref):
    c = c_ref[...]
    cond = c * _sigmoid(c)
    o_ref[0] = _dot(cond, w_ref[0], HI) + b_ref[0]


def _modulation(c, w_mod, b_mod):
    depth, d, d6 = w_mod.shape
    c8 = jnp.broadcast_to(c, (8, d))
    nj = d6 // d
    out = pl.pallas_call(
        _mod_kernel,
        grid=(depth, nj),
        in_specs=[
            pl.BlockSpec((8, d), lambda l, j: (0, 0)),
            pl.BlockSpec((1, d, d), lambda l, j: (l, 0, j)),
            pl.BlockSpec((1, 1, d), lambda l, j: (l, 0, j)),
        ],
        out_specs=pl.BlockSpec((1, 8, d), lambda l, j: (l, 0, j)),
        out_shape=jax.ShapeDtypeStruct((depth, 8, d6), F32),
        compiler_params=_params(("arbitrary", "arbitrary")),
        name="modulation",
    )(c8, w_mod, b_mod.reshape(depth, 1, d6))
    return out[:, 0:1, :]


def _proj_kernel(x_ref, sc_ref, sh_ref, w_ref, wlo_ref, kvn_ref, ikg_ref, ikb_ref,
                 rkv_ref, lora_ref, dq_ref, ckv_ref, iq_ref, ikw_ref, sq_ref, skv_ref):
    h = x_ref[...] * (1.0 + sc_ref[...]) + sh_ref[...]
    hb = h.astype(BF16)
    hl = (h - hb.astype(F32)).astype(BF16)

    def mm(c):
        return _dot(hb, w_ref[:, c[0]:c[1]])

    rkv_ref[...] = mm(C_RKV)
    lora_ref[...] = mm(C_LORA)
    dq_ref[...] = mm(C_DQ)
    sq_ref[...] = mm(C_SQ)
    skv_ref[...] = mm(C_SKV)
    ckv = mm(C_CKV)
    ckv_ref[...] = ckv * lax.rsqrt(jnp.mean(ckv * ckv, axis=-1, keepdims=True) + 1e-6) * kvn_ref[...]
    idx = mm(C_IDX) + _dot(hl, w_ref[:, C_IDX[0]:C_IDX[1]]) + _dot(hb, wlo_ref[...])
    iq_ref[...] = idx[:, 0:256]
    g3 = idx[:, 256:384]
    lane = _iota(g3.shape, 1)
    isk = lane < IDX_DIM
    mu = jnp.sum(jnp.where(isk, g3, 0.0), axis=-1, keepdims=True) * (1.0 / IDX_DIM)
    dk = jnp.where(isk, g3 - mu, 0.0)
    var = jnp.sum(dk * dk, axis=-1, keepdims=True) * (1.0 / IDX_DIM)
    ikn = dk * lax.rsqrt(var + LN_EPS) * ikg_ref[...] + ikb_ref[...]
    ikw_ref[...] = jnp.where(isk, ikn, g3 * (IDX_HEADS ** -0.5 * IDX_DIM ** -0.5))


def _input_proj(x, sc, sh, w_hi, w_idx_lo, kvn, ikg, ikb, tm=512):
    t, d = x.shape
    widths = [C_RKV, C_LORA, C_DQ, C_CKV, (0, 256), (0, 128), C_SQ, C_SKV]
    widths = [c[1] - c[0] for c in widths]
    const = lambda i: (0, 0)
    row = lambda i: (i, 0)
    return pl.pallas_call(
        _proj_kernel,
        grid=(t // tm,),
        in_specs=[
            pl.BlockSpec((tm, d), row),
            pl.BlockSpec((1, d), const),
            pl.BlockSpec((1, d), const),
            pl.BlockSpec((d, P_PAD), const),
            pl.BlockSpec((d, C_IDX[1] - C_IDX[0]), const),
            pl.BlockSpec((1, KV_LORA), const),
            pl.BlockSpec((1, LANES), const),
            pl.BlockSpec((1, LANES), const),
        ],
        out_specs=[pl.BlockSpec((tm, w), row) for w in widths],
        out_shape=[jax.ShapeDtypeStruct((t, w), F32) for w in widths],
        compiler_params=_params(("arbitrary",)),
        name="input_proj",
    )(x, sc, sh, w_hi, w_idx_lo, kvn, ikg, ikb)


RW_CHUNK = 64
RW_UNROLL = 2


def _rwkv_kernel(r_ref, k_ref, v_ref, lora_ref, rp_ref, kp_ref, vp_ref, lp_ref,
                 mur_ref, muk_ref, muv_ref, mul_ref, w0_ref, w2_ref, a0_ref, a2_ref, g2_ref,
                 kk_ref, ka_ref, rk_ref, lng_ref, lnb_ref, o_ref,
                 h_ref, y_ref, st_ref, wm_ref, ar_ref, rs_ref, vs_ref, lt_ref, zm_ref, y0_ref, gc_ref, *, tg):
    g = pl.program_id(0)
    c64 = RW_CHUNK
    nch = tg // c64
    npair = RWKV_W // LANES
    pair_lanes = [slice(p * LANES, (p + 1) * LANES) for p in range(npair)]
    lane = _iota((1, LANES), 1)
    first = g == 0

    @pl.when(first)
    def _():
        h_ref[...] = jnp.zeros_like(h_ref)

    rowid = _iota((tg, 1), 0)

    def shift_mix(cur_ref, prev_ref, mu_ref):
        cur = cur_ref[...]
        prev_row = jnp.where(first, 0.0, prev_ref[7:8, :])
        rolled = pltpu.roll(cur, 1, 0)
        shifted = jnp.where(rowid == 0, prev_row, rolled)
        return cur + (shifted - cur) * mu_ref[...]

    r = shift_mix(r_ref, rp_ref, mur_ref)
    k = shift_mix(k_ref, kp_ref, muk_ref)
    v = shift_mix(v_ref, vp_ref, muv_ref)
    lo = shift_mix(lora_ref, lp_ref, mul_ref)
    wl = lo[:, 0:DECAY_LORA]
    al = lo[:, DECAY_LORA:DECAY_LORA + AAA_LORA]
    gl = lo[:, 128:256]

    zw = -(w0_ref[...] + _dot3(jnp.tanh(wl), w2_ref[...]))
    softplus = jnp.maximum(zw, 0.0) + jnp.log(1.0 + jnp.exp(-jnp.abs(zw)))
    lw = -jnp.exp(-softplus - 0.5)
    a = _sigmoid(a0_ref[...] + _bdot(al, a2_ref[...]))
    gate = _bdot(_sigmoid(gl), g2_ref[...])

    ri = _iota((LANES, LANES), 0) // HEAD_DIM
    ci = _iota((LANES, LANES), 1) // HEAD_DIM
    bones = jnp.where(ri == ci, 1.0, 0.0).astype(BF16)

    def head_sum(xf):
        return jnp.concatenate([_dot2(xf[:, pl_], bones) for pl_ in pair_lanes], axis=1)

    kk = k * kk_ref[...]
    kk = kk / jnp.maximum(jnp.sqrt(head_sum(kk * kk)), 1e-12)
    k2 = k * (1.0 + (a - 1.0) * ka_ref[...])
    bonus = head_sum(r * k2 * rk_ref[...]) * v
    bvec = a * kk

    st_ref[0] = r
    st_ref[1] = k2
    st_ref[2] = v
    st_ref[3] = lw
    st_ref[4] = kk
    st_ref[5] = bvec

    rr = _iota((LANES, LANES), 0)
    cc = _iota((LANES, LANES), 1)
    same = (rr // c64) == (cc // c64)
    strict = same & ((rr % c64) > (cc % c64))
    incl = same & ((rr % c64) >= (cc % c64))
    eye = jnp.where(rr == cc, 1.0, 0.0)
    tril = jnp.where(_iota((c64, c64), 0) >= _iota((c64, c64), 1), 1.0, 0.0).astype(BF16)
    lo_half = lane < HEAD_DIM

    def stack(xc):
        return jnp.concatenate([jnp.where(lo_half, xc, 0.0), jnp.where(lo_half, 0.0, xc)], axis=0)

    def prepare(c, carry):
        chunks = [c * RW_UNROLL + j for j in range(RW_UNROLL)]
        sls = [pl.ds(pl.multiple_of(cj * c64, c64), c64) for cj in chunks]
        items = [(p, j) for j in range(RW_UNROLL) for p in range(npair)]
        pairs = range(len(items))
        idx = [p * nch + chunks[j] for p, j in items]
        ld = lambda q: [st_ref[q, sls[j], pair_lanes[p]] for p, j in items]
        rc, kc, vc, lwc, kkc, bc = ld(0), ld(1), ld(2), ld(3), ld(4), ld(5)
        cum = [_dot2_l(tril, lwc[p]) for p in pairs]
        tot = [cum[p][c64 - 1:c64, :] for p in pairs]
        g_in = [jnp.exp(cum[p]) for p in pairs]
        g_ex = [jnp.exp(cum[p] - lwc[p]) for p in pairs]
        g_inv = [jnp.exp(-cum[p]) for p in pairs]
        g_rest = [jnp.exp(tot[p] - cum[p]) for p in pairs]
        a_s = [stack(-kkc[p] * g_ex[p]).astype(BF16) for p in pairs]
        b_s = [stack(bc[p] * g_inv[p]).astype(BF16) for p in pairs]
        k_s = [stack(kc[p] * g_inv[p]).astype(BF16) for p in pairs]
        r_s = [stack(rc[p] * g_in[p]).astype(BF16) for p in pairs]
        v_s = [stack(vc[p]).astype(BF16) for p in pairs]
        nmat = [jnp.where(strict, _dot_nt(a_s[p], b_s[p]), 0.0) for p in pairs]
        aak = [jnp.where(strict, _dot_nt(a_s[p], k_s[p]), 0.0) for p in pairs]
        arb = [jnp.where(incl, _dot_nt(r_s[p], b_s[p]), 0.0) for p in pairs]
        ark = [jnp.where(incl, _dot_nt(r_s[p], k_s[p]), 0.0) for p in pairs]
        tinv = [eye + nmat[p] for p in pairs]
        pw = nmat
        for _ in range(5):
            pw = [_bdot(pw[p], pw[p]) for p in pairs]
            tinv = [_bdot(tinv[p], eye + pw[p]) for p in pairs]
        tinv = [tinv[p].astype(BF16) for p in pairs]
        akv = [_bdot(aak[p], v_s[p]).astype(BF16) for p in pairs]
        wmat = [_dot(tinv[p], a_s[p]) for p in pairs]
        zmat = [_dot(tinv[p], akv[p]) for p in pairs]
        y0 = [_bdot(ark[p], v_s[p]) for p in pairs]
        for p in pairs:
            wm_ref[idx[p]] = wmat[p].astype(BF16)
            zm_ref[idx[p]] = zmat[p]
            y0_ref[idx[p]] = y0[p]
            ar_ref[idx[p]] = arb[p].astype(BF16)
            rs_ref[idx[p]] = r_s[p]
            vs_ref[idx[p]] = v_s[p]
            lt_ref[idx[p]] = jnp.concatenate([stack(bc[p] * g_rest[p]), stack(kc[p] * g_rest[p])],
                                             axis=0).T.astype(BF16)
            gc_ref[idx[p]] = jnp.broadcast_to(jnp.sum(eye * jnp.exp(tot[p]), axis=1, keepdims=True),
                                              (LANES, LANES))
        return carry

    lax.fori_loop(0, nch // RW_UNROLL, prepare, 0)

    def advance(c, carry):
        sl = pl.ds(pl.multiple_of(c * c64, c64), c64)
        pairs = range(npair)
        idx = [p * nch + c for p in pairs]
        hst = [h_ref[p] for p in pairs]
        hb = [hst[p].astype(BF16) for p in pairs]
        u = [_dot(wm_ref[idx[p]], hb[p]) + zm_ref[idx[p]] for p in pairs]
        rh = [_dot(rs_ref[idx[p]], hb[p]) for p in pairs]
        ub = [u[p].astype(BF16) for p in pairs]
        hnew = [_dot(lt_ref[idx[p]], jnp.concatenate([ub[p], vs_ref[idx[p]]], axis=0)) for p in pairs]
        au = [_dot(ar_ref[idx[p]], ub[p]) for p in pairs]
        for p in pairs:
            h_ref[p] = gc_ref[idx[p]] * hst[p] + hnew[p]
            ys = rh[p] + au[p] + y0_ref[idx[p]]
            y_ref[sl, pair_lanes[p]] = ys[0:c64, :] + ys[c64:2 * c64, :]
        return carry

    lax.fori_loop(0, nch, advance, 0)

    y = y_ref[...]
    mean = head_sum(y) * (1.0 / HEAD_DIM)
    dy = y - mean
    var = head_sum(dy * dy) * (1.0 / HEAD_DIM)
    o = dy * lax.rsqrt(var + GN_EPS) * lng_ref[...] + lnb_ref[...]
    o_ref[...] = (o + bonus) * gate


def _rwkv_mix(rkv, lora, mu, w0, w2, a0, a2, g2, k_k, k_a, r_k, ln_g, ln_b, tg=512):
    t = rkv.shape[0]
    w = RWKV_W
    npair = w // LANES
    nmat = npair * (tg // RW_CHUNK)
    mu_r, mu_k, mu_v, mu_l = mu[:, 0:w], mu[:, w:2 * w], mu[:, 2 * w:3 * w], mu[:, 3 * w:3 * w + 256]
    blk = lambda off: pl.BlockSpec((tg, w), lambda g: (g, off))
    prev = lambda off: pl.BlockSpec((8, w), lambda g: (jnp.maximum(g * (tg // 8) - 1, 0), off))
    vec = pl.BlockSpec((1, w), lambda g: (0, 0))
    full = lambda rows: pl.BlockSpec((rows, w), lambda g: (0, 0))
    return pl.pallas_call(
        functools.partial(_rwkv_kernel, tg=tg),
        grid=(t // tg,),
        in_specs=[
            blk(0), blk(1), blk(2),
            pl.BlockSpec((tg, 256), lambda g: (g, 0)),
            prev(0), prev(1), prev(2),
            pl.BlockSpec((8, 256), lambda g: (jnp.maximum(g * (tg // 8) - 1, 0), 0)),
            vec, vec, vec,
            pl.BlockSpec((1, 256), lambda g: (0, 0)),
            vec, full(DECAY_LORA), vec, full(AAA_LORA), full(GATE_LORA),
            vec, vec, vec, vec, vec,
        ],
        out_specs=pl.BlockSpec((tg, w), lambda g: (g, 0)),
        out_shape=jax.ShapeDtypeStruct((t, w), F32),
        scratch_shapes=[
            pltpu.VMEM((npair, LANES, LANES), F32),
            pltpu.VMEM((tg, w), F32),
            pltpu.VMEM((6, tg, w), F32),
            pltpu.VMEM((nmat, LANES, LANES), BF16),
            pltpu.VMEM((nmat, LANES, LANES), BF16),
            pltpu.VMEM((nmat, LANES, LANES), BF16),
            pltpu.VMEM((nmat, LANES, LANES), BF16),
            pltpu.VMEM((nmat, LANES, 2 * LANES), BF16),
            pltpu.VMEM((nmat, LANES, LANES), F32),
            pltpu.VMEM((nmat, LANES, LANES), F32),
            pltpu.VMEM((nmat, LANES, LANES), F32),
        ],
        compiler_params=_params(("arbitrary",)),
        name="rwkv7_mix",
    )(rkv, rkv, rkv, lora, rkv, rkv, rkv, lora,
      mu_r, mu_k, mu_v, mu_l, w0, w2, a0, a2, g2, k_k, k_a, r_k, ln_g, ln_b)


DSA_QB = 128
DSA_KC = 1024
DSA_SUB = 512
CNT_ROWS = 64
TIE_BLK = 128


def _float_key(v):
    bits = lax.bitcast_convert_type(v, I32)
    return bits ^ ((bits >> 31) & 0x7FFFFFFF)


def _dsa_kernel(dq_ref, iq_ref, ikw_ref, ikx_ref, kf_ref, vft_ref, wuk_ref, wuv_ref, tril_ref, slc_ref,
                o_ref, sc_ref, acc_ref):
    i = pl.program_id(0)
    qb, kc, sc_rows = DSA_QB, DSA_KC, DSA_SUB
    nh = DSA_HEADS
    t0 = i * qb
    nch = (t0 + qb + kc - 1) // kc
    tq = t0 + _iota((1, qb), 1)

    iq = iq_ref[...]
    iq_hi = iq.astype(BF16).astype(F32)
    iq_lo = iq - iq_hi
    lhs = []
    for h in range(IDX_HEADS):
        s = slice(h * IDX_DIM, (h + 1) * IDX_DIM)
        lhs.append(jnp.concatenate([iq_hi[:, s], iq_hi[:, s], iq_lo[:, s], iq_lo[:, s]], axis=1))
    lhs_t = jnp.concatenate(lhs, axis=0).T.astype(BF16)
    ikw_t = ikw_ref[...].T
    iw = [ikw_t[IDX_DIM + h:IDX_DIM + h + 1, :] for h in range(IDX_HEADS)]

    def score_body(ch, carry, masked):
        m1, m2 = carry
        for sub in range(kc // sc_rows):
            k0 = pl.multiple_of(ch * kc + sub * sc_rows, sc_rows)
            s_all = _dot(ikx_ref[pl.ds(k0, sc_rows), :], lhs_t)
            acc = jnp.zeros((sc_rows, qb), F32)
            for h in range(IDX_HEADS):
                acc = acc + jnp.maximum(s_all[:, h * qb:(h + 1) * qb], 0.0) * iw[h]
            acc = jnp.where(acc == 0.0, 0.0, acc)
            key = _float_key(acc)
            if masked:
                causal = (k0 + _iota((sc_rows, 1), 0)) <= tq
                key = jnp.where(causal, key, INT_MIN)
                acc = jnp.where(causal, acc, -jnp.inf)
            sc_ref[pl.ds(k0, sc_rows), :] = key
            for j in range(sc_rows // LANES):
                xj = acc[j * LANES:(j + 1) * LANES, :]
                m2 = jnp.maximum(m2, jnp.minimum(m1, xj))
                m1 = jnp.maximum(m1, xj)
        return m1, m2

    ninf = jnp.full((LANES, qb), -jnp.inf, F32)
    n_below = t0 // kc
    top2 = lax.fori_loop(0, n_below, functools.partial(score_body, masked=False), (ninf, ninf))
    m1, m2 = lax.fori_loop(n_below, nch, functools.partial(score_body, masked=True), top2)

    def count_ge(cand):
        def body(ch, acc):
            k0 = pl.multiple_of(ch * kc, kc)
            m = jnp.where(sc_ref[pl.ds(k0, kc), :] >= cand, 1.0, 0.0)
            for j in range(kc // CNT_ROWS):
                acc = acc + m[j * CNT_ROWS:(j + 1) * CNT_ROWS, :]
            return acc
        acc = lax.fori_loop(0, nch, body, jnp.zeros((CNT_ROWS, qb), F32))
        return jnp.sum(acc, axis=0, keepdims=True)

    k_row = jnp.minimum(tq + 1, TOPK_MAX).astype(F32)
    hi0 = _float_key(jnp.max(m1, axis=0, keepdims=True))
    lo0 = jnp.minimum(_float_key(jnp.min(m2, axis=0, keepdims=True)), hi0)
    c_pos = count_ge(jnp.ones((1, qb), I32))
    c_nonneg = count_ge(jnp.zeros((1, qb), I32))
    at_zero = (c_pos < k_row) & (c_nonneg >= k_row)
    above = c_pos >= k_row
    lo0 = jnp.where(at_zero, 0, jnp.where(above, jnp.maximum(lo0, 1), lo0))
    hi0 = jnp.where(at_zero, 0, jnp.where(above, hi0, jnp.minimum(hi0, -1)))
    lo0 = jnp.minimum(lo0, hi0)

    def open_rows(lo, hi):
        return jnp.max(jnp.where(lo < hi, 1.0, 0.0))

    def bis_body(st):
        lo, hi, _ = st
        mid = (lo | hi) - ((lo ^ hi) >> 1)
        c = count_ge(mid)
        ge = c >= k_row
        lo_n = jnp.where(ge, mid, lo)
        hi_n = jnp.where(c == k_row, mid, jnp.where(ge, hi, mid - 1))
        return lo_n, hi_n, open_rows(lo_n, hi_n)

    thr, _, _ = lax.while_loop(lambda st: st[2] > 0.5, bis_body, (lo0, hi0, open_rows(lo0, hi0)))
    n_ge = count_ge(thr)
    has_tie = jnp.max(jnp.where(n_ge > k_row, 1.0, 0.0)) > 0.5

    dq = dq_ref[...]
    slc = slc_ref[...]
    qaug = []
    for h in range(nh):
        ql = _bdot(dq[:, h * HEAD_DIM:(h + 1) * HEAD_DIM], wuk_ref[h]) * HEAD_DIM ** -0.5
        qaug.append(jnp.concatenate([ql, jnp.broadcast_to(slc[h:h + 1, :], (qb, LANES))], axis=1))
    qaug_t = jnp.concatenate(qaug, axis=0).T.astype(BF16)
    acc_ref[...] = jnp.zeros_like(acc_ref)

    nsub = kc // sc_rows

    def sub_starts(ch):
        return [pl.multiple_of(ch * kc + sub * sc_rows, sc_rows) for sub in range(nsub)]

    def logits(k0):
        return _dot(kf_ref[pl.ds(k0, sc_rows), :], qaug_t)

    def attend(k0, lg_all, sel, m_old):
        ps, m_new = [], []
        for h in range(nh):
            cols = slice(h * qb, (h + 1) * qb)
            lg = jnp.where(sel, lg_all[:, cols], NEG)
            mh = jnp.maximum(m_old[:, cols], jnp.max(lg, axis=0, keepdims=True))
            ps.append(jnp.exp((lg - mh).astype(BF16)))
            m_new.append(mh)
        m_new = jnp.concatenate(m_new, axis=1)
        pv = _dot(vft_ref[:, pl.ds(k0, sc_rows)], jnp.concatenate(ps, axis=1))
        acc_ref[...] = jnp.exp(m_old - m_new) * acc_ref[...] + pv
        return m_new

    m_init = jnp.full((1, nh * qb), NEG, F32)

    @pl.when(jnp.logical_not(has_tie))
    def _():
        def body(ch, m_old):
            ks = sub_starts(ch)
            lgs = [logits(k0) for k0 in ks]
            for k0, lg in zip(ks, lgs):
                m_old = attend(k0, lg, sc_ref[pl.ds(k0, sc_rows), :] >= thr, m_old)
            return m_old
        lax.fori_loop(0, nch, body, m_init)

    @pl.when(has_tie)
    def _():
        need = k_row - count_ge(thr + 1)
        tril = tril_ref[...]

        def body(ch, carry):
            tie_run, m_old = carry
            ks = sub_starts(ch)
            lgs = [logits(k0) for k0 in ks]
            keys = [sc_ref[pl.ds(k0, sc_rows), :] for k0 in ks]
            blocks = [slice(j * TIE_BLK, (j + 1) * TIE_BLK) for j in range(sc_rows // TIE_BLK)]
            prefs = [[_dot(tril, jnp.where(key[bl, :] == thr, 1.0, 0.0).astype(BF16)) for bl in blocks] for key in keys]
            for k0, lg, key, pref in zip(ks, lgs, keys, prefs):
                ranks = []
                for pj in pref:
                    ranks.append(tie_run + pj)
                    tie_run = tie_run + pj[TIE_BLK - 1:TIE_BLK, :]
                sel = (key > thr) | ((key == thr) & (jnp.concatenate(ranks, axis=0) <= need))
                m_old = attend(k0, lg, sel, m_old)
            return tie_run, m_old
        lax.fori_loop(0, nch, body, (jnp.zeros((1, qb), F32), m_init))

    acc = acc_ref[...]
    o_lat = acc[0:KV_LORA, :] / acc[KV_LORA:KV_LORA + 1, :]
    outs = [_bdot(o_lat[:, h * qb:(h + 1) * qb].T, wuv_ref[h]) for h in range(nh)]
    o_ref[...] = jnp.concatenate(outs, axis=1)


DSA_VROWS = KV_LORA + 16


def _dsa_mix(dq, iq, ikw, ckv, w_uk, w_uv, slopes):
    t = dq.shape[0]
    assert t <= LANES * 256
    ikn = ikw[:, 0:IDX_DIM]
    ik_hi, ik_lo = _split2(ikn)
    ikx = jnp.concatenate([ik_hi, ik_lo, ik_hi, ik_lo], axis=1)
    ckv_b = ckv.astype(BF16)
    pos = jnp.arange(t, dtype=I32)
    pa = (pos // LANES).astype(BF16)[:, None]
    pb = (pos % LANES).astype(BF16)[:, None]
    kf = jnp.concatenate([ckv_b, pa, pa, pa, pb, pb, pb, jnp.zeros((t, LANES - 6), BF16)], axis=1)
    vft = jnp.concatenate([ckv_b.T, jnp.ones((1, t), BF16), jnp.zeros((DSA_VROWS - KV_LORA - 1, t), BF16)], axis=0)
    cols = []
    for sl in slopes:
        for coef in (sl * LANES, sl):
            c_hi = jnp.asarray(coef, F32).astype(BF16)
            r1 = jnp.asarray(coef, F32) - c_hi.astype(F32)
            c_mid = r1.astype(BF16)
            c_lo = (r1 - c_mid.astype(F32)).astype(BF16)
            cols += [c_hi.astype(F32), c_mid.astype(F32), c_lo.astype(F32)]
    slc = jnp.stack(cols).reshape(DSA_HEADS, 6)
    slc = jnp.pad(slc, ((0, 8 - DSA_HEADS), (0, LANES - 6)))
    assert t % DSA_KC == 0
    kc = TIE_BLK
    tril = jnp.asarray((np.arange(kc)[:, None] >= np.arange(kc)[None, :]).astype(np.float32), BF16)
    row = lambda i: (i, 0)
    const2 = lambda i: (0, 0)
    const3 = lambda i: (0, 0, 0)
    resident = lambda shape: pl.BlockSpec(shape, const2, pipeline_mode=pl.Buffered(1))
    return pl.pallas_call(
        _dsa_kernel,
        grid=(t // DSA_QB,),
        in_specs=[
            pl.BlockSpec((DSA_QB, DSA_W), row),
            pl.BlockSpec((DSA_QB, IDX_HEADS * IDX_DIM), row),
            pl.BlockSpec((DSA_QB, LANES), row),
            resident((t, 4 * IDX_DIM)),
            resident((t, 2 * LANES)),
            resident((DSA_VROWS, t)),
            pl.BlockSpec((DSA_HEADS, HEAD_DIM, KV_LORA), const3),
            pl.BlockSpec((DSA_HEADS, KV_LORA, HEAD_DIM), const3),
            resident((kc, kc)),
            pl.BlockSpec((8, LANES), const2),
        ],
        out_specs=pl.BlockSpec((DSA_QB, DSA_W), row),
        out_shape=jax.ShapeDtypeStruct((t, DSA_W), F32),
        scratch_shapes=[
            pltpu.VMEM((t, DSA_QB), I32),
            pltpu.VMEM((DSA_VROWS, DSA_HEADS * DSA_QB), F32),
        ],
        compiler_params=_params(("arbitrary",)),
        name="dsa_mix",
    )(dq, iq, ikw, ikx, kf, vft, w_uk, w_uv, tril, slc)


def _swa_kernel(q_ref, kv_ref, kvp_ref, sink_ref, o_ref, *, slopes):
    i = pl.program_id(0)
    w = WINDOW
    gsz = SWA_HEADS // SWA_KV_HEADS
    q = q_ref[...]
    kv = kv_ref[...]
    kvp = kvp_ref[...]
    qi = _iota((w, 2 * w), 0)
    kj = _iota((w, 2 * w), 1)
    dist = qi + w - kj
    valid = (dist >= 0) & (dist < w) & ((kj >= w) | (i > 0))
    distf = dist.astype(F32)
    sinks = sink_ref[...]
    k2, v2 = [], []
    for g in range(SWA_KV_HEADS):
        k2.append(jnp.concatenate([kvp[:, g * HEAD_DIM:(g + 1) * HEAD_DIM],
                                   kv[:, g * HEAD_DIM:(g + 1) * HEAD_DIM]], axis=0).astype(BF16))
        v2.append(jnp.concatenate([kvp[:, w + g * HEAD_DIM:w + (g + 1) * HEAD_DIM],
                                   kv[:, w + g * HEAD_DIM:w + (g + 1) * HEAD_DIM]], axis=0).astype(BF16))
    heads = range(SWA_HEADS)
    s = [_dot_nt(q[:, hd * HEAD_DIM:(hd + 1) * HEAD_DIM].astype(BF16), k2[hd // gsz]) * HEAD_DIM ** -0.5
         for hd in heads]
    s = [jnp.where(valid, s[hd] - slopes[hd] * distf, NEG) for hd in heads]
    sink = [sinks[0:1, hd:hd + 1] for hd in heads]
    m = [jnp.maximum(jnp.max(s[hd], axis=1, keepdims=True), sink[hd]) for hd in heads]
    e = [jnp.exp(s[hd] - m[hd]) for hd in heads]
    p = [e[hd] / (jnp.sum(e[hd], axis=1, keepdims=True) + jnp.exp(sink[hd] - m[hd])) for hd in heads]
    outs = [_dot(p[hd].astype(BF16), v2[hd // gsz]) for hd in heads]
    o_ref[...] = jnp.concatenate(outs, axis=1)


def _swa_mix(sq, skv, sinks, slopes):
    t = sq.shape[0]
    w = WINDOW
    return pl.pallas_call(
        functools.partial(_swa_kernel, slopes=slopes),
        grid=(t // w,),
        in_specs=[
            pl.BlockSpec((w, SWA_W), lambda i: (i, 0)),
            pl.BlockSpec((w, 2 * w), lambda i: (i, 0)),
            pl.BlockSpec((w, 2 * w), lambda i: (jnp.maximum(i - 1, 0), 0)),
            pl.BlockSpec((1, LANES), lambda i: (0, 0)),
        ],
        out_specs=pl.BlockSpec((w, SWA_W), lambda i: (i, 0)),
        out_shape=jax.ShapeDtypeStruct((t, SWA_W), F32),
        compiler_params=_params(("arbitrary",)),
        name="swa_mix",
    )(sq, skv, skv, sinks)


def _post_mix_kernel(x_ref, orw_ref, ods_ref, osw_ref, wout_ref, g1_ref, lng_ref, lnb_ref,
                     sc2_ref, sh2_ref, rwt_ref, rb_ref, tri_ref,
                     x1_ref, h2_ref, eidx_ref, rank_ref, gate_ref, cnt_ref, carry_ref):
    i = pl.program_id(0)

    @pl.when(i == 0)
    def _():
        carry_ref[...] = jnp.zeros_like(carry_ref)

    y = (_dot(orw_ref[...].astype(BF16), wout_ref[0:RWKV_W, :])
         + _dot(ods_ref[...].astype(BF16), wout_ref[RWKV_W:RWKV_W + DSA_W, :])
         + _dot(osw_ref[...].astype(BF16), wout_ref[RWKV_W + DSA_W:D_MODEL, :]))
    x1 = _layer_norm_rows(ALPHA * x_ref[...] + g1_ref[...] * y, lng_ref[...], lnb_ref[...])
    x1_ref[...] = x1
    h2 = x1 * (1.0 + sc2_ref[...]) + sh2_ref[...]
    h2_ref[...] = _pack_halves(h2)

    tm = h2.shape[0]
    ne = N_EXPERTS
    gs = ne // N_GROUPS
    scores = _sigmoid(_dot_nt(rwt_ref[...], h2, HI))
    sel = scores + rb_ref[...]
    sub = _iota((gs, tm), 0).astype(F32)
    gsc = []
    for j in range(N_GROUPS):
        gj = sel[j * gs:(j + 1) * gs, :]
        m1 = jnp.max(gj, axis=0, keepdims=True)
        f1 = jnp.min(jnp.where(gj == m1, sub, float(gs)), axis=0, keepdims=True)
        m2 = jnp.max(jnp.where(sub == f1, -jnp.inf, gj), axis=0, keepdims=True)
        gsc.append(m1 + m2)
    gsc = jnp.concatenate(gsc, axis=0)
    gid = _iota((N_GROUPS, tm), 0).astype(F32)
    gmask = jnp.zeros((N_GROUPS, tm), F32)
    for _ in range(TOPK_GROUPS):
        mx = jnp.max(gsc, axis=0, keepdims=True)
        fi = jnp.min(jnp.where(gsc == mx, gid, float(N_GROUPS)), axis=0, keepdims=True)
        pick = gid == fi
        gmask = jnp.where(pick, 1.0, gmask)
        gsc = jnp.where(pick, -jnp.inf, gsc)
    selm = jnp.concatenate(
        [jnp.where(gmask[j:j + 1, :] > 0.5, sel[j * gs:(j + 1) * gs, :], NEG) for j in range(N_GROUPS)], axis=0)
    eid = _iota((ne, tm), 0).astype(F32)
    gsel, eids = [], []
    chosen_f = jnp.zeros((ne, tm), F32)
    for _ in range(TOP_K):
        mx = jnp.max(selm, axis=0, keepdims=True)
        fi = jnp.min(jnp.where(selm == mx, eid, float(ne)), axis=0, keepdims=True)
        pick = eid == fi
        eids.append(fi)
        gsel.append(jnp.sum(jnp.where(pick, scores, 0.0), axis=0, keepdims=True))
        chosen_f = jnp.where(pick, 1.0, chosen_f)
        selm = jnp.where(pick, -jnp.inf, selm)
    gsum = gsel[0]
    for kx in range(1, TOP_K):
        gsum = gsum + gsel[kx]
    before = _dot(chosen_f.astype(BF16), tri_ref[...]) + carry_ref[:, 0:1]
    ranks = [jnp.sum(jnp.where(eid == eids[kx], before, 0.0), axis=0, keepdims=True) for kx in range(TOP_K)]
    eidx_ref[...] = jnp.concatenate(eids, axis=0).astype(I32)
    rank_ref[...] = jnp.concatenate(ranks, axis=0).astype(I32)
    gate_ref[...] = jnp.concatenate(gsel, axis=0) / gsum * ROUTED_SCALE
    carry_ref[...] = carry_ref[...] + jnp.sum(chosen_f, axis=1, keepdims=True)
    cnt_ref[...] = carry_ref[...]


def _post_mix(x, o_rw, o_ds, o_sw, w_out, g1, ln_g, ln_b, sc2, sh2, router_wt, router_b, tm=512):
    t, d = x.shape
    tri = (np.arange(tm)[:, None] < np.arange(tm)[None, :]).astype(np.float32)
    tri = jnp.asarray(tri, BF16)
    row = lambda i: (i, 0)
    const = lambda i: (0, 0)
    col = lambda i: (0, i)
    vec = pl.BlockSpec((1, d), const)
    return pl.pallas_call(
        _post_mix_kernel,
        grid=(t // tm,),
        in_specs=[
            pl.BlockSpec((tm, d), row),
            pl.BlockSpec((tm, RWKV_W), row),
            pl.BlockSpec((tm, DSA_W), row),
            pl.BlockSpec((tm, SWA_W), row),
            pl.BlockSpec((d, d), const),
            vec, vec, vec, vec, vec,
            pl.BlockSpec((N_EXPERTS, d), const),
            pl.BlockSpec((N_EXPERTS, 1), const),
            pl.BlockSpec((tm, tm), const),
        ],
        out_specs=[
            pl.BlockSpec((tm, d), row),
            pl.BlockSpec((tm, d // 2), row),
            pl.BlockSpec((TOP_K, tm), col),
            pl.BlockSpec((TOP_K, tm), col),
            pl.BlockSpec((TOP_K, tm), col),
            pl.BlockSpec((N_EXPERTS, LANES), const),
        ],
        out_shape=[
            jax.ShapeDtypeStruct((t, d), F32),
            jax.ShapeDtypeStruct((t, d // 2), I32),
            jax.ShapeDtypeStruct((TOP_K, t), I32),
            jax.ShapeDtypeStruct((TOP_K, t), I32),
            jax.ShapeDtypeStruct((TOP_K, t), F32),
            jax.ShapeDtypeStruct((N_EXPERTS, LANES), F32),
        ],
        scratch_shapes=[pltpu.VMEM((N_EXPERTS, LANES), F32)],
        compiler_params=_params(("arbitrary",)),
        name="post_mix_router",
    )(x, o_rw, o_ds, o_sw, w_out, g1, ln_g, ln_b, sc2, sh2, router_wt, router_b, tri)


MOE_ROWS = 512
MOE_TILE = 256


def _pack_halves(v):
    w = v.shape[1] // 2
    bits = lax.bitcast_convert_type(v.astype(BF16).astype(F32), I32)
    return bits[:, :w] | lax.shift_right_logical(bits[:, w:], 16)


def _unpack_halves(p):
    return lax.bitcast_convert_type(p & -65536, F32), lax.bitcast_convert_type(p << 16, F32)


def _row_copy(src_ref, src_row, dst_ref, dst_row, sem):
    return pltpu.make_async_copy(src_ref.at[pl.ds(src_row, 1), :], dst_ref.at[pl.ds(dst_row, 1), :], sem)


def _dispatch_kernel(slot_hbm, h_ref, xs_in, xs_out, slot_smem, sem_tab, sem_rows):
    del xs_in
    i = pl.program_id(0)
    tab = pltpu.make_async_copy(slot_hbm.at[i], slot_smem, sem_tab)
    tab.start()
    tab.wait()

    def issue(tt, carry):
        for kx in range(TOP_K):
            _row_copy(h_ref, tt, xs_out, slot_smem[kx, tt], sem_rows).start(priority=kx % 2)
        return carry

    lax.fori_loop(0, MOE_TILE, issue, 0)

    def drain(tt, carry):
        for kx in range(TOP_K):
            _row_copy(h_ref, 0, xs_out, 0, sem_rows).wait()
        return carry

    lax.fori_loop(0, MOE_TILE, drain, 0)


def _dispatch(slot_tiles, rows, cap):
    t, d = rows.shape
    xs0 = jnp.zeros((cap, d), rows.dtype)
    return pl.pallas_call(
        _dispatch_kernel,
        grid=(t // MOE_TILE,),
        in_specs=[
            pl.BlockSpec(memory_space=pl.ANY),
            pl.BlockSpec((MOE_TILE, d), lambda i: (i, 0)),
            pl.BlockSpec(memory_space=pl.ANY),
        ],
        out_specs=pl.BlockSpec(memory_space=pl.ANY),
        out_shape=jax.ShapeDtypeStruct((cap, d), rows.dtype),
        scratch_shapes=[
            pltpu.SMEM((TOP_K, MOE_TILE), I32),
            pltpu.SemaphoreType.DMA,
            pltpu.SemaphoreType.DMA,
        ],
        input_output_aliases={2: 0},
        compiler_params=_params(("arbitrary",)),
        name="moe_dispatch",
    )(slot_tiles, rows, xs0)


def _expert_kernel(be_ref, nb_ref, xs_ref, w1_ref, w3_ref, w2_ref, ys_ref, w1b, w3b, w2b):
    b = pl.program_id(0)
    changed = (b == 0) | (be_ref[b] != be_ref[jnp.maximum(b - 1, 0)])

    @pl.when(changed & (b < nb_ref[0]))
    def _():
        w1b[...] = w1_ref[0, 0].astype(BF16)
        w3b[...] = w3_ref[0, 0].astype(BF16)
        w2b[...] = w2_ref[0, 0].astype(BF16)

    @pl.when(b < nb_ref[0])
    def _():
        x_hi, x_lo = _unpack_halves(xs_ref[...])
        x_hi, x_lo = x_hi.astype(BF16), x_lo.astype(BF16)
        half = x_hi.shape[1]
        a = _dot(x_hi, w1b[0:half, :]) + _dot(x_lo, w1b[half:2 * half, :])
        gte = _dot(x_hi, w3b[0:half, :]) + _dot(x_lo, w3b[half:2 * half, :])
        hmid = (a * _sigmoid(a) * gte).astype(BF16)
        ys_ref[...] = _pack_halves(_dot(hmid, w2b[...]))

    @pl.when(b >= nb_ref[0])
    def _():
        ys_ref[...] = jnp.zeros_like(ys_ref)


def _experts(block_e, n_used, xs, w1, w3, w2, layer):
    cap, dp = xs.shape
    d = 2 * dp
    nb = cap // MOE_ROWS
    grid_spec = pltpu.PrefetchScalarGridSpec(
        num_scalar_prefetch=2,
        grid=(nb,),
        in_specs=[
            pl.BlockSpec((MOE_ROWS, dp), lambda b, be, nu: (b, 0)),
            pl.BlockSpec((1, 1, d, D_EXPERT), lambda b, be, nu: (layer, be[b], 0, 0)),
            pl.BlockSpec((1, 1, d, D_EXPERT), lambda b, be, nu: (layer, be[b], 0, 0)),
            pl.BlockSpec((1, 1, D_EXPERT, d), lambda b, be, nu: (layer, be[b], 0, 0)),
        ],
        out_specs=pl.BlockSpec((MOE_ROWS, dp), lambda b, be, nu: (b, 0)),
        scratch_shapes=[
            pltpu.VMEM((d, D_EXPERT), BF16),
            pltpu.VMEM((d, D_EXPERT), BF16),
            pltpu.VMEM((D_EXPERT, d), BF16),
        ],
    )
    return pl.pallas_call(
        _expert_kernel,
        grid_spec=grid_spec,
        out_shape=jax.ShapeDtypeStruct((cap, dp), I32),
        compiler_params=_params(("arbitrary",)),
        name="moe_experts",
    )(block_e, n_used, xs, w1, w3, w2)


def _combine_kernel(slot_hbm, ys_hbm, x1_ref, h2_ref, gate_ref, sw1_ref, sw3_ref, sw2_ref,
                    g2_ref, lng_ref, lnb_ref, o_ref, slot_smem, gbuf, sem_tab, sem_rows):
    i = pl.program_id(0)
    tab = pltpu.make_async_copy(slot_hbm.at[i], slot_smem, sem_tab)
    tab.start()
    tab.wait()

    def issue(tt, carry):
        for kx in range(TOP_K):
            _row_copy(ys_hbm, slot_smem[kx, tt], gbuf.at[kx], tt, sem_rows).start(priority=kx % 2)
        return carry

    lax.fori_loop(0, MOE_TILE, issue, 0)

    h_hi, h_lo = _unpack_halves(h2_ref[...])
    h_hi, h_lo = h_hi.astype(BF16), h_lo.astype(BF16)
    half = h_hi.shape[1]
    a = _dot(h_hi, sw1_ref[0:half, :]) + _dot(h_lo, sw1_ref[half:2 * half, :])
    gte = _dot(h_hi, sw3_ref[0:half, :]) + _dot(h_lo, sw3_ref[half:2 * half, :])
    y = _dot((a * _sigmoid(a) * gte).astype(BF16), sw2_ref[...])

    def drain(tt, carry):
        for kx in range(TOP_K):
            _row_copy(ys_hbm, 0, gbuf.at[kx], 0, sem_rows).wait()
        return carry

    lax.fori_loop(0, MOE_TILE, drain, 0)

    gates = gate_ref[...]
    r_hi = jnp.zeros((MOE_TILE, half), F32)
    r_lo = jnp.zeros((MOE_TILE, half), F32)
    for kx in range(TOP_K):
        e_hi, e_lo = _unpack_halves(gbuf[kx])
        r_hi = r_hi + gates[:, kx:kx + 1] * e_hi
        r_lo = r_lo + gates[:, kx:kx + 1] * e_lo
    y = y + jnp.concatenate([r_hi, r_lo], axis=1)
    o_ref[...] = _layer_norm_rows(ALPHA * x1_ref[...] + g2_ref[...] * y, lng_ref[...], lnb_ref[...])


def _combine(slot_tiles, ys, x1, h2, gates_t, sw1, sw3, sw2, g2, ln_g, ln_b):
    t, d = x1.shape
    row = lambda i: (i, 0)
    const = lambda i: (0, 0)
    vec = pl.BlockSpec((1, d), const)
    return pl.pallas_call(
        _combine_kernel,
        grid=(t // MOE_TILE,),
        in_specs=[
            pl.BlockSpec(memory_space=pl.ANY),
            pl.BlockSpec(memory_space=pl.ANY),
            pl.BlockSpec((MOE_TILE, d), row),
            pl.BlockSpec((MOE_TILE, d // 2), row),
            pl.BlockSpec((MOE_TILE, TOP_K), row),
            pl.BlockSpec((d, D_EXPERT), const),
            pl.BlockSpec((d, D_EXPERT), const),
            pl.BlockSpec((D_EXPERT, d), const),
            vec, vec, vec,
        ],
        out_specs=pl.BlockSpec((MOE_TILE, d), row),
        out_shape=jax.ShapeDtypeStruct((t, d), F32),
        scratch_shapes=[
            pltpu.SMEM((TOP_K, MOE_TILE), I32),
            pltpu.VMEM((TOP_K, MOE_TILE, d // 2), I32),
            pltpu.SemaphoreType.DMA,
            pltpu.SemaphoreType.DMA,
        ],
        compiler_params=_params(("arbitrary",)),
        name="moe_combine",
    )(slot_tiles, ys, x1, h2, gates_t, sw1, sw3, sw2, g2, ln_g, ln_b)


def _pad_w_in(w_in_l):
    d = w_in_l.shape[0]
    pad = jnp.zeros((d, C_SQ[0] - N_ORIG_BEFORE_PAD), w_in_l.dtype)
    return jnp.concatenate([w_in_l[:, :N_ORIG_BEFORE_PAD], pad, w_in_l[:, N_ORIG_BEFORE_PAD:]], axis=1)


def _pad_lanes(v, width=LANES):
    v = v.reshape(1, -1)
    return jnp.pad(v, ((0, 0), (0, width - v.shape[1])))


def _moe_tables(eidx, rank, counts):
    t = eidx.shape[1]
    cnt = counts[:, 0].astype(I32)
    padded = (cnt + MOE_ROWS - 1) // MOE_ROWS * MOE_ROWS
    pad_end = jnp.cumsum(padded)
    pad_start = pad_end - padded
    e_ids = jnp.arange(N_EXPERTS, dtype=I32)
    start_of = jnp.sum(jnp.where(eidx[..., None] == e_ids, pad_start, 0), axis=-1)
    slot = start_of + rank
    slot_tiles = slot.reshape(TOP_K, t // MOE_TILE, MOE_TILE).transpose(1, 0, 2)
    cap = t * TOP_K + N_EXPERTS * MOE_ROWS
    nb = cap // MOE_ROWS
    blk_row = jnp.arange(nb, dtype=I32)[:, None] * MOE_ROWS
    block_e = jnp.minimum(jnp.sum((pad_end[None, :] <= blk_row).astype(I32), axis=1), N_EXPERTS - 1)
    n_used = (pad_end[-1] // MOE_ROWS).astype(I32).reshape(1)
    return slot_tiles, block_e, n_used, cap


def kernel(x, c, w_mod, b_mod, w_in, rwkv_mu, rwkv_w0, rwkv_w2, rwkv_a0, rwkv_a2, rwkv_g2, rwkv_k_k, rwkv_k_a, rwkv_r_k, rwkv_ln_g, rwkv_ln_b, dsa_kv_norm, dsa_w_uk, dsa_w_uv, dsa_ik_g, dsa_ik_b, swa_sinks, w_out, ln_mix_g, ln_mix_b, router_w, router_bias, exp_w1, exp_w3, exp_w2, sh_w1, sh_w3, sh_w2, ln_ffn_g, ln_ffn_b):
    bsz, t, d = x.shape
    assert bsz == 1 and d == D_MODEL
    depth = w_mod.shape[0]
    n_sl = SWA_HEADS + DSA_HEADS
    slopes = [2.0 ** (-8.0 * (j + 1.0) / n_sl) for j in range(n_sl)]
    swa_slopes, dsa_slopes = slopes[:SWA_HEADS], slopes[SWA_HEADS:]

    mod = _modulation(c, w_mod, b_mod)
    xs_cur = x[0]
    row1 = lambda v: v.reshape(1, -1)
    for l in range(depth):
        sh1, sc1, g1, sh2, sc2, g2 = [mod[l, :, j * d:(j + 1) * d] for j in range(6)]
        wp = _pad_w_in(w_in[l])
        w_hi = wp.astype(BF16)
        w_idx = wp[:, C_IDX[0]:C_IDX[1]]
        w_idx_lo = (w_idx - w_idx.astype(BF16).astype(F32)).astype(BF16)
        rkv, lora, dq, ckv, iq, ikw, sq, skv = _input_proj(
            xs_cur, sc1, sh1, w_hi, w_idx_lo, row1(dsa_kv_norm[l]),
            _pad_lanes(dsa_ik_g[l]), _pad_lanes(dsa_ik_b[l]))
        o_rw = _rwkv_mix(rkv, lora, row1(rwkv_mu[l]), row1(rwkv_w0[l]), rwkv_w2[l], row1(rwkv_a0[l]),
                         rwkv_a2[l], rwkv_g2[l], row1(rwkv_k_k[l]), row1(rwkv_k_a[l]), row1(rwkv_r_k[l]),
                         row1(rwkv_ln_g[l]), row1(rwkv_ln_b[l]))
        o_ds = _dsa_mix(dq, iq, ikw, ckv, dsa_w_uk[l], dsa_w_uv[l], dsa_slopes)
        o_sw = _swa_mix(sq, skv, _pad_lanes(swa_sinks[l]), swa_slopes)
        x1, h2, eidx, rank, gates, counts = _post_mix(
            xs_cur, o_rw, o_ds, o_sw, w_out[l].astype(BF16), g1, row1(ln_mix_g[l]), row1(ln_mix_b[l]),
            sc2, sh2, router_w[l].T, router_bias[l].reshape(-1, 1))
        slot_tiles, block_e, n_used, cap = _moe_tables(eidx, rank, counts)
        xs_sorted = _dispatch(slot_tiles, h2, cap)
        ys = _experts(block_e, n_used, xs_sorted, exp_w1, exp_w3, exp_w2, l)
        xs_cur = _combine(slot_tiles, ys, x1, h2, gates.T, sh_w1[l].astype(BF16), sh_w3[l].astype(BF16),
                          sh_w2[l].astype(BF16), g2, row1(ln_ffn_g[l]), row1(ln_ffn_b[l]))
    return xs_cur[None]
```

```python
import functools
import math

import jax
import jax.numpy as jnp
import numpy as np
from jax import lax
from jax.experimental import pallas as pl
from jax.experimental.pallas import tpu as pltpu

F32 = jnp.float32
BF16 = jnp.bfloat16
I32 = jnp.int32
HI = lax.Precision.HIGHEST

D_MODEL = 1024
DEPTH = 4
HEAD_DIM = 64
RWKV_HEADS = 6
DSA_HEADS = 4
SWA_HEADS = 6
SWA_KV_HEADS = 2
RWKV_W = RWKV_HEADS * HEAD_DIM
DSA_W = DSA_HEADS * HEAD_DIM
SWA_W = SWA_HEADS * HEAD_DIM
DECAY_LORA = 64
AAA_LORA = 64
GATE_LORA = 128
GN_EPS = 64e-5
KV_LORA = 128
IDX_HEADS = 4
IDX_DIM = 64
TOPK_MAX = 256
WINDOW = 128
N_EXPERTS = 64
TOP_K = 8
N_GROUPS = 8
TOPK_GROUPS = 4
D_EXPERT = 256
ROUTED_SCALE = 2.5
ALPHA = (2 * DEPTH) ** 0.25
LN_EPS = 1e-5
NEG = -1e30
INT_MIN = -(2 ** 31)

LANES = 128
VMEM_LIMIT = 56 * 1024 * 1024

C_RKV = (0, 1152)
C_LORA = (1152, 1408)
C_DQ = (1408, 1664)
C_CKV = (1664, 1792)
C_IDX = (1792, 2176)
C_SQ = (2176, 2560)
C_SKV = (2560, 2816)
P_PAD = 2816
N_ORIG_BEFORE_PAD = 2116


def _dot(a, b, prec=None):
    return jnp.dot(a, b, preferred_element_type=F32, precision=prec)


def _dot_nt(a, b, prec=None):
    return lax.dot_general(a, b, (((1,), (1,)), ((), ())), preferred_element_type=F32, precision=prec)


def _split2(a):
    a_hi = a.astype(BF16)
    return a_hi, (a - a_hi.astype(F32)).astype(BF16)


def _bdot(a, b):
    return _dot(a.astype(BF16), b.astype(BF16))


def _bdot_nt(a, b):
    return _dot_nt(a.astype(BF16), b.astype(BF16))


def _dot2(a, b_exact):
    a_hi, a_lo = _split2(a)
    return _dot(a_hi, b_exact) + _dot(a_lo, b_exact)


def _dot2_l(a_exact, b):
    b_hi, b_lo = _split2(b)
    return _dot(a_exact, b_hi) + _dot(a_exact, b_lo)


def _dot3(a, b):
    a_hi, a_lo = _split2(a)
    b_hi, b_lo = _split2(b)
    return _dot(a_hi, b_hi) + (_dot(a_lo, b_hi) + _dot(a_hi, b_lo))


def _iota(shape, dim):
    return lax.broadcasted_iota(I32, shape, dim)


def _sigmoid(x):
    return 1.0 / (1.0 + jnp.exp(-x))


def _layer_norm_rows(v, g, b):
    mu = jnp.mean(v, axis=-1, keepdims=True)
    d = v - mu
    var = jnp.mean(d * d, axis=-1, keepdims=True)
    return d * lax.rsqrt(var + LN_EPS) * g + b


def _params(sem):
    return pltpu.CompilerParams(dimension_semantics=sem, vmem_limit_bytes=VMEM_LIMIT)


def _mod_kernel(c_ref, w_ref, b_ref, o_ref):
    c = c_ref[...]
    cond = c * _sigmoid(c)
    o_ref[0] = _dot(cond, w_ref[0], HI) + b_ref[0]


def _modulation(c, w_mod, b_mod):
    depth, d, d6 = w_mod.shape
    c8 = jnp.broadcast_to(c, (8, d))
    nj = d6 // d
    out = pl.pallas_call(
        _mod_kernel,
        grid=(depth, nj),
        in_specs=[
            pl.BlockSpec((8, d), lambda l, j: (0, 0)),
            pl.BlockSpec((1, d, d), lambda l, j: (l, 0, j)),
            pl.BlockSpec((1, 1, d), lambda l, j: (l, 0, j)),
        ],
        out_specs=pl.BlockSpec((1, 8, d), lambda l, j: (l, 0, j)),
        out_shape=jax.ShapeDtypeStruct((depth, 8, d6), F32),
        compiler_params=_params(("arbitrary", "arbitrary")),
        name="modulation",
    )(c8, w_mod, b_mod.reshape(depth, 1, d6))
    return out[:, 0:1, :]


def _proj_kernel(x_ref, sc_ref, sh_ref, w_ref, wlo_ref, kvn_ref, ikg_ref, ikb_ref,
                 rkv_ref, lora_ref, dq_ref, ckv_ref, iq_ref, ikw_ref, sq_ref, skv_ref):
    h = x_ref[...] * (1.0 + sc_ref[...]) + sh_ref[...]
    hb = h.astype(BF16)
    hl = (h - hb.astype(F32)).astype(BF16)

    def mm(c):
        return _dot(hb, w_ref[:, c[0]:c[1]])

    rkv_ref[...] = mm(C_RKV)
    lora_ref[...] = mm(C_LORA)
    dq_ref[...] = mm(C_DQ)
    sq_ref[...] = mm(C_SQ)
    skv_ref[...] = mm(C_SKV)
    ckv = mm(C_CKV)
    ckv_ref[...] = ckv * lax.rsqrt(jnp.mean(ckv * ckv, axis=-1, keepdims=True) + 1e-6) * kvn_ref[...]
    idx = mm(C_IDX) + _dot(hl, w_ref[:, C_IDX[0]:C_IDX[1]]) + _dot(hb, wlo_ref[...])
    iq_ref[...] = idx[:, 0:256]
    g3 = idx[:, 256:384]
    lane = _iota(g3.shape, 1)
    isk = lane < IDX_DIM
    mu = jnp.sum(jnp.where(isk, g3, 0.0), axis=-1, keepdims=True) * (1.0 / IDX_DIM)
    dk = jnp.where(isk, g3 - mu, 0.0)
    var = jnp.sum(dk * dk, axis=-1, keepdims=True) * (1.0 / IDX_DIM)
    ikn = dk * lax.rsqrt(var + LN_EPS) * ikg_ref[...] + ikb_ref[...]
    ikw_ref[...] = jnp.where(isk, ikn, g3 * (IDX_HEADS ** -0.5 * IDX_DIM ** -0.5))


def _input_proj(x, sc, sh, w_hi, w_idx_lo, kvn, ikg, ikb, tm=512):
    t, d = x.shape
    widths = [C_RKV, C_LORA, C_DQ, C_CKV, (0, 256), (0, 128), C_SQ, C_SKV]
    widths = [c[1] - c[0] for c in widths]
    const = lambda i: (0, 0)
    row = lambda i: (i, 0)
    return pl.pallas_call(
        _proj_kernel,
        grid=(t // tm,),
        in_specs=[
            pl.BlockSpec((tm, d), row),
            pl.BlockSpec((1, d), const),
            pl.BlockSpec((1, d), const),
            pl.BlockSpec((d, P_PAD), const),
            pl.BlockSpec((d, C_IDX[1] - C_IDX[0]), const),
            pl.BlockSpec((1, KV_LORA), const),
            pl.BlockSpec((1, LANES), const),
            pl.BlockSpec((1, LANES), const),
        ],
        out_specs=[pl.BlockSpec((tm, w), row) for w in widths],
        out_shape=[jax.ShapeDtypeStruct((t, w), F32) for w in widths],
        compiler_params=_params(("arbitrary",)),
        name="input_proj",
    )(x, sc, sh, w_hi, w_idx_lo, kvn, ikg, ikb)


RW_CHUNK = 64
RW_UNROLL = 2


def _rwkv_kernel(r_ref, k_ref, v_ref, lora_ref, rp_ref, kp_ref, vp_ref, lp_ref,
                 mur_ref, muk_ref, muv_ref, mul_ref, w0_ref, w2_ref, a0_ref, a2_ref, g2_ref,
                 kk_ref, ka_ref, rk_ref, lng_ref, lnb_ref, o_ref,
                 h_ref, y_ref, st_ref, wm_ref, ar_ref, rs_ref, vs_ref, lt_ref, zm_ref, y0_ref, gc_ref, *, tg):
    g = pl.program_id(0)
    c64 = RW_CHUNK
    nch = tg // c64
    npair = RWKV_W // LANES
    pair_lanes = [slice(p * LANES, (p + 1) * LANES) for p in range(npair)]
    lane = _iota((1, LANES), 1)
    first = g == 0

    @pl.when(first)
    def _():
        h_ref[...] = jnp.zeros_like(h_ref)

    rowid = _iota((tg, 1), 0)

    def shift_mix(cur_ref, prev_ref, mu_ref):
        cur = cur_ref[...]
        prev_row = jnp.where(first, 0.0, prev_ref[7:8, :])
        rolled = pltpu.roll(cur, 1, 0)
        shifted = jnp.where(rowid == 0, prev_row, rolled)
        return cur + (shifted - cur) * mu_ref[...]

    r = shift_mix(r_ref, rp_ref, mur_ref)
    k = shift_mix(k_ref, kp_ref, muk_ref)
    v = shift_mix(v_ref, vp_ref, muv_ref)
    lo = shift_mix(lora_ref, lp_ref, mul_ref)
    wl = lo[:, 0:DECAY_LORA]
    al = lo[:, DECAY_LORA:DECAY_LORA + AAA_LORA]
    gl = lo[:, 128:256]

    zw = -(w0_ref[...] + _dot3(jnp.tanh(wl), w2_ref[...]))
    softplus = jnp.maximum(zw, 0.0) + jnp.log(1.0 + jnp.exp(-jnp.abs(zw)))
    lw = -jnp.exp(-softplus - 0.5)
    a = _sigmoid(a0_ref[...] + _bdot(al, a2_ref[...]))
    gate = _bdot(_sigmoid(gl), g2_ref[...])

    ri = _iota((LANES, LANES), 0) // HEAD_DIM
    ci = _iota((LANES, LANES), 1) // HEAD_DIM
    bones = jnp.where(ri == ci, 1.0, 0.0).astype(BF16)

    def head_sum(xf):
        return jnp.concatenate([_dot2(xf[:, pl_], bones) for pl_ in pair_lanes], axis=1)

    kk = k * kk_ref[...]
    kk = kk / jnp.maximum(jnp.sqrt(head_sum(kk * kk)), 1e-12)
    k2 = k * (1.0 + (a - 1.0) * ka_ref[...])
    bonus = head_sum(r * k2 * rk_ref[...]) * v
    bvec = a * kk

    st_ref[0] = r
    st_ref[1] = k2
    st_ref[2] = v
    st_ref[3] = lw
    st_ref[4] = kk
    st_ref[5] = bvec

    rr = _iota((LANES, LANES), 0)
    cc = _iota((LANES, LANES), 1)
    same = (rr // c64) == (cc // c64)
    strict = same & ((rr % c64) > (cc % c64))
    incl = same & ((rr % c64) >= (cc % c64))
    eye = jnp.where(rr == cc, 1.0, 0.0)
    tril = jnp.where(_iota((c64, c64), 0) >= _iota((c64, c64), 1), 1.0, 0.0).astype(BF16)
    lo_half = lane < HEAD_DIM

    def stack(xc):
        return jnp.concatenate([jnp.where(lo_half, xc, 0.0), jnp.where(lo_half, 0.0, xc)], axis=0)

    def prepare(c, carry):
        chunks = [c * RW_UNROLL + j for j in range(RW_UNROLL)]
        sls = [pl.ds(pl.multiple_of(cj * c64, c64), c64) for cj in chunks]
        items = [(p, j) for j in range(RW_UNROLL) for p in range(npair)]
        pairs = range(len(items))
        idx = [p * nch + chunks[j] for p, j in items]
        ld = lambda q: [st_ref[q, sls[j], pair_lanes[p]] for p, j in items]
        rc, kc, vc, lwc, kkc, bc = ld(0), ld(1), ld(2), ld(3), ld(4), ld(5)
        cum = [_dot2_l(tril, lwc[p]) for p in pairs]
        tot = [cum[p][c64 - 1:c64, :] for p in pairs]
        g_in = [jnp.exp(cum[p]) for p in pairs]
        g_ex = [jnp.exp(cum[p] - lwc[p]) for p in pairs]
        g_inv = [jnp.exp(-cum[p]) for p in pairs]
        g_rest = [jnp.exp(tot[p] - cum[p]) for p in pairs]
        a_s = [stack(-kkc[p] * g_ex[p]).astype(BF16) for p in pairs]
        b_s = [stack(bc[p] * g_inv[p]).astype(BF16) for p in pairs]
        k_s = [stack(kc[p] * g_inv[p]).astype(BF16) for p in pairs]
        r_s = [stack(rc[p] * g_in[p]).astype(BF16) for p in pairs]
        v_s = [stack(vc[p]).astype(BF16) for p in pairs]
        nmat = [jnp.where(strict, _dot_nt(a_s[p], b_s[p]), 0.0) for p in pairs]
        aak = [jnp.where(strict, _dot_nt(a_s[p], k_s[p]), 0.0) for p in pairs]
        arb = [jnp.where(incl, _dot_nt(r_s[p], b_s[p]), 0.0) for p in pairs]
        ark = [jnp.where(incl, _dot_nt(r_s[p], k_s[p]), 0.0) for p in pairs]
        tinv = [eye + nmat[p] for p in pairs]
        pw = nmat
        for _ in range(5):
            pw = [_bdot(pw[p], pw[p]) for p in pairs]
            tinv = [_bdot(tinv[p], eye + pw[p]) for p in pairs]
        tinv = [tinv[p].astype(BF16) for p in pairs]
        akv = [_bdot(aak[p], v_s[p]).astype(BF16) for p in pairs]
        wmat = [_dot(tinv[p], a_s[p]) for p in pairs]
        zmat = [_dot(tinv[p], akv[p]) for p in pairs]
        y0 = [_bdot(ark[p], v_s[p]) for p in pairs]
        for p in pairs:
            wm_ref[idx[p]] = wmat[p].astype(BF16)
            zm_ref[idx[p]] = zmat[p]
            y0_ref[idx[p]] = y0[p]
            ar_ref[idx[p]] = arb[p].astype(BF16)
            rs_ref[idx[p]] = r_s[p]
            vs_ref[idx[p]] = v_s[p]
            lt_ref[idx[p]] = jnp.concatenate([stack(bc[p] * g_rest[p]), stack(kc[p] * g_rest[p])],
                                             axis=0).T.astype(BF16)
            gc_ref[idx[p]] = jnp.broadcast_to(jnp.sum(eye * jnp.exp(tot[p]), axis=1, keepdims=True),
                                              (LANES, LANES))
        return carry

    lax.fori_loop(0, nch // RW_UNROLL, prepare, 0)

    def advance(c, carry):
        sl = pl.ds(pl.multiple_of(c * c64, c64), c64)
        pairs = range(npair)
        idx = [p * nch + c for p in pairs]
        hst = [h_ref[p] for p in pairs]
        hb = [hst[p].astype(BF16) for p in pairs]
        u = [_dot(wm_ref[idx[p]], hb[p]) + zm_ref[idx[p]] for p in pairs]
        rh = [_dot(rs_ref[idx[p]], hb[p]) for p in pairs]
        ub = [u[p].astype(BF16) for p in pairs]
        hnew = [_dot(lt_ref[idx[p]], jnp.concatenate([ub[p], vs_ref[idx[p]]], axis=0)) for p in pairs]
        au = [_dot(ar_ref[idx[p]], ub[p]) for p in pairs]
        for p in pairs:
            h_ref[p] = gc_ref[idx[p]] * hst[p] + hnew[p]
            ys = rh[p] + au[p] + y0_ref[idx[p]]
            y_ref[sl, pair_lanes[p]] = ys[0:c64, :] + ys[c64:2 * c64, :]
        return carry

    lax.fori_loop(0, nch, advance, 0)

    y = y_ref[...]
    mean = head_sum(y) * (1.0 / HEAD_DIM)
    dy = y - mean
    var = head_sum(dy * dy) * (1.0 / HEAD_DIM)
    o = dy * lax.rsqrt(var + GN_EPS) * lng_ref[...] + lnb_ref[...]
    o_ref[...] = (o + bonus) * gate


def _rwkv_mix(rkv, lora, mu, w0, w2, a0, a2, g2, k_k, k_a, r_k, ln_g, ln_b, tg=512):
    t = rkv.shape[0]
    w = RWKV_W
    npair = w // LANES
    nmat = npair * (tg // RW_CHUNK)
    mu_r, mu_k, mu_v, mu_l = mu[:, 0:w], mu[:, w:2 * w], mu[:, 2 * w:3 * w], mu[:, 3 * w:3 * w + 256]
    blk = lambda off: pl.BlockSpec((tg, w), lambda g: (g, off))
    prev = lambda off: pl.BlockSpec((8, w), lambda g: (jnp.maximum(g * (tg // 8) - 1, 0), off))
    vec = pl.BlockSpec((1, w), lambda g: (0, 0))
    full = lambda rows: pl.BlockSpec((rows, w), lambda g: (0, 0))
    return pl.pallas_call(
        functools.partial(_rwkv_kernel, tg=tg),
        grid=(t // tg,),
        in_specs=[
            blk(0), blk(1), blk(2),
            pl.BlockSpec((tg, 256), lambda g: (g, 0)),
            prev(0), prev(1), prev(2),
            pl.BlockSpec((8, 256), lambda g: (jnp.maximum(g * (tg // 8) - 1, 0), 0)),
            vec, vec, vec,
            pl.BlockSpec((1, 256), lambda g: (0, 0)),
            vec, full(DECAY_LORA), vec, full(AAA_LORA), full(GATE_LORA),
            vec, vec, vec, vec, vec,
        ],
        out_specs=pl.BlockSpec((tg, w), lambda g: (g, 0)),
        out_shape=jax.ShapeDtypeStruct((t, w), F32),
        scratch_shapes=[
            pltpu.VMEM((npair, LANES, LANES), F32),
            pltpu.VMEM((tg, w), F32),
            pltpu.VMEM((6, tg, w), F32),
            pltpu.VMEM((nmat, LANES, LANES), BF16),
            pltpu.VMEM((nmat, LANES, LANES), BF16),
            pltpu.VMEM((nmat, LANES, LANES), BF16),
            pltpu.VMEM((nmat, LANES, LANES), BF16),
            pltpu.VMEM((nmat, LANES, 2 * LANES), BF16),
            pltpu.VMEM((nmat, LANES, LANES), F32),
            pltpu.VMEM((nmat, LANES, LANES), F32),
            pltpu.VMEM((nmat, LANES, LANES), F32),
        ],
        compiler_params=_params(("arbitrary",)),
        name="rwkv7_mix",
    )(rkv, rkv, rkv, lora, rkv, rkv, rkv, lora,
      mu_r, mu_k, mu_v, mu_l, w0, w2, a0, a2, g2, k_k, k_a, r_k, ln_g, ln_b)


DSA_QB = 128
DSA_KC = 1024
DSA_SUB = 512
CNT_ROWS = 64
TIE_BLK = 128
BIS_STEPS = 2


def _float_key(v):
    bits = lax.bitcast_convert_type(v, I32)
    return bits ^ ((bits >> 31) & 0x7FFFFFFF)


def _dsa_kernel(dq_ref, iq_ref, ikw_ref, ikx_ref, kf_ref, vft_ref, wuk_ref, wuv_ref, tril_ref, slc_ref,
                o_ref, sc_ref, acc_ref):
    i = pl.program_id(0)
    qb, kc, sc_rows = DSA_QB, DSA_KC, DSA_SUB
    nh = DSA_HEADS
    t0 = i * qb
    nch = (t0 + qb + kc - 1) // kc
    tq = t0 + _iota((1, qb), 1)

    iq = iq_ref[...]
    iq_hi = iq.astype(BF16).astype(F32)
    iq_lo = iq - iq_hi
    lhs = []
    for h in range(IDX_HEADS):
        s = slice(h * IDX_DIM, (h + 1) * IDX_DIM)
        lhs.append(jnp.concatenate([iq_hi[:, s], iq_hi[:, s], iq_lo[:, s], iq_lo[:, s]], axis=1))
    lhs_t = jnp.concatenate(lhs, axis=0).T.astype(BF16)
    ikw_t = ikw_ref[...].T
    iw = [ikw_t[IDX_DIM + h:IDX_DIM + h + 1, :] for h in range(IDX_HEADS)]

    def score_body(ch, carry, masked):
        m1, m2 = carry
        for sub in range(kc // sc_rows):
            k0 = pl.multiple_of(ch * kc + sub * sc_rows, sc_rows)
            s_all = _dot(ikx_ref[pl.ds(k0, sc_rows), :], lhs_t)
            acc = jnp.zeros((sc_rows, qb), F32)
            for h in range(IDX_HEADS):
                acc = acc + jnp.maximum(s_all[:, h * qb:(h + 1) * qb], 0.0) * iw[h]
            acc = jnp.where(acc == 0.0, 0.0, acc)
            key = _float_key(acc)
            if masked:
                causal = (k0 + _iota((sc_rows, 1), 0)) <= tq
                key = jnp.where(causal, key, INT_MIN)
                acc = jnp.where(causal, acc, -jnp.inf)
            sc_ref[pl.ds(k0, sc_rows), :] = key
            for j in range(sc_rows // LANES):
                xj = acc[j * LANES:(j + 1) * LANES, :]
                m2 = jnp.maximum(m2, jnp.minimum(m1, xj))
                m1 = jnp.maximum(m1, xj)
        return m1, m2

    ninf = jnp.full((LANES, qb), -jnp.inf, F32)
    n_below = t0 // kc
    top2 = lax.fori_loop(0, n_below, functools.partial(score_body, masked=False), (ninf, ninf))
    m1, m2 = lax.fori_loop(n_below, nch, functools.partial(score_body, masked=True), top2)

    def count_ge(cand):
        def body(ch, acc):
            k0 = pl.multiple_of(ch * kc, kc)
            m = jnp.where(sc_ref[pl.ds(k0, kc), :] >= cand, 1.0, 0.0)
            for j in range(kc // CNT_ROWS):
                acc = acc + m[j * CNT_ROWS:(j + 1) * CNT_ROWS, :]
            return acc
        acc = lax.fori_loop(0, nch, body, jnp.zeros((CNT_ROWS, qb), F32))
        return jnp.sum(acc, axis=0, keepdims=True)

    k_row = jnp.minimum(tq + 1, TOPK_MAX).astype(F32)
    hi0 = _float_key(jnp.max(m1, axis=0, keepdims=True))
    lo0 = jnp.minimum(_float_key(jnp.min(m2, axis=0, keepdims=True)), hi0)
    c_pos = count_ge(jnp.ones((1, qb), I32))
    c_nonneg = count_ge(jnp.zeros((1, qb), I32))
    at_zero = (c_pos < k_row) & (c_nonneg >= k_row)
    above = c_pos >= k_row
    lo0 = jnp.where(at_zero, 0, jnp.where(above, jnp.maximum(lo0, 1), lo0))
    hi0 = jnp.where(at_zero, 0, jnp.where(above, hi0, jnp.minimum(hi0, -1)))
    lo0 = jnp.minimum(lo0, hi0)

    def open_rows(lo, hi):
        return jnp.max(jnp.where(lo < hi, 1.0, 0.0))

    def bis_body(st):
        lo, hi, _ = st
        for _ in range(BIS_STEPS):
            mid = (lo | hi) - ((lo ^ hi) >> 1)
            c = count_ge(mid)
            ge = c >= k_row
            lo, hi = jnp.where(ge, mid, lo), jnp.where(c == k_row, mid, jnp.where(ge, hi, mid - 1))
        return lo, hi, open_rows(lo, hi)

    thr, _, _ = lax.while_loop(lambda st: st[2] > 0.5, bis_body, (lo0, hi0, open_rows(lo0, hi0)))

    dq = dq_ref[...]
    slc = slc_ref[...]
    qaug = []
    for h in range(nh):
        ql = _bdot(dq[:, h * HEAD_DIM:(h + 1) * HEAD_DIM], wuk_ref[h]) * HEAD_DIM ** -0.5
        qaug.append(jnp.concatenate([ql, jnp.broadcast_to(slc[h:h + 1, :], (qb, LANES))], axis=1))
    qaug_t = jnp.concatenate(qaug, axis=0).T.astype(BF16)
    acc_ref[...] = jnp.zeros_like(acc_ref)

    nsub = kc // sc_rows

    def sub_starts(ch):
        return [pl.multiple_of(ch * kc + sub * sc_rows, sc_rows) for sub in range(nsub)]

    def logits(k0):
        return _dot(kf_ref[pl.ds(k0, sc_rows), :], qaug_t)

    def attend(k0, lg_all, sel, m_old):
        ps, m_new = [], []
        for h in range(nh):
            cols = slice(h * qb, (h + 1) * qb)
            lg = jnp.where(sel, lg_all[:, cols], NEG)
            mh = jnp.maximum(m_old[:, cols], jnp.max(lg, axis=0, keepdims=True))
            ps.append(jnp.exp((lg - mh).astype(BF16)))
            m_new.append(mh)
        m_new = jnp.concatenate(m_new, axis=1)
        pv = _dot(vft_ref[:, pl.ds(k0, sc_rows)], jnp.concatenate(ps, axis=1))
        acc_ref[...] = jnp.exp(m_old - m_new) * acc_ref[...] + pv
        return m_new

    m_init = jnp.full((1, nh * qb), NEG, F32)

    need = k_row - count_ge(thr + 1)
    tril = tril_ref[...]

    def body(ch, carry):
        tie_run, m_old = carry
        ks = sub_starts(ch)
        lgs = [logits(k0) for k0 in ks]
        keys = [sc_ref[pl.ds(k0, sc_rows), :] for k0 in ks]
        blocks = [slice(j * TIE_BLK, (j + 1) * TIE_BLK) for j in range(sc_rows // TIE_BLK)]
        prefs = [[_dot(tril, jnp.where(key[bl, :] == thr, 1.0, 0.0).astype(BF16)) for bl in blocks] for key in keys]
        for k0, lg, key, pref in zip(ks, lgs, keys, prefs):
            ranks = []
            for pj in pref:
                ranks.append(tie_run + pj)
                tie_run = tie_run + pj[TIE_BLK - 1:TIE_BLK, :]
            sel = (key > thr) | ((key == thr) & (jnp.concatenate(ranks, axis=0) <= need))
            m_old = attend(k0, lg, sel, m_old)
        return tie_run, m_old

    lax.fori_loop(0, nch, body, (jnp.zeros((1, qb), F32), m_init))

    acc = acc_ref[...]
    o_lat = acc[0:KV_LORA, :] / acc[KV_LORA:KV_LORA + 1, :]
    outs = [_bdot(o_lat[:, h * qb:(h + 1) * qb].T, wuv_ref[h]) for h in range(nh)]
    o_ref[...] = jnp.concatenate(outs, axis=1)


DSA_VROWS = KV_LORA + 16


def _dsa_mix(dq, iq, ikw, ckv, w_uk, w_uv, slopes):
    t = dq.shape[0]
    assert t <= LANES * 256
    ikn = ikw[:, 0:IDX_DIM]
    ik_hi, ik_lo = _split2(ikn)
    ikx = jnp.concatenate([ik_hi, ik_lo, ik_hi, ik_lo], axis=1)
    ckv_b = ckv.astype(BF16)
    pos = jnp.arange(t, dtype=I32)
    pa = (pos // LANES).astype(BF16)[:, None]
    pb = (pos % LANES).astype(BF16)[:, None]
    kf = jnp.concatenate([ckv_b, pa, pa, pa, pb, pb, pb, jnp.zeros((t, LANES - 6), BF16)], axis=1)
    vft = jnp.concatenate([ckv_b.T, jnp.ones((1, t), BF16), jnp.zeros((DSA_VROWS - KV_LORA - 1, t), BF16)], axis=0)
    cols = []
    for sl in slopes:
        for coef in (sl * LANES, sl):
            c_hi = jnp.asarray(coef, F32).astype(BF16)
            r1 = jnp.asarray(coef, F32) - c_hi.astype(F32)
            c_mid = r1.astype(BF16)
            c_lo = (r1 - c_mid.astype(F32)).astype(BF16)
            cols += [c_hi.astype(F32), c_mid.astype(F32), c_lo.astype(F32)]
    slc = jnp.stack(cols).reshape(DSA_HEADS, 6)
    slc = jnp.pad(slc, ((0, 8 - DSA_HEADS), (0, LANES - 6)))
    assert t % DSA_KC == 0
    kc = TIE_BLK
    tril = jnp.asarray((np.arange(kc)[:, None] >= np.arange(kc)[None, :]).astype(np.float32), BF16)
    row = lambda i: (i, 0)
    const2 = lambda i: (0, 0)
    const3 = lambda i: (0, 0, 0)
    resident = lambda shape: pl.BlockSpec(shape, const2, pipeline_mode=pl.Buffered(1))
    return pl.pallas_call(
        _dsa_kernel,
        grid=(t // DSA_QB,),
        in_specs=[
            pl.BlockSpec((DSA_QB, DSA_W), row),
            pl.BlockSpec((DSA_QB, IDX_HEADS * IDX_DIM), row),
            pl.BlockSpec((DSA_QB, LANES), row),
            resident((t, 4 * IDX_DIM)),
            resident((t, 2 * LANES)),
            resident((DSA_VROWS, t)),
            pl.BlockSpec((DSA_HEADS, HEAD_DIM, KV_LORA), const3),
            pl.BlockSpec((DSA_HEADS, KV_LORA, HEAD_DIM), const3),
            resident((kc, kc)),
            pl.BlockSpec((8, LANES), const2),
        ],
        out_specs=pl.BlockSpec((DSA_QB, DSA_W), row),
        out_shape=jax.ShapeDtypeStruct((t, DSA_W), F32),
        scratch_shapes=[
            pltpu.VMEM((t, DSA_QB), I32),
            pltpu.VMEM((DSA_VROWS, DSA_HEADS * DSA_QB), F32),
        ],
        compiler_params=_params(("arbitrary",)),
        name="dsa_mix",
    )(dq, iq, ikw, ikx, kf, vft, w_uk, w_uv, tril, slc)


def _swa_kernel(q_ref, kv_ref, kvp_ref, sink_ref, o_ref, *, slopes):
    i = pl.program_id(0)
    w = WINDOW
    gsz = SWA_HEADS // SWA_KV_HEADS
    q = q_ref[...]
    kv = kv_ref[...]
    kvp = kvp_ref[...]
    qi = _iota((w, 2 * w), 0)
    kj = _iota((w, 2 * w), 1)
    dist = qi + w - kj
    in_band = (dist >= 0) & (dist < w)
    valid = [in_band & ((kj >= w) | (i > 0))] + [in_band] * (SWA_BLOCKS - 1)
    distf = dist.astype(F32)
    sinks = sink_ref[...]
    rows = [slice(b * w, (b + 1) * w) for b in range(SWA_BLOCKS)]
    prev = [kvp] + [kv[rows[b], :] for b in range(SWA_BLOCKS - 1)]
    k2 = [[jnp.concatenate([prev[b][:, g * HEAD_DIM:(g + 1) * HEAD_DIM],
                            kv[rows[b], g * HEAD_DIM:(g + 1) * HEAD_DIM]], axis=0).astype(BF16)
           for g in range(SWA_KV_HEADS)] for b in range(SWA_BLOCKS)]
    v2 = [[jnp.concatenate([prev[b][:, w + g * HEAD_DIM:w + (g + 1) * HEAD_DIM],
                            kv[rows[b], w + g * HEAD_DIM:w + (g + 1) * HEAD_DIM]], axis=0).astype(BF16)
           for g in range(SWA_KV_HEADS)] for b in range(SWA_BLOCKS)]
    items = [(b, hd) for b in range(SWA_BLOCKS) for hd in range(SWA_HEADS)]
    s = [_dot_nt(q[rows[b], hd * HEAD_DIM:(hd + 1) * HEAD_DIM].astype(BF16), k2[b][hd // gsz]) * HEAD_DIM ** -0.5
         for b, hd in items]
    s = [jnp.where(valid[b], s[n] - slopes[hd] * distf, NEG) for n, (b, hd) in enumerate(items)]
    sink = [sinks[0:1, hd:hd + 1] for _, hd in items]
    m = [jnp.maximum(jnp.max(s[n], axis=1, keepdims=True), sink[n]) for n in range(len(items))]
    e = [jnp.exp(s[n] - m[n]) for n in range(len(items))]
    p = [e[n] / (jnp.sum(e[n], axis=1, keepdims=True) + jnp.exp(sink[n] - m[n])) for n in range(len(items))]
    outs = [_dot(p[n].astype(BF16), v2[b][hd // gsz]) for n, (b, hd) in enumerate(items)]
    for b in range(SWA_BLOCKS):
        o_ref[rows[b], :] = jnp.concatenate(outs[b * SWA_HEADS:(b + 1) * SWA_HEADS], axis=1)


SWA_BLOCKS = 2


def _swa_mix(sq, skv, sinks, slopes):
    t = sq.shape[0]
    w = WINDOW
    step = SWA_BLOCKS * w
    return pl.pallas_call(
        functools.partial(_swa_kernel, slopes=slopes),
        grid=(t // step,),
        in_specs=[
            pl.BlockSpec((step, SWA_W), lambda i: (i, 0)),
            pl.BlockSpec((step, 2 * w), lambda i: (i, 0)),
            pl.BlockSpec((w, 2 * w), lambda i: (jnp.maximum(SWA_BLOCKS * i - 1, 0), 0)),
            pl.BlockSpec((1, LANES), lambda i: (0, 0)),
        ],
        out_specs=pl.BlockSpec((step, SWA_W), lambda i: (i, 0)),
        out_shape=jax.ShapeDtypeStruct((t, SWA_W), F32),
        compiler_params=_params(("arbitrary",)),
        name="swa_mix",
    )(sq, skv, skv, sinks)


def _post_mix_kernel(x_ref, orw_ref, ods_ref, osw_ref, wout_ref, g1_ref, lng_ref, lnb_ref,
                     sc2_ref, sh2_ref, rwt_ref, rb_ref, tri_ref,
                     x1_ref, h2_ref, eidx_ref, rank_ref, gate_ref, cnt_ref, carry_ref):
    i = pl.program_id(0)

    @pl.when(i == 0)
    def _():
        carry_ref[...] = jnp.zeros_like(carry_ref)

    y = (_dot(orw_ref[...].astype(BF16), wout_ref[0:RWKV_W, :])
         + _dot(ods_ref[...].astype(BF16), wout_ref[RWKV_W:RWKV_W + DSA_W, :])
         + _dot(osw_ref[...].astype(BF16), wout_ref[RWKV_W + DSA_W:D_MODEL, :]))
    x1 = _layer_norm_rows(ALPHA * x_ref[...] + g1_ref[...] * y, lng_ref[...], lnb_ref[...])
    x1_ref[...] = x1
    h2 = x1 * (1.0 + sc2_ref[...]) + sh2_ref[...]
    h2_ref[...] = _pack_halves(h2)

    tm = h2.shape[0]
    ne = N_EXPERTS
    gs = ne // N_GROUPS
    scores = _sigmoid(_dot_nt(rwt_ref[...], h2, HI))
    sel = scores + rb_ref[...]
    sub = _iota((gs, tm), 0).astype(F32)
    gsc = []
    for j in range(N_GROUPS):
        gj = sel[j * gs:(j + 1) * gs, :]
        m1 = jnp.max(gj, axis=0, keepdims=True)
        f1 = jnp.min(jnp.where(gj == m1, sub, float(gs)), axis=0, keepdims=True)
        m2 = jnp.max(jnp.where(sub == f1, -jnp.inf, gj), axis=0, keepdims=True)
        gsc.append(m1 + m2)
    gsc = jnp.concatenate(gsc, axis=0)
    gid = _iota((N_GROUPS, tm), 0).astype(F32)
    gmask = jnp.zeros((N_GROUPS, tm), F32)
    for _ in range(TOPK_GROUPS):
        mx = jnp.max(gsc, axis=0, keepdims=True)
        fi = jnp.min(jnp.where(gsc == mx, gid, float(N_GROUPS)), axis=0, keepdims=True)
        pick = gid == fi
        gmask = jnp.where(pick, 1.0, gmask)
        gsc = jnp.where(pick, -jnp.inf, gsc)
    selm = jnp.concatenate(
        [jnp.where(gmask[j:j + 1, :] > 0.5, sel[j * gs:(j + 1) * gs, :], NEG) for j in range(N_GROUPS)], axis=0)
    eid = _iota((ne, tm), 0).astype(F32)
    gsel, eids = [], []
    chosen_f = jnp.zeros((ne, tm), F32)
    for _ in range(TOP_K):
        mx = jnp.max(selm, axis=0, keepdims=True)
        fi = jnp.min(jnp.where(selm == mx, eid, float(ne)), axis=0, keepdims=True)
        pick = eid == fi
        eids.append(fi)
        gsel.append(jnp.sum(jnp.where(pick, scores, 0.0), axis=0, keepdims=True))
        chosen_f = jnp.where(pick, 1.0, chosen_f)
        selm = jnp.where(pick, -jnp.inf, selm)
    gsum = gsel[0]
    for kx in range(1, TOP_K):
        gsum = gsum + gsel[kx]
    before = _dot(chosen_f.astype(BF16), tri_ref[...]) + carry_ref[:, 0:1]
    ranks = [jnp.sum(jnp.where(eid == eids[kx], before, 0.0), axis=0, keepdims=True) for kx in range(TOP_K)]
    eidx_ref[...] = jnp.concatenate(eids, axis=0).astype(I32)
    rank_ref[...] = jnp.concatenate(ranks, axis=0).astype(I32)
    gate_ref[...] = jnp.concatenate(gsel, axis=0) / gsum * ROUTED_SCALE
    carry_ref[...] = carry_ref[...] + jnp.sum(chosen_f, axis=1, keepdims=True)
    cnt_ref[...] = carry_ref[...]


def _post_mix(x, o_rw, o_ds, o_sw, w_out, g1, ln_g, ln_b, sc2, sh2, router_wt, router_b, tm=512):
    t, d = x.shape
    tri = (np.arange(tm)[:, None] < np.arange(tm)[None, :]).astype(np.float32)
    tri = jnp.asarray(tri, BF16)
    row = lambda i: (i, 0)
    const = lambda i: (0, 0)
    col = lambda i: (0, i)
    vec = pl.BlockSpec((1, d), const)
    return pl.pallas_call(
        _post_mix_kernel,
        grid=(t // tm,),
        in_specs=[
            pl.BlockSpec((tm, d), row),
            pl.BlockSpec((tm, RWKV_W), row),
            pl.BlockSpec((tm, DSA_W), row),
            pl.BlockSpec((tm, SWA_W), row),
            pl.BlockSpec((d, d), const),
            vec, vec, vec, vec, vec,
            pl.BlockSpec((N_EXPERTS, d), const),
            pl.BlockSpec((N_EXPERTS, 1), const),
            pl.BlockSpec((tm, tm), const),
        ],
        out_specs=[
            pl.BlockSpec((tm, d), row),
            pl.BlockSpec((tm, d // 2), row),
            pl.BlockSpec((TOP_K, tm), col),
            pl.BlockSpec((TOP_K, tm), col),
            pl.BlockSpec((TOP_K, tm), col),
            pl.BlockSpec((N_EXPERTS, LANES), const),
        ],
        out_shape=[
            jax.ShapeDtypeStruct((t, d), F32),
            jax.ShapeDtypeStruct((t, d // 2), I32),
            jax.ShapeDtypeStruct((TOP_K, t), I32),
            jax.ShapeDtypeStruct((TOP_K, t), I32),
            jax.ShapeDtypeStruct((TOP_K, t), F32),
            jax.ShapeDtypeStruct((N_EXPERTS, LANES), F32),
        ],
        scratch_shapes=[pltpu.VMEM((N_EXPERTS, LANES), F32)],
        compiler_params=_params(("arbitrary",)),
        name="post_mix_router",
    )(x, o_rw, o_ds, o_sw, w_out, g1, ln_g, ln_b, sc2, sh2, router_wt, router_b, tri)


MOE_ROWS = 512
MOE_TILE = 256


def _pack_halves(v):
    w = v.shape[1] // 2
    bits = lax.bitcast_convert_type(v.astype(BF16).astype(F32), I32)
    return bits[:, :w] | lax.shift_right_logical(bits[:, w:], 16)


def _unpack_halves(p):
    return lax.bitcast_convert_type(p & -65536, F32), lax.bitcast_convert_type(p << 16, F32)


def _row_copy(src_ref, src_row, dst_ref, dst_row, sem):
    return pltpu.make_async_copy(src_ref.at[pl.ds(src_row, 1), :], dst_ref.at[pl.ds(dst_row, 1), :], sem)


def _dispatch_kernel(slot_hbm, h_ref, xs_in, xs_out, slot_smem, sem_tab, sem_rows):
    del xs_in
    i = pl.program_id(0)
    tab = pltpu.make_async_copy(slot_hbm.at[i], slot_smem, sem_tab)
    tab.start()
    tab.wait()

    def issue(tt, carry):
        for kx in range(TOP_K):
            _row_copy(h_ref, tt, xs_out, slot_smem[kx, tt], sem_rows).start(priority=kx % 2)
        return carry

    lax.fori_loop(0, MOE_TILE, issue, 0)

    def drain(tt, carry):
        for kx in range(TOP_K):
            _row_copy(h_ref, 0, xs_out, 0, sem_rows).wait()
        return carry

    lax.fori_loop(0, MOE_TILE, drain, 0)


def _dispatch(slot_tiles, rows, cap):
    t, d = rows.shape
    xs0 = jnp.zeros((cap, d), rows.dtype)
    return pl.pallas_call(
        _dispatch_kernel,
        grid=(t // MOE_TILE,),
        in_specs=[
            pl.BlockSpec(memory_space=pl.ANY),
            pl.BlockSpec((MOE_TILE, d), lambda i: (i, 0)),
            pl.BlockSpec(memory_space=pl.ANY),
        ],
        out_specs=pl.BlockSpec(memory_space=pl.ANY),
        out_shape=jax.ShapeDtypeStruct((cap, d), rows.dtype),
        scratch_shapes=[
            pltpu.SMEM((TOP_K, MOE_TILE), I32),
            pltpu.SemaphoreType.DMA,
            pltpu.SemaphoreType.DMA,
        ],
        input_output_aliases={2: 0},
        compiler_params=_params(("arbitrary",)),
        name="moe_dispatch",
    )(slot_tiles, rows, xs0)


def _expert_kernel(be_ref, nb_ref, xs_ref, w1_ref, w3_ref, w2_ref, ys_ref, w1b, w3b, w2b):
    b = pl.program_id(0)
    changed = (b == 0) | (be_ref[b] != be_ref[jnp.maximum(b - 1, 0)])

    @pl.when(changed & (b < nb_ref[0]))
    def _():
        w1b[...] = w1_ref[0, 0].astype(BF16)
        w3b[...] = w3_ref[0, 0].astype(BF16)
        w2b[...] = w2_ref[0, 0].astype(BF16)

    @pl.when(b < nb_ref[0])
    def _():
        x_hi, x_lo = _unpack_halves(xs_ref[...])
        x_hi, x_lo = x_hi.astype(BF16), x_lo.astype(BF16)
        half = x_hi.shape[1]
        a = _dot(x_hi, w1b[0:half, :]) + _dot(x_lo, w1b[half:2 * half, :])
        gte = _dot(x_hi, w3b[0:half, :]) + _dot(x_lo, w3b[half:2 * half, :])
        hmid = (a * _sigmoid(a) * gte).astype(BF16)
        ys_ref[...] = _pack_halves(_dot(hmid, w2b[...]))

    @pl.when(b >= nb_ref[0])
    def _():
        ys_ref[...] = jnp.zeros_like(ys_ref)


def _experts(block_e, n_used, xs, w1, w3, w2, layer):
    cap, dp = xs.shape
    d = 2 * dp
    nb = cap // MOE_ROWS
    grid_spec = pltpu.PrefetchScalarGridSpec(
        num_scalar_prefetch=2,
        grid=(nb,),
        in_specs=[
            pl.BlockSpec((MOE_ROWS, dp), lambda b, be, nu: (b, 0)),
            pl.BlockSpec((1, 1, d, D_EXPERT), lambda b, be, nu: (layer, be[b], 0, 0)),
            pl.BlockSpec((1, 1, d, D_EXPERT), lambda b, be, nu: (layer, be[b], 0, 0)),
            pl.BlockSpec((1, 1, D_EXPERT, d), lambda b, be, nu: (layer, be[b], 0, 0)),
        ],
        out_specs=pl.BlockSpec((MOE_ROWS, dp), lambda b, be, nu: (b, 0)),
        scratch_shapes=[
            pltpu.VMEM((d, D_EXPERT), BF16),
            pltpu.VMEM((d, D_EXPERT), BF16),
            pltpu.VMEM((D_EXPERT, d), BF16),
        ],
    )
    return pl.pallas_call(
        _expert_kernel,
        grid_spec=grid_spec,
        out_shape=jax.ShapeDtypeStruct((cap, dp), I32),
        compiler_params=_params(("arbitrary",)),
        name="moe_experts",
    )(block_e, n_used, xs, w1, w3, w2)


def _combine_kernel(slot_hbm, ys_hbm, x1_ref, h2_ref, gate_ref, sw1_ref, sw3_ref, sw2_ref,
                    g2_ref, lng_ref, lnb_ref, o_ref, slot_smem, gbuf, sem_tab, sem_rows):
    i = pl.program_id(0)
    tab = pltpu.make_async_copy(slot_hbm.at[i], slot_smem, sem_tab)
    tab.start()
    tab.wait()

    def issue(tt, carry):
        for kx in range(TOP_K):
            _row_copy(ys_hbm, slot_smem[kx, tt], gbuf.at[kx], tt, sem_rows).start(priority=kx % 2)
        return carry

    lax.fori_loop(0, MOE_TILE, issue, 0)

    h_hi, h_lo = _unpack_halves(h2_ref[...])
    h_hi, h_lo = h_hi.astype(BF16), h_lo.astype(BF16)
    half = h_hi.shape[1]
    a = _dot(h_hi, sw1_ref[0:half, :]) + _dot(h_lo, sw1_ref[half:2 * half, :])
    gte = _dot(h_hi, sw3_ref[0:half, :]) + _dot(h_lo, sw3_ref[half:2 * half, :])
    y = _dot((a * _sigmoid(a) * gte).astype(BF16), sw2_ref[...])

    def drain(tt, carry):
        for kx in range(TOP_K):
            _row_copy(ys_hbm, 0, gbuf.at[kx], 0, sem_rows).wait()
        return carry

    lax.fori_loop(0, MOE_TILE, drain, 0)

    gates = gate_ref[...]
    r_hi = jnp.zeros((MOE_TILE, half), F32)
    r_lo = jnp.zeros((MOE_TILE, half), F32)
    for kx in range(TOP_K):
        e_hi, e_lo = _unpack_halves(gbuf[kx])
        r_hi = r_hi + gates[:, kx:kx + 1] * e_hi
        r_lo = r_lo + gates[:, kx:kx + 1] * e_lo
    y = y + jnp.concatenate([r_hi, r_lo], axis=1)
    o_ref[...] = _layer_norm_rows(ALPHA * x1_ref[...] + g2_ref[...] * y, lng_ref[...], lnb_ref[...])


def _combine(slot_tiles, ys, x1, h2, gates_t, sw1, sw3, sw2, g2, ln_g, ln_b):
    t, d = x1.shape
    row = lambda i: (i, 0)
    const = lambda i: (0, 0)
    vec = pl.BlockSpec((1, d), const)
    return pl.pallas_call(
        _combine_kernel,
        grid=(t // MOE_TILE,),
        in_specs=[
            pl.BlockSpec(memory_space=pl.ANY),
            pl.BlockSpec(memory_space=pl.ANY),
            pl.BlockSpec((MOE_TILE, d), row),
            pl.BlockSpec((MOE_TILE, d // 2), row),
            pl.BlockSpec((MOE_TILE, TOP_K), row),
            pl.BlockSpec((d, D_EXPERT), const),
            pl.BlockSpec((d, D_EXPERT), const),
            pl.BlockSpec((D_EXPERT, d), const),
            vec, vec, vec,
        ],
        out_specs=pl.BlockSpec((MOE_TILE, d), row),
        out_shape=jax.ShapeDtypeStruct((t, d), F32),
        scratch_shapes=[
            pltpu.SMEM((TOP_K, MOE_TILE), I32),
            pltpu.VMEM((TOP_K, MOE_TILE, d // 2), I32),
            pltpu.SemaphoreType.DMA,
            pltpu.SemaphoreType.DMA,
        ],
        compiler_params=_params(("arbitrary",)),
        name="moe_combine",
    )(slot_tiles, ys, x1, h2, gates_t, sw1, sw3, sw2, g2, ln_g, ln_b)


def _pad_w_in(w_in_l):
    d = w_in_l.shape[0]
    pad = jnp.zeros((d, C_SQ[0] - N_ORIG_BEFORE_PAD), w_in_l.dtype)
    return jnp.concatenate([w_in_l[:, :N_ORIG_BEFORE_PAD], pad, w_in_l[:, N_ORIG_BEFORE_PAD:]], axis=1)


def _pad_lanes(v, width=LANES):
    v = v.reshape(1, -1)
    return jnp.pad(v, ((0, 0), (0, width - v.shape[1])))


def _moe_tables(eidx, rank, counts):
    t = eidx.shape[1]
    cnt = counts[:, 0].astype(I32)
    padded = (cnt + MOE_ROWS - 1) // MOE_ROWS * MOE_ROWS
    pad_end = jnp.cumsum(padded)
    pad_start = pad_end - padded
    e_ids = jnp.arange(N_EXPERTS, dtype=I32)
    start_of = jnp.sum(jnp.where(eidx[..., None] == e_ids, pad_start, 0), axis=-1)
    slot = start_of + rank
    slot_tiles = slot.reshape(TOP_K, t // MOE_TILE, MOE_TILE).transpose(1, 0, 2)
    cap = t * TOP_K + N_EXPERTS * MOE_ROWS
    nb = cap // MOE_ROWS
    blk_row = jnp.arange(nb, dtype=I32)[:, None] * MOE_ROWS
    block_e = jnp.minimum(jnp.sum((pad_end[None, :] <= blk_row).astype(I32), axis=1), N_EXPERTS - 1)
    n_used = (pad_end[-1] // MOE_ROWS).astype(I32).reshape(1)
    return slot_tiles, block_e, n_used, cap


def kernel(x, c, w_mod, b_mod, w_in, rwkv_mu, rwkv_w0, rwkv_w2, rwkv_a0, rwkv_a2, rwkv_g2, rwkv_k_k, rwkv_k_a, rwkv_r_k, rwkv_ln_g, rwkv_ln_b, dsa_kv_norm, dsa_w_uk, dsa_w_uv, dsa_ik_g, dsa_ik_b, swa_sinks, w_out, ln_mix_g, ln_mix_b, router_w, router_bias, exp_w1, exp_w3, exp_w2, sh_w1, sh_w3, sh_w2, ln_ffn_g, ln_ffn_b):
    bsz, t, d = x.shape
    assert bsz == 1 and d == D_MODEL
    depth = w_mod.shape[0]
    n_sl = SWA_HEADS + DSA_HEADS
    slopes = [2.0 ** (-8.0 * (j + 1.0) / n_sl) for j in range(n_sl)]
    swa_slopes, dsa_slopes = slopes[:SWA_HEADS], slopes[SWA_HEADS:]

    mod = _modulation(c, w_mod, b_mod)
    xs_cur = x[0]
    row1 = lambda v: v.reshape(1, -1)
    for l in range(depth):
        sh1, sc1, g1, sh2, sc2, g2 = [mod[l, :, j * d:(j + 1) * d] for j in range(6)]
        wp = _pad_w_in(w_in[l])
        w_hi = wp.astype(BF16)
        w_idx = wp[:, C_IDX[0]:C_IDX[1]]
        w_idx_lo = (w_idx - w_idx.astype(BF16).astype(F32)).astype(BF16)
        rkv, lora, dq, ckv, iq, ikw, sq, skv = _input_proj(
            xs_cur, sc1, sh1, w_hi, w_idx_lo, row1(dsa_kv_norm[l]),
            _pad_lanes(dsa_ik_g[l]), _pad_lanes(dsa_ik_b[l]))
        o_rw = _rwkv_mix(rkv, lora, row1(rwkv_mu[l]), row1(rwkv_w0[l]), rwkv_w2[l], row1(rwkv_a0[l]),
                         rwkv_a2[l], rwkv_g2[l], row1(rwkv_k_k[l]), row1(rwkv_k_a[l]), row1(rwkv_r_k[l]),
                         row1(rwkv_ln_g[l]), row1(rwkv_ln_b[l]))
        o_ds = _dsa_mix(dq, iq, ikw, ckv, dsa_w_uk[l], dsa_w_uv[l], dsa_slopes)
        o_sw = _swa_mix(sq, skv, _pad_lanes(swa_sinks[l]), swa_slopes)
        x1, h2, eidx, rank, gates, counts = _post_mix(
            xs_cur, o_rw, o_ds, o_sw, w_out[l].astype(BF16), g1, row1(ln_mix_g[l]), row1(ln_mix_b[l]),
            sc2, sh2, router_w[l].T, router_bias[l].reshape(-1, 1))
        slot_tiles, block_e, n_used, cap = _moe_tables(eidx, rank, counts)
        xs_sorted = _dispatch(slot_tiles, h2, cap)
        ys = _experts(block_e, n_used, xs_sorted, exp_w1, exp_w3, exp_w2, l)
        xs_cur = _combine(slot_tiles, ys, x1, h2, gates.T, sh_w1[l].astype(BF16), sh_w3[l].astype(BF16),
                          sh_w2[l].astype(BF16), g2, row1(ln_ffn_g[l]), row1(ln_ffn_b[l]))
    return xs_cur[None]
```

```python
import functools
import math

import jax
import jax.numpy as jnp
import numpy as np
from jax import lax
from jax.experimental import pallas as pl
from jax.experimental.pallas import tpu as pltpu

F32 = jnp.float32
BF16 = jnp.bfloat16
I32 = jnp.int32
HI = lax.Precision.HIGHEST

D_MODEL = 1024
DEPTH = 4
HEAD_DIM = 64
RWKV_HEADS = 6
DSA_HEADS = 4
SWA_HEADS = 6
SWA_KV_HEADS = 2
RWKV_W = RWKV_HEADS * HEAD_DIM
DSA_W = DSA_HEADS * HEAD_DIM
SWA_W = SWA_HEADS * HEAD_DIM
DECAY_LORA = 64
AAA_LORA = 64
GATE_LORA = 128
GN_EPS = 64e-5
KV_LORA = 128
IDX_HEADS = 4
IDX_DIM = 64
TOPK_MAX = 256
WINDOW = 128
N_EXPERTS = 64
TOP_K = 8
N_GROUPS = 8
TOPK_GROUPS = 4
D_EXPERT = 256
ROUTED_SCALE = 2.5
ALPHA = (2 * DEPTH) ** 0.25
LN_EPS = 1e-5
NEG = -1e30
INT_MIN = -(2 ** 31)

LANES = 128
VMEM_LIMIT = 56 * 1024 * 1024

C_RKV = (0, 1152)
C_LORA = (1152, 1408)
C_DQ = (1408, 1664)
C_CKV = (1664, 1792)
C_IDX = (1792, 2176)
C_SQ = (2176, 2560)
C_SKV = (2560, 2816)
P_PAD = 2816
N_ORIG_BEFORE_PAD = 2116


def _dot(a, b, prec=None):
    return jnp.dot(a, b, preferred_element_type=F32, precision=prec)


def _dot_nt(a, b, prec=None):
    return lax.dot_general(a, b, (((1,), (1,)), ((), ())), preferred_element_type=F32, precision=prec)


def _split2(a):
    a_hi = a.astype(BF16)
    return a_hi, (a - a_hi.astype(F32)).astype(BF16)


def _bdot(a, b):
    return _dot(a.astype(BF16), b.astype(BF16))


def _bdot_nt(a, b):
    return _dot_nt(a.astype(BF16), b.astype(BF16))


def _dot2(a, b_exact):
    a_hi, a_lo = _split2(a)
    return _dot(a_hi, b_exact) + _dot(a_lo, b_exact)


def _dot2_l(a_exact, b):
    b_hi, b_lo = _split2(b)
    return _dot(a_exact, b_hi) + _dot(a_exact, b_lo)


def _dot3(a, b):
    a_hi, a_lo = _split2(a)
    b_hi, b_lo = _split2(b)
    return _dot(a_hi, b_hi) + (_dot(a_lo, b_hi) + _dot(a_hi, b_lo))


def _iota(shape, dim):
    return lax.broadcasted_iota(I32, shape, dim)


def _sigmoid(x):
    return 1.0 / (1.0 + jnp.exp(-x))


def _layer_norm_rows(v, g, b):
    mu = jnp.mean(v, axis=-1, keepdims=True)
    d = v - mu
    var = jnp.mean(d * d, axis=-1, keepdims=True)
    return d * lax.rsqrt(var + LN_EPS) * g + b


def _params(sem):
    return pltpu.CompilerParams(dimension_semantics=sem, vmem_limit_bytes=VMEM_LIMIT)


def _mod_kernel(c_ref, w_ref, b_ref, o_ref):
    c = c_ref[...]
    cond = c * _sigmoid(c)
    o_ref[0] = _dot(cond, w_ref[0], HI) + b_ref[0]


def _modulation(c, w_mod, b_mod):
    depth, d, d6 = w_mod.shape
    c8 = jnp.broadcast_to(c, (8, d))
    nj = d6 // d
    out = pl.pallas_call(
        _mod_kernel,
        grid=(depth, nj),
        in_specs=[
            pl.BlockSpec((8, d), lambda l, j: (0, 0)),
            pl.BlockSpec((1, d, d), lambda l, j: (l, 0, j)),
            pl.BlockSpec((1, 1, d), lambda l, j: (l, 0, j)),
        ],
        out_specs=pl.BlockSpec((1, 8, d), lambda l, j: (l, 0, j)),
        out_shape=jax.ShapeDtypeStruct((depth, 8, d6), F32),
        compiler_params=_params(("arbitrary", "arbitrary")),
        name="modulation",
    )(c8, w_mod, b_mod.reshape(depth, 1, d6))
    return out[:, 0:1, :]


def _proj_kernel(x_ref, sc_ref, sh_ref, w_ref, wlo_ref, kvn_ref, ikg_ref, ikb_ref,
                 rkv_ref, lora_ref, dq_ref, ckv_ref, iq_ref, ikw_ref, sq_ref, skv_ref):
    h = x_ref[...] * (1.0 + sc_ref[...]) + sh_ref[...]
    hb = h.astype(BF16)
    hl = (h - hb.astype(F32)).astype(BF16)

    def mm(c):
        return _dot(hb, w_ref[:, c[0]:c[1]])

    rkv_ref[...] = mm(C_RKV)
    lora_ref[...] = mm(C_LORA)
    dq_ref[...] = mm(C_DQ)
    sq_ref[...] = mm(C_SQ)
    skv_ref[...] = mm(C_SKV)
    ckv = mm(C_CKV)
    ckv_ref[...] = ckv * lax.rsqrt(jnp.mean(ckv * ckv, axis=-1, keepdims=True) + 1e-6) * kvn_ref[...]
    idx = mm(C_IDX) + _dot(hl, w_ref[:, C_IDX[0]:C_IDX[1]]) + _dot(hb, wlo_ref[...])
    iq_ref[...] = idx[:, 0:256]
    g3 = idx[:, 256:384]
    lane = _iota(g3.shape, 1)
    isk = lane < IDX_DIM
    mu = jnp.sum(jnp.where(isk, g3, 0.0), axis=-1, keepdims=True) * (1.0 / IDX_DIM)
    dk = jnp.where(isk, g3 - mu, 0.0)
    var = jnp.sum(dk * dk, axis=-1, keepdims=True) * (1.0 / IDX_DIM)
    ikn = dk * lax.rsqrt(var + LN_EPS) * ikg_ref[...] + ikb_ref[...]
    ikw_ref[...] = jnp.where(isk, ikn, g3 * (IDX_HEADS ** -0.5 * IDX_DIM ** -0.5))


def _input_proj(x, sc, sh, w_hi, w_idx_lo, kvn, ikg, ikb, tm=512):
    t, d = x.shape
    widths = [C_RKV, C_LORA, C_DQ, C_CKV, (0, 256), (0, 128), C_SQ, C_SKV]
    widths = [c[1] - c[0] for c in widths]
    const = lambda i: (0, 0)
    row = lambda i: (i, 0)
    return pl.pallas_call(
        _proj_kernel,
        grid=(t // tm,),
        in_specs=[
            pl.BlockSpec((tm, d), row),
            pl.BlockSpec((1, d), const),
            pl.BlockSpec((1, d), const),
            pl.BlockSpec((d, P_PAD), const),
            pl.BlockSpec((d, C_IDX[1] - C_IDX[0]), const),
            pl.BlockSpec((1, KV_LORA), const),
            pl.BlockSpec((1, LANES), const),
            pl.BlockSpec((1, LANES), const),
        ],
        out_specs=[pl.BlockSpec((tm, w), row) for w in widths],
        out_shape=[jax.ShapeDtypeStruct((t, w), F32) for w in widths],
        compiler_params=_params(("arbitrary",)),
        name="input_proj",
    )(x, sc, sh, w_hi, w_idx_lo, kvn, ikg, ikb)


RW_CHUNK = 64
RW_UNROLL = 2


def _rwkv_kernel(r_ref, k_ref, v_ref, lora_ref, rp_ref, kp_ref, vp_ref, lp_ref,
                 mur_ref, muk_ref, muv_ref, mul_ref, w0_ref, w2_ref, a0_ref, a2_ref, g2_ref,
                 kk_ref, ka_ref, rk_ref, lng_ref, lnb_ref, o_ref,
                 h_ref, y_ref, st_ref, wm_ref, ar_ref, rs_ref, vs_ref, lt_ref, zm_ref, y0_ref, gc_ref, *, tg):
    g = pl.program_id(0)
    c64 = RW_CHUNK
    nch = tg // c64
    npair = RWKV_W // LANES
    pair_lanes = [slice(p * LANES, (p + 1) * LANES) for p in range(npair)]
    lane = _iota((1, LANES), 1)
    first = g == 0

    @pl.when(first)
    def _():
        h_ref[...] = jnp.zeros_like(h_ref)

    rowid = _iota((tg, 1), 0)

    def shift_mix(cur_ref, prev_ref, mu_ref):
        cur = cur_ref[...]
        prev_row = jnp.where(first, 0.0, prev_ref[7:8, :])
        rolled = pltpu.roll(cur, 1, 0)
        shifted = jnp.where(rowid == 0, prev_row, rolled)
        return cur + (shifted - cur) * mu_ref[...]

    r = shift_mix(r_ref, rp_ref, mur_ref)
    k = shift_mix(k_ref, kp_ref, muk_ref)
    v = shift_mix(v_ref, vp_ref, muv_ref)
    lo = shift_mix(lora_ref, lp_ref, mul_ref)
    wl = lo[:, 0:DECAY_LORA]
    al = lo[:, DECAY_LORA:DECAY_LORA + AAA_LORA]
    gl = lo[:, 128:256]

    zw = -(w0_ref[...] + _dot3(jnp.tanh(wl), w2_ref[...]))
    softplus = jnp.maximum(zw, 0.0) + jnp.log(1.0 + jnp.exp(-jnp.abs(zw)))
    lw = -jnp.exp(-softplus - 0.5)
    a = _sigmoid(a0_ref[...] + _bdot(al, a2_ref[...]))
    gate = _bdot(_sigmoid(gl), g2_ref[...])

    ri = _iota((LANES, LANES), 0) // HEAD_DIM
    ci = _iota((LANES, LANES), 1) // HEAD_DIM
    bones = jnp.where(ri == ci, 1.0, 0.0).astype(BF16)

    def head_sum(xf):
        return jnp.concatenate([_dot2(xf[:, pl_], bones) for pl_ in pair_lanes], axis=1)

    kk = k * kk_ref[...]
    kk = kk / jnp.maximum(jnp.sqrt(head_sum(kk * kk)), 1e-12)
    k2 = k * (1.0 + (a - 1.0) * ka_ref[...])
    bonus = head_sum(r * k2 * rk_ref[...]) * v
    bvec = a * kk

    st_ref[0] = r
    st_ref[1] = k2
    st_ref[2] = v
    st_ref[3] = lw
    st_ref[4] = kk
    st_ref[5] = bvec

    rr = _iota((LANES, LANES), 0)
    cc = _iota((LANES, LANES), 1)
    same = (rr // c64) == (cc // c64)
    strict = same & ((rr % c64) > (cc % c64))
    incl = same & ((rr % c64) >= (cc % c64))
    eye = jnp.where(rr == cc, 1.0, 0.0)
    tril = jnp.where(_iota((c64, c64), 0) >= _iota((c64, c64), 1), 1.0, 0.0).astype(BF16)
    lo_half = lane < HEAD_DIM

    def stack(xc):
        return jnp.concatenate([jnp.where(lo_half, xc, 0.0), jnp.where(lo_half, 0.0, xc)], axis=0)

    def prepare(c, carry):
        chunks = [c * RW_UNROLL + j for j in range(RW_UNROLL)]
        sls = [pl.ds(pl.multiple_of(cj * c64, c64), c64) for cj in chunks]
        items = [(p, j) for j in range(RW_UNROLL) for p in range(npair)]
        pairs = range(len(items))
        idx = [p * nch + chunks[j] for p, j in items]
        ld = lambda q: [st_ref[q, sls[j], pair_lanes[p]] for p, j in items]
        rc, kc, vc, lwc, kkc, bc = ld(0), ld(1), ld(2), ld(3), ld(4), ld(5)
        cum = [_dot2_l(tril, lwc[p]) for p in pairs]
        tot = [cum[p][c64 - 1:c64, :] for p in pairs]
        g_in = [jnp.exp(cum[p]) for p in pairs]
        g_ex = [jnp.exp(cum[p] - lwc[p]) for p in pairs]
        g_inv = [jnp.exp(-cum[p]) for p in pairs]
        g_rest = [jnp.exp(tot[p] - cum[p]) for p in pairs]
        a_s = [stack(-kkc[p] * g_ex[p]).astype(BF16) for p in pairs]
        b_s = [stack(bc[p] * g_inv[p]).astype(BF16) for p in pairs]
        k_s = [stack(kc[p] * g_inv[p]).astype(BF16) for p in pairs]
        r_s = [stack(rc[p] * g_in[p]).astype(BF16) for p in pairs]
        v_s = [stack(vc[p]).astype(BF16) for p in pairs]
        nmat = [jnp.where(strict, _dot_nt(a_s[p], b_s[p]), 0.0) for p in pairs]
        aak = [jnp.where(strict, _dot_nt(a_s[p], k_s[p]), 0.0) for p in pairs]
        arb = [jnp.where(incl, _dot_nt(r_s[p], b_s[p]), 0.0) for p in pairs]
        ark = [jnp.where(incl, _dot_nt(r_s[p], k_s[p]), 0.0) for p in pairs]
        tinv = [eye + nmat[p] for p in pairs]
        pw = nmat
        for _ in range(5):
            pw = [_bdot(pw[p], pw[p]) for p in pairs]
            tinv = [_bdot(tinv[p], eye + pw[p]) for p in pairs]
        tinv = [tinv[p].astype(BF16) for p in pairs]
        akv = [_bdot(aak[p], v_s[p]).astype(BF16) for p in pairs]
        wmat = [_dot(tinv[p], a_s[p]) for p in pairs]
        zmat = [_dot(tinv[p], akv[p]) for p in pairs]
        y0 = [_bdot(ark[p], v_s[p]) for p in pairs]
        for p in pairs:
            wm_ref[idx[p]] = wmat[p].astype(BF16)
            zm_ref[idx[p]] = zmat[p]
            y0_ref[idx[p]] = y0[p]
            ar_ref[idx[p]] = arb[p].astype(BF16)
            rs_ref[idx[p]] = r_s[p]
            vs_ref[idx[p]] = v_s[p]
            lt_ref[idx[p]] = jnp.concatenate([stack(bc[p] * g_rest[p]), stack(kc[p] * g_rest[p])],
                                             axis=0).T.astype(BF16)
            gc_ref[idx[p]] = jnp.broadcast_to(jnp.sum(eye * jnp.exp(tot[p]), axis=1, keepdims=True),
                                              (LANES, LANES))
        return carry

    lax.fori_loop(0, nch // RW_UNROLL, prepare, 0)

    def advance(c, carry):
        sl = pl.ds(pl.multiple_of(c * c64, c64), c64)
        pairs = range(npair)
        idx = [p * nch + c for p in pairs]
        hst = [h_ref[p] for p in pairs]
        hb = [hst[p].astype(BF16) for p in pairs]
        u = [_dot(wm_ref[idx[p]], hb[p]) + zm_ref[idx[p]] for p in pairs]
        rh = [_dot(rs_ref[idx[p]], hb[p]) for p in pairs]
        ub = [u[p].astype(BF16) for p in pairs]
        hnew = [_dot(lt_ref[idx[p]], jnp.concatenate([ub[p], vs_ref[idx[p]]], axis=0)) for p in pairs]
        au = [_dot(ar_ref[idx[p]], ub[p]) for p in pairs]
        for p in pairs:
            h_ref[p] = gc_ref[idx[p]] * hst[p] + hnew[p]
            ys = rh[p] + au[p] + y0_ref[idx[p]]
            y_ref[sl, pair_lanes[p]] = ys[0:c64, :] + ys[c64:2 * c64, :]
        return carry

    lax.fori_loop(0, nch, advance, 0)

    y = y_ref[...]
    mean = head_sum(y) * (1.0 / HEAD_DIM)
    dy = y - mean
    var = head_sum(dy * dy) * (1.0 / HEAD_DIM)
    o = dy * lax.rsqrt(var + GN_EPS) * lng_ref[...] + lnb_ref[...]
    o_ref[...] = (o + bonus) * gate


def _rwkv_mix(rkv, lora, mu, w0, w2, a0, a2, g2, k_k, k_a, r_k, ln_g, ln_b, tg=512):
    t = rkv.shape[0]
    w = RWKV_W
    npair = w // LANES
    nmat = npair * (tg // RW_CHUNK)
    mu_r, mu_k, mu_v, mu_l = mu[:, 0:w], mu[:, w:2 * w], mu[:, 2 * w:3 * w], mu[:, 3 * w:3 * w + 256]
    blk = lambda off: pl.BlockSpec((tg, w), lambda g: (g, off))
    prev = lambda off: pl.BlockSpec((8, w), lambda g: (jnp.maximum(g * (tg // 8) - 1, 0), off))
    vec = pl.BlockSpec((1, w), lambda g: (0, 0))
    full = lambda rows: pl.BlockSpec((rows, w), lambda g: (0, 0))
    return pl.pallas_call(
        functools.partial(_rwkv_kernel, tg=tg),
        grid=(t // tg,),
        in_specs=[
            blk(0), blk(1), blk(2),
            pl.BlockSpec((tg, 256), lambda g: (g, 0)),
            prev(0), prev(1), prev(2),
            pl.BlockSpec((8, 256), lambda g: (jnp.maximum(g * (tg // 8) - 1, 0), 0)),
            vec, vec, vec,
            pl.BlockSpec((1, 256), lambda g: (0, 0)),
            vec, full(DECAY_LORA), vec, full(AAA_LORA), full(GATE_LORA),
            vec, vec, vec, vec, vec,
        ],
        out_specs=pl.BlockSpec((tg, w), lambda g: (g, 0)),
        out_shape=jax.ShapeDtypeStruct((t, w), F32),
        scratch_shapes=[
            pltpu.VMEM((npair, LANES, LANES), F32),
            pltpu.VMEM((tg, w), F32),
            pltpu.VMEM((6, tg, w), F32),
            pltpu.VMEM((nmat, LANES, LANES), BF16),
            pltpu.VMEM((nmat, LANES, LANES), BF16),
            pltpu.VMEM((nmat, LANES, LANES), BF16),
            pltpu.VMEM((nmat, LANES, LANES), BF16),
            pltpu.VMEM((nmat, LANES, 2 * LANES), BF16),
            pltpu.VMEM((nmat, LANES, LANES), F32),
            pltpu.VMEM((nmat, LANES, LANES), F32),
            pltpu.VMEM((nmat, LANES, LANES), F32),
        ],
        compiler_params=_params(("arbitrary",)),
        name="rwkv7_mix",
    )(rkv, rkv, rkv, lora, rkv, rkv, rkv, lora,
      mu_r, mu_k, mu_v, mu_l, w0, w2, a0, a2, g2, k_k, k_a, r_k, ln_g, ln_b)


DSA_QB = 128
DSA_KC = 1024
DSA_SUB = 512
CNT_ROWS = 64
TIE_BLK = 128
BIS_STEPS = 2


def _float_key(v):
    bits = lax.bitcast_convert_type(v, I32)
    return bits ^ ((bits >> 31) & 0x7FFFFFFF)


def _dsa_kernel(dq_ref, iq_ref, ikw_ref, ikx_ref, kf_ref, vft_ref, wuk_ref, wuv_ref, tril_ref, slc_ref,
                o_ref, sc_ref, acc_ref):
    i = pl.program_id(0)
    qb, kc, sc_rows = DSA_QB, DSA_KC, DSA_SUB
    nh = DSA_HEADS
    t0 = i * qb
    nch = (t0 + qb + kc - 1) // kc
    tq = t0 + _iota((1, qb), 1)

    iq = iq_ref[...]
    iq_hi = iq.astype(BF16).astype(F32)
    iq_lo = iq - iq_hi
    lhs = []
    for h in range(IDX_HEADS):
        s = slice(h * IDX_DIM, (h + 1) * IDX_DIM)
        lhs.append(jnp.concatenate([iq_hi[:, s], iq_hi[:, s], iq_lo[:, s], iq_lo[:, s]], axis=1))
    lhs_t = jnp.concatenate(lhs, axis=0).T.astype(BF16)
    ikw_t = ikw_ref[...].T
    iw = [ikw_t[IDX_DIM + h:IDX_DIM + h + 1, :] for h in range(IDX_HEADS)]

    def score_body(ch, carry, masked):
        m1, m2 = carry
        sr = sc_rows
        for sub in range(kc // sr):
            k0 = pl.multiple_of(ch * kc + sub * sr, sr)
            s_all = _dot(ikx_ref[pl.ds(k0, sr), :], lhs_t)
            acc = jnp.zeros((sr, qb), F32)
            for h in range(IDX_HEADS):
                acc = acc + jnp.maximum(s_all[:, h * qb:(h + 1) * qb], 0.0) * iw[h]
            acc = jnp.where(acc == 0.0, 0.0, acc)
            key = _float_key(acc)
            if masked:
                causal = (k0 + _iota((sr, 1), 0)) <= tq
                key = jnp.where(causal, key, INT_MIN)
                acc = jnp.where(causal, acc, -jnp.inf)
            sc_ref[pl.ds(k0, sr), :] = key
            for j in range(sr // LANES):
                xj = acc[j * LANES:(j + 1) * LANES, :]
                m2 = jnp.maximum(m2, jnp.minimum(m1, xj))
                m1 = jnp.maximum(m1, xj)
        return m1, m2

    ninf = jnp.full((LANES, qb), -jnp.inf, F32)
    n_below = t0 // kc
    top2 = lax.fori_loop(0, n_below, functools.partial(score_body, masked=False), (ninf, ninf))
    m1, m2 = lax.fori_loop(n_below, nch, functools.partial(score_body, masked=True), top2)

    def count_ge(cand):
        def body(ch, acc):
            for j in range(kc // CNT_ROWS):
                kj = pl.multiple_of(ch * kc + j * CNT_ROWS, CNT_ROWS)
                acc = acc + jnp.where(sc_ref[pl.ds(kj, CNT_ROWS), :] >= cand, 1.0, 0.0)
            return acc
        acc = lax.fori_loop(0, nch, body, jnp.zeros((CNT_ROWS, qb), F32))
        return jnp.sum(acc, axis=0, keepdims=True)

    k_row = jnp.minimum(tq + 1, TOPK_MAX).astype(F32)
    hi0 = _float_key(jnp.max(m1, axis=0, keepdims=True))
    lo0 = jnp.minimum(_float_key(jnp.min(m2, axis=0, keepdims=True)), hi0)
    c_pos = count_ge(jnp.ones((1, qb), I32))
    c_nonneg = count_ge(jnp.zeros((1, qb), I32))
    at_zero = (c_pos < k_row) & (c_nonneg >= k_row)
    above = c_pos >= k_row
    lo0 = jnp.where(at_zero, 0, jnp.where(above, jnp.maximum(lo0, 1), lo0))
    hi0 = jnp.where(at_zero, 0, jnp.where(above, hi0, jnp.minimum(hi0, -1)))
    lo0 = jnp.minimum(lo0, hi0)

    def open_rows(lo, hi):
        return jnp.max(jnp.where(lo < hi, 1.0, 0.0))

    def bis_body(st):
        lo, hi, _ = st
        for _ in range(BIS_STEPS):
            mid = (lo | hi) - ((lo ^ hi) >> 1)
            c = count_ge(mid)
            ge = c >= k_row
            lo, hi = jnp.where(ge, mid, lo), jnp.where(c == k_row, mid, jnp.where(ge, hi, mid - 1))
        return lo, hi, open_rows(lo, hi)

    thr, _, _ = lax.while_loop(lambda st: st[2] > 0.5, bis_body, (lo0, hi0, open_rows(lo0, hi0)))

    dq = dq_ref[...]
    slc = slc_ref[...]
    qaug = []
    for h in range(nh):
        ql = _bdot(dq[:, h * HEAD_DIM:(h + 1) * HEAD_DIM], wuk_ref[h]) * HEAD_DIM ** -0.5
        qaug.append(jnp.concatenate([ql, jnp.broadcast_to(slc[h:h + 1, :], (qb, LANES))], axis=1))
    qaug_t = jnp.concatenate(qaug, axis=0).T.astype(BF16)
    acc_ref[...] = jnp.zeros_like(acc_ref)

    nsub = kc // sc_rows

    def sub_starts(ch):
        return [pl.multiple_of(ch * kc + sub * sc_rows, sc_rows) for sub in range(nsub)]

    def logits(k0):
        return _dot(kf_ref[pl.ds(k0, sc_rows), :], qaug_t)

    def attend(k0, lg_all, sel, m_old):
        ps, m_new = [], []
        for h in range(nh):
            cols = slice(h * qb, (h + 1) * qb)
            lg = jnp.where(sel, lg_all[:, cols], NEG)
            mh = jnp.maximum(m_old[:, cols], jnp.max(lg, axis=0, keepdims=True))
            ps.append(jnp.exp((lg - mh).astype(BF16)))
            m_new.append(mh)
        m_new = jnp.concatenate(m_new, axis=1)
        pv = _dot(vft_ref[:, pl.ds(k0, sc_rows)], jnp.concatenate(ps, axis=1))
        acc_ref[...] = jnp.exp(m_old - m_new) * acc_ref[...] + pv
        return m_new

    m_init = jnp.full((1, nh * qb), NEG, F32)

    need = k_row - count_ge(thr + 1)
    tril = tril_ref[...]

    def body(ch, carry):
        tie_run, m_old = carry
        ks = sub_starts(ch)
        lgs = [logits(k0) for k0 in ks]
        keys = [sc_ref[pl.ds(k0, sc_rows), :] for k0 in ks]
        blocks = [slice(j * TIE_BLK, (j + 1) * TIE_BLK) for j in range(sc_rows // TIE_BLK)]
        prefs = [[_dot(tril, jnp.where(key[bl, :] == thr, 1.0, 0.0).astype(BF16)) for bl in blocks] for key in keys]
        for k0, lg, key, pref in zip(ks, lgs, keys, prefs):
            ranks = []
            for pj in pref:
                ranks.append(tie_run + pj)
                tie_run = tie_run + pj[TIE_BLK - 1:TIE_BLK, :]
            sel = (key > thr) | ((key == thr) & (jnp.concatenate(ranks, axis=0) <= need))
            m_old = attend(k0, lg, sel, m_old)
        return tie_run, m_old

    lax.fori_loop(0, nch, body, (jnp.zeros((1, qb), F32), m_init))

    acc = acc_ref[...]
    o_lat = acc[0:KV_LORA, :] / acc[KV_LORA:KV_LORA + 1, :]
    outs = [_bdot(o_lat[:, h * qb:(h + 1) * qb].T, wuv_ref[h]) for h in range(nh)]
    o_ref[...] = jnp.concatenate(outs, axis=1)


DSA_VROWS = KV_LORA + 16


def _dsa_mix(dq, iq, ikw, ckv, w_uk, w_uv, slopes):
    t = dq.shape[0]
    assert t <= LANES * 256
    ikn = ikw[:, 0:IDX_DIM]
    ik_hi, ik_lo = _split2(ikn)
    ikx = jnp.concatenate([ik_hi, ik_lo, ik_hi, ik_lo], axis=1)
    ckv_b = ckv.astype(BF16)
    pos = jnp.arange(t, dtype=I32)
    pa = (pos // LANES).astype(BF16)[:, None]
    pb = (pos % LANES).astype(BF16)[:, None]
    kf = jnp.concatenate([ckv_b, pa, pa, pa, pb, pb, pb, jnp.zeros((t, LANES - 6), BF16)], axis=1)
    vft = jnp.concatenate([ckv_b.T, jnp.ones((1, t), BF16), jnp.zeros((DSA_VROWS - KV_LORA - 1, t), BF16)], axis=0)
    cols = []
    for sl in slopes:
        for coef in (sl * LANES, sl):
            c_hi = jnp.asarray(coef, F32).astype(BF16)
            r1 = jnp.asarray(coef, F32) - c_hi.astype(F32)
            c_mid = r1.astype(BF16)
            c_lo = (r1 - c_mid.astype(F32)).astype(BF16)
            cols += [c_hi.astype(F32), c_mid.astype(F32), c_lo.astype(F32)]
    slc = jnp.stack(cols).reshape(DSA_HEADS, 6)
    slc = jnp.pad(slc, ((0, 8 - DSA_HEADS), (0, LANES - 6)))
    assert t % DSA_KC == 0
    kc = TIE_BLK
    tril = jnp.asarray((np.arange(kc)[:, None] >= np.arange(kc)[None, :]).astype(np.float32), BF16)
    row = lambda i: (i, 0)
    const2 = lambda i: (0, 0)
    const3 = lambda i: (0, 0, 0)
    resident = lambda shape: pl.BlockSpec(shape, const2, pipeline_mode=pl.Buffered(1))
    return pl.pallas_call(
        _dsa_kernel,
        grid=(t // DSA_QB,),
        in_specs=[
            pl.BlockSpec((DSA_QB, DSA_W), row),
            pl.BlockSpec((DSA_QB, IDX_HEADS * IDX_DIM), row),
            pl.BlockSpec((DSA_QB, LANES), row),
            resident((t, 4 * IDX_DIM)),
            resident((t, 2 * LANES)),
            resident((DSA_VROWS, t)),
            pl.BlockSpec((DSA_HEADS, HEAD_DIM, KV_LORA), const3),
            pl.BlockSpec((DSA_HEADS, KV_LORA, HEAD_DIM), const3),
            resident((kc, kc)),
            pl.BlockSpec((8, LANES), const2),
        ],
        out_specs=pl.BlockSpec((DSA_QB, DSA_W), row),
        out_shape=jax.ShapeDtypeStruct((t, DSA_W), F32),
        scratch_shapes=[
            pltpu.VMEM((t, DSA_QB), I32),
            pltpu.VMEM((DSA_VROWS, DSA_HEADS * DSA_QB), F32),
        ],
        compiler_params=_params(("arbitrary",)),
        name="dsa_mix",
    )(dq, iq, ikw, ikx, kf, vft, w_uk, w_uv, tril, slc)


def _swa_kernel(q_ref, kv_ref, kvp_ref, sink_ref, o_ref, *, slopes):
    i = pl.program_id(0)
    w = WINDOW
    gsz = SWA_HEADS // SWA_KV_HEADS
    q = q_ref[...]
    kv = kv_ref[...]
    kvp = kvp_ref[...]
    qi = _iota((w, 2 * w), 0)
    kj = _iota((w, 2 * w), 1)
    dist = qi + w - kj
    in_band = (dist >= 0) & (dist < w)
    valid = [in_band & ((kj >= w) | (i > 0))] + [in_band] * (SWA_BLOCKS - 1)
    distf = dist.astype(F32)
    sinks = sink_ref[...]
    rows = [slice(b * w, (b + 1) * w) for b in range(SWA_BLOCKS)]
    prev = [kvp] + [kv[rows[b], :] for b in range(SWA_BLOCKS - 1)]
    k2 = [[jnp.concatenate([prev[b][:, g * HEAD_DIM:(g + 1) * HEAD_DIM],
                            kv[rows[b], g * HEAD_DIM:(g + 1) * HEAD_DIM]], axis=0).astype(BF16)
           for g in range(SWA_KV_HEADS)] for b in range(SWA_BLOCKS)]
    v2 = [[jnp.concatenate([prev[b][:, w + g * HEAD_DIM:w + (g + 1) * HEAD_DIM],
                            kv[rows[b], w + g * HEAD_DIM:w + (g + 1) * HEAD_DIM]], axis=0).astype(BF16)
           for g in range(SWA_KV_HEADS)] for b in range(SWA_BLOCKS)]
    items = [(b, hd) for b in range(SWA_BLOCKS) for hd in range(SWA_HEADS)]
    s = [_dot_nt(q[rows[b], hd * HEAD_DIM:(hd + 1) * HEAD_DIM].astype(BF16), k2[b][hd // gsz]) * HEAD_DIM ** -0.5
         for b, hd in items]
    s = [jnp.where(valid[b], s[n] - slopes[hd] * distf, NEG) for n, (b, hd) in enumerate(items)]
    sink = [sinks[0:1, hd:hd + 1] for _, hd in items]
    m = [jnp.maximum(jnp.max(s[n], axis=1, keepdims=True), sink[n]) for n in range(len(items))]
    e = [jnp.exp(s[n] - m[n]) for n in range(len(items))]
    p = [e[n] / (jnp.sum(e[n], axis=1, keepdims=True) + jnp.exp(sink[n] - m[n])) for n in range(len(items))]
    outs = [_dot(p[n].astype(BF16), v2[b][hd // gsz]) for n, (b, hd) in enumerate(items)]
    for b in range(SWA_BLOCKS):
        o_ref[rows[b], :] = jnp.concatenate(outs[b * SWA_HEADS:(b + 1) * SWA_HEADS], axis=1)


SWA_BLOCKS = 2


def _swa_mix(sq, skv, sinks, slopes):
    t = sq.shape[0]
    w = WINDOW
    step = SWA_BLOCKS * w
    return pl.pallas_call(
        functools.partial(_swa_kernel, slopes=slopes),
        grid=(t // step,),
        in_specs=[
            pl.BlockSpec((step, SWA_W), lambda i: (i, 0)),
            pl.BlockSpec((step, 2 * w), lambda i: (i, 0)),
            pl.BlockSpec((w, 2 * w), lambda i: (jnp.maximum(SWA_BLOCKS * i - 1, 0), 0)),
            pl.BlockSpec((1, LANES), lambda i: (0, 0)),
        ],
        out_specs=pl.BlockSpec((step, SWA_W), lambda i: (i, 0)),
        out_shape=jax.ShapeDtypeStruct((t, SWA_W), F32),
        compiler_params=_params(("arbitrary",)),
        name="swa_mix",
    )(sq, skv, skv, sinks)


def _post_mix_kernel(x_ref, orw_ref, ods_ref, osw_ref, wout_ref, g1_ref, lng_ref, lnb_ref,
                     sc2_ref, sh2_ref, rwt_ref, rb_ref, tri_ref,
                     x1_ref, h2_ref, eidx_ref, rank_ref, gate_ref, cnt_ref, carry_ref):
    i = pl.program_id(0)

    @pl.when(i == 0)
    def _():
        carry_ref[...] = jnp.zeros_like(carry_ref)

    y = (_dot(orw_ref[...].astype(BF16), wout_ref[0:RWKV_W, :])
         + _dot(ods_ref[...].astype(BF16), wout_ref[RWKV_W:RWKV_W + DSA_W, :])
         + _dot(osw_ref[...].astype(BF16), wout_ref[RWKV_W + DSA_W:D_MODEL, :]))
    x1 = _layer_norm_rows(ALPHA * x_ref[...] + g1_ref[...] * y, lng_ref[...], lnb_ref[...])
    x1_ref[...] = x1
    h2 = x1 * (1.0 + sc2_ref[...]) + sh2_ref[...]
    h2_ref[...] = _pack_halves(h2)

    tm = h2.shape[0]
    ne = N_EXPERTS
    gs = ne // N_GROUPS
    scores = _sigmoid(_dot_nt(rwt_ref[...], h2, HI))
    sel = scores + rb_ref[...]
    sub = _iota((gs, tm), 0).astype(F32)
    gsc = []
    for j in range(N_GROUPS):
        gj = sel[j * gs:(j + 1) * gs, :]
        m1 = jnp.max(gj, axis=0, keepdims=True)
        f1 = jnp.min(jnp.where(gj == m1, sub, float(gs)), axis=0, keepdims=True)
        m2 = jnp.max(jnp.where(sub == f1, -jnp.inf, gj), axis=0, keepdims=True)
        gsc.append(m1 + m2)
    gsc = jnp.concatenate(gsc, axis=0)
    gid = _iota((N_GROUPS, tm), 0).astype(F32)
    gmask = jnp.zeros((N_GROUPS, tm), F32)
    for _ in range(TOPK_GROUPS):
        mx = jnp.max(gsc, axis=0, keepdims=True)
        fi = jnp.min(jnp.where(gsc == mx, gid, float(N_GROUPS)), axis=0, keepdims=True)
        pick = gid == fi
        gmask = jnp.where(pick, 1.0, gmask)
        gsc = jnp.where(pick, -jnp.inf, gsc)
    selm = jnp.concatenate(
        [jnp.where(gmask[j:j + 1, :] > 0.5, sel[j * gs:(j + 1) * gs, :], NEG) for j in range(N_GROUPS)], axis=0)
    eid = _iota((ne, tm), 0).astype(F32)
    gsel, eids = [], []
    chosen_f = jnp.zeros((ne, tm), F32)
    for _ in range(TOP_K):
        mx = jnp.max(selm, axis=0, keepdims=True)
        fi = jnp.min(jnp.where(selm == mx, eid, float(ne)), axis=0, keepdims=True)
        pick = eid == fi
        eids.append(fi)
        gsel.append(jnp.sum(jnp.where(pick, scores, 0.0), axis=0, keepdims=True))
        chosen_f = jnp.where(pick, 1.0, chosen_f)
        selm = jnp.where(pick, -jnp.inf, selm)
    gsum = gsel[0]
    for kx in range(1, TOP_K):
        gsum = gsum + gsel[kx]
    before = _dot(chosen_f.astype(BF16), tri_ref[...]) + carry_ref[:, 0:1]
    ranks = [jnp.sum(jnp.where(eid == eids[kx], before, 0.0), axis=0, keepdims=True) for kx in range(TOP_K)]
    eidx_ref[...] = jnp.concatenate(eids, axis=0).astype(I32)
    rank_ref[...] = jnp.concatenate(ranks, axis=0).astype(I32)
    gate_ref[...] = jnp.concatenate(gsel, axis=0) / gsum * ROUTED_SCALE
    carry_ref[...] = carry_ref[...] + jnp.sum(chosen_f, axis=1, keepdims=True)
    cnt_ref[...] = carry_ref[...]


def _post_mix(x, o_rw, o_ds, o_sw, w_out, g1, ln_g, ln_b, sc2, sh2, router_wt, router_b, tm=512):
    t, d = x.shape
    tri = (np.arange(tm)[:, None] < np.arange(tm)[None, :]).astype(np.float32)
    tri = jnp.asarray(tri, BF16)
    row = lambda i: (i, 0)
    const = lambda i: (0, 0)
    col = lambda i: (0, i)
    vec = pl.BlockSpec((1, d), const)
    return pl.pallas_call(
        _post_mix_kernel,
        grid=(t // tm,),
        in_specs=[
            pl.BlockSpec((tm, d), row),
            pl.BlockSpec((tm, RWKV_W), row),
            pl.BlockSpec((tm, DSA_W), row),
            pl.BlockSpec((tm, SWA_W), row),
            pl.BlockSpec((d, d), const),
            vec, vec, vec, vec, vec,
            pl.BlockSpec((N_EXPERTS, d), const),
            pl.BlockSpec((N_EXPERTS, 1), const),
            pl.BlockSpec((tm, tm), const),
        ],
        out_specs=[
            pl.BlockSpec((tm, d), row),
            pl.BlockSpec((tm, d // 2), row),
            pl.BlockSpec((TOP_K, tm), col),
            pl.BlockSpec((TOP_K, tm), col),
            pl.BlockSpec((TOP_K, tm), col),
            pl.BlockSpec((N_EXPERTS, LANES), const),
        ],
        out_shape=[
            jax.ShapeDtypeStruct((t, d), F32),
            jax.ShapeDtypeStruct((t, d // 2), I32),
            jax.ShapeDtypeStruct((TOP_K, t), I32),
            jax.ShapeDtypeStruct((TOP_K, t), I32),
            jax.ShapeDtypeStruct((TOP_K, t), F32),
            jax.ShapeDtypeStruct((N_EXPERTS, LANES), F32),
        ],
        scratch_shapes=[pltpu.VMEM((N_EXPERTS, LANES), F32)],
        compiler_params=_params(("arbitrary",)),
        name="post_mix_router",
    )(x, o_rw, o_ds, o_sw, w_out, g1, ln_g, ln_b, sc2, sh2, router_wt, router_b, tri)


MOE_ROWS = 512
MOE_TILE = 256


def _pack_halves(v):
    w = v.shape[1] // 2
    bits = lax.bitcast_convert_type(v.astype(BF16).astype(F32), I32)
    return bits[:, :w] | lax.shift_right_logical(bits[:, w:], 16)


def _unpack_halves(p):
    return lax.bitcast_convert_type(p & -65536, F32), lax.bitcast_convert_type(p << 16, F32)


def _row_copy(src_ref, src_row, dst_ref, dst_row, sem):
    return pltpu.make_async_copy(src_ref.at[pl.ds(src_row, 1), :], dst_ref.at[pl.ds(dst_row, 1), :], sem)


def _dispatch_kernel(slot_hbm, h_ref, xs_in, xs_out, slot_smem, sem_tab, sem_rows):
    del xs_in
    i = pl.program_id(0)
    tab = pltpu.make_async_copy(slot_hbm.at[i], slot_smem, sem_tab)
    tab.start()
    tab.wait()

    def issue(tt, carry):
        for kx in range(TOP_K):
            _row_copy(h_ref, tt, xs_out, slot_smem[kx, tt], sem_rows).start(priority=kx % 2)
        return carry

    lax.fori_loop(0, MOE_TILE, issue, 0)

    def drain(tt, carry):
        for kx in range(TOP_K):
            _row_copy(h_ref, 0, xs_out, 0, sem_rows).wait()
        return carry

    lax.fori_loop(0, MOE_TILE, drain, 0)


def _dispatch(slot_tiles, rows, cap):
    t, d = rows.shape
    xs0 = jnp.zeros((cap, d), rows.dtype)
    return pl.pallas_call(
        _dispatch_kernel,
        grid=(t // MOE_TILE,),
        in_specs=[
            pl.BlockSpec(memory_space=pl.ANY),
            pl.BlockSpec((MOE_TILE, d), lambda i: (i, 0)),
            pl.BlockSpec(memory_space=pl.ANY),
        ],
        out_specs=pl.BlockSpec(memory_space=pl.ANY),
        out_shape=jax.ShapeDtypeStruct((cap, d), rows.dtype),
        scratch_shapes=[
            pltpu.SMEM((TOP_K, MOE_TILE), I32),
            pltpu.SemaphoreType.DMA,
            pltpu.SemaphoreType.DMA,
        ],
        input_output_aliases={2: 0},
        compiler_params=_params(("arbitrary",)),
        name="moe_dispatch",
    )(slot_tiles, rows, xs0)


def _expert_kernel(be_ref, nb_ref, xs_ref, w1_ref, w3_ref, w2_ref, ys_ref, w1b, w3b, w2b):
    b = pl.program_id(0)
    changed = (b == 0) | (be_ref[b] != be_ref[jnp.maximum(b - 1, 0)])

    @pl.when(changed & (b < nb_ref[0]))
    def _():
        w1b[...] = w1_ref[0, 0].astype(BF16)
        w3b[...] = w3_ref[0, 0].astype(BF16)
        w2b[...] = w2_ref[0, 0].astype(BF16)

    @pl.when(b < nb_ref[0])
    def _():
        x_hi, x_lo = _unpack_halves(xs_ref[...])
        x_hi, x_lo = x_hi.astype(BF16), x_lo.astype(BF16)
        half = x_hi.shape[1]
        a = _dot(x_hi, w1b[0:half, :]) + _dot(x_lo, w1b[half:2 * half, :])
        gte = _dot(x_hi, w3b[0:half, :]) + _dot(x_lo, w3b[half:2 * half, :])
        hmid = (a * _sigmoid(a) * gte).astype(BF16)
        ys_ref[...] = _pack_halves(_dot(hmid, w2b[...]))

    @pl.when(b >= nb_ref[0])
    def _():
        ys_ref[...] = jnp.zeros_like(ys_ref)


def _experts(block_e, n_used, xs, w1, w3, w2, layer):
    cap, dp = xs.shape
    d = 2 * dp
    nb = cap // MOE_ROWS
    grid_spec = pltpu.PrefetchScalarGridSpec(
        num_scalar_prefetch=2,
        grid=(nb,),
        in_specs=[
            pl.BlockSpec((MOE_ROWS, dp), lambda b, be, nu: (b, 0)),
            pl.BlockSpec((1, 1, d, D_EXPERT), lambda b, be, nu: (layer, be[b], 0, 0)),
            pl.BlockSpec((1, 1, d, D_EXPERT), lambda b, be, nu: (layer, be[b], 0, 0)),
            pl.BlockSpec((1, 1, D_EXPERT, d), lambda b, be, nu: (layer, be[b], 0, 0)),
        ],
        out_specs=pl.BlockSpec((MOE_ROWS, dp), lambda b, be, nu: (b, 0)),
        scratch_shapes=[
            pltpu.VMEM((d, D_EXPERT), BF16),
            pltpu.VMEM((d, D_EXPERT), BF16),
            pltpu.VMEM((D_EXPERT, d), BF16),
        ],
    )
    return pl.pallas_call(
        _expert_kernel,
        grid_spec=grid_spec,
        out_shape=jax.ShapeDtypeStruct((cap, dp), I32),
        compiler_params=_params(("arbitrary",)),
        name="moe_experts",
    )(block_e, n_used, xs, w1, w3, w2)


def _combine_kernel(slot_hbm, ys_hbm, x1_ref, h2_ref, gate_ref, sw1_ref, sw3_ref, sw2_ref,
                    g2_ref, lng_ref, lnb_ref, o_ref, slot_smem, gbuf, sem_tab, sem_rows):
    i = pl.program_id(0)
    tab = pltpu.make_async_copy(slot_hbm.at[i], slot_smem, sem_tab)
    tab.start()
    tab.wait()

    def issue(tt, carry):
        for kx in range(TOP_K):
            _row_copy(ys_hbm, slot_smem[kx, tt], gbuf.at[kx], tt, sem_rows).start(priority=kx % 2)
        return carry

    lax.fori_loop(0, MOE_TILE, issue, 0)

    h_hi, h_lo = _unpack_halves(h2_ref[...])
    h_hi, h_lo = h_hi.astype(BF16), h_lo.astype(BF16)
    half = h_hi.shape[1]
    a = _dot(h_hi, sw1_ref[0:half, :]) + _dot(h_lo, sw1_ref[half:2 * half, :])
    gte = _dot(h_hi, sw3_ref[0:half, :]) + _dot(h_lo, sw3_ref[half:2 * half, :])
    y = _dot((a * _sigmoid(a) * gte).astype(BF16), sw2_ref[...])

    def drain(tt, carry):
        for kx in range(TOP_K):
            _row_copy(ys_hbm, 0, gbuf.at[kx], 0, sem_rows).wait()
        return carry

    lax.fori_loop(0, MOE_TILE, drain, 0)

    gates = gate_ref[...]
    r_hi = jnp.zeros((MOE_TILE, half), F32)
    r_lo = jnp.zeros((MOE_TILE, half), F32)
    for kx in range(TOP_K):
        e_hi, e_lo = _unpack_halves(gbuf[kx])
        r_hi = r_hi + gates[:, kx:kx + 1] * e_hi
        r_lo = r_lo + gates[:, kx:kx + 1] * e_lo
    y = y + jnp.concatenate([r_hi, r_lo], axis=1)
    o_ref[...] = _layer_norm_rows(ALPHA * x1_ref[...] + g2_ref[...] * y, lng_ref[...], lnb_ref[...])


def _combine(slot_tiles, ys, x1, h2, gates_t, sw1, sw3, sw2, g2, ln_g, ln_b):
    t, d = x1.shape
    row = lambda i: (i, 0)
    const = lambda i: (0, 0)
    vec = pl.BlockSpec((1, d), const)
    return pl.pallas_call(
        _combine_kernel,
        grid=(t // MOE_TILE,),
        in_specs=[
            pl.BlockSpec(memory_space=pl.ANY),
            pl.BlockSpec(memory_space=pl.ANY),
            pl.BlockSpec((MOE_TILE, d), row),
            pl.BlockSpec((MOE_TILE, d // 2), row),
            pl.BlockSpec((MOE_TILE, TOP_K), row),
            pl.BlockSpec((d, D_EXPERT), const),
            pl.BlockSpec((d, D_EXPERT), const),
            pl.BlockSpec((D_EXPERT, d), const),
            vec, vec, vec,
        ],
        out_specs=pl.BlockSpec((MOE_TILE, d), row),
        out_shape=jax.ShapeDtypeStruct((t, d), F32),
        scratch_shapes=[
            pltpu.SMEM((TOP_K, MOE_TILE), I32),
            pltpu.VMEM((TOP_K, MOE_TILE, d // 2), I32),
            pltpu.SemaphoreType.DMA,
            pltpu.SemaphoreType.DMA,
        ],
        compiler_params=_params(("arbitrary",)),
        name="moe_combine",
    )(slot_tiles, ys, x1, h2, gates_t, sw1, sw3, sw2, g2, ln_g, ln_b)


def _pad_w_in(w_in_l):
    d = w_in_l.shape[0]
    pad = jnp.zeros((d, C_SQ[0] - N_ORIG_BEFORE_PAD), w_in_l.dtype)
    return jnp.concatenate([w_in_l[:, :N_ORIG_BEFORE_PAD], pad, w_in_l[:, N_ORIG_BEFORE_PAD:]], axis=1)


def _pad_lanes(v, width=LANES):
    v = v.reshape(1, -1)
    return jnp.pad(v, ((0, 0), (0, width - v.shape[1])))


def _moe_tables(eidx, rank, counts):
    t = eidx.shape[1]
    cnt = counts[:, 0].astype(I32)
    padded = (cnt + MOE_ROWS - 1) // MOE_ROWS * MOE_ROWS
    pad_end = jnp.cumsum(padded)
    pad_start = pad_end - padded
    e_ids = jnp.arange(N_EXPERTS, dtype=I32)
    start_of = jnp.sum(jnp.where(eidx[..., None] == e_ids, pad_start, 0), axis=-1)
    slot = start_of + rank
    slot_tiles = slot.reshape(TOP_K, t // MOE_TILE, MOE_TILE).transpose(1, 0, 2)
    cap = t * TOP_K + N_EXPERTS * MOE_ROWS
    nb = cap // MOE_ROWS
    blk_row = jnp.arange(nb, dtype=I32)[:, None] * MOE_ROWS
    block_e = jnp.minimum(jnp.sum((pad_end[None, :] <= blk_row).astype(I32), axis=1), N_EXPERTS - 1)
    n_used = (pad_end[-1] // MOE_ROWS).astype(I32).reshape(1)
    return slot_tiles, block_e, n_used, cap


def kernel(x, c, w_mod, b_mod, w_in, rwkv_mu, rwkv_w0, rwkv_w2, rwkv_a0, rwkv_a2, rwkv_g2, rwkv_k_k, rwkv_k_a, rwkv_r_k, rwkv_ln_g, rwkv_ln_b, dsa_kv_norm, dsa_w_uk, dsa_w_uv, dsa_ik_g, dsa_ik_b, swa_sinks, w_out, ln_mix_g, ln_mix_b, router_w, router_bias, exp_w1, exp_w3, exp_w2, sh_w1, sh_w3, sh_w2, ln_ffn_g, ln_ffn_b):
    bsz, t, d = x.shape
    assert bsz == 1 and d == D_MODEL
    depth = w_mod.shape[0]
    n_sl = SWA_HEADS + DSA_HEADS
    slopes = [2.0 ** (-8.0 * (j + 1.0) / n_sl) for j in range(n_sl)]
    swa_slopes, dsa_slopes = slopes[:SWA_HEADS], slopes[SWA_HEADS:]

    mod = _modulation(c, w_mod, b_mod)
    xs_cur = x[0]
    row1 = lambda v: v.reshape(1, -1)
    for l in range(depth):
        sh1, sc1, g1, sh2, sc2, g2 = [mod[l, :, j * d:(j + 1) * d] for j in range(6)]
        wp = _pad_w_in(w_in[l])
        w_hi = wp.astype(BF16)
        w_idx = wp[:, C_IDX[0]:C_IDX[1]]
        w_idx_lo = (w_idx - w_idx.astype(BF16).astype(F32)).astype(BF16)
        rkv, lora, dq, ckv, iq, ikw, sq, skv = _input_proj(
            xs_cur, sc1, sh1, w_hi, w_idx_lo, row1(dsa_kv_norm[l]),
            _pad_lanes(dsa_ik_g[l]), _pad_lanes(dsa_ik_b[l]))
        o_rw = _rwkv_mix(rkv, lora, row1(rwkv_mu[l]), row1(rwkv_w0[l]), rwkv_w2[l], row1(rwkv_a0[l]),
                         rwkv_a2[l], rwkv_g2[l], row1(rwkv_k_k[l]), row1(rwkv_k_a[l]), row1(rwkv_r_k[l]),
                         row1(rwkv_ln_g[l]), row1(rwkv_ln_b[l]))
        o_ds = _dsa_mix(dq, iq, ikw, ckv, dsa_w_uk[l], dsa_w_uv[l], dsa_slopes)
        o_sw = _swa_mix(sq, skv, _pad_lanes(swa_sinks[l]), swa_slopes)
        x1, h2, eidx, rank, gates, counts = _post_mix(
            xs_cur, o_rw, o_ds, o_sw, w_out[l].astype(BF16), g1, row1(ln_mix_g[l]), row1(ln_mix_b[l]),
            sc2, sh2, router_w[l].T, router_bias[l].reshape(-1, 1))
        slot_tiles, block_e, n_used, cap = _moe_tables(eidx, rank, counts)
        xs_sorted = _dispatch(slot_tiles, h2, cap)
        ys = _experts(block_e, n_used, xs_sorted, exp_w1, exp_w3, exp_w2, l)
        xs_cur = _combine(slot_tiles, ys, x1, h2, gates.T, sh_w1[l].astype(BF16), sh_w3[l].astype(BF16),
                          sh_w2[l].astype(BF16), g2, row1(ln_ffn_g[l]), row1(ln_ffn_b[l]))
    return xs_cur[None]
```

```python
import functools
import math

import jax
import jax.numpy as jnp
import numpy as np
from jax import lax
from jax.experimental import pallas as pl
from jax.experimental.pallas import tpu as pltpu

F32 = jnp.float32
BF16 = jnp.bfloat16
I32 = jnp.int32
HI = lax.Precision.HIGHEST

D_MODEL = 1024
DEPTH = 4
HEAD_DIM = 64
RWKV_HEADS = 6
DSA_HEADS = 4
SWA_HEADS = 6
SWA_KV_HEADS = 2
RWKV_W = RWKV_HEADS * HEAD_DIM
DSA_W = DSA_HEADS * HEAD_DIM
SWA_W = SWA_HEADS * HEAD_DIM
DECAY_LORA = 64
AAA_LORA = 64
GATE_LORA = 128
GN_EPS = 64e-5
KV_LORA = 128
IDX_HEADS = 4
IDX_DIM = 64
TOPK_MAX = 256
WINDOW = 128
N_EXPERTS = 64
TOP_K = 8
N_GROUPS = 8
TOPK_GROUPS = 4
D_EXPERT = 256
ROUTED_SCALE = 2.5
ALPHA = (2 * DEPTH) ** 0.25
LN_EPS = 1e-5
NEG = -1e30
INT_MIN = -(2 ** 31)

LANES = 128
VMEM_LIMIT = 56 * 1024 * 1024

C_RKV = (0, 1152)
C_LORA = (1152, 1408)
C_DQ = (1408, 1664)
C_CKV = (1664, 1792)
C_IDX = (1792, 2176)
C_SQ = (2176, 2560)
C_SKV = (2560, 2816)
P_PAD = 2816
N_ORIG_BEFORE_PAD = 2116


def _dot(a, b, prec=None):
    return jnp.dot(a, b, preferred_element_type=F32, precision=prec)


def _dot_nt(a, b, prec=None):
    return lax.dot_general(a, b, (((1,), (1,)), ((), ())), preferred_element_type=F32, precision=prec)


def _split2(a):
    a_hi = a.astype(BF16)
    return a_hi, (a - a_hi.astype(F32)).astype(BF16)


def _bdot(a, b):
    return _dot(a.astype(BF16), b.astype(BF16))


def _bdot_nt(a, b):
    return _dot_nt(a.astype(BF16), b.astype(BF16))


def _dot2(a, b_exact):
    a_hi, a_lo = _split2(a)
    return _dot(a_hi, b_exact) + _dot(a_lo, b_exact)


def _dot2_l(a_exact, b):
    b_hi, b_lo = _split2(b)
    return _dot(a_exact, b_hi) + _dot(a_exact, b_lo)


def _dot3(a, b):
    a_hi, a_lo = _split2(a)
    b_hi, b_lo = _split2(b)
    return _dot(a_hi, b_hi) + (_dot(a_lo, b_hi) + _dot(a_hi, b_lo))


def _iota(shape, dim):
    return lax.broadcasted_iota(I32, shape, dim)


def _sigmoid(x):
    return 1.0 / (1.0 + jnp.exp(-x))


def _layer_norm_rows(v, g, b):
    mu = jnp.mean(v, axis=-1, keepdims=True)
    d = v - mu
    var = jnp.mean(d * d, axis=-1, keepdims=True)
    return d * lax.rsqrt(var + LN_EPS) * g + b


def _params(sem):
    return pltpu.CompilerParams(dimension_semantics=sem, vmem_limit_bytes=VMEM_LIMIT)


def _mod_kernel(c_ref, w_ref, b_ref, o_ref):
    c = c_ref[...]
    cond = c * _sigmoid(c)
    o_ref[0] = _dot(cond, w_ref[0], HI) + b_ref[0]


def _modulation(c, w_mod, b_mod):
    depth, d, d6 = w_mod.shape
    c8 = jnp.broadcast_to(c, (8, d))
    nj = d6 // d
    out = pl.pallas_call(
        _mod_kernel,
        grid=(depth, nj),
        in_specs=[
            pl.BlockSpec((8, d), lambda l, j: (0, 0)),
            pl.BlockSpec((1, d, d), lambda l, j: (l, 0, j)),
            pl.BlockSpec((1, 1, d), lambda l, j: (l, 0, j)),
        ],
        out_specs=pl.BlockSpec((1, 8, d), lambda l, j: (l, 0, j)),
        out_shape=jax.ShapeDtypeStruct((depth, 8, d6), F32),
        compiler_params=_params(("arbitrary", "arbitrary")),
        name="modulation",
    )(c8, w_mod, b_mod.reshape(depth, 1, d6))
    return out[:, 0:1, :]


def _proj_kernel(x_ref, sc_ref, sh_ref, w_ref, wlo_ref, kvn_ref, ikg_ref, ikb_ref,
                 rkv_ref, lora_ref, dq_ref, ckv_ref, iq_ref, ikw_ref, sq_ref, skv_ref):
    h = x_ref[...] * (1.0 + sc_ref[...]) + sh_ref[...]
    hb = h.astype(BF16)
    hl = (h - hb.astype(F32)).astype(BF16)

    def mm(c):
        return _dot(hb, w_ref[:, c[0]:c[1]])

    rkv_ref[...] = mm(C_RKV)
    lora_ref[...] = mm(C_LORA)
    dq_ref[...] = mm(C_DQ)
    sq_ref[...] = mm(C_SQ)
    skv_ref[...] = mm(C_SKV)
    ckv = mm(C_CKV)
    ckv_ref[...] = ckv * lax.rsqrt(jnp.mean(ckv * ckv, axis=-1, keepdims=True) + 1e-6) * kvn_ref[...]
    idx = mm(C_IDX) + _dot(hl, w_ref[:, C_IDX[0]:C_IDX[1]]) + _dot(hb, wlo_ref[...])
    iq_ref[...] = idx[:, 0:256]
    g3 = idx[:, 256:384]
    lane = _iota(g3.shape, 1)
    isk = lane < IDX_DIM
    mu = jnp.sum(jnp.where(isk, g3, 0.0), axis=-1, keepdims=True) * (1.0 / IDX_DIM)
    dk = jnp.where(isk, g3 - mu, 0.0)
    var = jnp.sum(dk * dk, axis=-1, keepdims=True) * (1.0 / IDX_DIM)
    ikn = dk * lax.rsqrt(var + LN_EPS) * ikg_ref[...] + ikb_ref[...]
    ikw_ref[...] = jnp.where(isk, ikn, g3 * (IDX_HEADS ** -0.5 * IDX_DIM ** -0.5))


def _input_proj(x, sc, sh, w_hi, w_idx_lo, kvn, ikg, ikb, tm=512):
    t, d = x.shape
    widths = [C_RKV, C_LORA, C_DQ, C_CKV, (0, 256), (0, 128), C_SQ, C_SKV]
    widths = [c[1] - c[0] for c in widths]
    const = lambda i: (0, 0)
    row = lambda i: (i, 0)
    return pl.pallas_call(
        _proj_kernel,
        grid=(t // tm,),
        in_specs=[
            pl.BlockSpec((tm, d), row),
            pl.BlockSpec((1, d), const),
            pl.BlockSpec((1, d), const),
            pl.BlockSpec((d, P_PAD), const),
            pl.BlockSpec((d, C_IDX[1] - C_IDX[0]), const),
            pl.BlockSpec((1, KV_LORA), const),
            pl.BlockSpec((1, LANES), const),
            pl.BlockSpec((1, LANES), const),
        ],
        out_specs=[pl.BlockSpec((tm, w), row) for w in widths],
        out_shape=[jax.ShapeDtypeStruct((t, w), F32) for w in widths],
        compiler_params=_params(("arbitrary",)),
        name="input_proj",
    )(x, sc, sh, w_hi, w_idx_lo, kvn, ikg, ikb)


RW_CHUNK = 64
RW_UNROLL = 2


def _rwkv_kernel(r_ref, k_ref, v_ref, lora_ref, rp_ref, kp_ref, vp_ref, lp_ref,
                 mur_ref, muk_ref, muv_ref, mul_ref, w0_ref, w2_ref, a0_ref, a2_ref, g2_ref,
                 kk_ref, ka_ref, rk_ref, lng_ref, lnb_ref, o_ref,
                 h_ref, y_ref, st_ref, wm_ref, ar_ref, rs_ref, vs_ref, lt_ref, zm_ref, y0_ref, gc_ref, *, tg):
    g = pl.program_id(0)
    c64 = RW_CHUNK
    nch = tg // c64
    npair = RWKV_W // LANES
    pair_lanes = [slice(p * LANES, (p + 1) * LANES) for p in range(npair)]
    lane = _iota((1, LANES), 1)
    first = g == 0

    @pl.when(first)
    def _():
        h_ref[...] = jnp.zeros_like(h_ref)

    rowid = _iota((tg, 1), 0)

    def shift_mix(cur_ref, prev_ref, mu_ref):
        cur = cur_ref[...]
        prev_row = jnp.where(first, 0.0, prev_ref[7:8, :])
        rolled = pltpu.roll(cur, 1, 0)
        shifted = jnp.where(rowid == 0, prev_row, rolled)
        return cur + (shifted - cur) * mu_ref[...]

    r = shift_mix(r_ref, rp_ref, mur_ref)
    k = shift_mix(k_ref, kp_ref, muk_ref)
    v = shift_mix(v_ref, vp_ref, muv_ref)
    lo = shift_mix(lora_ref, lp_ref, mul_ref)
    wl = lo[:, 0:DECAY_LORA]
    al = lo[:, DECAY_LORA:DECAY_LORA + AAA_LORA]
    gl = lo[:, 128:256]

    zw = -(w0_ref[...] + _dot3(jnp.tanh(wl), w2_ref[...]))
    softplus = jnp.maximum(zw, 0.0) + jnp.log(1.0 + jnp.exp(-jnp.abs(zw)))
    lw = -jnp.exp(-softplus - 0.5)
    a = _sigmoid(a0_ref[...] + _bdot(al, a2_ref[...]))
    gate = _bdot(_sigmoid(gl), g2_ref[...])

    ri = _iota((LANES, LANES), 0) // HEAD_DIM
    ci = _iota((LANES, LANES), 1) // HEAD_DIM
    bones = jnp.where(ri == ci, 1.0, 0.0).astype(BF16)

    def head_sum(xf):
        return jnp.concatenate([_dot2(xf[:, pl_], bones) for pl_ in pair_lanes], axis=1)

    kk = k * kk_ref[...]
    kk = kk / jnp.maximum(jnp.sqrt(head_sum(kk * kk)), 1e-12)
    k2 = k * (1.0 + (a - 1.0) * ka_ref[...])
    bonus = head_sum(r * k2 * rk_ref[...]) * v
    bvec = a * kk

    st_ref[0] = r
    st_ref[1] = k2
    st_ref[2] = v
    st_ref[3] = lw
    st_ref[4] = kk
    st_ref[5] = bvec

    rr = _iota((LANES, LANES), 0)
    cc = _iota((LANES, LANES), 1)
    same = (rr // c64) == (cc // c64)
    strict = same & ((rr % c64) > (cc % c64))
    incl = same & ((rr % c64) >= (cc % c64))
    eye = jnp.where(rr == cc, 1.0, 0.0)
    tril = jnp.where(_iota((c64, c64), 0) >= _iota((c64, c64), 1), 1.0, 0.0).astype(BF16)
    lo_half = lane < HEAD_DIM

    def stack(xc):
        return jnp.concatenate([jnp.where(lo_half, xc, 0.0), jnp.where(lo_half, 0.0, xc)], axis=0)

    def prepare(c, carry):
        chunks = [c * RW_UNROLL + j for j in range(RW_UNROLL)]
        sls = [pl.ds(pl.multiple_of(cj * c64, c64), c64) for cj in chunks]
        items = [(p, j) for j in range(RW_UNROLL) for p in range(npair)]
        pairs = range(len(items))
        idx = [p * nch + chunks[j] for p, j in items]
        ld = lambda q: [st_ref[q, sls[j], pair_lanes[p]] for p, j in items]
        rc, kc, vc, lwc, kkc, bc = ld(0), ld(1), ld(2), ld(3), ld(4), ld(5)
        cum = [_dot2_l(tril, lwc[p]) for p in pairs]
        tot = [cum[p][c64 - 1:c64, :] for p in pairs]
        g_in = [jnp.exp(cum[p]) for p in pairs]
        g_ex = [jnp.exp(cum[p] - lwc[p]) for p in pairs]
        g_inv = [jnp.exp(-cum[p]) for p in pairs]
        g_rest = [jnp.exp(tot[p] - cum[p]) for p in pairs]
        a_s = [stack(-kkc[p] * g_ex[p]).astype(BF16) for p in pairs]
        b_s = [stack(bc[p] * g_inv[p]).astype(BF16) for p in pairs]
        k_s = [stack(kc[p] * g_inv[p]).astype(BF16) for p in pairs]
        r_s = [stack(rc[p] * g_in[p]).astype(BF16) for p in pairs]
        v_s = [stack(vc[p]).astype(BF16) for p in pairs]
        nmat = [jnp.where(strict, _dot_nt(a_s[p], b_s[p]), 0.0) for p in pairs]
        aak = [jnp.where(strict, _dot_nt(a_s[p], k_s[p]), 0.0) for p in pairs]
        arb = [jnp.where(incl, _dot_nt(r_s[p], b_s[p]), 0.0) for p in pairs]
        ark = [jnp.where(incl, _dot_nt(r_s[p], k_s[p]), 0.0) for p in pairs]
        tinv = [eye + nmat[p] for p in pairs]
        pw = nmat
        for _ in range(5):
            pw = [_bdot(pw[p], pw[p]) for p in pairs]
            tinv = [_bdot(tinv[p], eye + pw[p]) for p in pairs]
        tinv = [tinv[p].astype(BF16) for p in pairs]
        akv = [_bdot(aak[p], v_s[p]).astype(BF16) for p in pairs]
        wmat = [_dot(tinv[p], a_s[p]) for p in pairs]
        zmat = [_dot(tinv[p], akv[p]) for p in pairs]
        y0 = [_bdot(ark[p], v_s[p]) for p in pairs]
        for p in pairs:
            wm_ref[idx[p]] = wmat[p].astype(BF16)
            zm_ref[idx[p]] = zmat[p]
            y0_ref[idx[p]] = y0[p]
            ar_ref[idx[p]] = arb[p].astype(BF16)
            rs_ref[idx[p]] = r_s[p]
            vs_ref[idx[p]] = v_s[p]
            lt_ref[idx[p]] = jnp.concatenate([stack(bc[p] * g_rest[p]), stack(kc[p] * g_rest[p])],
                                             axis=0).T.astype(BF16)
            gc_ref[idx[p]] = jnp.broadcast_to(jnp.sum(eye * jnp.exp(tot[p]), axis=1, keepdims=True),
                                              (LANES, LANES))
        return carry

    lax.fori_loop(0, nch // RW_UNROLL, prepare, 0)

    def advance(c, carry):
        sl = pl.ds(pl.multiple_of(c * c64, c64), c64)
        pairs = range(npair)
        idx = [p * nch + c for p in pairs]
        hst = [h_ref[p] for p in pairs]
        hb = [hst[p].astype(BF16) for p in pairs]
        u = [_dot(wm_ref[idx[p]], hb[p]) + zm_ref[idx[p]] for p in pairs]
        rh = [_dot(rs_ref[idx[p]], hb[p]) for p in pairs]
        ub = [u[p].astype(BF16) for p in pairs]
        hnew = [_dot(lt_ref[idx[p]], jnp.concatenate([ub[p], vs_ref[idx[p]]], axis=0)) for p in pairs]
        au = [_dot(ar_ref[idx[p]], ub[p]) for p in pairs]
        for p in pairs:
            h_ref[p] = gc_ref[idx[p]] * hst[p] + hnew[p]
            ys = rh[p] + au[p] + y0_ref[idx[p]]
            y_ref[sl, pair_lanes[p]] = ys[0:c64, :] + ys[c64:2 * c64, :]
        return carry

    lax.fori_loop(0, nch, advance, 0)

    y = y_ref[...]
    mean = head_sum(y) * (1.0 / HEAD_DIM)
    dy = y - mean
    var = head_sum(dy * dy) * (1.0 / HEAD_DIM)
    o = dy * lax.rsqrt(var + GN_EPS) * lng_ref[...] + lnb_ref[...]
    o_ref[...] = (o + bonus) * gate


def _rwkv_mix(rkv, lora, mu, w0, w2, a0, a2, g2, k_k, k_a, r_k, ln_g, ln_b, tg=512):
    t = rkv.shape[0]
    w = RWKV_W
    npair = w // LANES
    nmat = npair * (tg // RW_CHUNK)
    mu_r, mu_k, mu_v, mu_l = mu[:, 0:w], mu[:, w:2 * w], mu[:, 2 * w:3 * w], mu[:, 3 * w:3 * w + 256]
    blk = lambda off: pl.BlockSpec((tg, w), lambda g: (g, off))
    prev = lambda off: pl.BlockSpec((8, w), lambda g: (jnp.maximum(g * (tg // 8) - 1, 0), off))
    vec = pl.BlockSpec((1, w), lambda g: (0, 0))
    full = lambda rows: pl.BlockSpec((rows, w), lambda g: (0, 0))
    return pl.pallas_call(
        functools.partial(_rwkv_kernel, tg=tg),
        grid=(t // tg,),
        in_specs=[
            blk(0), blk(1), blk(2),
            pl.BlockSpec((tg, 256), lambda g: (g, 0)),
            prev(0), prev(1), prev(2),
            pl.BlockSpec((8, 256), lambda g: (jnp.maximum(g * (tg // 8) - 1, 0), 0)),
            vec, vec, vec,
            pl.BlockSpec((1, 256), lambda g: (0, 0)),
            vec, full(DECAY_LORA), vec, full(AAA_LORA), full(GATE_LORA),
            vec, vec, vec, vec, vec,
        ],
        out_specs=pl.BlockSpec((tg, w), lambda g: (g, 0)),
        out_shape=jax.ShapeDtypeStruct((t, w), F32),
        scratch_shapes=[
            pltpu.VMEM((npair, LANES, LANES), F32),
            pltpu.VMEM((tg, w), F32),
            pltpu.VMEM((6, tg, w), F32),
            pltpu.VMEM((nmat, LANES, LANES), BF16),
            pltpu.VMEM((nmat, LANES, LANES), BF16),
            pltpu.VMEM((nmat, LANES, LANES), BF16),
            pltpu.VMEM((nmat, LANES, LANES), BF16),
            pltpu.VMEM((nmat, LANES, 2 * LANES), BF16),
            pltpu.VMEM((nmat, LANES, LANES), F32),
            pltpu.VMEM((nmat, LANES, LANES), F32),
            pltpu.VMEM((nmat, LANES, LANES), F32),
        ],
        compiler_params=_params(("arbitrary",)),
        name="rwkv7_mix",
    )(rkv, rkv, rkv, lora, rkv, rkv, rkv, lora,
      mu_r, mu_k, mu_v, mu_l, w0, w2, a0, a2, g2, k_k, k_a, r_k, ln_g, ln_b)


DSA_QB = 256
DSA_KC = 1024
DSA_SUB = 512
CNT_ROWS = 64
TIE_BLK = 128
BIS_STEPS = 2


def _float_key(v):
    bits = lax.bitcast_convert_type(v, I32)
    return bits ^ ((bits >> 31) & 0x7FFFFFFF)


def _dsa_kernel(dq_ref, iq_ref, ikw_ref, ikx_ref, kf_ref, vft_ref, wuk_ref, wuv_ref, tril_ref, slc_ref,
                o_ref, sc_ref, acc_ref):
    i = pl.program_id(0)
    qb, kc, sc_rows = DSA_QB, DSA_KC, DSA_SUB
    nh = DSA_HEADS
    t0 = i * qb
    nch = (t0 + qb + kc - 1) // kc
    tq = t0 + _iota((1, qb), 1)

    iq = iq_ref[...]
    iq_hi = iq.astype(BF16).astype(F32)
    iq_lo = iq - iq_hi
    lhs = []
    for h in range(IDX_HEADS):
        s = slice(h * IDX_DIM, (h + 1) * IDX_DIM)
        lhs.append(jnp.concatenate([iq_hi[:, s], iq_hi[:, s], iq_lo[:, s], iq_lo[:, s]], axis=1))
    lhs_t = jnp.concatenate(lhs, axis=0).T.astype(BF16)
    ikw_t = ikw_ref[...].T
    iw = [ikw_t[IDX_DIM + h:IDX_DIM + h + 1, :] for h in range(IDX_HEADS)]

    def score_body(ch, carry, masked):
        m1, m2 = carry
        sr = sc_rows
        for sub in range(kc // sr):
            k0 = pl.multiple_of(ch * kc + sub * sr, sr)
            s_all = _dot(ikx_ref[pl.ds(k0, sr), :], lhs_t)
            acc = jnp.zeros((sr, qb), F32)
            for h in range(IDX_HEADS):
                acc = acc + jnp.maximum(s_all[:, h * qb:(h + 1) * qb], 0.0) * iw[h]
            acc = jnp.where(acc == 0.0, 0.0, acc)
            key = _float_key(acc)
            if masked:
                causal = (k0 + _iota((sr, 1), 0)) <= tq
                key = jnp.where(causal, key, INT_MIN)
                acc = jnp.where(causal, acc, -jnp.inf)
            sc_ref[pl.ds(k0, sr), :] = key
            for j in range(sr // LANES):
                xj = acc[j * LANES:(j + 1) * LANES, :]
                m2 = jnp.maximum(m2, jnp.minimum(m1, xj))
                m1 = jnp.maximum(m1, xj)
        return m1, m2

    ninf = jnp.full((LANES, qb), -jnp.inf, F32)
    n_below = t0 // kc
    top2 = lax.fori_loop(0, n_below, functools.partial(score_body, masked=False), (ninf, ninf))
    m1, m2 = lax.fori_loop(n_below, nch, functools.partial(score_body, masked=True), top2)

    def count_ge(cand):
        def body(ch, acc):
            for j in range(kc // CNT_ROWS):
                kj = pl.multiple_of(ch * kc + j * CNT_ROWS, CNT_ROWS)
                acc = acc + jnp.where(sc_ref[pl.ds(kj, CNT_ROWS), :] >= cand, 1.0, 0.0)
            return acc
        acc = lax.fori_loop(0, nch, body, jnp.zeros((CNT_ROWS, qb), F32))
        return jnp.sum(acc, axis=0, keepdims=True)

    k_row = jnp.minimum(tq + 1, TOPK_MAX).astype(F32)
    hi0 = _float_key(jnp.max(m1, axis=0, keepdims=True))
    lo0 = jnp.minimum(_float_key(jnp.min(m2, axis=0, keepdims=True)), hi0)
    c_pos = count_ge(jnp.ones((1, qb), I32))
    c_nonneg = count_ge(jnp.zeros((1, qb), I32))
    at_zero = (c_pos < k_row) & (c_nonneg >= k_row)
    above = c_pos >= k_row
    lo0 = jnp.where(at_zero, 0, jnp.where(above, jnp.maximum(lo0, 1), lo0))
    hi0 = jnp.where(at_zero, 0, jnp.where(above, hi0, jnp.minimum(hi0, -1)))
    lo0 = jnp.minimum(lo0, hi0)

    def open_rows(lo, hi):
        return jnp.max(jnp.where(lo < hi, 1.0, 0.0))

    def bis_body(st):
        lo, hi, _ = st
        for _ in range(BIS_STEPS):
            mid = (lo | hi) - ((lo ^ hi) >> 1)
            c = count_ge(mid)
            ge = c >= k_row
            lo, hi = jnp.where(ge, mid, lo), jnp.where(c == k_row, mid, jnp.where(ge, hi, mid - 1))
        return lo, hi, open_rows(lo, hi)

    thr, _, _ = lax.while_loop(lambda st: st[2] > 0.5, bis_body, (lo0, hi0, open_rows(lo0, hi0)))

    dq = dq_ref[...]
    slc = slc_ref[...]
    qaug = []
    for h in range(nh):
        ql = _bdot(dq[:, h * HEAD_DIM:(h + 1) * HEAD_DIM], wuk_ref[h]) * HEAD_DIM ** -0.5
        qaug.append(jnp.concatenate([ql, jnp.broadcast_to(slc[h:h + 1, :], (qb, LANES))], axis=1))
    qaug_t = jnp.concatenate(qaug, axis=0).T.astype(BF16)
    acc_ref[...] = jnp.zeros_like(acc_ref)

    nsub = kc // sc_rows

    def sub_starts(ch):
        return [pl.multiple_of(ch * kc + sub * sc_rows, sc_rows) for sub in range(nsub)]

    def logits(k0):
        return _dot(kf_ref[pl.ds(k0, sc_rows), :], qaug_t)

    def attend(k0, lg_all, sel, m_old):
        ps, m_new = [], []
        for h in range(nh):
            cols = slice(h * qb, (h + 1) * qb)
            lg = jnp.where(sel, lg_all[:, cols], NEG)
            mh = jnp.maximum(m_old[:, cols], jnp.max(lg, axis=0, keepdims=True))
            ps.append(jnp.exp((lg - mh).astype(BF16)))
            m_new.append(mh)
        m_new = jnp.concatenate(m_new, axis=1)
        pv = _dot(vft_ref[:, pl.ds(k0, sc_rows)], jnp.concatenate(ps, axis=1))
        acc_ref[...] = jnp.exp(m_old - m_new) * acc_ref[...] + pv
        return m_new

    m_init = jnp.full((1, nh * qb), NEG, F32)

    need = k_row - count_ge(thr + 1)
    tril = tril_ref[...]

    def body(ch, carry):
        tie_run, m_old = carry
        ks = sub_starts(ch)
        lgs = [logits(k0) for k0 in ks]
        keys = [sc_ref[pl.ds(k0, sc_rows), :] for k0 in ks]
        blocks = [slice(j * TIE_BLK, (j + 1) * TIE_BLK) for j in range(sc_rows // TIE_BLK)]
        prefs = [[_dot(tril, jnp.where(key[bl, :] == thr, 1.0, 0.0).astype(BF16)) for bl in blocks] for key in keys]
        for k0, lg, key, pref in zip(ks, lgs, keys, prefs):
            ranks = []
            for pj in pref:
                ranks.append(tie_run + pj)
                tie_run = tie_run + pj[TIE_BLK - 1:TIE_BLK, :]
            sel = (key > thr) | ((key == thr) & (jnp.concatenate(ranks, axis=0) <= need))
            m_old = attend(k0, lg, sel, m_old)
        return tie_run, m_old

    lax.fori_loop(0, nch, body, (jnp.zeros((1, qb), F32), m_init))

    acc = acc_ref[...]
    o_lat = acc[0:KV_LORA, :] / acc[KV_LORA:KV_LORA + 1, :]
    outs = [_bdot(o_lat[:, h * qb:(h + 1) * qb].T, wuv_ref[h]) for h in range(nh)]
    o_ref[...] = jnp.concatenate(outs, axis=1)


DSA_VROWS = KV_LORA + 16


def _dsa_mix(dq, iq, ikw, ckv, w_uk, w_uv, slopes):
    t = dq.shape[0]
    assert t <= LANES * 256
    ikn = ikw[:, 0:IDX_DIM]
    ik_hi, ik_lo = _split2(ikn)
    ikx = jnp.concatenate([ik_hi, ik_lo, ik_hi, ik_lo], axis=1)
    ckv_b = ckv.astype(BF16)
    pos = jnp.arange(t, dtype=I32)
    pa = (pos // LANES).astype(BF16)[:, None]
    pb = (pos % LANES).astype(BF16)[:, None]
    kf = jnp.concatenate([ckv_b, pa, pa, pa, pb, pb, pb, jnp.zeros((t, LANES - 6), BF16)], axis=1)
    vft = jnp.concatenate([ckv_b.T, jnp.ones((1, t), BF16), jnp.zeros((DSA_VROWS - KV_LORA - 1, t), BF16)], axis=0)
    cols = []
    for sl in slopes:
        for coef in (sl * LANES, sl):
            c_hi = jnp.asarray(coef, F32).astype(BF16)
            r1 = jnp.asarray(coef, F32) - c_hi.astype(F32)
            c_mid = r1.astype(BF16)
            c_lo = (r1 - c_mid.astype(F32)).astype(BF16)
            cols += [c_hi.astype(F32), c_mid.astype(F32), c_lo.astype(F32)]
    slc = jnp.stack(cols).reshape(DSA_HEADS, 6)
    slc = jnp.pad(slc, ((0, 8 - DSA_HEADS), (0, LANES - 6)))
    assert t % DSA_KC == 0
    kc = TIE_BLK
    tril = jnp.asarray((np.arange(kc)[:, None] >= np.arange(kc)[None, :]).astype(np.float32), BF16)
    row = lambda i: (i, 0)
    const2 = lambda i: (0, 0)
    const3 = lambda i: (0, 0, 0)
    resident = lambda shape: pl.BlockSpec(shape, const2, pipeline_mode=pl.Buffered(1))
    return pl.pallas_call(
        _dsa_kernel,
        grid=(t // DSA_QB,),
        in_specs=[
            pl.BlockSpec((DSA_QB, DSA_W), row),
            pl.BlockSpec((DSA_QB, IDX_HEADS * IDX_DIM), row),
            pl.BlockSpec((DSA_QB, LANES), row),
            resident((t, 4 * IDX_DIM)),
            resident((t, 2 * LANES)),
            resident((DSA_VROWS, t)),
            pl.BlockSpec((DSA_HEADS, HEAD_DIM, KV_LORA), const3),
            pl.BlockSpec((DSA_HEADS, KV_LORA, HEAD_DIM), const3),
            resident((kc, kc)),
            pl.BlockSpec((8, LANES), const2),
        ],
        out_specs=pl.BlockSpec((DSA_QB, DSA_W), row),
        out_shape=jax.ShapeDtypeStruct((t, DSA_W), F32),
        scratch_shapes=[
            pltpu.VMEM((t, DSA_QB), I32),
            pltpu.VMEM((DSA_VROWS, DSA_HEADS * DSA_QB), F32),
        ],
        compiler_params=_params(("arbitrary",)),
        name="dsa_mix",
    )(dq, iq, ikw, ikx, kf, vft, w_uk, w_uv, tril, slc)


def _swa_kernel(q_ref, kv_ref, kvp_ref, sink_ref, o_ref, *, slopes):
    i = pl.program_id(0)
    w = WINDOW
    gsz = SWA_HEADS // SWA_KV_HEADS
    q = q_ref[...]
    kv = kv_ref[...]
    kvp = kvp_ref[...]
    qi = _iota((w, 2 * w), 0)
    kj = _iota((w, 2 * w), 1)
    dist = qi + w - kj
    in_band = (dist >= 0) & (dist < w)
    valid = [in_band & ((kj >= w) | (i > 0))] + [in_band] * (SWA_BLOCKS - 1)
    distf = dist.astype(F32)
    sinks = sink_ref[...]
    rows = [slice(b * w, (b + 1) * w) for b in range(SWA_BLOCKS)]
    prev = [kvp] + [kv[rows[b], :] for b in range(SWA_BLOCKS - 1)]
    k2 = [[jnp.concatenate([prev[b][:, g * HEAD_DIM:(g + 1) * HEAD_DIM],
                            kv[rows[b], g * HEAD_DIM:(g + 1) * HEAD_DIM]], axis=0).astype(BF16)
           for g in range(SWA_KV_HEADS)] for b in range(SWA_BLOCKS)]
    v2 = [[jnp.concatenate([prev[b][:, w + g * HEAD_DIM:w + (g + 1) * HEAD_DIM],
                            kv[rows[b], w + g * HEAD_DIM:w + (g + 1) * HEAD_DIM]], axis=0).astype(BF16)
           for g in range(SWA_KV_HEADS)] for b in range(SWA_BLOCKS)]
    items = [(b, hd) for b in range(SWA_BLOCKS) for hd in range(SWA_HEADS)]
    s = [_dot_nt(q[rows[b], hd * HEAD_DIM:(hd + 1) * HEAD_DIM].astype(BF16), k2[b][hd // gsz]) * HEAD_DIM ** -0.5
         for b, hd in items]
    s = [jnp.where(valid[b], s[n] - slopes[hd] * distf, NEG) for n, (b, hd) in enumerate(items)]
    sink = [sinks[0:1, hd:hd + 1] for _, hd in items]
    m = [jnp.maximum(jnp.max(s[n], axis=1, keepdims=True), sink[n]) for n in range(len(items))]
    e = [jnp.exp(s[n] - m[n]) for n in range(len(items))]
    p = [e[n] / (jnp.sum(e[n], axis=1, keepdims=True) + jnp.exp(sink[n] - m[n])) for n in range(len(items))]
    outs = [_dot(p[n].astype(BF16), v2[b][hd // gsz]) for n, (b, hd) in enumerate(items)]
    for b in range(SWA_BLOCKS):
        o_ref[rows[b], :] = jnp.concatenate(outs[b * SWA_HEADS:(b + 1) * SWA_HEADS], axis=1)


SWA_BLOCKS = 2


def _swa_mix(sq, skv, sinks, slopes):
    t = sq.shape[0]
    w = WINDOW
    step = SWA_BLOCKS * w
    return pl.pallas_call(
        functools.partial(_swa_kernel, slopes=slopes),
        grid=(t // step,),
        in_specs=[
            pl.BlockSpec((step, SWA_W), lambda i: (i, 0)),
            pl.BlockSpec((step, 2 * w), lambda i: (i, 0)),
            pl.BlockSpec((w, 2 * w), lambda i: (jnp.maximum(SWA_BLOCKS * i - 1, 0), 0)),
            pl.BlockSpec((1, LANES), lambda i: (0, 0)),
        ],
        out_specs=pl.BlockSpec((step, SWA_W), lambda i: (i, 0)),
        out_shape=jax.ShapeDtypeStruct((t, SWA_W), F32),
        compiler_params=_params(("arbitrary",)),
        name="swa_mix",
    )(sq, skv, skv, sinks)


def _post_mix_kernel(x_ref, orw_ref, ods_ref, osw_ref, wout_ref, g1_ref, lng_ref, lnb_ref,
                     sc2_ref, sh2_ref, rwt_ref, rb_ref, tri_ref,
                     x1_ref, h2_ref, eidx_ref, rank_ref, gate_ref, cnt_ref, carry_ref):
    i = pl.program_id(0)

    @pl.when(i == 0)
    def _():
        carry_ref[...] = jnp.zeros_like(carry_ref)

    y = (_dot(orw_ref[...].astype(BF16), wout_ref[0:RWKV_W, :])
         + _dot(ods_ref[...].astype(BF16), wout_ref[RWKV_W:RWKV_W + DSA_W, :])
         + _dot(osw_ref[...].astype(BF16), wout_ref[RWKV_W + DSA_W:D_MODEL, :]))
    x1 = _layer_norm_rows(ALPHA * x_ref[...] + g1_ref[...] * y, lng_ref[...], lnb_ref[...])
    x1_ref[...] = x1
    h2 = x1 * (1.0 + sc2_ref[...]) + sh2_ref[...]
    h2_ref[...] = _pack_halves(h2)

    tm = h2.shape[0]
    ne = N_EXPERTS
    gs = ne // N_GROUPS
    scores = _sigmoid(_dot_nt(rwt_ref[...], h2, HI))
    sel = scores + rb_ref[...]
    sub = _iota((gs, tm), 0).astype(F32)
    gsc = []
    for j in range(N_GROUPS):
        gj = sel[j * gs:(j + 1) * gs, :]
        m1 = jnp.max(gj, axis=0, keepdims=True)
        f1 = jnp.min(jnp.where(gj == m1, sub, float(gs)), axis=0, keepdims=True)
        m2 = jnp.max(jnp.where(sub == f1, -jnp.inf, gj), axis=0, keepdims=True)
        gsc.append(m1 + m2)
    gsc = jnp.concatenate(gsc, axis=0)
    gid = _iota((N_GROUPS, tm), 0).astype(F32)
    gmask = jnp.zeros((N_GROUPS, tm), F32)
    for _ in range(TOPK_GROUPS):
        mx = jnp.max(gsc, axis=0, keepdims=True)
        fi = jnp.min(jnp.where(gsc == mx, gid, float(N_GROUPS)), axis=0, keepdims=True)
        pick = gid == fi
        gmask = jnp.where(pick, 1.0, gmask)
        gsc = jnp.where(pick, -jnp.inf, gsc)
    selm = jnp.concatenate(
        [jnp.where(gmask[j:j + 1, :] > 0.5, sel[j * gs:(j + 1) * gs, :], NEG) for j in range(N_GROUPS)], axis=0)
    eid = _iota((ne, tm), 0).astype(F32)
    gsel, eids = [], []
    chosen_f = jnp.zeros((ne, tm), F32)
    for _ in range(TOP_K):
        mx = jnp.max(selm, axis=0, keepdims=True)
        fi = jnp.min(jnp.where(selm == mx, eid, float(ne)), axis=0, keepdims=True)
        pick = eid == fi
        eids.append(fi)
        gsel.append(jnp.sum(jnp.where(pick, scores, 0.0), axis=0, keepdims=True))
        chosen_f = jnp.where(pick, 1.0, chosen_f)
        selm = jnp.where(pick, -jnp.inf, selm)
    gsum = gsel[0]
    for kx in range(1, TOP_K):
        gsum = gsum + gsel[kx]
    before = _dot(chosen_f.astype(BF16), tri_ref[...]) + carry_ref[:, 0:1]
    ranks = [jnp.sum(jnp.where(eid == eids[kx], before, 0.0), axis=0, keepdims=True) for kx in range(TOP_K)]
    eidx_ref[...] = jnp.concatenate(eids, axis=0).astype(I32)
    rank_ref[...] = jnp.concatenate(ranks, axis=0).astype(I32)
    gate_ref[...] = jnp.concatenate(gsel, axis=0) / gsum * ROUTED_SCALE
    carry_ref[...] = carry_ref[...] + jnp.sum(chosen_f, axis=1, keepdims=True)
    cnt_ref[...] = carry_ref[...]


def _post_mix(x, o_rw, o_ds, o_sw, w_out, g1, ln_g, ln_b, sc2, sh2, router_wt, router_b, tm=512):
    t, d = x.shape
    tri = (np.arange(tm)[:, None] < np.arange(tm)[None, :]).astype(np.float32)
    tri = jnp.asarray(tri, BF16)
    row = lambda i: (i, 0)
    const = lambda i: (0, 0)
    col = lambda i: (0, i)
    vec = pl.BlockSpec((1, d), const)
    return pl.pallas_call(
        _post_mix_kernel,
        grid=(t // tm,),
        in_specs=[
            pl.BlockSpec((tm, d), row),
            pl.BlockSpec((tm, RWKV_W), row),
            pl.BlockSpec((tm, DSA_W), row),
            pl.BlockSpec((tm, SWA_W), row),
            pl.BlockSpec((d, d), const),
            vec, vec, vec, vec, vec,
            pl.BlockSpec((N_EXPERTS, d), const),
            pl.BlockSpec((N_EXPERTS, 1), const),
            pl.BlockSpec((tm, tm), const),
        ],
        out_specs=[
            pl.BlockSpec((tm, d), row),
            pl.BlockSpec((tm, d // 2), row),
            pl.BlockSpec((TOP_K, tm), col),
            pl.BlockSpec((TOP_K, tm), col),
            pl.BlockSpec((TOP_K, tm), col),
            pl.BlockSpec((N_EXPERTS, LANES), const),
        ],
        out_shape=[
            jax.ShapeDtypeStruct((t, d), F32),
            jax.ShapeDtypeStruct((t, d // 2), I32),
            jax.ShapeDtypeStruct((TOP_K, t), I32),
            jax.ShapeDtypeStruct((TOP_K, t), I32),
            jax.ShapeDtypeStruct((TOP_K, t), F32),
            jax.ShapeDtypeStruct((N_EXPERTS, LANES), F32),
        ],
        scratch_shapes=[pltpu.VMEM((N_EXPERTS, LANES), F32)],
        compiler_params=_params(("arbitrary",)),
        name="post_mix_router",
    )(x, o_rw, o_ds, o_sw, w_out, g1, ln_g, ln_b, sc2, sh2, router_wt, router_b, tri)


MOE_ROWS = 512
MOE_TILE = 256


def _pack_halves(v):
    w = v.shape[1] // 2
    bits = lax.bitcast_convert_type(v.astype(BF16).astype(F32), I32)
    return bits[:, :w] | lax.shift_right_logical(bits[:, w:], 16)


def _unpack_halves(p):
    return lax.bitcast_convert_type(p & -65536, F32), lax.bitcast_convert_type(p << 16, F32)


def _row_copy(src_ref, src_row, dst_ref, dst_row, sem):
    return pltpu.make_async_copy(src_ref.at[pl.ds(src_row, 1), :], dst_ref.at[pl.ds(dst_row, 1), :], sem)


def _dispatch_kernel(slot_hbm, h_ref, xs_in, xs_out, slot_smem, sem_tab, sem_rows):
    del xs_in
    i = pl.program_id(0)
    tab = pltpu.make_async_copy(slot_hbm.at[i], slot_smem, sem_tab)
    tab.start()
    tab.wait()

    def issue(tt, carry):
        for kx in range(TOP_K):
            _row_copy(h_ref, tt, xs_out, slot_smem[kx, tt], sem_rows).start(priority=kx % 2)
        return carry

    lax.fori_loop(0, MOE_TILE, issue, 0)

    def drain(tt, carry):
        for kx in range(TOP_K):
            _row_copy(h_ref, 0, xs_out, 0, sem_rows).wait()
        return carry

    lax.fori_loop(0, MOE_TILE, drain, 0)


def _dispatch(slot_tiles, rows, cap):
    t, d = rows.shape
    xs0 = jnp.zeros((cap, d), rows.dtype)
    return pl.pallas_call(
        _dispatch_kernel,
        grid=(t // MOE_TILE,),
        in_specs=[
            pl.BlockSpec(memory_space=pl.ANY),
            pl.BlockSpec((MOE_TILE, d), lambda i: (i, 0)),
            pl.BlockSpec(memory_space=pl.ANY),
        ],
        out_specs=pl.BlockSpec(memory_space=pl.ANY),
        out_shape=jax.ShapeDtypeStruct((cap, d), rows.dtype),
        scratch_shapes=[
            pltpu.SMEM((TOP_K, MOE_TILE), I32),
            pltpu.SemaphoreType.DMA,
            pltpu.SemaphoreType.DMA,
        ],
        input_output_aliases={2: 0},
        compiler_params=_params(("arbitrary",)),
        name="moe_dispatch",
    )(slot_tiles, rows, xs0)


def _expert_kernel(be_ref, nb_ref, xs_ref, w1_ref, w3_ref, w2_ref, ys_ref, w1b, w3b, w2b):
    b = pl.program_id(0)
    changed = (b == 0) | (be_ref[b] != be_ref[jnp.maximum(b - 1, 0)])

    @pl.when(changed & (b < nb_ref[0]))
    def _():
        w1b[...] = w1_ref[0, 0].astype(BF16)
        w3b[...] = w3_ref[0, 0].astype(BF16)
        w2b[...] = w2_ref[0, 0].astype(BF16)

    @pl.when(b < nb_ref[0])
    def _():
        x_hi, x_lo = _unpack_halves(xs_ref[...])
        x_hi, x_lo = x_hi.astype(BF16), x_lo.astype(BF16)
        half = x_hi.shape[1]
        a = _dot(x_hi, w1b[0:half, :]) + _dot(x_lo, w1b[half:2 * half, :])
        gte = _dot(x_hi, w3b[0:half, :]) + _dot(x_lo, w3b[half:2 * half, :])
        hmid = (a * _sigmoid(a) * gte).astype(BF16)
        ys_ref[...] = _pack_halves(_dot(hmid, w2b[...]))

    @pl.when(b >= nb_ref[0])
    def _():
        ys_ref[...] = jnp.zeros_like(ys_ref)


def _experts(block_e, n_used, xs, w1, w3, w2, layer):
    cap, dp = xs.shape
    d = 2 * dp
    nb = cap // MOE_ROWS
    grid_spec = pltpu.PrefetchScalarGridSpec(
        num_scalar_prefetch=2,
        grid=(nb,),
        in_specs=[
            pl.BlockSpec((MOE_ROWS, dp), lambda b, be, nu: (b, 0)),
            pl.BlockSpec((1, 1, d, D_EXPERT), lambda b, be, nu: (layer, be[b], 0, 0)),
            pl.BlockSpec((1, 1, d, D_EXPERT), lambda b, be, nu: (layer, be[b], 0, 0)),
            pl.BlockSpec((1, 1, D_EXPERT, d), lambda b, be, nu: (layer, be[b], 0, 0)),
        ],
        out_specs=pl.BlockSpec((MOE_ROWS, dp), lambda b, be, nu: (b, 0)),
        scratch_shapes=[
            pltpu.VMEM((d, D_EXPERT), BF16),
            pltpu.VMEM((d, D_EXPERT), BF16),
            pltpu.VMEM((D_EXPERT, d), BF16),
        ],
    )
    return pl.pallas_call(
        _expert_kernel,
        grid_spec=grid_spec,
        out_shape=jax.ShapeDtypeStruct((cap, dp), I32),
        compiler_params=_params(("arbitrary",)),
        name="moe_experts",
    )(block_e, n_used, xs, w1, w3, w2)


def _combine_kernel(slot_hbm, ys_hbm, x1_ref, h2_ref, gate_ref, sw1_ref, sw3_ref, sw2_ref,
                    g2_ref, lng_ref, lnb_ref, o_ref, slot_smem, gbuf, sem_tab, sem_rows):
    i = pl.program_id(0)
    tab = pltpu.make_async_copy(slot_hbm.at[i], slot_smem, sem_tab)
    tab.start()
    tab.wait()

    def issue(tt, carry):
        for kx in range(TOP_K):
            _row_copy(ys_hbm, slot_smem[kx, tt], gbuf.at[kx], tt, sem_rows).start(priority=kx % 2)
        return carry

    lax.fori_loop(0, MOE_TILE, issue, 0)

    h_hi, h_lo = _unpack_halves(h2_ref[...])
    h_hi, h_lo = h_hi.astype(BF16), h_lo.astype(BF16)
    half = h_hi.shape[1]
    a = _dot(h_hi, sw1_ref[0:half, :]) + _dot(h_lo, sw1_ref[half:2 * half, :])
    gte = _dot(h_hi, sw3_ref[0:half, :]) + _dot(h_lo, sw3_ref[half:2 * half, :])
    y = _dot((a * _sigmoid(a) * gte).astype(BF16), sw2_ref[...])

    def drain(tt, carry):
        for kx in range(TOP_K):
            _row_copy(ys_hbm, 0, gbuf.at[kx], 0, sem_rows).wait()
        return carry

    lax.fori_loop(0, MOE_TILE, drain, 0)

    gates = gate_ref[...]
    r_hi = jnp.zeros((MOE_TILE, half), F32)
    r_lo = jnp.zeros((MOE_TILE, half), F32)
    for kx in range(TOP_K):
        e_hi, e_lo = _unpack_halves(gbuf[kx])
        r_hi = r_hi + gates[:, kx:kx + 1] * e_hi
        r_lo = r_lo + gates[:, kx:kx + 1] * e_lo
    y = y + jnp.concatenate([r_hi, r_lo], axis=1)
    o_ref[...] = _layer_norm_rows(ALPHA * x1_ref[...] + g2_ref[...] * y, lng_ref[...], lnb_ref[...])


def _combine(slot_tiles, ys, x1, h2, gates_t, sw1, sw3, sw2, g2, ln_g, ln_b):
    t, d = x1.shape
    row = lambda i: (i, 0)
    const = lambda i: (0, 0)
    vec = pl.BlockSpec((1, d), const)
    return pl.pallas_call(
        _combine_kernel,
        grid=(t // MOE_TILE,),
        in_specs=[
            pl.BlockSpec(memory_space=pl.ANY),
            pl.BlockSpec(memory_space=pl.ANY),
            pl.BlockSpec((MOE_TILE, d), row),
            pl.BlockSpec((MOE_TILE, d // 2), row),
            pl.BlockSpec((MOE_TILE, TOP_K), row),
            pl.BlockSpec((d, D_EXPERT), const),
            pl.BlockSpec((d, D_EXPERT), const),
            pl.BlockSpec((D_EXPERT, d), const),
            vec, vec, vec,
        ],
        out_specs=pl.BlockSpec((MOE_TILE, d), row),
        out_shape=jax.ShapeDtypeStruct((t, d), F32),
        scratch_shapes=[
            pltpu.SMEM((TOP_K, MOE_TILE), I32),
            pltpu.VMEM((TOP_K, MOE_TILE, d // 2), I32),
            pltpu.SemaphoreType.DMA,
            pltpu.SemaphoreType.DMA,
        ],
        compiler_params=_params(("arbitrary",)),
        name="moe_combine",
    )(slot_tiles, ys, x1, h2, gates_t, sw1, sw3, sw2, g2, ln_g, ln_b)


def _pad_w_in(w_in_l):
    d = w_in_l.shape[0]
    pad = jnp.zeros((d, C_SQ[0] - N_ORIG_BEFORE_PAD), w_in_l.dtype)
    return jnp.concatenate([w_in_l[:, :N_ORIG_BEFORE_PAD], pad, w_in_l[:, N_ORIG_BEFORE_PAD:]], axis=1)


def _pad_lanes(v, width=LANES):
    v = v.reshape(1, -1)
    return jnp.pad(v, ((0, 0), (0, width - v.shape[1])))


def _moe_tables(eidx, rank, counts):
    t = eidx.shape[1]
    cnt = counts[:, 0].astype(I32)
    padded = (cnt + MOE_ROWS - 1) // MOE_ROWS * MOE_ROWS
    pad_end = jnp.cumsum(padded)
    pad_start = pad_end - padded
    e_ids = jnp.arange(N_EXPERTS, dtype=I32)
    start_of = jnp.sum(jnp.where(eidx[..., None] == e_ids, pad_start, 0), axis=-1)
    slot = start_of + rank
    slot_tiles = slot.reshape(TOP_K, t // MOE_TILE, MOE_TILE).transpose(1, 0, 2)
    cap = t * TOP_K + N_EXPERTS * MOE_ROWS
    nb = cap // MOE_ROWS
    blk_row = jnp.arange(nb, dtype=I32)[:, None] * MOE_ROWS
    block_e = jnp.minimum(jnp.sum((pad_end[None, :] <= blk_row).astype(I32), axis=1), N_EXPERTS - 1)
    n_used = (pad_end[-1] // MOE_ROWS).astype(I32).reshape(1)
    return slot_tiles, block_e, n_used, cap


def kernel(x, c, w_mod, b_mod, w_in, rwkv_mu, rwkv_w0, rwkv_w2, rwkv_a0, rwkv_a2, rwkv_g2, rwkv_k_k, rwkv_k_a, rwkv_r_k, rwkv_ln_g, rwkv_ln_b, dsa_kv_norm, dsa_w_uk, dsa_w_uv, dsa_ik_g, dsa_ik_b, swa_sinks, w_out, ln_mix_g, ln_mix_b, router_w, router_bias, exp_w1, exp_w3, exp_w2, sh_w1, sh_w3, sh_w2, ln_ffn_g, ln_ffn_b):
    bsz, t, d = x.shape
    assert bsz == 1 and d == D_MODEL
    depth = w_mod.shape[0]
    n_sl = SWA_HEADS + DSA_HEADS
    slopes = [2.0 ** (-8.0 * (j + 1.0) / n_sl) for j in range(n_sl)]
    swa_slopes, dsa_slopes = slopes[:SWA_HEADS], slopes[SWA_HEADS:]

    mod = _modulation(c, w_mod, b_mod)
    xs_cur = x[0]
    row1 = lambda v: v.reshape(1, -1)
    for l in range(depth):
        sh1, sc1, g1, sh2, sc2, g2 = [mod[l, :, j * d:(j + 1) * d] for j in range(6)]
        wp = _pad_w_in(w_in[l])
        w_hi = wp.astype(BF16)
        w_idx = wp[:, C_IDX[0]:C_IDX[1]]
        w_idx_lo = (w_idx - w_idx.astype(BF16).astype(F32)).astype(BF16)
        rkv, lora, dq, ckv, iq, ikw, sq, skv = _input_proj(
            xs_cur, sc1, sh1, w_hi, w_idx_lo, row1(dsa_kv_norm[l]),
            _pad_lanes(dsa_ik_g[l]), _pad_lanes(dsa_ik_b[l]))
        o_rw = _rwkv_mix(rkv, lora, row1(rwkv_mu[l]), row1(rwkv_w0[l]), rwkv_w2[l], row1(rwkv_a0[l]),
                         rwkv_a2[l], rwkv_g2[l], row1(rwkv_k_k[l]), row1(rwkv_k_a[l]), row1(rwkv_r_k[l]),
                         row1(rwkv_ln_g[l]), row1(rwkv_ln_b[l]))
        o_ds = _dsa_mix(dq, iq, ikw, ckv, dsa_w_uk[l], dsa_w_uv[l], dsa_slopes)
        o_sw = _swa_mix(sq, skv, _pad_lanes(swa_sinks[l]), swa_slopes)
        x1, h2, eidx, rank, gates, counts = _post_mix(
            xs_cur, o_rw, o_ds, o_sw, w_out[l].astype(BF16), g1, row1(ln_mix_g[l]), row1(ln_mix_b[l]),
            sc2, sh2, router_w[l].T, router_bias[l].reshape(-1, 1))
        slot_tiles, block_e, n_used, cap = _moe_tables(eidx, rank, counts)
        xs_sorted = _dispatch(slot_tiles, h2, cap)
        ys = _experts(block_e, n_used, xs_sorted, exp_w1, exp_w3, exp_w2, l)
        xs_cur = _combine(slot_tiles, ys, x1, h2, gates.T, sh_w1[l].astype(BF16), sh_w3[l].astype(BF16),
                          sh_w2[l].astype(BF16), g2, row1(ln_ffn_g[l]), row1(ln_ffn_b[l]))
    return xs_cur[None]
```

```python
import functools
import math

import jax
import jax.numpy as jnp
import numpy as np
from jax import lax
from jax.experimental import pallas as pl
from jax.experimental.pallas import tpu as pltpu

F32 = jnp.float32
BF16 = jnp.bfloat16
I32 = jnp.int32
HI = lax.Precision.HIGHEST

D_MODEL = 1024
DEPTH = 4
HEAD_DIM = 64
RWKV_HEADS = 6
DSA_HEADS = 4
SWA_HEADS = 6
SWA_KV_HEADS = 2
RWKV_W = RWKV_HEADS * HEAD_DIM
DSA_W = DSA_HEADS * HEAD_DIM
SWA_W = SWA_HEADS * HEAD_DIM
DECAY_LORA = 64
AAA_LORA = 64
GATE_LORA = 128
GN_EPS = 64e-5
KV_LORA = 128
IDX_HEADS = 4
IDX_DIM = 64
TOPK_MAX = 256
WINDOW = 128
N_EXPERTS = 64
TOP_K = 8
N_GROUPS = 8
TOPK_GROUPS = 4
D_EXPERT = 256
ROUTED_SCALE = 2.5
ALPHA = (2 * DEPTH) ** 0.25
LN_EPS = 1e-5
NEG = -1e30
INT_MIN = -(2 ** 31)

LANES = 128
VMEM_LIMIT = 56 * 1024 * 1024

C_RKV = (0, 1152)
C_LORA = (1152, 1408)
C_DQ = (1408, 1664)
C_CKV = (1664, 1792)
C_IDX = (1792, 2176)
C_SQ = (2176, 2560)
C_SKV = (2560, 2816)
P_PAD = 2816
N_ORIG_BEFORE_PAD = 2116


def _dot(a, b, prec=None):
    return jnp.dot(a, b, preferred_element_type=F32, precision=prec)


def _dot_nt(a, b, prec=None):
    return lax.dot_general(a, b, (((1,), (1,)), ((), ())), preferred_element_type=F32, precision=prec)


def _split2(a):
    a_hi = a.astype(BF16)
    return a_hi, (a - a_hi.astype(F32)).astype(BF16)


def _bdot(a, b):
    return _dot(a.astype(BF16), b.astype(BF16))


def _bdot_nt(a, b):
    return _dot_nt(a.astype(BF16), b.astype(BF16))


def _dot2(a, b_exact):
    a_hi, a_lo = _split2(a)
    return _dot(a_hi, b_exact) + _dot(a_lo, b_exact)


def _dot2_l(a_exact, b):
    b_hi, b_lo = _split2(b)
    return _dot(a_exact, b_hi) + _dot(a_exact, b_lo)


def _dot3(a, b):
    a_hi, a_lo = _split2(a)
    b_hi, b_lo = _split2(b)
    return _dot(a_hi, b_hi) + (_dot(a_lo, b_hi) + _dot(a_hi, b_lo))


def _iota(shape, dim):
    return lax.broadcasted_iota(I32, shape, dim)


def _sigmoid(x):
    return 1.0 / (1.0 + jnp.exp(-x))


def _layer_norm_rows(v, g, b):
    mu = jnp.mean(v, axis=-1, keepdims=True)
    d = v - mu
    var = jnp.mean(d * d, axis=-1, keepdims=True)
    return d * lax.rsqrt(var + LN_EPS) * g + b


def _params(sem):
    return pltpu.CompilerParams(dimension_semantics=sem, vmem_limit_bytes=VMEM_LIMIT)


def _mod_kernel(c_ref, w_ref, b_ref, o_ref):
    c = c_ref[...]
    cond = c * _sigmoid(c)
    o_ref[0] = _dot(cond, w_ref[0], HI) + b_ref[0]


def _modulation(c, w_mod, b_mod):
    depth, d, d6 = w_mod.shape
    c8 = jnp.broadcast_to(c, (8, d))
    nj = d6 // d
    out = pl.pallas_call(
        _mod_kernel,
        grid=(depth, nj),
        in_specs=[
            pl.BlockSpec((8, d), lambda l, j: (0, 0)),
            pl.BlockSpec((1, d, d), lambda l, j: (l, 0, j)),
            pl.BlockSpec((1, 1, d), lambda l, j: (l, 0, j)),
        ],
        out_specs=pl.BlockSpec((1, 8, d), lambda l, j: (l, 0, j)),
        out_shape=jax.ShapeDtypeStruct((depth, 8, d6), F32),
        compiler_params=_params(("arbitrary", "arbitrary")),
        name="modulation",
    )(c8, w_mod, b_mod.reshape(depth, 1, d6))
    return out[:, 0:1, :]


def _proj_kernel(x_ref, sc_ref, sh_ref, w_ref, wlo_ref, kvn_ref, ikg_ref, ikb_ref,
                 rkv_ref, lora_ref, dq_ref, ckv_ref, iq_ref, ikw_ref, sq_ref, skv_ref):
    h = x_ref[...] * (1.0 + sc_ref[...]) + sh_ref[...]
    hb = h.astype(BF16)
    hl = (h - hb.astype(F32)).astype(BF16)

    def mm(c):
        return _dot(hb, w_ref[:, c[0]:c[1]])

    rkv_ref[...] = mm(C_RKV)
    lora_ref[...] = mm(C_LORA)
    dq_ref[...] = mm(C_DQ)
    sq_ref[...] = mm(C_SQ)
    skv_ref[...] = mm(C_SKV)
    ckv = mm(C_CKV)
    ckv_ref[...] = ckv * lax.rsqrt(jnp.mean(ckv * ckv, axis=-1, keepdims=True) + 1e-6) * kvn_ref[...]
    idx = mm(C_IDX) + _dot(hl, w_ref[:, C_IDX[0]:C_IDX[1]]) + _dot(hb, wlo_ref[...])
    iq_ref[...] = idx[:, 0:256]
    g3 = idx[:, 256:384]
    lane = _iota(g3.shape, 1)
    isk = lane < IDX_DIM
    mu = jnp.sum(jnp.where(isk, g3, 0.0), axis=-1, keepdims=True) * (1.0 / IDX_DIM)
    dk = jnp.where(isk, g3 - mu, 0.0)
    var = jnp.sum(dk * dk, axis=-1, keepdims=True) * (1.0 / IDX_DIM)
    ikn = dk * lax.rsqrt(var + LN_EPS) * ikg_ref[...] + ikb_ref[...]
    ikw_ref[...] = jnp.where(isk, ikn, g3 * (IDX_HEADS ** -0.5 * IDX_DIM ** -0.5))


def _input_proj(x, sc, sh, w_hi, w_idx_lo, kvn, ikg, ikb, tm=512):
    t, d = x.shape
    widths = [C_RKV, C_LORA, C_DQ, C_CKV, (0, 256), (0, 128), C_SQ, C_SKV]
    widths = [c[1] - c[0] for c in widths]
    const = lambda i: (0, 0)
    row = lambda i: (i, 0)
    return pl.pallas_call(
        _proj_kernel,
        grid=(t // tm,),
        in_specs=[
            pl.BlockSpec((tm, d), row),
            pl.BlockSpec((1, d), const),
            pl.BlockSpec((1, d), const),
            pl.BlockSpec((d, P_PAD), const),
            pl.BlockSpec((d, C_IDX[1] - C_IDX[0]), const),
            pl.BlockSpec((1, KV_LORA), const),
            pl.BlockSpec((1, LANES), const),
            pl.BlockSpec((1, LANES), const),
        ],
        out_specs=[pl.BlockSpec((tm, w), row) for w in widths],
        out_shape=[jax.ShapeDtypeStruct((t, w), F32) for w in widths],
        compiler_params=_params(("arbitrary",)),
        name="input_proj",
    )(x, sc, sh, w_hi, w_idx_lo, kvn, ikg, ikb)


RW_CHUNK = 64
RW_UNROLL = 4


def _rwkv_kernel(r_ref, k_ref, v_ref, lora_ref, rp_ref, kp_ref, vp_ref, lp_ref,
                 mur_ref, muk_ref, muv_ref, mul_ref, w0_ref, w2_ref, a0_ref, a2_ref, g2_ref,
                 kk_ref, ka_ref, rk_ref, lng_ref, lnb_ref, o_ref,
                 h_ref, y_ref, st_ref, wm_ref, ar_ref, rs_ref, vs_ref, lt_ref, zm_ref, y0_ref, gc_ref, *, tg):
    g = pl.program_id(0)
    c64 = RW_CHUNK
    nch = tg // c64
    npair = RWKV_W // LANES
    pair_lanes = [slice(p * LANES, (p + 1) * LANES) for p in range(npair)]
    lane = _iota((1, LANES), 1)
    first = g == 0

    @pl.when(first)
    def _():
        h_ref[...] = jnp.zeros_like(h_ref)

    rowid = _iota((tg, 1), 0)

    def shift_mix(cur_ref, prev_ref, mu_ref):
        cur = cur_ref[...]
        prev_row = jnp.where(first, 0.0, prev_ref[7:8, :])
        rolled = pltpu.roll(cur, 1, 0)
        shifted = jnp.where(rowid == 0, prev_row, rolled)
        return cur + (shifted - cur) * mu_ref[...]

    r = shift_mix(r_ref, rp_ref, mur_ref)
    k = shift_mix(k_ref, kp_ref, muk_ref)
    v = shift_mix(v_ref, vp_ref, muv_ref)
    lo = shift_mix(lora_ref, lp_ref, mul_ref)
    wl = lo[:, 0:DECAY_LORA]
    al = lo[:, DECAY_LORA:DECAY_LORA + AAA_LORA]
    gl = lo[:, 128:256]

    zw = -(w0_ref[...] + _dot3(jnp.tanh(wl), w2_ref[...]))
    softplus = jnp.maximum(zw, 0.0) + jnp.log(1.0 + jnp.exp(-jnp.abs(zw)))
    lw = -jnp.exp(-softplus - 0.5)
    a = _sigmoid(a0_ref[...] + _bdot(al, a2_ref[...]))
    gate = _bdot(_sigmoid(gl), g2_ref[...])

    ri = _iota((LANES, LANES), 0) // HEAD_DIM
    ci = _iota((LANES, LANES), 1) // HEAD_DIM
    bones = jnp.where(ri == ci, 1.0, 0.0).astype(BF16)

    def head_sum(xf):
        return jnp.concatenate([_dot2(xf[:, pl_], bones) for pl_ in pair_lanes], axis=1)

    kk = k * kk_ref[...]
    kk = kk / jnp.maximum(jnp.sqrt(head_sum(kk * kk)), 1e-12)
    k2 = k * (1.0 + (a - 1.0) * ka_ref[...])
    bonus = head_sum(r * k2 * rk_ref[...]) * v
    bvec = a * kk

    st_ref[0] = r
    st_ref[1] = k2
    st_ref[2] = v
    st_ref[3] = lw
    st_ref[4] = kk
    st_ref[5] = bvec

    rr = _iota((LANES, LANES), 0)
    cc = _iota((LANES, LANES), 1)
    same = (rr // c64) == (cc // c64)
    strict = same & ((rr % c64) > (cc % c64))
    incl = same & ((rr % c64) >= (cc % c64))
    eye = jnp.where(rr == cc, 1.0, 0.0)
    tril = jnp.where(_iota((c64, c64), 0) >= _iota((c64, c64), 1), 1.0, 0.0).astype(BF16)
    lo_half = lane < HEAD_DIM

    def stack(xc):
        return jnp.concatenate([jnp.where(lo_half, xc, 0.0), jnp.where(lo_half, 0.0, xc)], axis=0)

    def prepare(c, carry):
        chunks = [c * RW_UNROLL + j for j in range(RW_UNROLL)]
        sls = [pl.ds(pl.multiple_of(cj * c64, c64), c64) for cj in chunks]
        items = [(p, j) for j in range(RW_UNROLL) for p in range(npair)]
        pairs = range(len(items))
        idx = [p * nch + chunks[j] for p, j in items]
        ld = lambda q: [st_ref[q, sls[j], pair_lanes[p]] for p, j in items]
        rc, kc, vc, lwc, kkc, bc = ld(0), ld(1), ld(2), ld(3), ld(4), ld(5)
        cum = [_dot2_l(tril, lwc[p]) for p in pairs]
        tot = [cum[p][c64 - 1:c64, :] for p in pairs]
        g_in = [jnp.exp(cum[p]) for p in pairs]
        g_ex = [jnp.exp(cum[p] - lwc[p]) for p in pairs]
        g_inv = [jnp.exp(-cum[p]) for p in pairs]
        g_rest = [jnp.exp(tot[p] - cum[p]) for p in pairs]
        a_s = [stack(-kkc[p] * g_ex[p]).astype(BF16) for p in pairs]
        b_s = [stack(bc[p] * g_inv[p]).astype(BF16) for p in pairs]
        k_s = [stack(kc[p] * g_inv[p]).astype(BF16) for p in pairs]
        r_s = [stack(rc[p] * g_in[p]).astype(BF16) for p in pairs]
        v_s = [stack(vc[p]).astype(BF16) for p in pairs]
        nmat = [jnp.where(strict, _dot_nt(a_s[p], b_s[p]), 0.0) for p in pairs]
        aak = [jnp.where(strict, _dot_nt(a_s[p], k_s[p]), 0.0) for p in pairs]
        arb = [jnp.where(incl, _dot_nt(r_s[p], b_s[p]), 0.0) for p in pairs]
        ark = [jnp.where(incl, _dot_nt(r_s[p], k_s[p]), 0.0) for p in pairs]
        tinv = [eye + nmat[p] for p in pairs]
        pw = nmat
        for _ in range(5):
            pw = [_bdot(pw[p], pw[p]) for p in pairs]
            tinv = [_bdot(tinv[p], eye + pw[p]) for p in pairs]
        tinv = [tinv[p].astype(BF16) for p in pairs]
        akv = [_bdot(aak[p], v_s[p]).astype(BF16) for p in pairs]
        wmat = [_dot(tinv[p], a_s[p]) for p in pairs]
        zmat = [_dot(tinv[p], akv[p]) for p in pairs]
        y0 = [_bdot(ark[p], v_s[p]) for p in pairs]
        for p in pairs:
            wm_ref[idx[p]] = wmat[p].astype(BF16)
            zm_ref[idx[p]] = zmat[p]
            y0_ref[idx[p]] = y0[p]
            ar_ref[idx[p]] = arb[p].astype(BF16)
            rs_ref[idx[p]] = r_s[p]
            vs_ref[idx[p]] = v_s[p]
            lt_ref[idx[p]] = jnp.concatenate([stack(bc[p] * g_rest[p]), stack(kc[p] * g_rest[p])],
                                             axis=0).T.astype(BF16)
            gc_ref[idx[p]] = jnp.broadcast_to(jnp.sum(eye * jnp.exp(tot[p]), axis=1, keepdims=True),
                                              (LANES, LANES))
        return carry

    lax.fori_loop(0, nch // RW_UNROLL, prepare, 0)

    def advance(c, carry):
        sl = pl.ds(pl.multiple_of(c * c64, c64), c64)
        pairs = range(npair)
        idx = [p * nch + c for p in pairs]
        hst = [h_ref[p] for p in pairs]
        hb = [hst[p].astype(BF16) for p in pairs]
        u = [_dot(wm_ref[idx[p]], hb[p]) + zm_ref[idx[p]] for p in pairs]
        rh = [_dot(rs_ref[idx[p]], hb[p]) for p in pairs]
        ub = [u[p].astype(BF16) for p in pairs]
        hnew = [_dot(lt_ref[idx[p]], jnp.concatenate([ub[p], vs_ref[idx[p]]], axis=0)) for p in pairs]
        au = [_dot(ar_ref[idx[p]], ub[p]) for p in pairs]
        for p in pairs:
            h_ref[p] = gc_ref[idx[p]] * hst[p] + hnew[p]
            ys = rh[p] + au[p] + y0_ref[idx[p]]
            y_ref[sl, pair_lanes[p]] = ys[0:c64, :] + ys[c64:2 * c64, :]
        return carry

    lax.fori_loop(0, nch, advance, 0)

    y = y_ref[...]
    mean = head_sum(y) * (1.0 / HEAD_DIM)
    dy = y - mean
    var = head_sum(dy * dy) * (1.0 / HEAD_DIM)
    o = dy * lax.rsqrt(var + GN_EPS) * lng_ref[...] + lnb_ref[...]
    o_ref[...] = (o + bonus) * gate


def _rwkv_mix(rkv, lora, mu, w0, w2, a0, a2, g2, k_k, k_a, r_k, ln_g, ln_b, tg=512):
    t = rkv.shape[0]
    w = RWKV_W
    npair = w // LANES
    nmat = npair * (tg // RW_CHUNK)
    mu_r, mu_k, mu_v, mu_l = mu[:, 0:w], mu[:, w:2 * w], mu[:, 2 * w:3 * w], mu[:, 3 * w:3 * w + 256]
    blk = lambda off: pl.BlockSpec((tg, w), lambda g: (g, off))
    prev = lambda off: pl.BlockSpec((8, w), lambda g: (jnp.maximum(g * (tg // 8) - 1, 0), off))
    vec = pl.BlockSpec((1, w), lambda g: (0, 0))
    full = lambda rows: pl.BlockSpec((rows, w), lambda g: (0, 0))
    return pl.pallas_call(
        functools.partial(_rwkv_kernel, tg=tg),
        grid=(t // tg,),
        in_specs=[
            blk(0), blk(1), blk(2),
            pl.BlockSpec((tg, 256), lambda g: (g, 0)),
            prev(0), prev(1), prev(2),
            pl.BlockSpec((8, 256), lambda g: (jnp.maximum(g * (tg // 8) - 1, 0), 0)),
            vec, vec, vec,
            pl.BlockSpec((1, 256), lambda g: (0, 0)),
            vec, full(DECAY_LORA), vec, full(AAA_LORA), full(GATE_LORA),
            vec, vec, vec, vec, vec,
        ],
        out_specs=pl.BlockSpec((tg, w), lambda g: (g, 0)),
        out_shape=jax.ShapeDtypeStruct((t, w), F32),
        scratch_shapes=[
            pltpu.VMEM((npair, LANES, LANES), F32),
            pltpu.VMEM((tg, w), F32),
            pltpu.VMEM((6, tg, w), F32),
            pltpu.VMEM((nmat, LANES, LANES), BF16),
            pltpu.VMEM((nmat, LANES, LANES), BF16),
            pltpu.VMEM((nmat, LANES, LANES), BF16),
            pltpu.VMEM((nmat, LANES, LANES), BF16),
            pltpu.VMEM((nmat, LANES, 2 * LANES), BF16),
            pltpu.VMEM((nmat, LANES, LANES), F32),
            pltpu.VMEM((nmat, LANES, LANES), F32),
            pltpu.VMEM((nmat, LANES, LANES), F32),
        ],
        compiler_params=_params(("arbitrary",)),
        name="rwkv7_mix",
    )(rkv, rkv, rkv, lora, rkv, rkv, rkv, lora,
      mu_r, mu_k, mu_v, mu_l, w0, w2, a0, a2, g2, k_k, k_a, r_k, ln_g, ln_b)


DSA_QB = 256
DSA_KC = 1024
DSA_SUB = 512
CNT_ROWS = 64
TIE_BLK = 128
BIS_STEPS = 2


def _float_key(v):
    bits = lax.bitcast_convert_type(v, I32)
    return bits ^ ((bits >> 31) & 0x7FFFFFFF)


def _dsa_kernel(dq_ref, iq_ref, ikw_ref, ikx_ref, kf_ref, vft_ref, wuk_ref, wuv_ref, tril_ref, slc_ref,
                o_ref, sc_ref, acc_ref):
    i = pl.program_id(0)
    qb, kc, sc_rows = DSA_QB, DSA_KC, DSA_SUB
    nh = DSA_HEADS
    t0 = i * qb
    nch = (t0 + qb + kc - 1) // kc
    tq = t0 + _iota((1, qb), 1)

    iq = iq_ref[...]
    iq_hi = iq.astype(BF16).astype(F32)
    iq_lo = iq - iq_hi
    lhs = []
    for h in range(IDX_HEADS):
        s = slice(h * IDX_DIM, (h + 1) * IDX_DIM)
        lhs.append(jnp.concatenate([iq_hi[:, s], iq_hi[:, s], iq_lo[:, s], iq_lo[:, s]], axis=1))
    lhs_t = jnp.concatenate(lhs, axis=0).T.astype(BF16)
    ikw_t = ikw_ref[...].T
    iw = [ikw_t[IDX_DIM + h:IDX_DIM + h + 1, :] for h in range(IDX_HEADS)]

    def score_body(ch, carry, masked):
        m1, m2 = carry
        sr = sc_rows
        for sub in range(kc // sr):
            k0 = pl.multiple_of(ch * kc + sub * sr, sr)
            s_all = _dot(ikx_ref[pl.ds(k0, sr), :], lhs_t)
            acc = jnp.zeros((sr, qb), F32)
            for h in range(IDX_HEADS):
                acc = acc + jnp.maximum(s_all[:, h * qb:(h + 1) * qb], 0.0) * iw[h]
            acc = jnp.where(acc == 0.0, 0.0, acc)
            key = _float_key(acc)
            if masked:
                causal = (k0 + _iota((sr, 1), 0)) <= tq
                key = jnp.where(causal, key, INT_MIN)
                acc = jnp.where(causal, acc, -jnp.inf)
            sc_ref[pl.ds(k0, sr), :] = key
            for j in range(sr // LANES):
                xj = acc[j * LANES:(j + 1) * LANES, :]
                m2 = jnp.maximum(m2, jnp.minimum(m1, xj))
                m1 = jnp.maximum(m1, xj)
        return m1, m2

    ninf = jnp.full((LANES, qb), -jnp.inf, F32)
    n_below = t0 // kc
    top2 = lax.fori_loop(0, n_below, functools.partial(score_body, masked=False), (ninf, ninf))
    m1, m2 = lax.fori_loop(n_below, nch, functools.partial(score_body, masked=True), top2)

    def count_ge(cand):
        def body(ch, acc):
            for j in range(kc // CNT_ROWS):
                kj = pl.multiple_of(ch * kc + j * CNT_ROWS, CNT_ROWS)
                acc = acc + jnp.where(sc_ref[pl.ds(kj, CNT_ROWS), :] >= cand, 1.0, 0.0)
            return acc
        acc = lax.fori_loop(0, nch, body, jnp.zeros((CNT_ROWS, qb), F32))
        return jnp.sum(acc, axis=0, keepdims=True)

    k_row = jnp.minimum(tq + 1, TOPK_MAX).astype(F32)
    hi0 = _float_key(jnp.max(m1, axis=0, keepdims=True))
    lo0 = jnp.minimum(_float_key(jnp.min(m2, axis=0, keepdims=True)), hi0)
    c_pos = count_ge(jnp.ones((1, qb), I32))
    c_nonneg = count_ge(jnp.zeros((1, qb), I32))
    at_zero = (c_pos < k_row) & (c_nonneg >= k_row)
    above = c_pos >= k_row
    lo0 = jnp.where(at_zero, 0, jnp.where(above, jnp.maximum(lo0, 1), lo0))
    n_above0 = jnp.where(at_zero, c_pos, jnp.where(above | (hi0 < 0), 0.0, c_nonneg))
    hi0 = jnp.where(at_zero, 0, jnp.where(above, hi0, jnp.minimum(hi0, -1)))
    lo0 = jnp.minimum(lo0, hi0)

    def open_rows(lo, hi):
        return jnp.max(jnp.where(lo < hi, 1.0, 0.0))

    def bis_body(st):
        lo, hi, n_above, _ = st
        for _ in range(BIS_STEPS):
            mid = (lo | hi) - ((lo ^ hi) >> 1)
            c = count_ge(mid)
            ge = c >= k_row
            exact = c == k_row
            lo, hi = jnp.where(ge, mid, lo), jnp.where(exact, mid, jnp.where(ge, hi, mid - 1))
            n_above = jnp.where(exact, -1.0, jnp.where(ge, n_above, c))
        return lo, hi, n_above, open_rows(lo, hi)

    thr, _, n_above, _ = lax.while_loop(lambda st: st[3] > 0.5, bis_body, (lo0, hi0, n_above0, open_rows(lo0, hi0)))

    dq = dq_ref[...]
    slc = slc_ref[...]
    qaug = []
    for h in range(nh):
        ql = _bdot(dq[:, h * HEAD_DIM:(h + 1) * HEAD_DIM], wuk_ref[h]) * HEAD_DIM ** -0.5
        qaug.append(jnp.concatenate([ql, jnp.broadcast_to(slc[h:h + 1, :], (qb, LANES))], axis=1))
    qaug_t = jnp.concatenate(qaug, axis=0).T.astype(BF16)
    acc_ref[...] = jnp.zeros_like(acc_ref)

    nsub = kc // sc_rows

    def sub_starts(ch):
        return [pl.multiple_of(ch * kc + sub * sc_rows, sc_rows) for sub in range(nsub)]

    def logits(k0):
        return _dot(kf_ref[pl.ds(k0, sc_rows), :], qaug_t)

    def attend(k0, lg_all, sel, m_old):
        ps, m_new = [], []
        for h in range(nh):
            cols = slice(h * qb, (h + 1) * qb)
            lg = jnp.where(sel, lg_all[:, cols], NEG)
            mh = jnp.maximum(m_old[:, cols], jnp.max(lg, axis=0, keepdims=True))
            ps.append(jnp.exp((lg - mh).astype(BF16)))
            m_new.append(mh)
        m_new = jnp.concatenate(m_new, axis=1)
        pv = _dot(vft_ref[:, pl.ds(k0, sc_rows)], jnp.concatenate(ps, axis=1))
        acc_ref[...] = jnp.exp(m_old - m_new) * acc_ref[...] + pv
        return m_new

    m_init = jnp.full((1, nh * qb), NEG, F32)

    need = jnp.where(n_above < 0, float(TOPK_MAX), k_row - n_above)
    tril = tril_ref[...]

    def body(ch, carry):
        tie_run, m_old = carry
        ks = sub_starts(ch)
        lgs = [logits(k0) for k0 in ks]
        keys = [sc_ref[pl.ds(k0, sc_rows), :] for k0 in ks]
        blocks = [slice(j * TIE_BLK, (j + 1) * TIE_BLK) for j in range(sc_rows // TIE_BLK)]
        prefs = [[_dot(tril, jnp.where(key[bl, :] == thr, 1.0, 0.0).astype(BF16)) for bl in blocks] for key in keys]
        for k0, lg, key, pref in zip(ks, lgs, keys, prefs):
            ranks = []
            for pj in pref:
                ranks.append(tie_run + pj)
                tie_run = tie_run + pj[TIE_BLK - 1:TIE_BLK, :]
            sel = (key > thr) | ((key == thr) & (jnp.concatenate(ranks, axis=0) <= need))
            m_old = attend(k0, lg, sel, m_old)
        return tie_run, m_old

    lax.fori_loop(0, nch, body, (jnp.zeros((1, qb), F32), m_init))

    acc = acc_ref[...]
    o_lat = acc[0:KV_LORA, :] / acc[KV_LORA:KV_LORA + 1, :]
    outs = [_bdot(o_lat[:, h * qb:(h + 1) * qb].T, wuv_ref[h]) for h in range(nh)]
    o_ref[...] = jnp.concatenate(outs, axis=1)


DSA_VROWS = KV_LORA + 16


def _dsa_mix(dq, iq, ikw, ckv, w_uk, w_uv, slopes):
    t = dq.shape[0]
    assert t <= LANES * 256
    ikn = ikw[:, 0:IDX_DIM]
    ik_hi, ik_lo = _split2(ikn)
    ikx = jnp.concatenate([ik_hi, ik_lo, ik_hi, ik_lo], axis=1)
    ckv_b = ckv.astype(BF16)
    pos = jnp.arange(t, dtype=I32)
    pa = (pos // LANES).astype(BF16)[:, None]
    pb = (pos % LANES).astype(BF16)[:, None]
    kf = jnp.concatenate([ckv_b, pa, pa, pa, pb, pb, pb, jnp.zeros((t, LANES - 6), BF16)], axis=1)
    vft = jnp.concatenate([ckv_b.T, jnp.ones((1, t), BF16), jnp.zeros((DSA_VROWS - KV_LORA - 1, t), BF16)], axis=0)
    cols = []
    for sl in slopes:
        for coef in (sl * LANES, sl):
            c_hi = jnp.asarray(coef, F32).astype(BF16)
            r1 = jnp.asarray(coef, F32) - c_hi.astype(F32)
            c_mid = r1.astype(BF16)
            c_lo = (r1 - c_mid.astype(F32)).astype(BF16)
            cols += [c_hi.astype(F32), c_mid.astype(F32), c_lo.astype(F32)]
    slc = jnp.stack(cols).reshape(DSA_HEADS, 6)
    slc = jnp.pad(slc, ((0, 8 - DSA_HEADS), (0, LANES - 6)))
    assert t % DSA_KC == 0
    kc = TIE_BLK
    tril = jnp.asarray((np.arange(kc)[:, None] >= np.arange(kc)[None, :]).astype(np.float32), BF16)
    row = lambda i: (i, 0)
    const2 = lambda i: (0, 0)
    const3 = lambda i: (0, 0, 0)
    resident = lambda shape: pl.BlockSpec(shape, const2, pipeline_mode=pl.Buffered(1))
    return pl.pallas_call(
        _dsa_kernel,
        grid=(t // DSA_QB,),
        in_specs=[
            pl.BlockSpec((DSA_QB, DSA_W), row),
            pl.BlockSpec((DSA_QB, IDX_HEADS * IDX_DIM), row),
            pl.BlockSpec((DSA_QB, LANES), row),
            resident((t, 4 * IDX_DIM)),
            resident((t, 2 * LANES)),
            resident((DSA_VROWS, t)),
            pl.BlockSpec((DSA_HEADS, HEAD_DIM, KV_LORA), const3),
            pl.BlockSpec((DSA_HEADS, KV_LORA, HEAD_DIM), const3),
            resident((kc, kc)),
            pl.BlockSpec((8, LANES), const2),
        ],
        out_specs=pl.BlockSpec((DSA_QB, DSA_W), row),
        out_shape=jax.ShapeDtypeStruct((t, DSA_W), F32),
        scratch_shapes=[
            pltpu.VMEM((t, DSA_QB), I32),
            pltpu.VMEM((DSA_VROWS, DSA_HEADS * DSA_QB), F32),
        ],
        compiler_params=_params(("arbitrary",)),
        name="dsa_mix",
    )(dq, iq, ikw, ikx, kf, vft, w_uk, w_uv, tril, slc)


def _swa_kernel(q_ref, kv_ref, kvp_ref, sink_ref, o_ref, *, slopes):
    i = pl.program_id(0)
    w = WINDOW
    gsz = SWA_HEADS // SWA_KV_HEADS
    q = q_ref[...]
    kv = kv_ref[...]
    kvp = kvp_ref[...]
    qi = _iota((w, 2 * w), 0)
    kj = _iota((w, 2 * w), 1)
    dist = qi + w - kj
    in_band = (dist >= 0) & (dist < w)
    valid = [in_band & ((kj >= w) | (i > 0))] + [in_band] * (SWA_BLOCKS - 1)
    distf = dist.astype(F32)
    sinks = sink_ref[...]
    rows = [slice(b * w, (b + 1) * w) for b in range(SWA_BLOCKS)]
    prev = [kvp] + [kv[rows[b], :] for b in range(SWA_BLOCKS - 1)]
    k2 = [[jnp.concatenate([prev[b][:, g * HEAD_DIM:(g + 1) * HEAD_DIM],
                            kv[rows[b], g * HEAD_DIM:(g + 1) * HEAD_DIM]], axis=0).astype(BF16)
           for g in range(SWA_KV_HEADS)] for b in range(SWA_BLOCKS)]
    v2 = [[jnp.concatenate([prev[b][:, w + g * HEAD_DIM:w + (g + 1) * HEAD_DIM],
                            kv[rows[b], w + g * HEAD_DIM:w + (g + 1) * HEAD_DIM]], axis=0).astype(BF16)
           for g in range(SWA_KV_HEADS)] for b in range(SWA_BLOCKS)]
    items = [(b, hd) for b in range(SWA_BLOCKS) for hd in range(SWA_HEADS)]
    s = [_dot_nt(q[rows[b], hd * HEAD_DIM:(hd + 1) * HEAD_DIM].astype(BF16), k2[b][hd // gsz]) * HEAD_DIM ** -0.5
         for b, hd in items]
    s = [jnp.where(valid[b], s[n] - slopes[hd] * distf, NEG) for n, (b, hd) in enumerate(items)]
    sink = [sinks[0:1, hd:hd + 1] for _, hd in items]
    m = [jnp.maximum(jnp.max(s[n], axis=1, keepdims=True), sink[n]) for n in range(len(items))]
    e = [jnp.exp(s[n] - m[n]) for n in range(len(items))]
    p = [e[n] / (jnp.sum(e[n], axis=1, keepdims=True) + jnp.exp(sink[n] - m[n])) for n in range(len(items))]
    outs = [_dot(p[n].astype(BF16), v2[b][hd // gsz]) for n, (b, hd) in enumerate(items)]
    for b in range(SWA_BLOCKS):
        o_ref[rows[b], :] = jnp.concatenate(outs[b * SWA_HEADS:(b + 1) * SWA_HEADS], axis=1)


SWA_BLOCKS = 2


def _swa_mix(sq, skv, sinks, slopes):
    t = sq.shape[0]
    w = WINDOW
    step = SWA_BLOCKS * w
    return pl.pallas_call(
        functools.partial(_swa_kernel, slopes=slopes),
        grid=(t // step,),
        in_specs=[
            pl.BlockSpec((step, SWA_W), lambda i: (i, 0)),
            pl.BlockSpec((step, 2 * w), lambda i: (i, 0)),
            pl.BlockSpec((w, 2 * w), lambda i: (jnp.maximum(SWA_BLOCKS * i - 1, 0), 0)),
            pl.BlockSpec((1, LANES), lambda i: (0, 0)),
        ],
        out_specs=pl.BlockSpec((step, SWA_W), lambda i: (i, 0)),
        out_shape=jax.ShapeDtypeStruct((t, SWA_W), F32),
        compiler_params=_params(("arbitrary",)),
        name="swa_mix",
    )(sq, skv, skv, sinks)


def _post_mix_kernel(x_ref, orw_ref, ods_ref, osw_ref, wout_ref, g1_ref, lng_ref, lnb_ref,
                     sc2_ref, sh2_ref, rwt_ref, rb_ref, tri_ref,
                     x1_ref, h2_ref, eidx_ref, rank_ref, gate_ref, cnt_ref, carry_ref):
    i = pl.program_id(0)

    @pl.when(i == 0)
    def _():
        carry_ref[...] = jnp.zeros_like(carry_ref)

    y = (_dot(orw_ref[...].astype(BF16), wout_ref[0:RWKV_W, :])
         + _dot(ods_ref[...].astype(BF16), wout_ref[RWKV_W:RWKV_W + DSA_W, :])
         + _dot(osw_ref[...].astype(BF16), wout_ref[RWKV_W + DSA_W:D_MODEL, :]))
    x1 = _layer_norm_rows(ALPHA * x_ref[...] + g1_ref[...] * y, lng_ref[...], lnb_ref[...])
    x1_ref[...] = x1
    h2 = x1 * (1.0 + sc2_ref[...]) + sh2_ref[...]
    h2_ref[...] = _pack_halves(h2)

    tm = h2.shape[0]
    ne = N_EXPERTS
    gs = ne // N_GROUPS
    scores = _sigmoid(_dot_nt(rwt_ref[...], h2, HI))
    sel = scores + rb_ref[...]
    sub = _iota((gs, tm), 0).astype(F32)
    gsc = []
    for j in range(N_GROUPS):
        gj = sel[j * gs:(j + 1) * gs, :]
        m1 = jnp.max(gj, axis=0, keepdims=True)
        f1 = jnp.min(jnp.where(gj == m1, sub, float(gs)), axis=0, keepdims=True)
        m2 = jnp.max(jnp.where(sub == f1, -jnp.inf, gj), axis=0, keepdims=True)
        gsc.append(m1 + m2)
    gsc = jnp.concatenate(gsc, axis=0)
    gid = _iota((N_GROUPS, tm), 0).astype(F32)
    gmask = jnp.zeros((N_GROUPS, tm), F32)
    for _ in range(TOPK_GROUPS):
        mx = jnp.max(gsc, axis=0, keepdims=True)
        fi = jnp.min(jnp.where(gsc == mx, gid, float(N_GROUPS)), axis=0, keepdims=True)
        pick = gid == fi
        gmask = jnp.where(pick, 1.0, gmask)
        gsc = jnp.where(pick, -jnp.inf, gsc)
    selm = jnp.concatenate(
        [jnp.where(gmask[j:j + 1, :] > 0.5, sel[j * gs:(j + 1) * gs, :], NEG) for j in range(N_GROUPS)], axis=0)
    eid = _iota((ne, tm), 0).astype(F32)
    gsel, eids = [], []
    chosen_f = jnp.zeros((ne, tm), F32)
    for _ in range(TOP_K):
        mx = jnp.max(selm, axis=0, keepdims=True)
        fi = jnp.min(jnp.where(selm == mx, eid, float(ne)), axis=0, keepdims=True)
        pick = eid == fi
        eids.append(fi)
        gsel.append(jnp.sum(jnp.where(pick, scores, 0.0), axis=0, keepdims=True))
        chosen_f = jnp.where(pick, 1.0, chosen_f)
        selm = jnp.where(pick, -jnp.inf, selm)
    gsum = gsel[0]
    for kx in range(1, TOP_K):
        gsum = gsum + gsel[kx]
    before = _dot(chosen_f.astype(BF16), tri_ref[...]) + carry_ref[:, 0:1]
    ranks = [jnp.sum(jnp.where(eid == eids[kx], before, 0.0), axis=0, keepdims=True) for kx in range(TOP_K)]
    eidx_ref[...] = jnp.concatenate(eids, axis=0).astype(I32)
    rank_ref[...] = jnp.concatenate(ranks, axis=0).astype(I32)
    gate_ref[...] = jnp.concatenate(gsel, axis=0) / gsum * ROUTED_SCALE
    carry_ref[...] = carry_ref[...] + jnp.sum(chosen_f, axis=1, keepdims=True)
    cnt_ref[...] = carry_ref[...]


def _post_mix(x, o_rw, o_ds, o_sw, w_out, g1, ln_g, ln_b, sc2, sh2, router_wt, router_b, tm=512):
    t, d = x.shape
    tri = (np.arange(tm)[:, None] < np.arange(tm)[None, :]).astype(np.float32)
    tri = jnp.asarray(tri, BF16)
    row = lambda i: (i, 0)
    const = lambda i: (0, 0)
    col = lambda i: (0, i)
    vec = pl.BlockSpec((1, d), const)
    return pl.pallas_call(
        _post_mix_kernel,
        grid=(t // tm,),
        in_specs=[
            pl.BlockSpec((tm, d), row),
            pl.BlockSpec((tm, RWKV_W), row),
            pl.BlockSpec((tm, DSA_W), row),
            pl.BlockSpec((tm, SWA_W), row),
            pl.BlockSpec((d, d), const),
            vec, vec, vec, vec, vec,
            pl.BlockSpec((N_EXPERTS, d), const),
            pl.BlockSpec((N_EXPERTS, 1), const),
            pl.BlockSpec((tm, tm), const),
        ],
        out_specs=[
            pl.BlockSpec((tm, d), row),
            pl.BlockSpec((tm, d // 2), row),
            pl.BlockSpec((TOP_K, tm), col),
            pl.BlockSpec((TOP_K, tm), col),
            pl.BlockSpec((TOP_K, tm), col),
            pl.BlockSpec((N_EXPERTS, LANES), const),
        ],
        out_shape=[
            jax.ShapeDtypeStruct((t, d), F32),
            jax.ShapeDtypeStruct((t, d // 2), I32),
            jax.ShapeDtypeStruct((TOP_K, t), I32),
            jax.ShapeDtypeStruct((TOP_K, t), I32),
            jax.ShapeDtypeStruct((TOP_K, t), F32),
            jax.ShapeDtypeStruct((N_EXPERTS, LANES), F32),
        ],
        scratch_shapes=[pltpu.VMEM((N_EXPERTS, LANES), F32)],
        compiler_params=_params(("arbitrary",)),
        name="post_mix_router",
    )(x, o_rw, o_ds, o_sw, w_out, g1, ln_g, ln_b, sc2, sh2, router_wt, router_b, tri)


MOE_ROWS = 512
MOE_TILE = 256


def _pack_halves(v):
    w = v.shape[1] // 2
    bits = lax.bitcast_convert_type(v.astype(BF16).astype(F32), I32)
    return bits[:, :w] | lax.shift_right_logical(bits[:, w:], 16)


def _unpack_halves(p):
    return lax.bitcast_convert_type(p & -65536, F32), lax.bitcast_convert_type(p << 16, F32)


def _row_copy(src_ref, src_row, dst_ref, dst_row, sem):
    return pltpu.make_async_copy(src_ref.at[pl.ds(src_row, 1), :], dst_ref.at[pl.ds(dst_row, 1), :], sem)


def _dispatch_kernel(slot_hbm, h_ref, xs_in, xs_out, slot_smem, sem_tab, sem_rows):
    del xs_in
    i = pl.program_id(0)
    tab = pltpu.make_async_copy(slot_hbm.at[i], slot_smem, sem_tab)
    tab.start()
    tab.wait()

    def issue(tt, carry):
        for kx in range(TOP_K):
            _row_copy(h_ref, tt, xs_out, slot_smem[kx, tt], sem_rows).start(priority=kx % 2)
        return carry

    lax.fori_loop(0, MOE_TILE, issue, 0)

    def drain(tt, carry):
        for kx in range(TOP_K):
            _row_copy(h_ref, 0, xs_out, 0, sem_rows).wait()
        return carry

    lax.fori_loop(0, MOE_TILE, drain, 0)


def _dispatch(slot_tiles, rows, cap):
    t, d = rows.shape
    xs0 = jnp.zeros((cap, d), rows.dtype)
    return pl.pallas_call(
        _dispatch_kernel,
        grid=(t // MOE_TILE,),
        in_specs=[
            pl.BlockSpec(memory_space=pl.ANY),
            pl.BlockSpec((MOE_TILE, d), lambda i: (i, 0)),
            pl.BlockSpec(memory_space=pl.ANY),
        ],
        out_specs=pl.BlockSpec(memory_space=pl.ANY),
        out_shape=jax.ShapeDtypeStruct((cap, d), rows.dtype),
        scratch_shapes=[
            pltpu.SMEM((TOP_K, MOE_TILE), I32),
            pltpu.SemaphoreType.DMA,
            pltpu.SemaphoreType.DMA,
        ],
        input_output_aliases={2: 0},
        compiler_params=_params(("arbitrary",)),
        name="moe_dispatch",
    )(slot_tiles, rows, xs0)


def _expert_kernel(be_ref, nb_ref, xs_ref, w1_ref, w3_ref, w2_ref, ys_ref, w1b, w3b, w2b):
    b = pl.program_id(0)
    changed = (b == 0) | (be_ref[b] != be_ref[jnp.maximum(b - 1, 0)])

    @pl.when(changed & (b < nb_ref[0]))
    def _():
        w1b[...] = w1_ref[0, 0].astype(BF16)
        w3b[...] = w3_ref[0, 0].astype(BF16)
        w2b[...] = w2_ref[0, 0].astype(BF16)

    @pl.when(b < nb_ref[0])
    def _():
        x_hi, x_lo = _unpack_halves(xs_ref[...])
        x_hi, x_lo = x_hi.astype(BF16), x_lo.astype(BF16)
        half = x_hi.shape[1]
        a = _dot(x_hi, w1b[0:half, :]) + _dot(x_lo, w1b[half:2 * half, :])
        gte = _dot(x_hi, w3b[0:half, :]) + _dot(x_lo, w3b[half:2 * half, :])
        hmid = (a * _sigmoid(a) * gte).astype(BF16)
        ys_ref[...] = _pack_halves(_dot(hmid, w2b[...]))

    @pl.when(b >= nb_ref[0])
    def _():
        ys_ref[...] = jnp.zeros_like(ys_ref)


def _experts(block_e, n_used, xs, w1, w3, w2, layer):
    cap, dp = xs.shape
    d = 2 * dp
    nb = cap // MOE_ROWS
    grid_spec = pltpu.PrefetchScalarGridSpec(
        num_scalar_prefetch=2,
        grid=(nb,),
        in_specs=[
            pl.BlockSpec((MOE_ROWS, dp), lambda b, be, nu: (b, 0)),
            pl.BlockSpec((1, 1, d, D_EXPERT), lambda b, be, nu: (layer, be[b], 0, 0)),
            pl.BlockSpec((1, 1, d, D_EXPERT), lambda b, be, nu: (layer, be[b], 0, 0)),
            pl.BlockSpec((1, 1, D_EXPERT, d), lambda b, be, nu: (layer, be[b], 0, 0)),
        ],
        out_specs=pl.BlockSpec((MOE_ROWS, dp), lambda b, be, nu: (b, 0)),
        scratch_shapes=[
            pltpu.VMEM((d, D_EXPERT), BF16),
            pltpu.VMEM((d, D_EXPERT), BF16),
            pltpu.VMEM((D_EXPERT, d), BF16),
        ],
    )
    return pl.pallas_call(
        _expert_kernel,
        grid_spec=grid_spec,
        out_shape=jax.ShapeDtypeStruct((cap, dp), I32),
        compiler_params=_params(("arbitrary",)),
        name="moe_experts",
    )(block_e, n_used, xs, w1, w3, w2)


def _combine_kernel(slot_hbm, ys_hbm, x1_ref, h2_ref, gate_ref, sw1_ref, sw3_ref, sw2_ref,
                    g2_ref, lng_ref, lnb_ref, o_ref, slot_smem, gbuf, sem_tab, sem_rows):
    i = pl.program_id(0)
    tab = pltpu.make_async_copy(slot_hbm.at[i], slot_smem, sem_tab)
    tab.start()
    tab.wait()

    def issue(tt, carry):
        for kx in range(TOP_K):
            _row_copy(ys_hbm, slot_smem[kx, tt], gbuf.at[kx], tt, sem_rows).start(priority=kx % 2)
        return carry

    lax.fori_loop(0, MOE_TILE, issue, 0)

    h_hi, h_lo = _unpack_halves(h2_ref[...])
    h_hi, h_lo = h_hi.astype(BF16), h_lo.astype(BF16)
    half = h_hi.shape[1]
    a = _dot(h_hi, sw1_ref[0:half, :]) + _dot(h_lo, sw1_ref[half:2 * half, :])
    gte = _dot(h_hi, sw3_ref[0:half, :]) + _dot(h_lo, sw3_ref[half:2 * half, :])
    y = _dot((a * _sigmoid(a) * gte).astype(BF16), sw2_ref[...])

    def drain(tt, carry):
        for kx in range(TOP_K):
            _row_copy(ys_hbm, 0, gbuf.at[kx], 0, sem_rows).wait()
        return carry

    lax.fori_loop(0, MOE_TILE, drain, 0)

    gates = gate_ref[...]
    r_hi = jnp.zeros((MOE_TILE, half), F32)
    r_lo = jnp.zeros((MOE_TILE, half), F32)
    for kx in range(TOP_K):
        e_hi, e_lo = _unpack_halves(gbuf[kx])
        r_hi = r_hi + gates[:, kx:kx + 1] * e_hi
        r_lo = r_lo + gates[:, kx:kx + 1] * e_lo
    y = y + jnp.concatenate([r_hi, r_lo], axis=1)
    o_ref[...] = _layer_norm_rows(ALPHA * x1_ref[...] + g2_ref[...] * y, lng_ref[...], lnb_ref[...])


def _combine(slot_tiles, ys, x1, h2, gates_t, sw1, sw3, sw2, g2, ln_g, ln_b):
    t, d = x1.shape
    row = lambda i: (i, 0)
    const = lambda i: (0, 0)
    vec = pl.BlockSpec((1, d), const)
    return pl.pallas_call(
        _combine_kernel,
        grid=(t // MOE_TILE,),
        in_specs=[
            pl.BlockSpec(memory_space=pl.ANY),
            pl.BlockSpec(memory_space=pl.ANY),
            pl.BlockSpec((MOE_TILE, d), row),
            pl.BlockSpec((MOE_TILE, d // 2), row),
            pl.BlockSpec((MOE_TILE, TOP_K), row),
            pl.BlockSpec((d, D_EXPERT), const),
            pl.BlockSpec((d, D_EXPERT), const),
            pl.BlockSpec((D_EXPERT, d), const),
            vec, vec, vec,
        ],
        out_specs=pl.BlockSpec((MOE_TILE, d), row),
        out_shape=jax.ShapeDtypeStruct((t, d), F32),
        scratch_shapes=[
            pltpu.SMEM((TOP_K, MOE_TILE), I32),
            pltpu.VMEM((TOP_K, MOE_TILE, d // 2), I32),
            pltpu.SemaphoreType.DMA,
            pltpu.SemaphoreType.DMA,
        ],
        compiler_params=_params(("arbitrary",)),
        name="moe_combine",
    )(slot_tiles, ys, x1, h2, gates_t, sw1, sw3, sw2, g2, ln_g, ln_b)


def _pad_w_in(w_in_l):
    d = w_in_l.shape[0]
    pad = jnp.zeros((d, C_SQ[0] - N_ORIG_BEFORE_PAD), w_in_l.dtype)
    return jnp.concatenate([w_in_l[:, :N_ORIG_BEFORE_PAD], pad, w_in_l[:, N_ORIG_BEFORE_PAD:]], axis=1)


def _pad_lanes(v, width=LANES):
    v = v.reshape(1, -1)
    return jnp.pad(v, ((0, 0), (0, width - v.shape[1])))


def _moe_tables(eidx, rank, counts):
    t = eidx.shape[1]
    cnt = counts[:, 0].astype(I32)
    padded = (cnt + MOE_ROWS - 1) // MOE_ROWS * MOE_ROWS
    pad_end = jnp.cumsum(padded)
    pad_start = pad_end - padded
    e_ids = jnp.arange(N_EXPERTS, dtype=I32)
    start_of = jnp.sum(jnp.where(eidx[..., None] == e_ids, pad_start, 0), axis=-1)
    slot = start_of + rank
    slot_tiles = slot.reshape(TOP_K, t // MOE_TILE, MOE_TILE).transpose(1, 0, 2)
    cap = t * TOP_K + N_EXPERTS * MOE_ROWS
    nb = cap // MOE_ROWS
    blk_row = jnp.arange(nb, dtype=I32)[:, None] * MOE_ROWS
    block_e = jnp.minimum(jnp.sum((pad_end[None, :] <= blk_row).astype(I32), axis=1), N_EXPERTS - 1)
    n_used = (pad_end[-1] // MOE_ROWS).astype(I32).reshape(1)
    return slot_tiles, block_e, n_used, cap


def kernel(x, c, w_mod, b_mod, w_in, rwkv_mu, rwkv_w0, rwkv_w2, rwkv_a0, rwkv_a2, rwkv_g2, rwkv_k_k, rwkv_k_a, rwkv_r_k, rwkv_ln_g, rwkv_ln_b, dsa_kv_norm, dsa_w_uk, dsa_w_uv, dsa_ik_g, dsa_ik_b, swa_sinks, w_out, ln_mix_g, ln_mix_b, router_w, router_bias, exp_w1, exp_w3, exp_w2, sh_w1, sh_w3, sh_w2, ln_ffn_g, ln_ffn_b):
    bsz, t, d = x.shape
    assert bsz == 1 and d == D_MODEL
    depth = w_mod.shape[0]
    n_sl = SWA_HEADS + DSA_HEADS
    slopes = [2.0 ** (-8.0 * (j + 1.0) / n_sl) for j in range(n_sl)]
    swa_slopes, dsa_slopes = slopes[:SWA_HEADS], slopes[SWA_HEADS:]

    mod = _modulation(c, w_mod, b_mod)
    xs_cur = x[0]
    row1 = lambda v: v.reshape(1, -1)
    for l in range(depth):
        sh1, sc1, g1, sh2, sc2, g2 = [mod[l, :, j * d:(j + 1) * d] for j in range(6)]
        wp = _pad_w_in(w_in[l])
        w_hi = wp.astype(BF16)
        w_idx = wp[:, C_IDX[0]:C_IDX[1]]
        w_idx_lo = (w_idx - w_idx.astype(BF16).astype(F32)).astype(BF16)
        rkv, lora, dq, ckv, iq, ikw, sq, skv = _input_proj(
            xs_cur, sc1, sh1, w_hi, w_idx_lo, row1(dsa_kv_norm[l]),
            _pad_lanes(dsa_ik_g[l]), _pad_lanes(dsa_ik_b[l]))
        o_rw = _rwkv_mix(rkv, lora, row1(rwkv_mu[l]), row1(rwkv_w0[l]), rwkv_w2[l], row1(rwkv_a0[l]),
                         rwkv_a2[l], rwkv_g2[l], row1(rwkv_k_k[l]), row1(rwkv_k_a[l]), row1(rwkv_r_k[l]),
                         row1(rwkv_ln_g[l]), row1(rwkv_ln_b[l]))
        o_ds = _dsa_mix(dq, iq, ikw, ckv, dsa_w_uk[l], dsa_w_uv[l], dsa_slopes)
        o_sw = _swa_mix(sq, skv, _pad_lanes(swa_sinks[l]), swa_slopes)
        x1, h2, eidx, rank, gates, counts = _post_mix(
            xs_cur, o_rw, o_ds, o_sw, w_out[l].astype(BF16), g1, row1(ln_mix_g[l]), row1(ln_mix_b[l]),
            sc2, sh2, router_w[l].T, router_bias[l].reshape(-1, 1))
        slot_tiles, block_e, n_used, cap = _moe_tables(eidx, rank, counts)
        xs_sorted = _dispatch(slot_tiles, h2, cap)
        ys = _experts(block_e, n_used, xs_sorted, exp_w1, exp_w3, exp_w2, l)
        xs_cur = _combine(slot_tiles, ys, x1, h2, gates.T, sh_w1[l].astype(BF16), sh_w3[l].astype(BF16),
                          sh_w2[l].astype(BF16), g2, row1(ln_ffn_g[l]), row1(ln_ffn_b[l]))
    return xs_cur[None]
```

```python
import functools
import math

import jax
import jax.numpy as jnp
import numpy as np
from jax import lax
from jax.experimental import pallas as pl
from jax.experimental.pallas import tpu as pltpu

F32 = jnp.float32
BF16 = jnp.bfloat16
I32 = jnp.int32
HI = lax.Precision.HIGHEST

D_MODEL = 1024
DEPTH = 4
HEAD_DIM = 64
RWKV_HEADS = 6
DSA_HEADS = 4
SWA_HEADS = 6
SWA_KV_HEADS = 2
RWKV_W = RWKV_HEADS * HEAD_DIM
DSA_W = DSA_HEADS * HEAD_DIM
SWA_W = SWA_HEADS * HEAD_DIM
DECAY_LORA = 64
AAA_LORA = 64
GATE_LORA = 128
GN_EPS = 64e-5
KV_LORA = 128
IDX_HEADS = 4
IDX_DIM = 64
TOPK_MAX = 256
WINDOW = 128
N_EXPERTS = 64
TOP_K = 8
N_GROUPS = 8
TOPK_GROUPS = 4
D_EXPERT = 256
ROUTED_SCALE = 2.5
ALPHA = (2 * DEPTH) ** 0.25
LN_EPS = 1e-5
NEG = -1e30
INT_MIN = -(2 ** 31)

LANES = 128
VMEM_LIMIT = 56 * 1024 * 1024

C_RKV = (0, 1152)
C_LORA = (1152, 1408)
C_DQ = (1408, 1664)
C_CKV = (1664, 1792)
C_IDX = (1792, 2176)
C_SQ = (2176, 2560)
C_SKV = (2560, 2816)
P_PAD = 2816
N_ORIG_BEFORE_PAD = 2116


def _dot(a, b, prec=None):
    return jnp.dot(a, b, preferred_element_type=F32, precision=prec)


def _dot_nt(a, b, prec=None):
    return lax.dot_general(a, b, (((1,), (1,)), ((), ())), preferred_element_type=F32, precision=prec)


def _split2(a):
    a_hi = a.astype(BF16)
    return a_hi, (a - a_hi.astype(F32)).astype(BF16)


def _bdot(a, b):
    return _dot(a.astype(BF16), b.astype(BF16))


def _bdot_nt(a, b):
    return _dot_nt(a.astype(BF16), b.astype(BF16))


def _dot2(a, b_exact):
    a_hi, a_lo = _split2(a)
    return _dot(a_hi, b_exact) + _dot(a_lo, b_exact)


def _dot2_l(a_exact, b):
    b_hi, b_lo = _split2(b)
    return _dot(a_exact, b_hi) + _dot(a_exact, b_lo)


def _dot3(a, b):
    a_hi, a_lo = _split2(a)
    b_hi, b_lo = _split2(b)
    return _dot(a_hi, b_hi) + (_dot(a_lo, b_hi) + _dot(a_hi, b_lo))


def _iota(shape, dim):
    return lax.broadcasted_iota(I32, shape, dim)


def _sigmoid(x):
    return 1.0 / (1.0 + jnp.exp(-x))


def _layer_norm_rows(v, g, b):
    mu = jnp.mean(v, axis=-1, keepdims=True)
    d = v - mu
    var = jnp.mean(d * d, axis=-1, keepdims=True)
    return d * lax.rsqrt(var + LN_EPS) * g + b


def _params(sem):
    return pltpu.CompilerParams(dimension_semantics=sem, vmem_limit_bytes=VMEM_LIMIT)


def _mod_kernel(c_ref, w_ref, b_ref, o_ref):
    c = c_ref[...]
    cond = c * _sigmoid(c)
    o_ref[0] = _dot(cond, w_ref[0], HI) + b_ref[0]


def _modulation(c, w_mod, b_mod):
    depth, d, d6 = w_mod.shape
    c8 = jnp.broadcast_to(c, (8, d))
    nj = d6 // d
    out = pl.pallas_call(
        _mod_kernel,
        grid=(depth, nj),
        in_specs=[
            pl.BlockSpec((8, d), lambda l, j: (0, 0)),
            pl.BlockSpec((1, d, d), lambda l, j: (l, 0, j)),
            pl.BlockSpec((1, 1, d), lambda l, j: (l, 0, j)),
        ],
        out_specs=pl.BlockSpec((1, 8, d), lambda l, j: (l, 0, j)),
        out_shape=jax.ShapeDtypeStruct((depth, 8, d6), F32),
        compiler_params=_params(("arbitrary", "arbitrary")),
        name="modulation",
    )(c8, w_mod, b_mod.reshape(depth, 1, d6))
    return out[:, 0:1, :]


def _proj_kernel(x_ref, sc_ref, sh_ref, w_ref, wlo_ref, kvn_ref, ikg_ref, ikb_ref,
                 rkv_ref, lora_ref, dq_ref, ckv_ref, iq_ref, ikw_ref, sq_ref, skv_ref):
    h = x_ref[...] * (1.0 + sc_ref[...]) + sh_ref[...]
    hb = h.astype(BF16)
    hl = (h - hb.astype(F32)).astype(BF16)

    def mm(c):
        return _dot(hb, w_ref[:, c[0]:c[1]])

    rkv_ref[...] = mm(C_RKV)
    lora_ref[...] = mm(C_LORA)
    dq_ref[...] = mm(C_DQ)
    sq_ref[...] = mm(C_SQ)
    skv_ref[...] = mm(C_SKV)
    ckv = mm(C_CKV)
    ckv_ref[...] = ckv * lax.rsqrt(jnp.mean(ckv * ckv, axis=-1, keepdims=True) + 1e-6) * kvn_ref[...]
    idx = mm(C_IDX) + _dot(hl, w_ref[:, C_IDX[0]:C_IDX[1]]) + _dot(hb, wlo_ref[...])
    iq_ref[...] = idx[:, 0:256]
    g3 = idx[:, 256:384]
    lane = _iota(g3.shape, 1)
    isk = lane < IDX_DIM
    mu = jnp.sum(jnp.where(isk, g3, 0.0), axis=-1, keepdims=True) * (1.0 / IDX_DIM)
    dk = jnp.where(isk, g3 - mu, 0.0)
    var = jnp.sum(dk * dk, axis=-1, keepdims=True) * (1.0 / IDX_DIM)
    ikn = dk * lax.rsqrt(var + LN_EPS) * ikg_ref[...] + ikb_ref[...]
    ikw_ref[...] = jnp.where(isk, ikn, g3 * (IDX_HEADS ** -0.5 * IDX_DIM ** -0.5))


def _input_proj(x, sc, sh, w_hi, w_idx_lo, kvn, ikg, ikb, tm=512):
    t, d = x.shape
    widths = [C_RKV, C_LORA, C_DQ, C_CKV, (0, 256), (0, 128), C_SQ, C_SKV]
    widths = [c[1] - c[0] for c in widths]
    const = lambda i: (0, 0)
    row = lambda i: (i, 0)
    return pl.pallas_call(
        _proj_kernel,
        grid=(t // tm,),
        in_specs=[
            pl.BlockSpec((tm, d), row),
            pl.BlockSpec((1, d), const),
            pl.BlockSpec((1, d), const),
            pl.BlockSpec((d, P_PAD), const),
            pl.BlockSpec((d, C_IDX[1] - C_IDX[0]), const),
            pl.BlockSpec((1, KV_LORA), const),
            pl.BlockSpec((1, LANES), const),
            pl.BlockSpec((1, LANES), const),
        ],
        out_specs=[pl.BlockSpec((tm, w), row) for w in widths],
        out_shape=[jax.ShapeDtypeStruct((t, w), F32) for w in widths],
        compiler_params=_params(("arbitrary",)),
        name="input_proj",
    )(x, sc, sh, w_hi, w_idx_lo, kvn, ikg, ikb)


RW_CHUNK = 64
RW_UNROLL = 4


def _rwkv_kernel(r_ref, k_ref, v_ref, lora_ref, rp_ref, kp_ref, vp_ref, lp_ref,
                 mur_ref, muk_ref, muv_ref, mul_ref, w0_ref, w2_ref, a0_ref, a2_ref, g2_ref,
                 kk_ref, ka_ref, rk_ref, lng_ref, lnb_ref, o_ref,
                 h_ref, y_ref, st_ref, wm_ref, ar_ref, rs_ref, vs_ref, lt_ref, zm_ref, y0_ref, gc_ref, *, tg):
    g = pl.program_id(0)
    c64 = RW_CHUNK
    nch = tg // c64
    npair = RWKV_W // LANES
    pair_lanes = [slice(p * LANES, (p + 1) * LANES) for p in range(npair)]
    lane = _iota((1, LANES), 1)
    first = g == 0

    @pl.when(first)
    def _():
        h_ref[...] = jnp.zeros_like(h_ref)

    rowid = _iota((tg, 1), 0)

    def shift_mix(cur_ref, prev_ref, mu_ref):
        cur = cur_ref[...]
        prev_row = jnp.where(first, 0.0, prev_ref[7:8, :])
        rolled = pltpu.roll(cur, 1, 0)
        shifted = jnp.where(rowid == 0, prev_row, rolled)
        return cur + (shifted - cur) * mu_ref[...]

    r = shift_mix(r_ref, rp_ref, mur_ref)
    k = shift_mix(k_ref, kp_ref, muk_ref)
    v = shift_mix(v_ref, vp_ref, muv_ref)
    lo = shift_mix(lora_ref, lp_ref, mul_ref)
    wl = lo[:, 0:DECAY_LORA]
    al = lo[:, DECAY_LORA:DECAY_LORA + AAA_LORA]
    gl = lo[:, 128:256]

    zw = -(w0_ref[...] + _dot3(jnp.tanh(wl), w2_ref[...]))
    softplus = jnp.maximum(zw, 0.0) + jnp.log(1.0 + jnp.exp(-jnp.abs(zw)))
    lw = -jnp.exp(-softplus - 0.5)
    a = _sigmoid(a0_ref[...] + _bdot(al, a2_ref[...]))
    gate = _bdot(_sigmoid(gl), g2_ref[...])

    ri = _iota((LANES, LANES), 0) // HEAD_DIM
    ci = _iota((LANES, LANES), 1) // HEAD_DIM
    bones = jnp.where(ri == ci, 1.0, 0.0).astype(BF16)

    def head_sum(xf):
        return jnp.concatenate([_dot2(xf[:, pl_], bones) for pl_ in pair_lanes], axis=1)

    kk = k * kk_ref[...]
    kk = kk / jnp.maximum(jnp.sqrt(head_sum(kk * kk)), 1e-12)
    k2 = k * (1.0 + (a - 1.0) * ka_ref[...])
    bonus = head_sum(r * k2 * rk_ref[...]) * v
    bvec = a * kk

    st_ref[0] = r
    st_ref[1] = k2
    st_ref[2] = v
    st_ref[3] = lw
    st_ref[4] = kk
    st_ref[5] = bvec

    rr = _iota((LANES, LANES), 0)
    cc = _iota((LANES, LANES), 1)
    same = (rr // c64) == (cc // c64)
    strict = same & ((rr % c64) > (cc % c64))
    incl = same & ((rr % c64) >= (cc % c64))
    eye = jnp.where(rr == cc, 1.0, 0.0)
    tril = jnp.where(_iota((c64, c64), 0) >= _iota((c64, c64), 1), 1.0, 0.0).astype(BF16)
    lo_half = lane < HEAD_DIM

    def stack(xc):
        return jnp.concatenate([jnp.where(lo_half, xc, 0.0), jnp.where(lo_half, 0.0, xc)], axis=0)

    def prepare(c, carry):
        chunks = [c * RW_UNROLL + j for j in range(RW_UNROLL)]
        sls = [pl.ds(pl.multiple_of(cj * c64, c64), c64) for cj in chunks]
        items = [(p, j) for j in range(RW_UNROLL) for p in range(npair)]
        pairs = range(len(items))
        idx = [p * nch + chunks[j] for p, j in items]
        ld = lambda q: [st_ref[q, sls[j], pair_lanes[p]] for p, j in items]
        rc, kc, vc, lwc, kkc, bc = ld(0), ld(1), ld(2), ld(3), ld(4), ld(5)
        cum = [_dot2_l(tril, lwc[p]) for p in pairs]
        tot = [cum[p][c64 - 1:c64, :] for p in pairs]
        g_in = [jnp.exp(cum[p]) for p in pairs]
        g_ex = [jnp.exp(cum[p] - lwc[p]) for p in pairs]
        g_inv = [jnp.exp(-cum[p]) for p in pairs]
        g_rest = [jnp.exp(tot[p] - cum[p]) for p in pairs]
        a_s = [stack(-kkc[p] * g_ex[p]).astype(BF16) for p in pairs]
        b_s = [stack(bc[p] * g_inv[p]).astype(BF16) for p in pairs]
        k_s = [stack(kc[p] * g_inv[p]).astype(BF16) for p in pairs]
        r_s = [stack(rc[p] * g_in[p]).astype(BF16) for p in pairs]
        v_s = [stack(vc[p]).astype(BF16) for p in pairs]
        nmat = [jnp.where(strict, _dot_nt(a_s[p], b_s[p]), 0.0) for p in pairs]
        aak = [jnp.where(strict, _dot_nt(a_s[p], k_s[p]), 0.0) for p in pairs]
        arb = [jnp.where(incl, _dot_nt(r_s[p], b_s[p]), 0.0) for p in pairs]
        ark = [jnp.where(incl, _dot_nt(r_s[p], k_s[p]), 0.0) for p in pairs]
        tinv = [eye + nmat[p] for p in pairs]
        pw = nmat
        for _ in range(5):
            pw = [_bdot(pw[p], pw[p]) for p in pairs]
            tinv = [_bdot(tinv[p], eye + pw[p]) for p in pairs]
        tinv = [tinv[p].astype(BF16) for p in pairs]
        akv = [_bdot(aak[p], v_s[p]).astype(BF16) for p in pairs]
        wmat = [_dot(tinv[p], a_s[p]) for p in pairs]
        zmat = [_dot(tinv[p], akv[p]) for p in pairs]
        y0 = [_bdot(ark[p], v_s[p]) for p in pairs]
        for p in pairs:
            wm_ref[idx[p]] = wmat[p].astype(BF16)
            zm_ref[idx[p]] = zmat[p]
            y0_ref[idx[p]] = y0[p]
            ar_ref[idx[p]] = arb[p].astype(BF16)
            rs_ref[idx[p]] = r_s[p]
            vs_ref[idx[p]] = v_s[p]
            lt_ref[idx[p]] = jnp.concatenate([stack(bc[p] * g_rest[p]), stack(kc[p] * g_rest[p])],
                                             axis=0).T.astype(BF16)
            gc_ref[idx[p]] = jnp.broadcast_to(jnp.sum(eye * jnp.exp(tot[p]), axis=1, keepdims=True),
                                              (LANES, LANES))
        return carry

    lax.fori_loop(0, nch // RW_UNROLL, prepare, 0)

    def advance(c, carry):
        sl = pl.ds(pl.multiple_of(c * c64, c64), c64)
        pairs = range(npair)
        idx = [p * nch + c for p in pairs]
        hst = [h_ref[p] for p in pairs]
        hb = [hst[p].astype(BF16) for p in pairs]
        u = [_dot(wm_ref[idx[p]], hb[p]) + zm_ref[idx[p]] for p in pairs]
        rh = [_dot(rs_ref[idx[p]], hb[p]) for p in pairs]
        ub = [u[p].astype(BF16) for p in pairs]
        hnew = [_dot(lt_ref[idx[p]], jnp.concatenate([ub[p], vs_ref[idx[p]]], axis=0)) for p in pairs]
        au = [_dot(ar_ref[idx[p]], ub[p]) for p in pairs]
        for p in pairs:
            h_ref[p] = gc_ref[idx[p]] * hst[p] + hnew[p]
            ys = rh[p] + au[p] + y0_ref[idx[p]]
            y_ref[sl, pair_lanes[p]] = ys[0:c64, :] + ys[c64:2 * c64, :]
        return carry

    lax.fori_loop(0, nch, advance, 0)

    y = y_ref[...]
    mean = head_sum(y) * (1.0 / HEAD_DIM)
    dy = y - mean
    var = head_sum(dy * dy) * (1.0 / HEAD_DIM)
    o = dy * lax.rsqrt(var + GN_EPS) * lng_ref[...] + lnb_ref[...]
    o_ref[...] = (o + bonus) * gate


def _rwkv_mix(rkv, lora, mu, w0, w2, a0, a2, g2, k_k, k_a, r_k, ln_g, ln_b, tg=512):
    t = rkv.shape[0]
    w = RWKV_W
    npair = w // LANES
    nmat = npair * (tg // RW_CHUNK)
    mu_r, mu_k, mu_v, mu_l = mu[:, 0:w], mu[:, w:2 * w], mu[:, 2 * w:3 * w], mu[:, 3 * w:3 * w + 256]
    blk = lambda off: pl.BlockSpec((tg, w), lambda g: (g, off))
    prev = lambda off: pl.BlockSpec((8, w), lambda g: (jnp.maximum(g * (tg // 8) - 1, 0), off))
    vec = pl.BlockSpec((1, w), lambda g: (0, 0))
    full = lambda rows: pl.BlockSpec((rows, w), lambda g: (0, 0))
    return pl.pallas_call(
        functools.partial(_rwkv_kernel, tg=tg),
        grid=(t // tg,),
        in_specs=[
            blk(0), blk(1), blk(2),
            pl.BlockSpec((tg, 256), lambda g: (g, 0)),
            prev(0), prev(1), prev(2),
            pl.BlockSpec((8, 256), lambda g: (jnp.maximum(g * (tg // 8) - 1, 0), 0)),
            vec, vec, vec,
            pl.BlockSpec((1, 256), lambda g: (0, 0)),
            vec, full(DECAY_LORA), vec, full(AAA_LORA), full(GATE_LORA),
            vec, vec, vec, vec, vec,
        ],
        out_specs=pl.BlockSpec((tg, w), lambda g: (g, 0)),
        out_shape=jax.ShapeDtypeStruct((t, w), F32),
        scratch_shapes=[
            pltpu.VMEM((npair, LANES, LANES), F32),
            pltpu.VMEM((tg, w), F32),
            pltpu.VMEM((6, tg, w), F32),
            pltpu.VMEM((nmat, LANES, LANES), BF16),
            pltpu.VMEM((nmat, LANES, LANES), BF16),
            pltpu.VMEM((nmat, LANES, LANES), BF16),
            pltpu.VMEM((nmat, LANES, LANES), BF16),
            pltpu.VMEM((nmat, LANES, 2 * LANES), BF16),
            pltpu.VMEM((nmat, LANES, LANES), F32),
            pltpu.VMEM((nmat, LANES, LANES), F32),
            pltpu.VMEM((nmat, LANES, LANES), F32),
        ],
        compiler_params=_params(("arbitrary",)),
        name="rwkv7_mix",
    )(rkv, rkv, rkv, lora, rkv, rkv, rkv, lora,
      mu_r, mu_k, mu_v, mu_l, w0, w2, a0, a2, g2, k_k, k_a, r_k, ln_g, ln_b)


DSA_QB = 256
DSA_KC = 1024
DSA_SUB = 512
CNT_ROWS = 64
TIE_BLK = 128
BIS_STEPS = 2


def _float_key(v):
    bits = lax.bitcast_convert_type(v, I32)
    return bits ^ ((bits >> 31) & 0x7FFFFFFF)


def _dsa_kernel(dq_ref, iq_ref, ikw_ref, ikx_ref, kf_ref, vft_ref, wuk_ref, wuv_ref, tril_ref, slc_ref,
                o_ref, sc_ref, acc_ref):
    i = pl.program_id(0)
    qb, kc, sc_rows = DSA_QB, DSA_KC, DSA_SUB
    nh = DSA_HEADS
    t0 = i * qb
    nch = (t0 + qb + kc - 1) // kc
    tq = t0 + _iota((1, qb), 1)

    iq = iq_ref[...]
    iq_hi = iq.astype(BF16).astype(F32)
    iq_lo = iq - iq_hi
    lhs = []
    for h in range(IDX_HEADS):
        s = slice(h * IDX_DIM, (h + 1) * IDX_DIM)
        lhs.append(jnp.concatenate([iq_hi[:, s], iq_hi[:, s], iq_lo[:, s], iq_lo[:, s]], axis=1))
    lhs_t = jnp.concatenate(lhs, axis=0).T.astype(BF16)
    ikw_t = ikw_ref[...].T
    iw = [ikw_t[IDX_DIM + h:IDX_DIM + h + 1, :] for h in range(IDX_HEADS)]

    def score_body(ch, carry, masked):
        m1, m2 = carry
        sr = sc_rows
        for sub in range(kc // sr):
            k0 = pl.multiple_of(ch * kc + sub * sr, sr)
            s_all = _dot(ikx_ref[pl.ds(k0, sr), :], lhs_t)
            acc = jnp.zeros((sr, qb), F32)
            for h in range(IDX_HEADS):
                acc = acc + jnp.maximum(s_all[:, h * qb:(h + 1) * qb], 0.0) * iw[h]
            acc = jnp.where(acc == 0.0, 0.0, acc)
            key = _float_key(acc)
            if masked:
                causal = (k0 + _iota((sr, 1), 0)) <= tq
                key = jnp.where(causal, key, INT_MIN)
                acc = jnp.where(causal, acc, -jnp.inf)
            sc_ref[pl.ds(k0, sr), :] = key
            for j in range(sr // LANES):
                xj = acc[j * LANES:(j + 1) * LANES, :]
                m2 = jnp.maximum(m2, jnp.minimum(m1, xj))
                m1 = jnp.maximum(m1, xj)
        return m1, m2

    ninf = jnp.full((LANES, qb), -jnp.inf, F32)
    n_below = t0 // kc
    top2 = lax.fori_loop(0, n_below, functools.partial(score_body, masked=False), (ninf, ninf))
    m1, m2 = lax.fori_loop(n_below, nch, functools.partial(score_body, masked=True), top2)

    def count_ge(cand):
        def body(ch, acc):
            for j in range(kc // CNT_ROWS):
                kj = pl.multiple_of(ch * kc + j * CNT_ROWS, CNT_ROWS)
                acc = acc + jnp.where(sc_ref[pl.ds(kj, CNT_ROWS), :] >= cand, 1.0, 0.0)
            return acc
        acc = lax.fori_loop(0, nch, body, jnp.zeros((CNT_ROWS, qb), F32))
        return jnp.sum(acc, axis=0, keepdims=True)

    k_row = jnp.minimum(tq + 1, TOPK_MAX).astype(F32)
    hi0 = _float_key(jnp.max(m1, axis=0, keepdims=True))
    lo0 = jnp.minimum(_float_key(jnp.min(m2, axis=0, keepdims=True)), hi0)
    c_pos = count_ge(jnp.ones((1, qb), I32))
    c_nonneg = count_ge(jnp.zeros((1, qb), I32))
    at_zero = (c_pos < k_row) & (c_nonneg >= k_row)
    above = c_pos >= k_row
    lo0 = jnp.where(at_zero, 0, jnp.where(above, jnp.maximum(lo0, 1), lo0))
    n_above0 = jnp.where(at_zero, c_pos, jnp.where(above | (hi0 < 0), 0.0, c_nonneg))
    hi0 = jnp.where(at_zero, 0, jnp.where(above, hi0, jnp.minimum(hi0, -1)))
    lo0 = jnp.minimum(lo0, hi0)

    def open_rows(lo, hi):
        return jnp.max(jnp.where(lo < hi, 1.0, 0.0))

    def bis_body(st):
        lo, hi, n_above, _ = st
        for _ in range(BIS_STEPS):
            mid = (lo | hi) - ((lo ^ hi) >> 1)
            c = count_ge(mid)
            ge = c >= k_row
            exact = c == k_row
            lo, hi = jnp.where(ge, mid, lo), jnp.where(exact, mid, jnp.where(ge, hi, mid - 1))
            n_above = jnp.where(exact, -1.0, jnp.where(ge, n_above, c))
        return lo, hi, n_above, open_rows(lo, hi)

    thr, _, n_above, _ = lax.while_loop(lambda st: st[3] > 0.5, bis_body, (lo0, hi0, n_above0, open_rows(lo0, hi0)))

    dq = dq_ref[...]
    slc = slc_ref[...]
    qaug = []
    for h in range(nh):
        ql = _bdot(dq[:, h * HEAD_DIM:(h + 1) * HEAD_DIM], wuk_ref[h]) * HEAD_DIM ** -0.5
        qaug.append(jnp.concatenate([ql, jnp.broadcast_to(slc[h:h + 1, :], (qb, LANES))], axis=1))
    qaug_t = jnp.concatenate(qaug, axis=0).T.astype(BF16)
    acc_ref[...] = jnp.zeros_like(acc_ref)

    nsub = kc // sc_rows

    def sub_starts(ch):
        return [pl.multiple_of(ch * kc + sub * sc_rows, sc_rows) for sub in range(nsub)]

    def logits(k0):
        return _dot(kf_ref[pl.ds(k0, sc_rows), :], qaug_t)

    def attend(k0, lg_all, sel, m_old):
        ps, m_new = [], []
        for h in range(nh):
            cols = slice(h * qb, (h + 1) * qb)
            lg = jnp.where(sel, lg_all[:, cols], NEG)
            mh = jnp.maximum(m_old[:, cols], jnp.max(lg, axis=0, keepdims=True))
            ps.append(jnp.exp((lg - mh).astype(BF16)))
            m_new.append(mh)
        m_new = jnp.concatenate(m_new, axis=1)
        pv = _dot(vft_ref[:, pl.ds(k0, sc_rows)], jnp.concatenate(ps, axis=1))
        acc_ref[...] = jnp.exp(m_old - m_new) * acc_ref[...] + pv
        return m_new

    m_init = jnp.full((1, nh * qb), NEG, F32)

    need = jnp.where(n_above < 0, float(TOPK_MAX), k_row - n_above)
    tril = tril_ref[...]

    def body(ch, carry):
        tie_run, m_old = carry
        ks = sub_starts(ch)
        lgs = [logits(k0) for k0 in ks]
        keys = [sc_ref[pl.ds(k0, sc_rows), :] for k0 in ks]
        blocks = [slice(j * TIE_BLK, (j + 1) * TIE_BLK) for j in range(sc_rows // TIE_BLK)]
        prefs = [[_dot(tril, jnp.where(key[bl, :] == thr, 1.0, 0.0).astype(BF16)) for bl in blocks] for key in keys]
        for k0, lg, key, pref in zip(ks, lgs, keys, prefs):
            ranks = []
            for pj in pref:
                ranks.append(tie_run + pj)
                tie_run = tie_run + pj[TIE_BLK - 1:TIE_BLK, :]
            sel = (key > thr) | ((key == thr) & (jnp.concatenate(ranks, axis=0) <= need))
            m_old = attend(k0, lg, sel, m_old)
        return tie_run, m_old

    lax.fori_loop(0, nch, body, (jnp.zeros((1, qb), F32), m_init))

    acc = acc_ref[...]
    o_lat = acc[0:KV_LORA, :] / acc[KV_LORA:KV_LORA + 1, :]
    outs = [_bdot(o_lat[:, h * qb:(h + 1) * qb].T, wuv_ref[h]) for h in range(nh)]
    o_ref[...] = jnp.concatenate(outs, axis=1)


DSA_VROWS = KV_LORA + 16


def _dsa_mix(dq, iq, ikw, ckv, w_uk, w_uv, slopes):
    t = dq.shape[0]
    assert t <= LANES * 256
    ikn = ikw[:, 0:IDX_DIM]
    ik_hi, ik_lo = _split2(ikn)
    ikx = jnp.concatenate([ik_hi, ik_lo, ik_hi, ik_lo], axis=1)
    ckv_b = ckv.astype(BF16)
    pos = jnp.arange(t, dtype=I32)
    pa = (pos // LANES).astype(BF16)[:, None]
    pb = (pos % LANES).astype(BF16)[:, None]
    kf = jnp.concatenate([ckv_b, pa, pa, pa, pb, pb, pb, jnp.zeros((t, LANES - 6), BF16)], axis=1)
    vft = jnp.concatenate([ckv_b.T, jnp.ones((1, t), BF16), jnp.zeros((DSA_VROWS - KV_LORA - 1, t), BF16)], axis=0)
    cols = []
    for sl in slopes:
        for coef in (sl * LANES, sl):
            c_hi = jnp.asarray(coef, F32).astype(BF16)
            r1 = jnp.asarray(coef, F32) - c_hi.astype(F32)
            c_mid = r1.astype(BF16)
            c_lo = (r1 - c_mid.astype(F32)).astype(BF16)
            cols += [c_hi.astype(F32), c_mid.astype(F32), c_lo.astype(F32)]
    slc = jnp.stack(cols).reshape(DSA_HEADS, 6)
    slc = jnp.pad(slc, ((0, 8 - DSA_HEADS), (0, LANES - 6)))
    assert t % DSA_KC == 0
    kc = TIE_BLK
    tril = jnp.asarray((np.arange(kc)[:, None] >= np.arange(kc)[None, :]).astype(np.float32), BF16)
    row = lambda i: (i, 0)
    const2 = lambda i: (0, 0)
    const3 = lambda i: (0, 0, 0)
    resident = lambda shape: pl.BlockSpec(shape, const2, pipeline_mode=pl.Buffered(1))
    return pl.pallas_call(
        _dsa_kernel,
        grid=(t // DSA_QB,),
        in_specs=[
            pl.BlockSpec((DSA_QB, DSA_W), row),
            pl.BlockSpec((DSA_QB, IDX_HEADS * IDX_DIM), row),
            pl.BlockSpec((DSA_QB, LANES), row),
            resident((t, 4 * IDX_DIM)),
            resident((t, 2 * LANES)),
            resident((DSA_VROWS, t)),
            pl.BlockSpec((DSA_HEADS, HEAD_DIM, KV_LORA), const3),
            pl.BlockSpec((DSA_HEADS, KV_LORA, HEAD_DIM), const3),
            resident((kc, kc)),
            pl.BlockSpec((8, LANES), const2),
        ],
        out_specs=pl.BlockSpec((DSA_QB, DSA_W), row),
        out_shape=jax.ShapeDtypeStruct((t, DSA_W), F32),
        scratch_shapes=[
            pltpu.VMEM((t, DSA_QB), I32),
            pltpu.VMEM((DSA_VROWS, DSA_HEADS * DSA_QB), F32),
        ],
        compiler_params=_params(("arbitrary",)),
        name="dsa_mix",
    )(dq, iq, ikw, ikx, kf, vft, w_uk, w_uv, tril, slc)


def _swa_kernel(q_ref, kv_ref, kvp_ref, sink_ref, o_ref, *, slopes):
    i = pl.program_id(0)
    w = WINDOW
    gsz = SWA_HEADS // SWA_KV_HEADS
    q = q_ref[...]
    kv = kv_ref[...]
    kvp = kvp_ref[...]
    qi = _iota((w, 2 * w), 0)
    kj = _iota((w, 2 * w), 1)
    dist = qi + w - kj
    in_band = (dist >= 0) & (dist < w)
    valid = [in_band & ((kj >= w) | (i > 0))] + [in_band] * (SWA_BLOCKS - 1)
    distf = dist.astype(F32)
    sinks = sink_ref[...]
    rows = [slice(b * w, (b + 1) * w) for b in range(SWA_BLOCKS)]
    prev = [kvp] + [kv[rows[b], :] for b in range(SWA_BLOCKS - 1)]
    k2 = [[jnp.concatenate([prev[b][:, g * HEAD_DIM:(g + 1) * HEAD_DIM],
                            kv[rows[b], g * HEAD_DIM:(g + 1) * HEAD_DIM]], axis=0).astype(BF16)
           for g in range(SWA_KV_HEADS)] for b in range(SWA_BLOCKS)]
    v2 = [[jnp.concatenate([prev[b][:, w + g * HEAD_DIM:w + (g + 1) * HEAD_DIM],
                            kv[rows[b], w + g * HEAD_DIM:w + (g + 1) * HEAD_DIM]], axis=0).astype(BF16)
           for g in range(SWA_KV_HEADS)] for b in range(SWA_BLOCKS)]
    items = [(b, hd) for b in range(SWA_BLOCKS) for hd in range(SWA_HEADS)]
    s = [_dot_nt(q[rows[b], hd * HEAD_DIM:(hd + 1) * HEAD_DIM].astype(BF16), k2[b][hd // gsz]) * HEAD_DIM ** -0.5
         for b, hd in items]
    s = [jnp.where(valid[b], s[n] - slopes[hd] * distf, NEG) for n, (b, hd) in enumerate(items)]
    sink = [sinks[0:1, hd:hd + 1] for _, hd in items]
    m = [jnp.maximum(jnp.max(s[n], axis=1, keepdims=True), sink[n]) for n in range(len(items))]
    e = [jnp.exp(s[n] - m[n]) for n in range(len(items))]
    p = [e[n] / (jnp.sum(e[n], axis=1, keepdims=True) + jnp.exp(sink[n] - m[n])) for n in range(len(items))]
    outs = [_dot(p[n].astype(BF16), v2[b][hd // gsz]) for n, (b, hd) in enumerate(items)]
    for b in range(SWA_BLOCKS):
        o_ref[rows[b], :] = jnp.concatenate(outs[b * SWA_HEADS:(b + 1) * SWA_HEADS], axis=1)


SWA_BLOCKS = 2


def _swa_mix(sq, skv, sinks, slopes):
    t = sq.shape[0]
    w = WINDOW
    step = SWA_BLOCKS * w
    return pl.pallas_call(
        functools.partial(_swa_kernel, slopes=slopes),
        grid=(t // step,),
        in_specs=[
            pl.BlockSpec((step, SWA_W), lambda i: (i, 0)),
            pl.BlockSpec((step, 2 * w), lambda i: (i, 0)),
            pl.BlockSpec((w, 2 * w), lambda i: (jnp.maximum(SWA_BLOCKS * i - 1, 0), 0)),
            pl.BlockSpec((1, LANES), lambda i: (0, 0)),
        ],
        out_specs=pl.BlockSpec((step, SWA_W), lambda i: (i, 0)),
        out_shape=jax.ShapeDtypeStruct((t, SWA_W), F32),
        compiler_params=_params(("arbitrary",)),
        name="swa_mix",
    )(sq, skv, skv, sinks)


def _post_mix_kernel(x_ref, orw_ref, ods_ref, osw_ref, wout_ref, g1_ref, lng_ref, lnb_ref,
                     sc2_ref, sh2_ref, rwt_ref, rb_ref, tri_ref,
                     x1_ref, h2_ref, eidx_ref, rank_ref, gate_ref, cnt_ref, carry_ref):
    i = pl.program_id(0)

    @pl.when(i == 0)
    def _():
        carry_ref[...] = jnp.zeros_like(carry_ref)

    y = (_dot(orw_ref[...].astype(BF16), wout_ref[0:RWKV_W, :])
         + _dot(ods_ref[...].astype(BF16), wout_ref[RWKV_W:RWKV_W + DSA_W, :])
         + _dot(osw_ref[...].astype(BF16), wout_ref[RWKV_W + DSA_W:D_MODEL, :]))
    x1 = _layer_norm_rows(ALPHA * x_ref[...] + g1_ref[...] * y, lng_ref[...], lnb_ref[...])
    x1_ref[...] = x1
    h2 = x1 * (1.0 + sc2_ref[...]) + sh2_ref[...]
    h2_ref[...] = _pack_halves(h2)

    tm = h2.shape[0]
    ne = N_EXPERTS
    gs = ne // N_GROUPS
    scores = _sigmoid(_dot_nt(rwt_ref[...], h2, HI))
    sel = scores + rb_ref[...]
    sub = _iota((gs, tm), 0).astype(F32)
    gsc = []
    for j in range(N_GROUPS):
        gj = sel[j * gs:(j + 1) * gs, :]
        m1 = jnp.max(gj, axis=0, keepdims=True)
        f1 = jnp.min(jnp.where(gj == m1, sub, float(gs)), axis=0, keepdims=True)
        m2 = jnp.max(jnp.where(sub == f1, -jnp.inf, gj), axis=0, keepdims=True)
        gsc.append(m1 + m2)
    gsc = jnp.concatenate(gsc, axis=0)
    gid = _iota((N_GROUPS, tm), 0).astype(F32)
    gmask = jnp.zeros((N_GROUPS, tm), F32)
    for _ in range(TOPK_GROUPS):
        mx = jnp.max(gsc, axis=0, keepdims=True)
        fi = jnp.min(jnp.where(gsc == mx, gid, float(N_GROUPS)), axis=0, keepdims=True)
        pick = gid == fi
        gmask = jnp.where(pick, 1.0, gmask)
        gsc = jnp.where(pick, -jnp.inf, gsc)
    selm = jnp.concatenate(
        [jnp.where(gmask[j:j + 1, :] > 0.5, sel[j * gs:(j + 1) * gs, :], NEG) for j in range(N_GROUPS)], axis=0)
    eid = _iota((ne, tm), 0).astype(F32)
    gsel, eids = [], []
    chosen_f = jnp.zeros((ne, tm), F32)
    for _ in range(TOP_K):
        mx = jnp.max(selm, axis=0, keepdims=True)
        fi = jnp.min(jnp.where(selm == mx, eid, float(ne)), axis=0, keepdims=True)
        pick = eid == fi
        eids.append(fi)
        gsel.append(jnp.sum(jnp.where(pick, scores, 0.0), axis=0, keepdims=True))
        chosen_f = jnp.where(pick, 1.0, chosen_f)
        selm = jnp.where(pick, -jnp.inf, selm)
    gsum = gsel[0]
    for kx in range(1, TOP_K):
        gsum = gsum + gsel[kx]
    before = _dot(chosen_f.astype(BF16), tri_ref[...]) + carry_ref[:, 0:1]
    ranks = [jnp.sum(jnp.where(eid == eids[kx], before, 0.0), axis=0, keepdims=True) for kx in range(TOP_K)]
    eidx_ref[...] = jnp.concatenate(eids, axis=0).astype(I32)
    rank_ref[...] = jnp.concatenate(ranks, axis=0).astype(I32)
    gate_ref[...] = jnp.concatenate(gsel, axis=0) / gsum * ROUTED_SCALE
    carry_ref[...] = carry_ref[...] + jnp.sum(chosen_f, axis=1, keepdims=True)
    cnt_ref[...] = carry_ref[...]


def _post_mix(x, o_rw, o_ds, o_sw, w_out, g1, ln_g, ln_b, sc2, sh2, router_wt, router_b, tm=512):
    t, d = x.shape
    tri = (np.arange(tm)[:, None] < np.arange(tm)[None, :]).astype(np.float32)
    tri = jnp.asarray(tri, BF16)
    row = lambda i: (i, 0)
    const = lambda i: (0, 0)
    col = lambda i: (0, i)
    vec = pl.BlockSpec((1, d), const)
    return pl.pallas_call(
        _post_mix_kernel,
        grid=(t // tm,),
        in_specs=[
            pl.BlockSpec((tm, d), row),
            pl.BlockSpec((tm, RWKV_W), row),
            pl.BlockSpec((tm, DSA_W), row),
            pl.BlockSpec((tm, SWA_W), row),
            pl.BlockSpec((d, d), const),
            vec, vec, vec, vec, vec,
            pl.BlockSpec((N_EXPERTS, d), const),
            pl.BlockSpec((N_EXPERTS, 1), const),
            pl.BlockSpec((tm, tm), const),
        ],
        out_specs=[
            pl.BlockSpec((tm, d), row),
            pl.BlockSpec((tm, d // 2), row),
            pl.BlockSpec((TOP_K, tm), col),
            pl.BlockSpec((TOP_K, tm), col),
            pl.BlockSpec((TOP_K, tm), col),
            pl.BlockSpec((N_EXPERTS, LANES), const),
        ],
        out_shape=[
            jax.ShapeDtypeStruct((t, d), F32),
            jax.ShapeDtypeStruct((t, d // 2), I32),
            jax.ShapeDtypeStruct((TOP_K, t), I32),
            jax.ShapeDtypeStruct((TOP_K, t), I32),
            jax.ShapeDtypeStruct((TOP_K, t), F32),
            jax.ShapeDtypeStruct((N_EXPERTS, LANES), F32),
        ],
        scratch_shapes=[pltpu.VMEM((N_EXPERTS, LANES), F32)],
        compiler_params=_params(("arbitrary",)),
        name="post_mix_router",
    )(x, o_rw, o_ds, o_sw, w_out, g1, ln_g, ln_b, sc2, sh2, router_wt, router_b, tri)


MOE_ROWS = 512
MOE_TILE = 256


def _pack_halves(v):
    w = v.shape[1] // 2
    bits = lax.bitcast_convert_type(v.astype(BF16).astype(F32), I32)
    return bits[:, :w] | lax.shift_right_logical(bits[:, w:], 16)


def _unpack_halves(p):
    return lax.bitcast_convert_type(p & -65536, F32), lax.bitcast_convert_type(p << 16, F32)


def _row_copy(src_ref, src_row, dst_ref, dst_row, sem):
    return pltpu.make_async_copy(src_ref.at[pl.ds(src_row, 1), :], dst_ref.at[pl.ds(dst_row, 1), :], sem)


def _dispatch_kernel(slot_hbm, h_ref, xs_in, xs_out, slot_smem, sem_tab, sem_rows):
    del xs_in
    i = pl.program_id(0)
    tab = pltpu.make_async_copy(slot_hbm.at[i], slot_smem, sem_tab)
    tab.start()
    tab.wait()

    def issue(tt, carry):
        for kx in range(TOP_K):
            _row_copy(h_ref, tt, xs_out, slot_smem[kx, tt], sem_rows).start(priority=kx % 2)
        return carry

    lax.fori_loop(0, MOE_TILE, issue, 0)

    def drain(tt, carry):
        for kx in range(TOP_K):
            _row_copy(h_ref, 0, xs_out, 0, sem_rows).wait()
        return carry

    lax.fori_loop(0, MOE_TILE, drain, 0)


def _dispatch(slot_tiles, rows, cap, recycled=None):
    t, d = rows.shape
    xs0 = jnp.zeros((cap, d), rows.dtype) if recycled is None else recycled
    return pl.pallas_call(
        _dispatch_kernel,
        grid=(t // MOE_TILE,),
        in_specs=[
            pl.BlockSpec(memory_space=pl.ANY),
            pl.BlockSpec((MOE_TILE, d), lambda i: (i, 0)),
            pl.BlockSpec(memory_space=pl.ANY),
        ],
        out_specs=pl.BlockSpec(memory_space=pl.ANY),
        out_shape=jax.ShapeDtypeStruct((cap, d), rows.dtype),
        scratch_shapes=[
            pltpu.SMEM((TOP_K, MOE_TILE), I32),
            pltpu.SemaphoreType.DMA,
            pltpu.SemaphoreType.DMA,
        ],
        input_output_aliases={2: 0},
        compiler_params=_params(("arbitrary",)),
        name="moe_dispatch",
    )(slot_tiles, rows, xs0)


def _expert_kernel(be_ref, nb_ref, xs_ref, w1_ref, w3_ref, w2_ref, ys_ref, w1b, w3b, w2b):
    b = pl.program_id(0)
    changed = (b == 0) | (be_ref[b] != be_ref[jnp.maximum(b - 1, 0)])

    @pl.when(changed & (b < nb_ref[0]))
    def _():
        w1b[...] = w1_ref[0, 0].astype(BF16)
        w3b[...] = w3_ref[0, 0].astype(BF16)
        w2b[...] = w2_ref[0, 0].astype(BF16)

    @pl.when(b < nb_ref[0])
    def _():
        x_hi, x_lo = _unpack_halves(xs_ref[...])
        x_hi, x_lo = x_hi.astype(BF16), x_lo.astype(BF16)
        half = x_hi.shape[1]
        a = _dot(x_hi, w1b[0:half, :]) + _dot(x_lo, w1b[half:2 * half, :])
        gte = _dot(x_hi, w3b[0:half, :]) + _dot(x_lo, w3b[half:2 * half, :])
        hmid = (a * _sigmoid(a) * gte).astype(BF16)
        ys_ref[...] = _pack_halves(_dot(hmid, w2b[...]))

    @pl.when(b >= nb_ref[0])
    def _():
        ys_ref[...] = jnp.zeros_like(ys_ref)


def _experts(block_e, n_used, xs, w1, w3, w2, layer):
    cap, dp = xs.shape
    d = 2 * dp
    nb = cap // MOE_ROWS
    grid_spec = pltpu.PrefetchScalarGridSpec(
        num_scalar_prefetch=2,
        grid=(nb,),
        in_specs=[
            pl.BlockSpec((MOE_ROWS, dp), lambda b, be, nu: (b, 0)),
            pl.BlockSpec((1, 1, d, D_EXPERT), lambda b, be, nu: (layer, be[b], 0, 0)),
            pl.BlockSpec((1, 1, d, D_EXPERT), lambda b, be, nu: (layer, be[b], 0, 0)),
            pl.BlockSpec((1, 1, D_EXPERT, d), lambda b, be, nu: (layer, be[b], 0, 0)),
        ],
        out_specs=pl.BlockSpec((MOE_ROWS, dp), lambda b, be, nu: (b, 0)),
        scratch_shapes=[
            pltpu.VMEM((d, D_EXPERT), BF16),
            pltpu.VMEM((d, D_EXPERT), BF16),
            pltpu.VMEM((D_EXPERT, d), BF16),
        ],
    )
    return pl.pallas_call(
        _expert_kernel,
        grid_spec=grid_spec,
        out_shape=jax.ShapeDtypeStruct((cap, dp), I32),
        compiler_params=_params(("arbitrary",)),
        name="moe_experts",
    )(block_e, n_used, xs, w1, w3, w2)


def _combine_kernel(slot_hbm, ys_hbm, x1_ref, h2_ref, gate_ref, sw1_ref, sw3_ref, sw2_ref,
                    g2_ref, lng_ref, lnb_ref, o_ref, slot_smem, gbuf, sem_tab, sem_rows):
    i = pl.program_id(0)
    tab = pltpu.make_async_copy(slot_hbm.at[i], slot_smem, sem_tab)
    tab.start()
    tab.wait()

    def issue(tt, carry):
        for kx in range(TOP_K):
            _row_copy(ys_hbm, slot_smem[kx, tt], gbuf.at[kx], tt, sem_rows).start(priority=kx % 2)
        return carry

    lax.fori_loop(0, MOE_TILE, issue, 0)

    h_hi, h_lo = _unpack_halves(h2_ref[...])
    h_hi, h_lo = h_hi.astype(BF16), h_lo.astype(BF16)
    half = h_hi.shape[1]
    a = _dot(h_hi, sw1_ref[0:half, :]) + _dot(h_lo, sw1_ref[half:2 * half, :])
    gte = _dot(h_hi, sw3_ref[0:half, :]) + _dot(h_lo, sw3_ref[half:2 * half, :])
    y = _dot((a * _sigmoid(a) * gte).astype(BF16), sw2_ref[...])

    def drain(tt, carry):
        for kx in range(TOP_K):
            _row_copy(ys_hbm, 0, gbuf.at[kx], 0, sem_rows).wait()
        return carry

    lax.fori_loop(0, MOE_TILE, drain, 0)

    gates = gate_ref[...]
    r_hi = jnp.zeros((MOE_TILE, half), F32)
    r_lo = jnp.zeros((MOE_TILE, half), F32)
    for kx in range(TOP_K):
        e_hi, e_lo = _unpack_halves(gbuf[kx])
        r_hi = r_hi + gates[:, kx:kx + 1] * e_hi
        r_lo = r_lo + gates[:, kx:kx + 1] * e_lo
    y = y + jnp.concatenate([r_hi, r_lo], axis=1)
    o_ref[...] = _layer_norm_rows(ALPHA * x1_ref[...] + g2_ref[...] * y, lng_ref[...], lnb_ref[...])


def _combine(slot_tiles, ys, x1, h2, gates_t, sw1, sw3, sw2, g2, ln_g, ln_b):
    t, d = x1.shape
    row = lambda i: (i, 0)
    const = lambda i: (0, 0)
    vec = pl.BlockSpec((1, d), const)
    return pl.pallas_call(
        _combine_kernel,
        grid=(t // MOE_TILE,),
        in_specs=[
            pl.BlockSpec(memory_space=pl.ANY),
            pl.BlockSpec(memory_space=pl.ANY),
            pl.BlockSpec((MOE_TILE, d), row),
            pl.BlockSpec((MOE_TILE, d // 2), row),
            pl.BlockSpec((MOE_TILE, TOP_K), row),
            pl.BlockSpec((d, D_EXPERT), const),
            pl.BlockSpec((d, D_EXPERT), const),
            pl.BlockSpec((D_EXPERT, d), const),
            vec, vec, vec,
        ],
        out_specs=pl.BlockSpec((MOE_TILE, d), row),
        out_shape=jax.ShapeDtypeStruct((t, d), F32),
        scratch_shapes=[
            pltpu.SMEM((TOP_K, MOE_TILE), I32),
            pltpu.VMEM((TOP_K, MOE_TILE, d // 2), I32),
            pltpu.SemaphoreType.DMA,
            pltpu.SemaphoreType.DMA,
        ],
        compiler_params=_params(("arbitrary",)),
        name="moe_combine",
    )(slot_tiles, ys, x1, h2, gates_t, sw1, sw3, sw2, g2, ln_g, ln_b)


def _pad_w_in(w_in_l):
    d = w_in_l.shape[0]
    pad = jnp.zeros((d, C_SQ[0] - N_ORIG_BEFORE_PAD), w_in_l.dtype)
    return jnp.concatenate([w_in_l[:, :N_ORIG_BEFORE_PAD], pad, w_in_l[:, N_ORIG_BEFORE_PAD:]], axis=1)


def _pad_lanes(v, width=LANES):
    v = v.reshape(1, -1)
    return jnp.pad(v, ((0, 0), (0, width - v.shape[1])))


def _moe_tables(eidx, rank, counts):
    t = eidx.shape[1]
    cnt = counts[:, 0].astype(I32)
    padded = (cnt + MOE_ROWS - 1) // MOE_ROWS * MOE_ROWS
    pad_end = jnp.cumsum(padded)
    pad_start = pad_end - padded
    e_ids = jnp.arange(N_EXPERTS, dtype=I32)
    start_of = jnp.sum(jnp.where(eidx[..., None] == e_ids, pad_start, 0), axis=-1)
    slot = start_of + rank
    slot_tiles = slot.reshape(TOP_K, t // MOE_TILE, MOE_TILE).transpose(1, 0, 2)
    cap = t * TOP_K + N_EXPERTS * MOE_ROWS
    nb = cap // MOE_ROWS
    blk_row = jnp.arange(nb, dtype=I32)[:, None] * MOE_ROWS
    block_e = jnp.minimum(jnp.sum((pad_end[None, :] <= blk_row).astype(I32), axis=1), N_EXPERTS - 1)
    n_used = (pad_end[-1] // MOE_ROWS).astype(I32).reshape(1)
    return slot_tiles, block_e, n_used, cap


def kernel(x, c, w_mod, b_mod, w_in, rwkv_mu, rwkv_w0, rwkv_w2, rwkv_a0, rwkv_a2, rwkv_g2, rwkv_k_k, rwkv_k_a, rwkv_r_k, rwkv_ln_g, rwkv_ln_b, dsa_kv_norm, dsa_w_uk, dsa_w_uv, dsa_ik_g, dsa_ik_b, swa_sinks, w_out, ln_mix_g, ln_mix_b, router_w, router_bias, exp_w1, exp_w3, exp_w2, sh_w1, sh_w3, sh_w2, ln_ffn_g, ln_ffn_b):
    bsz, t, d = x.shape
    assert bsz == 1 and d == D_MODEL
    depth = w_mod.shape[0]
    n_sl = SWA_HEADS + DSA_HEADS
    slopes = [2.0 ** (-8.0 * (j + 1.0) / n_sl) for j in range(n_sl)]
    swa_slopes, dsa_slopes = slopes[:SWA_HEADS], slopes[SWA_HEADS:]

    mod = _modulation(c, w_mod, b_mod)
    xs_cur = x[0]
    xs_sorted = None
    row1 = lambda v: v.reshape(1, -1)
    for l in range(depth):
        sh1, sc1, g1, sh2, sc2, g2 = [mod[l, :, j * d:(j + 1) * d] for j in range(6)]
        wp = _pad_w_in(w_in[l])
        w_hi = wp.astype(BF16)
        w_idx = wp[:, C_IDX[0]:C_IDX[1]]
        w_idx_lo = (w_idx - w_idx.astype(BF16).astype(F32)).astype(BF16)
        rkv, lora, dq, ckv, iq, ikw, sq, skv = _input_proj(
            xs_cur, sc1, sh1, w_hi, w_idx_lo, row1(dsa_kv_norm[l]),
            _pad_lanes(dsa_ik_g[l]), _pad_lanes(dsa_ik_b[l]))
        o_rw = _rwkv_mix(rkv, lora, row1(rwkv_mu[l]), row1(rwkv_w0[l]), rwkv_w2[l], row1(rwkv_a0[l]),
                         rwkv_a2[l], rwkv_g2[l], row1(rwkv_k_k[l]), row1(rwkv_k_a[l]), row1(rwkv_r_k[l]),
                         row1(rwkv_ln_g[l]), row1(rwkv_ln_b[l]))
        o_ds = _dsa_mix(dq, iq, ikw, ckv, dsa_w_uk[l], dsa_w_uv[l], dsa_slopes)
        o_sw = _swa_mix(sq, skv, _pad_lanes(swa_sinks[l]), swa_slopes)
        x1, h2, eidx, rank, gates, counts = _post_mix(
            xs_cur, o_rw, o_ds, o_sw, w_out[l].astype(BF16), g1, row1(ln_mix_g[l]), row1(ln_mix_b[l]),
            sc2, sh2, router_w[l].T, router_bias[l].reshape(-1, 1))
        slot_tiles, block_e, n_used, cap = _moe_tables(eidx, rank, counts)
        xs_sorted = _dispatch(slot_tiles, h2, cap, xs_sorted)
        ys = _experts(block_e, n_used, xs_sorted, exp_w1, exp_w3, exp_w2, l)
        xs_cur = _combine(slot_tiles, ys, x1, h2, gates.T, sh_w1[l].astype(BF16), sh_w3[l].astype(BF16),
                          sh_w2[l].astype(BF16), g2, row1(ln_ffn_g[l]), row1(ln_ffn_b[l]))
    return xs_cur[None]
```

```python
import functools
import math

import jax
import jax.numpy as jnp
import numpy as np
from jax import lax
from jax.experimental import pallas as pl
from jax.experimental.pallas import tpu as pltpu

F32 = jnp.float32
BF16 = jnp.bfloat16
I32 = jnp.int32
HI = lax.Precision.HIGHEST

D_MODEL = 1024
DEPTH = 4
HEAD_DIM = 64
RWKV_HEADS = 6
DSA_HEADS = 4
SWA_HEADS = 6
SWA_KV_HEADS = 2
RWKV_W = RWKV_HEADS * HEAD_DIM
DSA_W = DSA_HEADS * HEAD_DIM
SWA_W = SWA_HEADS * HEAD_DIM
DECAY_LORA = 64
AAA_LORA = 64
GATE_LORA = 128
GN_EPS = 64e-5
KV_LORA = 128
IDX_HEADS = 4
IDX_DIM = 64
TOPK_MAX = 256
WINDOW = 128
N_EXPERTS = 64
TOP_K = 8
N_GROUPS = 8
TOPK_GROUPS = 4
D_EXPERT = 256
ROUTED_SCALE = 2.5
ALPHA = (2 * DEPTH) ** 0.25
LN_EPS = 1e-5
NEG = -1e30
INT_MIN = -(2 ** 31)

LANES = 128
VMEM_LIMIT = 56 * 1024 * 1024

C_RKV = (0, 1152)
C_LORA = (1152, 1408)
C_DQ = (1408, 1664)
C_CKV = (1664, 1792)
C_IDX = (1792, 2176)
C_SQ = (2176, 2560)
C_SKV = (2560, 2816)
P_PAD = 2816
N_ORIG_BEFORE_PAD = 2116


def _dot(a, b, prec=None):
    return jnp.dot(a, b, preferred_element_type=F32, precision=prec)


def _dot_nt(a, b, prec=None):
    return lax.dot_general(a, b, (((1,), (1,)), ((), ())), preferred_element_type=F32, precision=prec)


def _split2(a):
    a_hi = a.astype(BF16)
    return a_hi, (a - a_hi.astype(F32)).astype(BF16)


def _bdot(a, b):
    return _dot(a.astype(BF16), b.astype(BF16))


def _bdot_nt(a, b):
    return _dot_nt(a.astype(BF16), b.astype(BF16))


def _dot2(a, b_exact):
    a_hi, a_lo = _split2(a)
    return _dot(a_hi, b_exact) + _dot(a_lo, b_exact)


def _dot2_l(a_exact, b):
    b_hi, b_lo = _split2(b)
    return _dot(a_exact, b_hi) + _dot(a_exact, b_lo)


def _dot3(a, b):
    a_hi, a_lo = _split2(a)
    b_hi, b_lo = _split2(b)
    return _dot(a_hi, b_hi) + (_dot(a_lo, b_hi) + _dot(a_hi, b_lo))


def _iota(shape, dim):
    return lax.broadcasted_iota(I32, shape, dim)


def _sigmoid(x):
    return 1.0 / (1.0 + jnp.exp(-x))


def _layer_norm_rows(v, g, b):
    mu = jnp.mean(v, axis=-1, keepdims=True)
    d = v - mu
    var = jnp.mean(d * d, axis=-1, keepdims=True)
    return d * lax.rsqrt(var + LN_EPS) * g + b


def _params(sem):
    return pltpu.CompilerParams(dimension_semantics=sem, vmem_limit_bytes=VMEM_LIMIT)


def _mod_kernel(c_ref, w_ref, b_ref, o_ref):
    c = c_ref[...]
    cond = c * _sigmoid(c)
    o_ref[0] = _dot(cond, w_ref[0], HI) + b_ref[0]


def _modulation(c, w_mod, b_mod):
    depth, d, d6 = w_mod.shape
    c8 = jnp.broadcast_to(c, (8, d))
    nj = d6 // d
    out = pl.pallas_call(
        _mod_kernel,
        grid=(depth, nj),
        in_specs=[
            pl.BlockSpec((8, d), lambda l, j: (0, 0)),
            pl.BlockSpec((1, d, d), lambda l, j: (l, 0, j)),
            pl.BlockSpec((1, 1, d), lambda l, j: (l, 0, j)),
        ],
        out_specs=pl.BlockSpec((1, 8, d), lambda l, j: (l, 0, j)),
        out_shape=jax.ShapeDtypeStruct((depth, 8, d6), F32),
        compiler_params=_params(("arbitrary", "arbitrary")),
        name="modulation",
    )(c8, w_mod, b_mod.reshape(depth, 1, d6))
    return out[:, 0:1, :]


def _proj_kernel(x_ref, sc_ref, sh_ref, w_ref, wlo_ref, kvn_ref, ikg_ref, ikb_ref,
                 rkv_ref, lora_ref, dq_ref, ckv_ref, iq_ref, ikw_ref, sq_ref, skv_ref):
    h = x_ref[...] * (1.0 + sc_ref[...]) + sh_ref[...]
    hb = h.astype(BF16)
    hl = (h - hb.astype(F32)).astype(BF16)

    def mm(c):
        return _dot(hb, w_ref[:, c[0]:c[1]])

    rkv_ref[...] = mm(C_RKV)
    lora_ref[...] = mm(C_LORA)
    dq_ref[...] = mm(C_DQ)
    sq_ref[...] = mm(C_SQ)
    skv_ref[...] = mm(C_SKV)
    ckv = mm(C_CKV)
    ckv_ref[...] = ckv * lax.rsqrt(jnp.mean(ckv * ckv, axis=-1, keepdims=True) + 1e-6) * kvn_ref[...]
    idx = mm(C_IDX) + _dot(hl, w_ref[:, C_IDX[0]:C_IDX[1]]) + _dot(hb, wlo_ref[...])
    iq_ref[...] = idx[:, 0:256]
    g3 = idx[:, 256:384]
    lane = _iota(g3.shape, 1)
    isk = lane < IDX_DIM
    mu = jnp.sum(jnp.where(isk, g3, 0.0), axis=-1, keepdims=True) * (1.0 / IDX_DIM)
    dk = jnp.where(isk, g3 - mu, 0.0)
    var = jnp.sum(dk * dk, axis=-1, keepdims=True) * (1.0 / IDX_DIM)
    ikn = dk * lax.rsqrt(var + LN_EPS) * ikg_ref[...] + ikb_ref[...]
    ikw_ref[...] = jnp.where(isk, ikn, g3 * (IDX_HEADS ** -0.5 * IDX_DIM ** -0.5))


def _input_proj(x, sc, sh, w_hi, w_idx_lo, kvn, ikg, ikb, tm=512):
    t, d = x.shape
    widths = [C_RKV, C_LORA, C_DQ, C_CKV, (0, 256), (0, 128), C_SQ, C_SKV]
    widths = [c[1] - c[0] for c in widths]
    const = lambda i: (0, 0)
    row = lambda i: (i, 0)
    return pl.pallas_call(
        _proj_kernel,
        grid=(t // tm,),
        in_specs=[
            pl.BlockSpec((tm, d), row),
            pl.BlockSpec((1, d), const),
            pl.BlockSpec((1, d), const),
            pl.BlockSpec((d, P_PAD), const),
            pl.BlockSpec((d, C_IDX[1] - C_IDX[0]), const),
            pl.BlockSpec((1, KV_LORA), const),
            pl.BlockSpec((1, LANES), const),
            pl.BlockSpec((1, LANES), const),
        ],
        out_specs=[pl.BlockSpec((tm, w), row) for w in widths],
        out_shape=[jax.ShapeDtypeStruct((t, w), F32) for w in widths],
        compiler_params=_params(("arbitrary",)),
        name="input_proj",
    )(x, sc, sh, w_hi, w_idx_lo, kvn, ikg, ikb)


RW_CHUNK = 64
RW_UNROLL = 8


def _rwkv_kernel(r_ref, k_ref, v_ref, lora_ref, rp_ref, kp_ref, vp_ref, lp_ref,
                 mur_ref, muk_ref, muv_ref, mul_ref, w0_ref, w2_ref, a0_ref, a2_ref, g2_ref,
                 kk_ref, ka_ref, rk_ref, lng_ref, lnb_ref, o_ref,
                 h_ref, y_ref, st_ref, wm_ref, ar_ref, rs_ref, vs_ref, lt_ref, zm_ref, y0_ref, gc_ref, *, tg):
    g = pl.program_id(0)
    c64 = RW_CHUNK
    nch = tg // c64
    npair = RWKV_W // LANES
    pair_lanes = [slice(p * LANES, (p + 1) * LANES) for p in range(npair)]
    lane = _iota((1, LANES), 1)
    first = g == 0

    @pl.when(first)
    def _():
        h_ref[...] = jnp.zeros_like(h_ref)

    rowid = _iota((tg, 1), 0)

    def shift_mix(cur_ref, prev_ref, mu_ref):
        cur = cur_ref[...]
        prev_row = jnp.where(first, 0.0, prev_ref[7:8, :])
        rolled = pltpu.roll(cur, 1, 0)
        shifted = jnp.where(rowid == 0, prev_row, rolled)
        return cur + (shifted - cur) * mu_ref[...]

    r = shift_mix(r_ref, rp_ref, mur_ref)
    k = shift_mix(k_ref, kp_ref, muk_ref)
    v = shift_mix(v_ref, vp_ref, muv_ref)
    lo = shift_mix(lora_ref, lp_ref, mul_ref)
    wl = lo[:, 0:DECAY_LORA]
    al = lo[:, DECAY_LORA:DECAY_LORA + AAA_LORA]
    gl = lo[:, 128:256]

    zw = -(w0_ref[...] + _dot3(jnp.tanh(wl), w2_ref[...]))
    softplus = jnp.maximum(zw, 0.0) + jnp.log(1.0 + jnp.exp(-jnp.abs(zw)))
    lw = -jnp.exp(-softplus - 0.5)
    a = _sigmoid(a0_ref[...] + _bdot(al, a2_ref[...]))
    gate = _bdot(_sigmoid(gl), g2_ref[...])

    ri = _iota((LANES, LANES), 0) // HEAD_DIM
    ci = _iota((LANES, LANES), 1) // HEAD_DIM
    bones = jnp.where(ri == ci, 1.0, 0.0).astype(BF16)

    def head_sum(xf):
        return jnp.concatenate([_dot2(xf[:, pl_], bones) for pl_ in pair_lanes], axis=1)

    kk = k * kk_ref[...]
    kk = kk / jnp.maximum(jnp.sqrt(head_sum(kk * kk)), 1e-12)
    k2 = k * (1.0 + (a - 1.0) * ka_ref[...])
    bonus = head_sum(r * k2 * rk_ref[...]) * v
    bvec = a * kk

    st_ref[0] = r
    st_ref[1] = k2
    st_ref[2] = v
    st_ref[3] = lw
    st_ref[4] = kk
    st_ref[5] = bvec

    rr = _iota((LANES, LANES), 0)
    cc = _iota((LANES, LANES), 1)
    same = (rr // c64) == (cc // c64)
    strict = same & ((rr % c64) > (cc % c64))
    incl = same & ((rr % c64) >= (cc % c64))
    eye = jnp.where(rr == cc, 1.0, 0.0)
    tril = jnp.where(_iota((c64, c64), 0) >= _iota((c64, c64), 1), 1.0, 0.0).astype(BF16)
    lo_half = lane < HEAD_DIM

    def stack(xc):
        return jnp.concatenate([jnp.where(lo_half, xc, 0.0), jnp.where(lo_half, 0.0, xc)], axis=0)

    def prepare(c, carry):
        chunks = [c * RW_UNROLL + j for j in range(RW_UNROLL)]
        sls = [pl.ds(pl.multiple_of(cj * c64, c64), c64) for cj in chunks]
        items = [(p, j) for j in range(RW_UNROLL) for p in range(npair)]
        pairs = range(len(items))
        idx = [p * nch + chunks[j] for p, j in items]
        ld = lambda q: [st_ref[q, sls[j], pair_lanes[p]] for p, j in items]
        rc, kc, vc, lwc, kkc, bc = ld(0), ld(1), ld(2), ld(3), ld(4), ld(5)
        cum = [_dot2_l(tril, lwc[p]) for p in pairs]
        tot = [cum[p][c64 - 1:c64, :] for p in pairs]
        g_in = [jnp.exp(cum[p]) for p in pairs]
        g_ex = [jnp.exp(cum[p] - lwc[p]) for p in pairs]
        g_inv = [jnp.exp(-cum[p]) for p in pairs]
        g_rest = [jnp.exp(tot[p] - cum[p]) for p in pairs]
        a_s = [stack(-kkc[p] * g_ex[p]).astype(BF16) for p in pairs]
        b_s = [stack(bc[p] * g_inv[p]).astype(BF16) for p in pairs]
        k_s = [stack(kc[p] * g_inv[p]).astype(BF16) for p in pairs]
        r_s = [stack(rc[p] * g_in[p]).astype(BF16) for p in pairs]
        v_s = [stack(vc[p]).astype(BF16) for p in pairs]
        nmat = [jnp.where(strict, _dot_nt(a_s[p], b_s[p]), 0.0) for p in pairs]
        aak = [jnp.where(strict, _dot_nt(a_s[p], k_s[p]), 0.0) for p in pairs]
        arb = [jnp.where(incl, _dot_nt(r_s[p], b_s[p]), 0.0) for p in pairs]
        ark = [jnp.where(incl, _dot_nt(r_s[p], k_s[p]), 0.0) for p in pairs]
        tinv = [eye + nmat[p] for p in pairs]
        pw = nmat
        for _ in range(5):
            pw = [_bdot(pw[p], pw[p]) for p in pairs]
            tinv = [_bdot(tinv[p], eye + pw[p]) for p in pairs]
        tinv = [tinv[p].astype(BF16) for p in pairs]
        akv = [_bdot(aak[p], v_s[p]).astype(BF16) for p in pairs]
        wmat = [_dot(tinv[p], a_s[p]) for p in pairs]
        zmat = [_dot(tinv[p], akv[p]) for p in pairs]
        y0 = [_bdot(ark[p], v_s[p]) for p in pairs]
        for p in pairs:
            wm_ref[idx[p]] = wmat[p].astype(BF16)
            zm_ref[idx[p]] = zmat[p]
            y0_ref[idx[p]] = y0[p]
            ar_ref[idx[p]] = arb[p].astype(BF16)
            rs_ref[idx[p]] = r_s[p]
            vs_ref[idx[p]] = v_s[p]
            lt_ref[idx[p]] = jnp.concatenate([stack(bc[p] * g_rest[p]), stack(kc[p] * g_rest[p])],
                                             axis=0).T.astype(BF16)
            gc_ref[idx[p]] = jnp.broadcast_to(jnp.sum(eye * jnp.exp(tot[p]), axis=1, keepdims=True),
                                              (LANES, LANES))
        return carry

    lax.fori_loop(0, nch // RW_UNROLL, prepare, 0)

    def advance(c, carry):
        sl = pl.ds(pl.multiple_of(c * c64, c64), c64)
        pairs = range(npair)
        idx = [p * nch + c for p in pairs]
        hst = [h_ref[p] for p in pairs]
        hb = [hst[p].astype(BF16) for p in pairs]
        u = [_dot(wm_ref[idx[p]], hb[p]) + zm_ref[idx[p]] for p in pairs]
        rh = [_dot(rs_ref[idx[p]], hb[p]) for p in pairs]
        ub = [u[p].astype(BF16) for p in pairs]
        hnew = [_dot(lt_ref[idx[p]], jnp.concatenate([ub[p], vs_ref[idx[p]]], axis=0)) for p in pairs]
        au = [_dot(ar_ref[idx[p]], ub[p]) for p in pairs]
        for p in pairs:
            h_ref[p] = gc_ref[idx[p]] * hst[p] + hnew[p]
            ys = rh[p] + au[p] + y0_ref[idx[p]]
            y_ref[sl, pair_lanes[p]] = ys[0:c64, :] + ys[c64:2 * c64, :]
        return carry

    lax.fori_loop(0, nch, advance, 0)

    y = y_ref[...]
    mean = head_sum(y) * (1.0 / HEAD_DIM)
    dy = y - mean
    var = head_sum(dy * dy) * (1.0 / HEAD_DIM)
    o = dy * lax.rsqrt(var + GN_EPS) * lng_ref[...] + lnb_ref[...]
    o_ref[...] = (o + bonus) * gate


def _rwkv_mix(rkv, lora, mu, w0, w2, a0, a2, g2, k_k, k_a, r_k, ln_g, ln_b, tg=512):
    t = rkv.shape[0]
    w = RWKV_W
    npair = w // LANES
    nmat = npair * (tg // RW_CHUNK)
    mu_r, mu_k, mu_v, mu_l = mu[:, 0:w], mu[:, w:2 * w], mu[:, 2 * w:3 * w], mu[:, 3 * w:3 * w + 256]
    blk = lambda off: pl.BlockSpec((tg, w), lambda g: (g, off))
    prev = lambda off: pl.BlockSpec((8, w), lambda g: (jnp.maximum(g * (tg // 8) - 1, 0), off))
    vec = pl.BlockSpec((1, w), lambda g: (0, 0))
    full = lambda rows: pl.BlockSpec((rows, w), lambda g: (0, 0))
    return pl.pallas_call(
        functools.partial(_rwkv_kernel, tg=tg),
        grid=(t // tg,),
        in_specs=[
            blk(0), blk(1), blk(2),
            pl.BlockSpec((tg, 256), lambda g: (g, 0)),
            prev(0), prev(1), prev(2),
            pl.BlockSpec((8, 256), lambda g: (jnp.maximum(g * (tg // 8) - 1, 0), 0)),
            vec, vec, vec,
            pl.BlockSpec((1, 256), lambda g: (0, 0)),
            vec, full(DECAY_LORA), vec, full(AAA_LORA), full(GATE_LORA),
            vec, vec, vec, vec, vec,
        ],
        out_specs=pl.BlockSpec((tg, w), lambda g: (g, 0)),
        out_shape=jax.ShapeDtypeStruct((t, w), F32),
        scratch_shapes=[
            pltpu.VMEM((npair, LANES, LANES), F32),
            pltpu.VMEM((tg, w), F32),
            pltpu.VMEM((6, tg, w), F32),
            pltpu.VMEM((nmat, LANES, LANES), BF16),
            pltpu.VMEM((nmat, LANES, LANES), BF16),
            pltpu.VMEM((nmat, LANES, LANES), BF16),
            pltpu.VMEM((nmat, LANES, LANES), BF16),
            pltpu.VMEM((nmat, LANES, 2 * LANES), BF16),
            pltpu.VMEM((nmat, LANES, LANES), F32),
            pltpu.VMEM((nmat, LANES, LANES), F32),
            pltpu.VMEM((nmat, LANES, LANES), F32),
        ],
        compiler_params=_params(("arbitrary",)),
        name="rwkv7_mix",
    )(rkv, rkv, rkv, lora, rkv, rkv, rkv, lora,
      mu_r, mu_k, mu_v, mu_l, w0, w2, a0, a2, g2, k_k, k_a, r_k, ln_g, ln_b)


DSA_QB = 256
DSA_KC = 1024
DSA_SUB = 512
CNT_ROWS = 64
TIE_BLK = 128
BIS_STEPS = 2


def _float_key(v):
    bits = lax.bitcast_convert_type(v, I32)
    return bits ^ ((bits >> 31) & 0x7FFFFFFF)


def _dsa_kernel(dq_ref, iq_ref, ikw_ref, ikx_ref, kf_ref, vft_ref, wuk_ref, wuv_ref, tril_ref, slc_ref,
                o_ref, sc_ref, acc_ref):
    i = pl.program_id(0)
    qb, kc, sc_rows = DSA_QB, DSA_KC, DSA_SUB
    nh = DSA_HEADS
    t0 = i * qb
    nch = (t0 + qb + kc - 1) // kc
    tq = t0 + _iota((1, qb), 1)

    iq = iq_ref[...]
    iq_hi = iq.astype(BF16).astype(F32)
    iq_lo = iq - iq_hi
    lhs = []
    for h in range(IDX_HEADS):
        s = slice(h * IDX_DIM, (h + 1) * IDX_DIM)
        lhs.append(jnp.concatenate([iq_hi[:, s], iq_hi[:, s], iq_lo[:, s], iq_lo[:, s]], axis=1))
    lhs_t = jnp.concatenate(lhs, axis=0).T.astype(BF16)
    ikw_t = ikw_ref[...].T
    iw = [ikw_t[IDX_DIM + h:IDX_DIM + h + 1, :] for h in range(IDX_HEADS)]

    def score_body(ch, carry, masked):
        m1, m2 = carry
        sr = sc_rows
        for sub in range(kc // sr):
            k0 = pl.multiple_of(ch * kc + sub * sr, sr)
            s_all = _dot(ikx_ref[pl.ds(k0, sr), :], lhs_t)
            acc = jnp.zeros((sr, qb), F32)
            for h in range(IDX_HEADS):
                acc = acc + jnp.maximum(s_all[:, h * qb:(h + 1) * qb], 0.0) * iw[h]
            acc = jnp.where(acc == 0.0, 0.0, acc)
            key = _float_key(acc)
            if masked:
                causal = (k0 + _iota((sr, 1), 0)) <= tq
                key = jnp.where(causal, key, INT_MIN)
                acc = jnp.where(causal, acc, -jnp.inf)
            sc_ref[pl.ds(k0, sr), :] = key
            for j in range(sr // LANES):
                xj = acc[j * LANES:(j + 1) * LANES, :]
                m2 = jnp.maximum(m2, jnp.minimum(m1, xj))
                m1 = jnp.maximum(m1, xj)
        return m1, m2

    ninf = jnp.full((LANES, qb), -jnp.inf, F32)
    n_below = t0 // kc
    top2 = lax.fori_loop(0, n_below, functools.partial(score_body, masked=False), (ninf, ninf))
    m1, m2 = lax.fori_loop(n_below, nch, functools.partial(score_body, masked=True), top2)

    def count_ge(cand):
        def body(ch, acc):
            for j in range(kc // CNT_ROWS):
                kj = pl.multiple_of(ch * kc + j * CNT_ROWS, CNT_ROWS)
                acc = acc + jnp.where(sc_ref[pl.ds(kj, CNT_ROWS), :] >= cand, 1.0, 0.0)
            return acc
        acc = lax.fori_loop(0, nch, body, jnp.zeros((CNT_ROWS, qb), F32))
        return jnp.sum(acc, axis=0, keepdims=True)

    k_row = jnp.minimum(tq + 1, TOPK_MAX).astype(F32)
    hi0 = _float_key(jnp.max(m1, axis=0, keepdims=True))
    lo0 = jnp.minimum(_float_key(jnp.min(m2, axis=0, keepdims=True)), hi0)
    c_pos = count_ge(jnp.ones((1, qb), I32))
    c_nonneg = count_ge(jnp.zeros((1, qb), I32))
    at_zero = (c_pos < k_row) & (c_nonneg >= k_row)
    above = c_pos >= k_row
    lo0 = jnp.where(at_zero, 0, jnp.where(above, jnp.maximum(lo0, 1), lo0))
    n_above0 = jnp.where(at_zero, c_pos, jnp.where(above | (hi0 < 0), 0.0, c_nonneg))
    hi0 = jnp.where(at_zero, 0, jnp.where(above, hi0, jnp.minimum(hi0, -1)))
    lo0 = jnp.minimum(lo0, hi0)

    def open_rows(lo, hi):
        return jnp.max(jnp.where(lo < hi, 1.0, 0.0))

    def bis_body(st):
        lo, hi, n_above, _ = st
        for _ in range(BIS_STEPS):
            mid = (lo | hi) - ((lo ^ hi) >> 1)
            c = count_ge(mid)
            ge = c >= k_row
            exact = c == k_row
            lo, hi = jnp.where(ge, mid, lo), jnp.where(exact, mid, jnp.where(ge, hi, mid - 1))
            n_above = jnp.where(exact, -1.0, jnp.where(ge, n_above, c))
        return lo, hi, n_above, open_rows(lo, hi)

    thr, _, n_above, _ = lax.while_loop(lambda st: st[3] > 0.5, bis_body, (lo0, hi0, n_above0, open_rows(lo0, hi0)))

    dq = dq_ref[...]
    slc = slc_ref[...]
    qaug = []
    for h in range(nh):
        ql = _bdot(dq[:, h * HEAD_DIM:(h + 1) * HEAD_DIM], wuk_ref[h]) * HEAD_DIM ** -0.5
        qaug.append(jnp.concatenate([ql, jnp.broadcast_to(slc[h:h + 1, :], (qb, LANES))], axis=1))
    qaug_t = jnp.concatenate(qaug, axis=0).T.astype(BF16)
    acc_ref[...] = jnp.zeros_like(acc_ref)

    nsub = kc // sc_rows

    def sub_starts(ch):
        return [pl.multiple_of(ch * kc + sub * sc_rows, sc_rows) for sub in range(nsub)]

    def logits(k0):
        return _dot(kf_ref[pl.ds(k0, sc_rows), :], qaug_t)

    def attend(k0, lg_all, sel, m_old):
        ps, m_new = [], []
        for h in range(nh):
            cols = slice(h * qb, (h + 1) * qb)
            lg = jnp.where(sel, lg_all[:, cols], NEG)
            mh = jnp.maximum(m_old[:, cols], jnp.max(lg, axis=0, keepdims=True))
            ps.append(jnp.exp((lg - mh).astype(BF16)))
            m_new.append(mh)
        m_new = jnp.concatenate(m_new, axis=1)
        pv = _dot(vft_ref[:, pl.ds(k0, sc_rows)], jnp.concatenate(ps, axis=1))
        acc_ref[...] = jnp.exp(m_old - m_new) * acc_ref[...] + pv
        return m_new

    m_init = jnp.full((1, nh * qb), NEG, F32)

    need = jnp.where(n_above < 0, float(TOPK_MAX), k_row - n_above)
    tril = tril_ref[...]

    def body(ch, carry):
        tie_run, m_old = carry
        ks = sub_starts(ch)
        lgs = [logits(k0) for k0 in ks]
        keys = [sc_ref[pl.ds(k0, sc_rows), :] for k0 in ks]
        blocks = [slice(j * TIE_BLK, (j + 1) * TIE_BLK) for j in range(sc_rows // TIE_BLK)]
        prefs = [[_dot(tril, jnp.where(key[bl, :] == thr, 1.0, 0.0).astype(BF16)) for bl in blocks] for key in keys]
        for k0, lg, key, pref in zip(ks, lgs, keys, prefs):
            ranks = []
            for pj in pref:
                ranks.append(tie_run + pj)
                tie_run = tie_run + pj[TIE_BLK - 1:TIE_BLK, :]
            sel = (key > thr) | ((key == thr) & (jnp.concatenate(ranks, axis=0) <= need))
            m_old = attend(k0, lg, sel, m_old)
        return tie_run, m_old

    lax.fori_loop(0, nch, body, (jnp.zeros((1, qb), F32), m_init))

    acc = acc_ref[...]
    o_lat = acc[0:KV_LORA, :] / acc[KV_LORA:KV_LORA + 1, :]
    outs = [_bdot(o_lat[:, h * qb:(h + 1) * qb].T, wuv_ref[h]) for h in range(nh)]
    o_ref[...] = jnp.concatenate(outs, axis=1)


DSA_VROWS = KV_LORA + 16


def _dsa_mix(dq, iq, ikw, ckv, w_uk, w_uv, slopes):
    t = dq.shape[0]
    assert t <= LANES * 256
    ikn = ikw[:, 0:IDX_DIM]
    ik_hi, ik_lo = _split2(ikn)
    ikx = jnp.concatenate([ik_hi, ik_lo, ik_hi, ik_lo], axis=1)
    ckv_b = ckv.astype(BF16)
    pos = jnp.arange(t, dtype=I32)
    pa = (pos // LANES).astype(BF16)[:, None]
    pb = (pos % LANES).astype(BF16)[:, None]
    kf = jnp.concatenate([ckv_b, pa, pa, pa, pb, pb, pb, jnp.zeros((t, LANES - 6), BF16)], axis=1)
    vft = jnp.concatenate([ckv_b.T, jnp.ones((1, t), BF16), jnp.zeros((DSA_VROWS - KV_LORA - 1, t), BF16)], axis=0)
    cols = []
    for sl in slopes:
        for coef in (sl * LANES, sl):
            c_hi = jnp.asarray(coef, F32).astype(BF16)
            r1 = jnp.asarray(coef, F32) - c_hi.astype(F32)
            c_mid = r1.astype(BF16)
            c_lo = (r1 - c_mid.astype(F32)).astype(BF16)
            cols += [c_hi.astype(F32), c_mid.astype(F32), c_lo.astype(F32)]
    slc = jnp.stack(cols).reshape(DSA_HEADS, 6)
    slc = jnp.pad(slc, ((0, 8 - DSA_HEADS), (0, LANES - 6)))
    assert t % DSA_KC == 0
    kc = TIE_BLK
    tril = jnp.asarray((np.arange(kc)[:, None] >= np.arange(kc)[None, :]).astype(np.float32), BF16)
    row = lambda i: (i, 0)
    const2 = lambda i: (0, 0)
    const3 = lambda i: (0, 0, 0)
    resident = lambda shape: pl.BlockSpec(shape, const2, pipeline_mode=pl.Buffered(1))
    return pl.pallas_call(
        _dsa_kernel,
        grid=(t // DSA_QB,),
        in_specs=[
            pl.BlockSpec((DSA_QB, DSA_W), row),
            pl.BlockSpec((DSA_QB, IDX_HEADS * IDX_DIM), row),
            pl.BlockSpec((DSA_QB, LANES), row),
            resident((t, 4 * IDX_DIM)),
            resident((t, 2 * LANES)),
            resident((DSA_VROWS, t)),
            pl.BlockSpec((DSA_HEADS, HEAD_DIM, KV_LORA), const3),
            pl.BlockSpec((DSA_HEADS, KV_LORA, HEAD_DIM), const3),
            resident((kc, kc)),
            pl.BlockSpec((8, LANES), const2),
        ],
        out_specs=pl.BlockSpec((DSA_QB, DSA_W), row),
        out_shape=jax.ShapeDtypeStruct((t, DSA_W), F32),
        scratch_shapes=[
            pltpu.VMEM((t, DSA_QB), I32),
            pltpu.VMEM((DSA_VROWS, DSA_HEADS * DSA_QB), F32),
        ],
        compiler_params=_params(("arbitrary",)),
        name="dsa_mix",
    )(dq, iq, ikw, ikx, kf, vft, w_uk, w_uv, tril, slc)


def _swa_kernel(q_ref, kv_ref, kvp_ref, sink_ref, o_ref, *, slopes):
    i = pl.program_id(0)
    w = WINDOW
    gsz = SWA_HEADS // SWA_KV_HEADS
    q = q_ref[...]
    kv = kv_ref[...]
    kvp = kvp_ref[...]
    qi = _iota((w, 2 * w), 0)
    kj = _iota((w, 2 * w), 1)
    dist = qi + w - kj
    in_band = (dist >= 0) & (dist < w)
    valid = [in_band & ((kj >= w) | (i > 0))] + [in_band] * (SWA_BLOCKS - 1)
    distf = dist.astype(F32)
    sinks = sink_ref[...]
    rows = [slice(b * w, (b + 1) * w) for b in range(SWA_BLOCKS)]
    prev = [kvp] + [kv[rows[b], :] for b in range(SWA_BLOCKS - 1)]
    k2 = [[jnp.concatenate([prev[b][:, g * HEAD_DIM:(g + 1) * HEAD_DIM],
                            kv[rows[b], g * HEAD_DIM:(g + 1) * HEAD_DIM]], axis=0).astype(BF16)
           for g in range(SWA_KV_HEADS)] for b in range(SWA_BLOCKS)]
    v2 = [[jnp.concatenate([prev[b][:, w + g * HEAD_DIM:w + (g + 1) * HEAD_DIM],
                            kv[rows[b], w + g * HEAD_DIM:w + (g + 1) * HEAD_DIM]], axis=0).astype(BF16)
           for g in range(SWA_KV_HEADS)] for b in range(SWA_BLOCKS)]
    items = [(b, hd) for b in range(SWA_BLOCKS) for hd in range(SWA_HEADS)]
    s = [_dot_nt(q[rows[b], hd * HEAD_DIM:(hd + 1) * HEAD_DIM].astype(BF16), k2[b][hd // gsz]) * HEAD_DIM ** -0.5
         for b, hd in items]
    s = [jnp.where(valid[b], s[n] - slopes[hd] * distf, NEG) for n, (b, hd) in enumerate(items)]
    sink = [sinks[0:1, hd:hd + 1] for _, hd in items]
    m = [jnp.maximum(jnp.max(s[n], axis=1, keepdims=True), sink[n]) for n in range(len(items))]
    e = [jnp.exp(s[n] - m[n]) for n in range(len(items))]
    p = [e[n] / (jnp.sum(e[n], axis=1, keepdims=True) + jnp.exp(sink[n] - m[n])) for n in range(len(items))]
    outs = [_dot(p[n].astype(BF16), v2[b][hd // gsz]) for n, (b, hd) in enumerate(items)]
    for b in range(SWA_BLOCKS):
        o_ref[rows[b], :] = jnp.concatenate(outs[b * SWA_HEADS:(b + 1) * SWA_HEADS], axis=1)


SWA_BLOCKS = 4


def _swa_mix(sq, skv, sinks, slopes):
    t = sq.shape[0]
    w = WINDOW
    step = SWA_BLOCKS * w
    return pl.pallas_call(
        functools.partial(_swa_kernel, slopes=slopes),
        grid=(t // step,),
        in_specs=[
            pl.BlockSpec((step, SWA_W), lambda i: (i, 0)),
            pl.BlockSpec((step, 2 * w), lambda i: (i, 0)),
            pl.BlockSpec((w, 2 * w), lambda i: (jnp.maximum(SWA_BLOCKS * i - 1, 0), 0)),
            pl.BlockSpec((1, LANES), lambda i: (0, 0)),
        ],
        out_specs=pl.BlockSpec((step, SWA_W), lambda i: (i, 0)),
        out_shape=jax.ShapeDtypeStruct((t, SWA_W), F32),
        compiler_params=_params(("arbitrary",)),
        name="swa_mix",
    )(sq, skv, skv, sinks)


def _post_mix_kernel(x_ref, orw_ref, ods_ref, osw_ref, wout_ref, g1_ref, lng_ref, lnb_ref,
                     sc2_ref, sh2_ref, rwt_ref, rb_ref, tri_ref,
                     x1_ref, h2_ref, eidx_ref, rank_ref, gate_ref, cnt_ref, carry_ref):
    i = pl.program_id(0)

    @pl.when(i == 0)
    def _():
        carry_ref[...] = jnp.zeros_like(carry_ref)

    y = (_dot(orw_ref[...].astype(BF16), wout_ref[0:RWKV_W, :])
         + _dot(ods_ref[...].astype(BF16), wout_ref[RWKV_W:RWKV_W + DSA_W, :])
         + _dot(osw_ref[...].astype(BF16), wout_ref[RWKV_W + DSA_W:D_MODEL, :]))
    x1 = _layer_norm_rows(ALPHA * x_ref[...] + g1_ref[...] * y, lng_ref[...], lnb_ref[...])
    x1_ref[...] = x1
    h2 = x1 * (1.0 + sc2_ref[...]) + sh2_ref[...]
    h2_ref[...] = _pack_halves(h2)

    tm = h2.shape[0]
    ne = N_EXPERTS
    gs = ne // N_GROUPS
    scores = _sigmoid(_dot_nt(rwt_ref[...], h2, HI))
    sel = scores + rb_ref[...]
    sub = _iota((gs, tm), 0).astype(F32)
    gsc = []
    for j in range(N_GROUPS):
        gj = sel[j * gs:(j + 1) * gs, :]
        m1 = jnp.max(gj, axis=0, keepdims=True)
        f1 = jnp.min(jnp.where(gj == m1, sub, float(gs)), axis=0, keepdims=True)
        m2 = jnp.max(jnp.where(sub == f1, -jnp.inf, gj), axis=0, keepdims=True)
        gsc.append(m1 + m2)
    gsc = jnp.concatenate(gsc, axis=0)
    gid = _iota((N_GROUPS, tm), 0).astype(F32)
    gmask = jnp.zeros((N_GROUPS, tm), F32)
    for _ in range(TOPK_GROUPS):
        mx = jnp.max(gsc, axis=0, keepdims=True)
        fi = jnp.min(jnp.where(gsc == mx, gid, float(N_GROUPS)), axis=0, keepdims=True)
        pick = gid == fi
        gmask = jnp.where(pick, 1.0, gmask)
        gsc = jnp.where(pick, -jnp.inf, gsc)
    selm = jnp.concatenate(
        [jnp.where(gmask[j:j + 1, :] > 0.5, sel[j * gs:(j + 1) * gs, :], NEG) for j in range(N_GROUPS)], axis=0)
    eid = _iota((ne, tm), 0).astype(F32)
    gsel, eids = [], []
    chosen_f = jnp.zeros((ne, tm), F32)
    for _ in range(TOP_K):
        mx = jnp.max(selm, axis=0, keepdims=True)
        fi = jnp.min(jnp.where(selm == mx, eid, float(ne)), axis=0, keepdims=True)
        pick = eid == fi
        eids.append(fi)
        gsel.append(jnp.sum(jnp.where(pick, scores, 0.0), axis=0, keepdims=True))
        chosen_f = jnp.where(pick, 1.0, chosen_f)
        selm = jnp.where(pick, -jnp.inf, selm)
    gsum = gsel[0]
    for kx in range(1, TOP_K):
        gsum = gsum + gsel[kx]
    before = _dot(chosen_f.astype(BF16), tri_ref[...]) + carry_ref[:, 0:1]
    ranks = [jnp.sum(jnp.where(eid == eids[kx], before, 0.0), axis=0, keepdims=True) for kx in range(TOP_K)]
    eidx_ref[...] = jnp.concatenate(eids, axis=0).astype(I32)
    rank_ref[...] = jnp.concatenate(ranks, axis=0).astype(I32)
    gate_ref[...] = jnp.concatenate(gsel, axis=0) / gsum * ROUTED_SCALE
    carry_ref[...] = carry_ref[...] + jnp.sum(chosen_f, axis=1, keepdims=True)
    cnt_ref[...] = carry_ref[...]


def _post_mix(x, o_rw, o_ds, o_sw, w_out, g1, ln_g, ln_b, sc2, sh2, router_wt, router_b, tm=512):
    t, d = x.shape
    tri = (np.arange(tm)[:, None] < np.arange(tm)[None, :]).astype(np.float32)
    tri = jnp.asarray(tri, BF16)
    row = lambda i: (i, 0)
    const = lambda i: (0, 0)
    col = lambda i: (0, i)
    vec = pl.BlockSpec((1, d), const)
    return pl.pallas_call(
        _post_mix_kernel,
        grid=(t // tm,),
        in_specs=[
            pl.BlockSpec((tm, d), row),
            pl.BlockSpec((tm, RWKV_W), row),
            pl.BlockSpec((tm, DSA_W), row),
            pl.BlockSpec((tm, SWA_W), row),
            pl.BlockSpec((d, d), const),
            vec, vec, vec, vec, vec,
            pl.BlockSpec((N_EXPERTS, d), const),
            pl.BlockSpec((N_EXPERTS, 1), const),
            pl.BlockSpec((tm, tm), const),
        ],
        out_specs=[
            pl.BlockSpec((tm, d), row),
            pl.BlockSpec((tm, d // 2), row),
            pl.BlockSpec((TOP_K, tm), col),
            pl.BlockSpec((TOP_K, tm), col),
            pl.BlockSpec((TOP_K, tm), col),
            pl.BlockSpec((N_EXPERTS, LANES), const),
        ],
        out_shape=[
            jax.ShapeDtypeStruct((t, d), F32),
            jax.ShapeDtypeStruct((t, d // 2), I32),
            jax.ShapeDtypeStruct((TOP_K, t), I32),
            jax.ShapeDtypeStruct((TOP_K, t), I32),
            jax.ShapeDtypeStruct((TOP_K, t), F32),
            jax.ShapeDtypeStruct((N_EXPERTS, LANES), F32),
        ],
        scratch_shapes=[pltpu.VMEM((N_EXPERTS, LANES), F32)],
        compiler_params=_params(("arbitrary",)),
        name="post_mix_router",
    )(x, o_rw, o_ds, o_sw, w_out, g1, ln_g, ln_b, sc2, sh2, router_wt, router_b, tri)


MOE_ROWS = 512
MOE_TILE = 256


def _pack_halves(v):
    w = v.shape[1] // 2
    bits = lax.bitcast_convert_type(v.astype(BF16).astype(F32), I32)
    return bits[:, :w] | lax.shift_right_logical(bits[:, w:], 16)


def _unpack_halves(p):
    return lax.bitcast_convert_type(p & -65536, F32), lax.bitcast_convert_type(p << 16, F32)


def _row_copy(src_ref, src_row, dst_ref, dst_row, sem):
    return pltpu.make_async_copy(src_ref.at[pl.ds(src_row, 1), :], dst_ref.at[pl.ds(dst_row, 1), :], sem)


def _dispatch_kernel(slot_hbm, h_ref, xs_in, xs_out, slot_smem, sem_tab, sem_rows):
    del xs_in
    i = pl.program_id(0)
    tab = pltpu.make_async_copy(slot_hbm.at[i], slot_smem, sem_tab)
    tab.start()
    tab.wait()

    def issue(tt, carry):
        for kx in range(TOP_K):
            _row_copy(h_ref, tt, xs_out, slot_smem[kx, tt], sem_rows).start(priority=kx % 2)
        return carry

    lax.fori_loop(0, MOE_TILE, issue, 0)

    def drain(tt, carry):
        for kx in range(TOP_K):
            _row_copy(h_ref, 0, xs_out, 0, sem_rows).wait()
        return carry

    lax.fori_loop(0, MOE_TILE, drain, 0)


def _dispatch(slot_tiles, rows, cap, recycled=None):
    t, d = rows.shape
    xs0 = jnp.zeros((cap, d), rows.dtype) if recycled is None else recycled
    return pl.pallas_call(
        _dispatch_kernel,
        grid=(t // MOE_TILE,),
        in_specs=[
            pl.BlockSpec(memory_space=pl.ANY),
            pl.BlockSpec((MOE_TILE, d), lambda i: (i, 0)),
            pl.BlockSpec(memory_space=pl.ANY),
        ],
        out_specs=pl.BlockSpec(memory_space=pl.ANY),
        out_shape=jax.ShapeDtypeStruct((cap, d), rows.dtype),
        scratch_shapes=[
            pltpu.SMEM((TOP_K, MOE_TILE), I32),
            pltpu.SemaphoreType.DMA,
            pltpu.SemaphoreType.DMA,
        ],
        input_output_aliases={2: 0},
        compiler_params=_params(("arbitrary",)),
        name="moe_dispatch",
    )(slot_tiles, rows, xs0)


def _expert_kernel(be_ref, nb_ref, xs_ref, w1_ref, w3_ref, w2_ref, ys_ref, w1b, w3b, w2b):
    b = pl.program_id(0)
    changed = (b == 0) | (be_ref[b] != be_ref[jnp.maximum(b - 1, 0)])

    @pl.when(changed & (b < nb_ref[0]))
    def _():
        w1b[...] = w1_ref[0, 0].astype(BF16)
        w3b[...] = w3_ref[0, 0].astype(BF16)
        w2b[...] = w2_ref[0, 0].astype(BF16)

    @pl.when(b < nb_ref[0])
    def _():
        x_hi, x_lo = _unpack_halves(xs_ref[...])
        x_hi, x_lo = x_hi.astype(BF16), x_lo.astype(BF16)
        half = x_hi.shape[1]
        a = _dot(x_hi, w1b[0:half, :]) + _dot(x_lo, w1b[half:2 * half, :])
        gte = _dot(x_hi, w3b[0:half, :]) + _dot(x_lo, w3b[half:2 * half, :])
        hmid = (a * _sigmoid(a) * gte).astype(BF16)
        ys_ref[...] = _pack_halves(_dot(hmid, w2b[...]))

    @pl.when(b >= nb_ref[0])
    def _():
        ys_ref[...] = jnp.zeros_like(ys_ref)


def _experts(block_e, n_used, xs, w1, w3, w2, layer):
    cap, dp = xs.shape
    d = 2 * dp
    nb = cap // MOE_ROWS
    grid_spec = pltpu.PrefetchScalarGridSpec(
        num_scalar_prefetch=2,
        grid=(nb,),
        in_specs=[
            pl.BlockSpec((MOE_ROWS, dp), lambda b, be, nu: (b, 0)),
            pl.BlockSpec((1, 1, d, D_EXPERT), lambda b, be, nu: (layer, be[b], 0, 0)),
            pl.BlockSpec((1, 1, d, D_EXPERT), lambda b, be, nu: (layer, be[b], 0, 0)),
            pl.BlockSpec((1, 1, D_EXPERT, d), lambda b, be, nu: (layer, be[b], 0, 0)),
        ],
        out_specs=pl.BlockSpec((MOE_ROWS, dp), lambda b, be, nu: (b, 0)),
        scratch_shapes=[
            pltpu.VMEM((d, D_EXPERT), BF16),
            pltpu.VMEM((d, D_EXPERT), BF16),
            pltpu.VMEM((D_EXPERT, d), BF16),
        ],
    )
    return pl.pallas_call(
        _expert_kernel,
        grid_spec=grid_spec,
        out_shape=jax.ShapeDtypeStruct((cap, dp), I32),
        compiler_params=_params(("arbitrary",)),
        name="moe_experts",
    )(block_e, n_used, xs, w1, w3, w2)


def _combine_kernel(slot_hbm, ys_hbm, x1_ref, h2_ref, gate_ref, sw1_ref, sw3_ref, sw2_ref,
                    g2_ref, lng_ref, lnb_ref, o_ref, slot_smem, gbuf, sem_tab, sem_rows):
    i = pl.program_id(0)
    tab = pltpu.make_async_copy(slot_hbm.at[i], slot_smem, sem_tab)
    tab.start()
    tab.wait()

    def issue(tt, carry):
        for kx in range(TOP_K):
            _row_copy(ys_hbm, slot_smem[kx, tt], gbuf.at[kx], tt, sem_rows).start(priority=kx % 2)
        return carry

    lax.fori_loop(0, MOE_TILE, issue, 0)

    h_hi, h_lo = _unpack_halves(h2_ref[...])
    h_hi, h_lo = h_hi.astype(BF16), h_lo.astype(BF16)
    half = h_hi.shape[1]
    a = _dot(h_hi, sw1_ref[0:half, :]) + _dot(h_lo, sw1_ref[half:2 * half, :])
    gte = _dot(h_hi, sw3_ref[0:half, :]) + _dot(h_lo, sw3_ref[half:2 * half, :])
    y = _dot((a * _sigmoid(a) * gte).astype(BF16), sw2_ref[...])

    def drain(tt, carry):
        for kx in range(TOP_K):
            _row_copy(ys_hbm, 0, gbuf.at[kx], 0, sem_rows).wait()
        return carry

    lax.fori_loop(0, MOE_TILE, drain, 0)

    gates = gate_ref[...]
    r_hi = jnp.zeros((MOE_TILE, half), F32)
    r_lo = jnp.zeros((MOE_TILE, half), F32)
    for kx in range(TOP_K):
        e_hi, e_lo = _unpack_halves(gbuf[kx])
        r_hi = r_hi + gates[:, kx:kx + 1] * e_hi
        r_lo = r_lo + gates[:, kx:kx + 1] * e_lo
    y = y + jnp.concatenate([r_hi, r_lo], axis=1)
    o_ref[...] = _layer_norm_rows(ALPHA * x1_ref[...] + g2_ref[...] * y, lng_ref[...], lnb_ref[...])


def _combine(slot_tiles, ys, x1, h2, gates_t, sw1, sw3, sw2, g2, ln_g, ln_b):
    t, d = x1.shape
    row = lambda i: (i, 0)
    const = lambda i: (0, 0)
    vec = pl.BlockSpec((1, d), const)
    return pl.pallas_call(
        _combine_kernel,
        grid=(t // MOE_TILE,),
        in_specs=[
            pl.BlockSpec(memory_space=pl.ANY),
            pl.BlockSpec(memory_space=pl.ANY),
            pl.BlockSpec((MOE_TILE, d), row),
            pl.BlockSpec((MOE_TILE, d // 2), row),
            pl.BlockSpec((MOE_TILE, TOP_K), row),
            pl.BlockSpec((d, D_EXPERT), const),
            pl.BlockSpec((d, D_EXPERT), const),
            pl.BlockSpec((D_EXPERT, d), const),
            vec, vec, vec,
        ],
        out_specs=pl.BlockSpec((MOE_TILE, d), row),
        out_shape=jax.ShapeDtypeStruct((t, d), F32),
        scratch_shapes=[
            pltpu.SMEM((TOP_K, MOE_TILE), I32),
            pltpu.VMEM((TOP_K, MOE_TILE, d // 2), I32),
            pltpu.SemaphoreType.DMA,
            pltpu.SemaphoreType.DMA,
        ],
        compiler_params=_params(("arbitrary",)),
        name="moe_combine",
    )(slot_tiles, ys, x1, h2, gates_t, sw1, sw3, sw2, g2, ln_g, ln_b)


def _pad_w_in(w_in_l):
    d = w_in_l.shape[0]
    pad = jnp.zeros((d, C_SQ[0] - N_ORIG_BEFORE_PAD), w_in_l.dtype)
    return jnp.concatenate([w_in_l[:, :N_ORIG_BEFORE_PAD], pad, w_in_l[:, N_ORIG_BEFORE_PAD:]], axis=1)


def _pad_lanes(v, width=LANES):
    v = v.reshape(1, -1)
    return jnp.pad(v, ((0, 0), (0, width - v.shape[1])))


def _moe_tables(eidx, rank, counts):
    t = eidx.shape[1]
    cnt = counts[:, 0].astype(I32)
    padded = (cnt + MOE_ROWS - 1) // MOE_ROWS * MOE_ROWS
    pad_end = jnp.cumsum(padded)
    pad_start = pad_end - padded
    e_ids = jnp.arange(N_EXPERTS, dtype=I32)
    start_of = jnp.sum(jnp.where(eidx[..., None] == e_ids, pad_start, 0), axis=-1)
    slot = start_of + rank
    slot_tiles = slot.reshape(TOP_K, t // MOE_TILE, MOE_TILE).transpose(1, 0, 2)
    cap = t * TOP_K + N_EXPERTS * MOE_ROWS
    nb = cap // MOE_ROWS
    blk_row = jnp.arange(nb, dtype=I32)[:, None] * MOE_ROWS
    block_e = jnp.minimum(jnp.sum((pad_end[None, :] <= blk_row).astype(I32), axis=1), N_EXPERTS - 1)
    n_used = (pad_end[-1] // MOE_ROWS).astype(I32).reshape(1)
    return slot_tiles, block_e, n_used, cap


def kernel(x, c, w_mod, b_mod, w_in, rwkv_mu, rwkv_w0, rwkv_w2, rwkv_a0, rwkv_a2, rwkv_g2, rwkv_k_k, rwkv_k_a, rwkv_r_k, rwkv_ln_g, rwkv_ln_b, dsa_kv_norm, dsa_w_uk, dsa_w_uv, dsa_ik_g, dsa_ik_b, swa_sinks, w_out, ln_mix_g, ln_mix_b, router_w, router_bias, exp_w1, exp_w3, exp_w2, sh_w1, sh_w3, sh_w2, ln_ffn_g, ln_ffn_b):
    bsz, t, d = x.shape
    assert bsz == 1 and d == D_MODEL
    depth = w_mod.shape[0]
    n_sl = SWA_HEADS + DSA_HEADS
    slopes = [2.0 ** (-8.0 * (j + 1.0) / n_sl) for j in range(n_sl)]
    swa_slopes, dsa_slopes = slopes[:SWA_HEADS], slopes[SWA_HEADS:]

    mod = _modulation(c, w_mod, b_mod)
    xs_cur = x[0]
    xs_sorted = None
    row1 = lambda v: v.reshape(1, -1)
    for l in range(depth):
        sh1, sc1, g1, sh2, sc2, g2 = [mod[l, :, j * d:(j + 1) * d] for j in range(6)]
        wp = _pad_w_in(w_in[l])
        w_hi = wp.astype(BF16)
        w_idx = wp[:, C_IDX[0]:C_IDX[1]]
        w_idx_lo = (w_idx - w_idx.astype(BF16).astype(F32)).astype(BF16)
        rkv, lora, dq, ckv, iq, ikw, sq, skv = _input_proj(
            xs_cur, sc1, sh1, w_hi, w_idx_lo, row1(dsa_kv_norm[l]),
            _pad_lanes(dsa_ik_g[l]), _pad_lanes(dsa_ik_b[l]))
        o_rw = _rwkv_mix(rkv, lora, row1(rwkv_mu[l]), row1(rwkv_w0[l]), rwkv_w2[l], row1(rwkv_a0[l]),
                         rwkv_a2[l], rwkv_g2[l], row1(rwkv_k_k[l]), row1(rwkv_k_a[l]), row1(rwkv_r_k[l]),
                         row1(rwkv_ln_g[l]), row1(rwkv_ln_b[l]))
        o_ds = _dsa_mix(dq, iq, ikw, ckv, dsa_w_uk[l], dsa_w_uv[l], dsa_slopes)
        o_sw = _swa_mix(sq, skv, _pad_lanes(swa_sinks[l]), swa_slopes)
        x1, h2, eidx, rank, gates, counts = _post_mix(
            xs_cur, o_rw, o_ds, o_sw, w_out[l].astype(BF16), g1, row1(ln_mix_g[l]), row1(ln_mix_b[l]),
            sc2, sh2, router_w[l].T, router_bias[l].reshape(-1, 1))
        slot_tiles, block_e, n_used, cap = _moe_tables(eidx, rank, counts)
        xs_sorted = _dispatch(slot_tiles, h2, cap, xs_sorted)
        ys = _experts(block_e, n_used, xs_sorted, exp_w1, exp_w3, exp_w2, l)
        xs_cur = _combine(slot_tiles, ys, x1, h2, gates.T, sh_w1[l].astype(BF16), sh_w3[l].astype(BF16),
                          sh_w2[l].astype(BF16), g2, row1(ln_ffn_g[l]), row1(ln_ffn_b[l]))
    return xs_cur[None]
```

```python
import functools
import math

import jax
import jax.numpy as jnp
import numpy as np
from jax import lax
from jax.experimental import pallas as pl
from jax.experimental.pallas import tpu as pltpu

F32 = jnp.float32
BF16 = jnp.bfloat16
I32 = jnp.int32
HI = lax.Precision.HIGHEST

D_MODEL = 1024
DEPTH = 4
HEAD_DIM = 64
RWKV_HEADS = 6
DSA_HEADS = 4
SWA_HEADS = 6
SWA_KV_HEADS = 2
RWKV_W = RWKV_HEADS * HEAD_DIM
DSA_W = DSA_HEADS * HEAD_DIM
SWA_W = SWA_HEADS * HEAD_DIM
DECAY_LORA = 64
AAA_LORA = 64
GATE_LORA = 128
GN_EPS = 64e-5
KV_LORA = 128
IDX_HEADS = 4
IDX_DIM = 64
TOPK_MAX = 256
WINDOW = 128
N_EXPERTS = 64
TOP_K = 8
N_GROUPS = 8
TOPK_GROUPS = 4
D_EXPERT = 256
ROUTED_SCALE = 2.5
ALPHA = (2 * DEPTH) ** 0.25
LN_EPS = 1e-5
NEG = -1e30
INT_MIN = -(2 ** 31)

LANES = 128
VMEM_LIMIT = 56 * 1024 * 1024

C_RKV = (0, 1152)
C_LORA = (1152, 1408)
C_DQ = (1408, 1664)
C_CKV = (1664, 1792)
C_IDX = (1792, 2176)
C_SQ = (2176, 2560)
C_SKV = (2560, 2816)
P_PAD = 2816
N_ORIG_BEFORE_PAD = 2116


def _dot(a, b, prec=None):
    return jnp.dot(a, b, preferred_element_type=F32, precision=prec)


def _dot_nt(a, b, prec=None):
    return lax.dot_general(a, b, (((1,), (1,)), ((), ())), preferred_element_type=F32, precision=prec)


def _split2(a):
    a_hi = a.astype(BF16)
    return a_hi, (a - a_hi.astype(F32)).astype(BF16)


def _bdot(a, b):
    return _dot(a.astype(BF16), b.astype(BF16))


def _bdot_nt(a, b):
    return _dot_nt(a.astype(BF16), b.astype(BF16))


def _dot2(a, b_exact):
    a_hi, a_lo = _split2(a)
    return _dot(a_hi, b_exact) + _dot(a_lo, b_exact)


def _dot2_l(a_exact, b):
    b_hi, b_lo = _split2(b)
    return _dot(a_exact, b_hi) + _dot(a_exact, b_lo)


def _dot3(a, b):
    a_hi, a_lo = _split2(a)
    b_hi, b_lo = _split2(b)
    return _dot(a_hi, b_hi) + (_dot(a_lo, b_hi) + _dot(a_hi, b_lo))


def _iota(shape, dim):
    return lax.broadcasted_iota(I32, shape, dim)


def _sigmoid(x):
    return 1.0 / (1.0 + jnp.exp(-x))


def _layer_norm_rows(v, g, b):
    mu = jnp.mean(v, axis=-1, keepdims=True)
    d = v - mu
    var = jnp.mean(d * d, axis=-1, keepdims=True)
    return d * lax.rsqrt(var + LN_EPS) * g + b


def _params(sem):
    return pltpu.CompilerParams(dimension_semantics=sem, vmem_limit_bytes=VMEM_LIMIT)


def _mod_kernel(c_ref, w_ref, b_ref, o_ref):
    c = c_ref[...]
    cond = c * _sigmoid(c)
    o_ref[0] = _dot(cond, w_ref[0], HI) + b_ref[0]


def _modulation(c, w_mod, b_mod):
    depth, d, d6 = w_mod.shape
    c8 = jnp.broadcast_to(c, (8, d))
    nj = d6 // d
    out = pl.pallas_call(
        _mod_kernel,
        grid=(depth, nj),
        in_specs=[
            pl.BlockSpec((8, d), lambda l, j: (0, 0)),
            pl.BlockSpec((1, d, d), lambda l, j: (l, 0, j)),
            pl.BlockSpec((1, 1, d), lambda l, j: (l, 0, j)),
        ],
        out_specs=pl.BlockSpec((1, 8, d), lambda l, j: (l, 0, j)),
        out_shape=jax.ShapeDtypeStruct((depth, 8, d6), F32),
        compiler_params=_params(("arbitrary", "arbitrary")),
        name="modulation",
    )(c8, w_mod, b_mod.reshape(depth, 1, d6))
    return out[:, 0:1, :]


def _proj_kernel(x_ref, sc_ref, sh_ref, w_ref, wlo_ref, kvn_ref, ikg_ref, ikb_ref,
                 rkv_ref, lora_ref, dq_ref, ckv_ref, iq_ref, ikw_ref, sq_ref, skv_ref):
    h = x_ref[...] * (1.0 + sc_ref[...]) + sh_ref[...]
    hb = h.astype(BF16)
    hl = (h - hb.astype(F32)).astype(BF16)

    def mm(c):
        return _dot(hb, w_ref[:, c[0]:c[1]])

    rkv_ref[...] = mm(C_RKV)
    lora_ref[...] = mm(C_LORA)
    dq_ref[...] = mm(C_DQ)
    sq_ref[...] = mm(C_SQ)
    skv_ref[...] = mm(C_SKV)
    ckv = mm(C_CKV)
    ckv_ref[...] = ckv * lax.rsqrt(jnp.mean(ckv * ckv, axis=-1, keepdims=True) + 1e-6) * kvn_ref[...]
    idx = mm(C_IDX) + _dot(hl, w_ref[:, C_IDX[0]:C_IDX[1]]) + _dot(hb, wlo_ref[...])
    iq_ref[...] = idx[:, 0:256]
    g3 = idx[:, 256:384]
    lane = _iota(g3.shape, 1)
    isk = lane < IDX_DIM
    mu = jnp.sum(jnp.where(isk, g3, 0.0), axis=-1, keepdims=True) * (1.0 / IDX_DIM)
    dk = jnp.where(isk, g3 - mu, 0.0)
    var = jnp.sum(dk * dk, axis=-1, keepdims=True) * (1.0 / IDX_DIM)
    ikn = dk * lax.rsqrt(var + LN_EPS) * ikg_ref[...] + ikb_ref[...]
    ikw_ref[...] = jnp.where(isk, ikn, g3 * (IDX_HEADS ** -0.5 * IDX_DIM ** -0.5))


def _input_proj(x, sc, sh, w_hi, w_idx_lo, kvn, ikg, ikb, tm=512):
    t, d = x.shape
    widths = [C_RKV, C_LORA, C_DQ, C_CKV, (0, 256), (0, 128), C_SQ, C_SKV]
    widths = [c[1] - c[0] for c in widths]
    const = lambda i: (0, 0)
    row = lambda i: (i, 0)
    return pl.pallas_call(
        _proj_kernel,
        grid=(t // tm,),
        in_specs=[
            pl.BlockSpec((tm, d), row),
            pl.BlockSpec((1, d), const),
            pl.BlockSpec((1, d), const),
            pl.BlockSpec((d, P_PAD), const),
            pl.BlockSpec((d, C_IDX[1] - C_IDX[0]), const),
            pl.BlockSpec((1, KV_LORA), const),
            pl.BlockSpec((1, LANES), const),
            pl.BlockSpec((1, LANES), const),
        ],
        out_specs=[pl.BlockSpec((tm, w), row) for w in widths],
        out_shape=[jax.ShapeDtypeStruct((t, w), F32) for w in widths],
        compiler_params=_params(("arbitrary",)),
        name="input_proj",
    )(x, sc, sh, w_hi, w_idx_lo, kvn, ikg, ikb)


RW_CHUNK = 64
RW_UNROLL = 8


def _rwkv_kernel(r_ref, k_ref, v_ref, lora_ref, rp_ref, kp_ref, vp_ref, lp_ref,
                 mur_ref, muk_ref, muv_ref, mul_ref, w0_ref, w2_ref, a0_ref, a2_ref, g2_ref,
                 kk_ref, ka_ref, rk_ref, lng_ref, lnb_ref, o_ref,
                 h_ref, y_ref, st_ref, wm_ref, ar_ref, rs_ref, vs_ref, lt_ref, zm_ref, y0_ref, gc_ref, *, tg):
    g = pl.program_id(0)
    c64 = RW_CHUNK
    nch = tg // c64
    npair = RWKV_W // LANES
    pair_lanes = [slice(p * LANES, (p + 1) * LANES) for p in range(npair)]
    lane = _iota((1, LANES), 1)
    first = g == 0

    @pl.when(first)
    def _():
        h_ref[...] = jnp.zeros_like(h_ref)

    rowid = _iota((tg, 1), 0)

    def shift_mix(cur_ref, prev_ref, mu_ref):
        cur = cur_ref[...]
        prev_row = jnp.where(first, 0.0, prev_ref[7:8, :])
        rolled = pltpu.roll(cur, 1, 0)
        shifted = jnp.where(rowid == 0, prev_row, rolled)
        return cur + (shifted - cur) * mu_ref[...]

    r = shift_mix(r_ref, rp_ref, mur_ref)
    k = shift_mix(k_ref, kp_ref, muk_ref)
    v = shift_mix(v_ref, vp_ref, muv_ref)
    lo = shift_mix(lora_ref, lp_ref, mul_ref)
    wl = lo[:, 0:DECAY_LORA]
    al = lo[:, DECAY_LORA:DECAY_LORA + AAA_LORA]
    gl = lo[:, 128:256]

    zw = -(w0_ref[...] + _dot3(jnp.tanh(wl), w2_ref[...]))
    softplus = jnp.maximum(zw, 0.0) + jnp.log(1.0 + jnp.exp(-jnp.abs(zw)))
    lw = -jnp.exp(-softplus - 0.5)
    a = _sigmoid(a0_ref[...] + _bdot(al, a2_ref[...]))
    gate = _bdot(_sigmoid(gl), g2_ref[...])

    ri = _iota((LANES, LANES), 0) // HEAD_DIM
    ci = _iota((LANES, LANES), 1) // HEAD_DIM
    bones = jnp.where(ri == ci, 1.0, 0.0).astype(BF16)

    def head_sum(xf):
        return jnp.concatenate([_dot2(xf[:, pl_], bones) for pl_ in pair_lanes], axis=1)

    kk = k * kk_ref[...]
    kk = kk / jnp.maximum(jnp.sqrt(head_sum(kk * kk)), 1e-12)
    k2 = k * (1.0 + (a - 1.0) * ka_ref[...])
    bonus = head_sum(r * k2 * rk_ref[...]) * v
    bvec = a * kk

    st_ref[0] = r
    st_ref[1] = k2
    st_ref[2] = v
    st_ref[3] = lw
    st_ref[4] = kk
    st_ref[5] = bvec

    rr = _iota((LANES, LANES), 0)
    cc = _iota((LANES, LANES), 1)
    same = (rr // c64) == (cc // c64)
    strict = same & ((rr % c64) > (cc % c64))
    incl = same & ((rr % c64) >= (cc % c64))
    eye = jnp.where(rr == cc, 1.0, 0.0)
    tril = jnp.where(_iota((c64, c64), 0) >= _iota((c64, c64), 1), 1.0, 0.0).astype(BF16)
    lo_half = lane < HEAD_DIM

    def stack(xc):
        return jnp.concatenate([jnp.where(lo_half, xc, 0.0), jnp.where(lo_half, 0.0, xc)], axis=0)

    def prepare(c, carry):
        chunks = [c * RW_UNROLL + j for j in range(RW_UNROLL)]
        sls = [pl.ds(pl.multiple_of(cj * c64, c64), c64) for cj in chunks]
        items = [(p, j) for j in range(RW_UNROLL) for p in range(npair)]
        pairs = range(len(items))
        idx = [p * nch + chunks[j] for p, j in items]
        ld = lambda q: [st_ref[q, sls[j], pair_lanes[p]] for p, j in items]
        rc, kc, vc, lwc, kkc, bc = ld(0), ld(1), ld(2), ld(3), ld(4), ld(5)
        cum = [_dot2_l(tril, lwc[p]) for p in pairs]
        tot = [cum[p][c64 - 1:c64, :] for p in pairs]
        g_in = [jnp.exp(cum[p]) for p in pairs]
        g_ex = [jnp.exp(cum[p] - lwc[p]) for p in pairs]
        g_inv = [jnp.exp(-cum[p]) for p in pairs]
        g_rest = [jnp.exp(tot[p] - cum[p]) for p in pairs]
        a_s = [stack(-kkc[p] * g_ex[p]).astype(BF16) for p in pairs]
        b_s = [stack(bc[p] * g_inv[p]).astype(BF16) for p in pairs]
        k_s = [stack(kc[p] * g_inv[p]).astype(BF16) for p in pairs]
        r_s = [stack(rc[p] * g_in[p]).astype(BF16) for p in pairs]
        v_s = [stack(vc[p]).astype(BF16) for p in pairs]
        nmat = [jnp.where(strict, _dot_nt(a_s[p], b_s[p]), 0.0) for p in pairs]
        aak = [jnp.where(strict, _dot_nt(a_s[p], k_s[p]), 0.0) for p in pairs]
        arb = [jnp.where(incl, _dot_nt(r_s[p], b_s[p]), 0.0) for p in pairs]
        ark = [jnp.where(incl, _dot_nt(r_s[p], k_s[p]), 0.0) for p in pairs]
        tinv = [eye + nmat[p] for p in pairs]
        pw = nmat
        for _ in range(5):
            pw = [_bdot(pw[p], pw[p]) for p in pairs]
            tinv = [_bdot(tinv[p], eye + pw[p]) for p in pairs]
        tinv = [tinv[p].astype(BF16) for p in pairs]
        akv = [_bdot(aak[p], v_s[p]).astype(BF16) for p in pairs]
        wmat = [_dot(tinv[p], a_s[p]) for p in pairs]
        zmat = [_dot(tinv[p], akv[p]) for p in pairs]
        y0 = [_bdot(ark[p], v_s[p]) for p in pairs]
        for p in pairs:
            wm_ref[idx[p]] = wmat[p].astype(BF16)
            zm_ref[idx[p]] = zmat[p]
            y0_ref[idx[p]] = y0[p]
            ar_ref[idx[p]] = arb[p].astype(BF16)
            rs_ref[idx[p]] = r_s[p]
            vs_ref[idx[p]] = v_s[p]
            lt_ref[idx[p]] = jnp.concatenate([stack(bc[p] * g_rest[p]), stack(kc[p] * g_rest[p])],
                                             axis=0).T.astype(BF16)
            gc_ref[idx[p]] = jnp.broadcast_to(jnp.sum(eye * jnp.exp(tot[p]), axis=1, keepdims=True),
                                              (LANES, LANES))
        return carry

    lax.fori_loop(0, nch // RW_UNROLL, prepare, 0)

    def advance(c, carry):
        sl = pl.ds(pl.multiple_of(c * c64, c64), c64)
        pairs = range(npair)
        idx = [p * nch + c for p in pairs]
        hst = [h_ref[p] for p in pairs]
        hb = [hst[p].astype(BF16) for p in pairs]
        u = [_dot(wm_ref[idx[p]], hb[p]) + zm_ref[idx[p]] for p in pairs]
        rh = [_dot(rs_ref[idx[p]], hb[p]) for p in pairs]
        ub = [u[p].astype(BF16) for p in pairs]
        hnew = [_dot(lt_ref[idx[p]], jnp.concatenate([ub[p], vs_ref[idx[p]]], axis=0)) for p in pairs]
        au = [_dot(ar_ref[idx[p]], ub[p]) for p in pairs]
        for p in pairs:
            h_ref[p] = gc_ref[idx[p]] * hst[p] + hnew[p]
            ys = rh[p] + au[p] + y0_ref[idx[p]]
            y_ref[sl, pair_lanes[p]] = ys[0:c64, :] + ys[c64:2 * c64, :]
        return carry

    lax.fori_loop(0, nch, advance, 0)

    y = y_ref[...]
    mean = head_sum(y) * (1.0 / HEAD_DIM)
    dy = y - mean
    var = head_sum(dy * dy) * (1.0 / HEAD_DIM)
    o = dy * lax.rsqrt(var + GN_EPS) * lng_ref[...] + lnb_ref[...]
    o_ref[...] = (o + bonus) * gate


def _rwkv_mix(rkv, lora, mu, w0, w2, a0, a2, g2, k_k, k_a, r_k, ln_g, ln_b, tg=512):
    t = rkv.shape[0]
    w = RWKV_W
    npair = w // LANES
    nmat = npair * (tg // RW_CHUNK)
    mu_r, mu_k, mu_v, mu_l = mu[:, 0:w], mu[:, w:2 * w], mu[:, 2 * w:3 * w], mu[:, 3 * w:3 * w + 256]
    blk = lambda off: pl.BlockSpec((tg, w), lambda g: (g, off))
    prev = lambda off: pl.BlockSpec((8, w), lambda g: (jnp.maximum(g * (tg // 8) - 1, 0), off))
    vec = pl.BlockSpec((1, w), lambda g: (0, 0))
    full = lambda rows: pl.BlockSpec((rows, w), lambda g: (0, 0))
    return pl.pallas_call(
        functools.partial(_rwkv_kernel, tg=tg),
        grid=(t // tg,),
        in_specs=[
            blk(0), blk(1), blk(2),
            pl.BlockSpec((tg, 256), lambda g: (g, 0)),
            prev(0), prev(1), prev(2),
            pl.BlockSpec((8, 256), lambda g: (jnp.maximum(g * (tg // 8) - 1, 0), 0)),
            vec, vec, vec,
            pl.BlockSpec((1, 256), lambda g: (0, 0)),
            vec, full(DECAY_LORA), vec, full(AAA_LORA), full(GATE_LORA),
            vec, vec, vec, vec, vec,
        ],
        out_specs=pl.BlockSpec((tg, w), lambda g: (g, 0)),
        out_shape=jax.ShapeDtypeStruct((t, w), F32),
        scratch_shapes=[
            pltpu.VMEM((npair, LANES, LANES), F32),
            pltpu.VMEM((tg, w), F32),
            pltpu.VMEM((6, tg, w), F32),
            pltpu.VMEM((nmat, LANES, LANES), BF16),
            pltpu.VMEM((nmat, LANES, LANES), BF16),
            pltpu.VMEM((nmat, LANES, LANES), BF16),
            pltpu.VMEM((nmat, LANES, LANES), BF16),
            pltpu.VMEM((nmat, LANES, 2 * LANES), BF16),
            pltpu.VMEM((nmat, LANES, LANES), F32),
            pltpu.VMEM((nmat, LANES, LANES), F32),
            pltpu.VMEM((nmat, LANES, LANES), F32),
        ],
        compiler_params=_params(("arbitrary",)),
        name="rwkv7_mix",
    )(rkv, rkv, rkv, lora, rkv, rkv, rkv, lora,
      mu_r, mu_k, mu_v, mu_l, w0, w2, a0, a2, g2, k_k, k_a, r_k, ln_g, ln_b)


DSA_QB = 256
DSA_KC = 1024
DSA_SUB = 512
CNT_ROWS = 64
TIE_BLK = 128
BIS_STEPS = 2


def _float_key(v):
    bits = lax.bitcast_convert_type(v, I32)
    return bits ^ ((bits >> 31) & 0x7FFFFFFF)


def _dsa_kernel(dq_ref, iq_ref, ikw_ref, ikx_ref, kf_ref, vft_ref, wuk_ref, wuv_ref, tril_ref, slc_ref,
                o_ref, sc_ref, acc_ref):
    i = pl.program_id(0)
    qb, kc, sc_rows = DSA_QB, DSA_KC, DSA_SUB
    nh = DSA_HEADS
    t0 = i * qb
    nch = (t0 + qb + kc - 1) // kc
    tq = t0 + _iota((1, qb), 1)

    iq = iq_ref[...]
    iq_hi = iq.astype(BF16).astype(F32)
    iq_lo = iq - iq_hi
    lhs = []
    for h in range(IDX_HEADS):
        s = slice(h * IDX_DIM, (h + 1) * IDX_DIM)
        lhs.append(jnp.concatenate([iq_hi[:, s], iq_hi[:, s], iq_lo[:, s], iq_lo[:, s]], axis=1))
    lhs_t = jnp.concatenate(lhs, axis=0).T.astype(BF16)
    ikw_t = ikw_ref[...].T
    iw = [ikw_t[IDX_DIM + h:IDX_DIM + h + 1, :] for h in range(IDX_HEADS)]

    def score_body(ch, carry, masked):
        m1, m2 = carry
        sr = sc_rows
        for sub in range(kc // sr):
            k0 = pl.multiple_of(ch * kc + sub * sr, sr)
            s_all = _dot(ikx_ref[pl.ds(k0, sr), :], lhs_t)
            acc = jnp.zeros((sr, qb), F32)
            for h in range(IDX_HEADS):
                acc = acc + jnp.maximum(s_all[:, h * qb:(h + 1) * qb], 0.0) * iw[h]
            acc = jnp.where(acc == 0.0, 0.0, acc)
            key = _float_key(acc)
            if masked:
                causal = (k0 + _iota((sr, 1), 0)) <= tq
                key = jnp.where(causal, key, INT_MIN)
                acc = jnp.where(causal, acc, -jnp.inf)
            sc_ref[pl.ds(k0, sr), :] = key
            for j in range(sr // LANES):
                xj = acc[j * LANES:(j + 1) * LANES, :]
                m2 = jnp.maximum(m2, jnp.minimum(m1, xj))
                m1 = jnp.maximum(m1, xj)
        return m1, m2

    ninf = jnp.full((LANES, qb), -jnp.inf, F32)
    n_below = t0 // kc
    top2 = lax.fori_loop(0, n_below, functools.partial(score_body, masked=False), (ninf, ninf))
    m1, m2 = lax.fori_loop(n_below, nch, functools.partial(score_body, masked=True), top2)

    def count_ge(cand):
        def body(ch, acc):
            for j in range(kc // CNT_ROWS):
                kj = pl.multiple_of(ch * kc + j * CNT_ROWS, CNT_ROWS)
                acc = acc + jnp.where(sc_ref[pl.ds(kj, CNT_ROWS), :] >= cand, 1.0, 0.0)
            return acc
        acc = lax.fori_loop(0, nch, body, jnp.zeros((CNT_ROWS, qb), F32))
        return jnp.sum(acc, axis=0, keepdims=True)

    k_row = jnp.minimum(tq + 1, TOPK_MAX).astype(F32)
    hi0 = _float_key(jnp.max(m1, axis=0, keepdims=True))
    lo0 = jnp.minimum(_float_key(jnp.min(m2, axis=0, keepdims=True)), hi0)
    c_pos = count_ge(jnp.ones((1, qb), I32))
    c_nonneg = count_ge(jnp.zeros((1, qb), I32))
    at_zero = (c_pos < k_row) & (c_nonneg >= k_row)
    above = c_pos >= k_row
    lo0 = jnp.where(at_zero, 0, jnp.where(above, jnp.maximum(lo0, 1), lo0))
    n_above0 = jnp.where(at_zero, c_pos, jnp.where(above | (hi0 < 0), 0.0, c_nonneg))
    hi0 = jnp.where(at_zero, 0, jnp.where(above, hi0, jnp.minimum(hi0, -1)))
    lo0 = jnp.minimum(lo0, hi0)

    def open_rows(lo, hi):
        return jnp.max(jnp.where(lo < hi, 1.0, 0.0))

    def bis_body(st):
        lo, hi, n_above, _ = st
        for _ in range(BIS_STEPS):
            mid = (lo | hi) - ((lo ^ hi) >> 1)
            c = count_ge(mid)
            ge = c >= k_row
            exact = c == k_row
            lo, hi = jnp.where(ge, mid, lo), jnp.where(exact, mid, jnp.where(ge, hi, mid - 1))
            n_above = jnp.where(exact, -1.0, jnp.where(ge, n_above, c))
        return lo, hi, n_above, open_rows(lo, hi)

    thr, _, n_above, _ = lax.while_loop(lambda st: st[3] > 0.5, bis_body, (lo0, hi0, n_above0, open_rows(lo0, hi0)))

    dq = dq_ref[...]
    slc = slc_ref[...]
    qaug = []
    for h in range(nh):
        ql = _bdot(dq[:, h * HEAD_DIM:(h + 1) * HEAD_DIM], wuk_ref[h]) * HEAD_DIM ** -0.5
        qaug.append(jnp.concatenate([ql, jnp.broadcast_to(slc[h:h + 1, :], (qb, LANES))], axis=1))
    qaug_t = jnp.concatenate(qaug, axis=0).T.astype(BF16)
    acc_ref[...] = jnp.zeros_like(acc_ref)

    nsub = kc // sc_rows

    def sub_starts(ch):
        return [pl.multiple_of(ch * kc + sub * sc_rows, sc_rows) for sub in range(nsub)]

    def logits(k0):
        return _dot(kf_ref[pl.ds(k0, sc_rows), :], qaug_t)

    def attend(k0, lg_all, sel, m_old):
        ps, m_new = [], []
        for h in range(nh):
            cols = slice(h * qb, (h + 1) * qb)
            lg = jnp.where(sel, lg_all[:, cols], NEG)
            mh = jnp.maximum(m_old[:, cols], jnp.max(lg, axis=0, keepdims=True))
            ps.append(jnp.exp((lg - mh).astype(BF16)))
            m_new.append(mh)
        m_new = jnp.concatenate(m_new, axis=1)
        pv = _dot(vft_ref[:, pl.ds(k0, sc_rows)], jnp.concatenate(ps, axis=1))
        acc_ref[...] = jnp.exp(m_old - m_new) * acc_ref[...] + pv
        return m_new

    m_init = jnp.full((1, nh * qb), NEG, F32)

    need = jnp.where(n_above < 0, float(TOPK_MAX), k_row - n_above)
    tril = tril_ref[...]

    def body(ch, carry):
        tie_run, m_old = carry
        ks = sub_starts(ch)
        lgs = [logits(k0) for k0 in ks]
        keys = [sc_ref[pl.ds(k0, sc_rows), :] for k0 in ks]
        blocks = [slice(j * TIE_BLK, (j + 1) * TIE_BLK) for j in range(sc_rows // TIE_BLK)]
        prefs = [[_dot(tril, jnp.where(key[bl, :] == thr, 1.0, 0.0).astype(BF16)) for bl in blocks] for key in keys]
        for k0, lg, key, pref in zip(ks, lgs, keys, prefs):
            ranks = []
            for pj in pref:
                ranks.append(tie_run + pj)
                tie_run = tie_run + pj[TIE_BLK - 1:TIE_BLK, :]
            sel = (key > thr) | ((key == thr) & (jnp.concatenate(ranks, axis=0) <= need))
            m_old = attend(k0, lg, sel, m_old)
        return tie_run, m_old

    lax.fori_loop(0, nch, body, (jnp.zeros((1, qb), F32), m_init))

    acc = acc_ref[...]
    o_lat = acc[0:KV_LORA, :] / acc[KV_LORA:KV_LORA + 1, :]
    outs = [_bdot(o_lat[:, h * qb:(h + 1) * qb].T, wuv_ref[h]) for h in range(nh)]
    o_ref[...] = jnp.concatenate(outs, axis=1)


DSA_VROWS = KV_LORA + 16


def _dsa_mix(dq, iq, ikw, ckv, w_uk, w_uv, slopes):
    t = dq.shape[0]
    assert t <= LANES * 256
    ikn = ikw[:, 0:IDX_DIM]
    ik_hi, ik_lo = _split2(ikn)
    ikx = jnp.concatenate([ik_hi, ik_lo, ik_hi, ik_lo], axis=1)
    ckv_b = ckv.astype(BF16)
    pos = jnp.arange(t, dtype=I32)
    pa = (pos // LANES).astype(BF16)[:, None]
    pb = (pos % LANES).astype(BF16)[:, None]
    kf = jnp.concatenate([ckv_b, pa, pa, pa, pb, pb, pb, jnp.zeros((t, LANES - 6), BF16)], axis=1)
    vft = jnp.concatenate([ckv_b.T, jnp.ones((1, t), BF16), jnp.zeros((DSA_VROWS - KV_LORA - 1, t), BF16)], axis=0)
    cols = []
    for sl in slopes:
        for coef in (sl * LANES, sl):
            c_hi = jnp.asarray(coef, F32).astype(BF16)
            r1 = jnp.asarray(coef, F32) - c_hi.astype(F32)
            c_mid = r1.astype(BF16)
            c_lo = (r1 - c_mid.astype(F32)).astype(BF16)
            cols += [c_hi.astype(F32), c_mid.astype(F32), c_lo.astype(F32)]
    slc = jnp.stack(cols).reshape(DSA_HEADS, 6)
    slc = jnp.pad(slc, ((0, 8 - DSA_HEADS), (0, LANES - 6)))
    assert t % DSA_KC == 0
    kc = TIE_BLK
    tril = jnp.asarray((np.arange(kc)[:, None] >= np.arange(kc)[None, :]).astype(np.float32), BF16)
    row = lambda i: (i, 0)
    const2 = lambda i: (0, 0)
    const3 = lambda i: (0, 0, 0)
    resident = lambda shape: pl.BlockSpec(shape, const2, pipeline_mode=pl.Buffered(1))
    return pl.pallas_call(
        _dsa_kernel,
        grid=(t // DSA_QB,),
        in_specs=[
            pl.BlockSpec((DSA_QB, DSA_W), row),
            pl.BlockSpec((DSA_QB, IDX_HEADS * IDX_DIM), row),
            pl.BlockSpec((DSA_QB, LANES), row),
            resident((t, 4 * IDX_DIM)),
            resident((t, 2 * LANES)),
            resident((DSA_VROWS, t)),
            pl.BlockSpec((DSA_HEADS, HEAD_DIM, KV_LORA), const3),
            pl.BlockSpec((DSA_HEADS, KV_LORA, HEAD_DIM), const3),
            resident((kc, kc)),
            pl.BlockSpec((8, LANES), const2),
        ],
        out_specs=pl.BlockSpec((DSA_QB, DSA_W), row),
        out_shape=jax.ShapeDtypeStruct((t, DSA_W), F32),
        scratch_shapes=[
            pltpu.VMEM((t, DSA_QB), I32),
            pltpu.VMEM((DSA_VROWS, DSA_HEADS * DSA_QB), F32),
        ],
        compiler_params=_params(("arbitrary",)),
        name="dsa_mix",
    )(dq, iq, ikw, ikx, kf, vft, w_uk, w_uv, tril, slc)


def _swa_kernel(q_ref, kv_ref, kvp_ref, sink_ref, o_ref, *, slopes):
    i = pl.program_id(0)
    w = WINDOW
    gsz = SWA_HEADS // SWA_KV_HEADS
    q = q_ref[...]
    kv = kv_ref[...]
    kvp = kvp_ref[...]
    qi = _iota((w, 2 * w), 0)
    kj = _iota((w, 2 * w), 1)
    dist = qi + w - kj
    in_band = (dist >= 0) & (dist < w)
    valid = [in_band & ((kj >= w) | (i > 0))] + [in_band] * (SWA_BLOCKS - 1)
    distf = dist.astype(F32)
    sinks = sink_ref[...]
    rows = [slice(b * w, (b + 1) * w) for b in range(SWA_BLOCKS)]
    prev = [kvp] + [kv[rows[b], :] for b in range(SWA_BLOCKS - 1)]
    k2 = [[jnp.concatenate([prev[b][:, g * HEAD_DIM:(g + 1) * HEAD_DIM],
                            kv[rows[b], g * HEAD_DIM:(g + 1) * HEAD_DIM]], axis=0).astype(BF16)
           for g in range(SWA_KV_HEADS)] for b in range(SWA_BLOCKS)]
    v2 = [[jnp.concatenate([prev[b][:, w + g * HEAD_DIM:w + (g + 1) * HEAD_DIM],
                            kv[rows[b], w + g * HEAD_DIM:w + (g + 1) * HEAD_DIM]], axis=0).astype(BF16)
           for g in range(SWA_KV_HEADS)] for b in range(SWA_BLOCKS)]
    items = [(b, hd) for b in range(SWA_BLOCKS) for hd in range(SWA_HEADS)]
    s = [_dot_nt(q[rows[b], hd * HEAD_DIM:(hd + 1) * HEAD_DIM].astype(BF16), k2[b][hd // gsz]) * HEAD_DIM ** -0.5
         for b, hd in items]
    s = [jnp.where(valid[b], s[n] - slopes[hd] * distf, NEG) for n, (b, hd) in enumerate(items)]
    sink = [sinks[0:1, hd:hd + 1] for _, hd in items]
    m = [jnp.maximum(jnp.max(s[n], axis=1, keepdims=True), sink[n]) for n in range(len(items))]
    e = [jnp.exp(s[n] - m[n]) for n in range(len(items))]
    p = [e[n] / (jnp.sum(e[n], axis=1, keepdims=True) + jnp.exp(sink[n] - m[n])) for n in range(len(items))]
    outs = [_dot(p[n].astype(BF16), v2[b][hd // gsz]) for n, (b, hd) in enumerate(items)]
    for b in range(SWA_BLOCKS):
        o_ref[rows[b], :] = jnp.concatenate(outs[b * SWA_HEADS:(b + 1) * SWA_HEADS], axis=1)


SWA_BLOCKS = 4


def _swa_mix(sq, skv, sinks, slopes):
    t = sq.shape[0]
    w = WINDOW
    step = SWA_BLOCKS * w
    return pl.pallas_call(
        functools.partial(_swa_kernel, slopes=slopes),
        grid=(t // step,),
        in_specs=[
            pl.BlockSpec((step, SWA_W), lambda i: (i, 0)),
            pl.BlockSpec((step, 2 * w), lambda i: (i, 0)),
            pl.BlockSpec((w, 2 * w), lambda i: (jnp.maximum(SWA_BLOCKS * i - 1, 0), 0)),
            pl.BlockSpec((1, LANES), lambda i: (0, 0)),
        ],
        out_specs=pl.BlockSpec((step, SWA_W), lambda i: (i, 0)),
        out_shape=jax.ShapeDtypeStruct((t, SWA_W), F32),
        compiler_params=_params(("arbitrary",)),
        name="swa_mix",
    )(sq, skv, skv, sinks)


def _post_mix_kernel(x_ref, orw_ref, ods_ref, osw_ref, wout_ref, g1_ref, lng_ref, lnb_ref,
                     sc2_ref, sh2_ref, rwt_ref, rb_ref, tri_ref,
                     x1_ref, h2_ref, eidx_ref, rank_ref, gate_ref, cnt_ref, carry_ref):
    i = pl.program_id(0)

    @pl.when(i == 0)
    def _():
        carry_ref[...] = jnp.zeros_like(carry_ref)

    y = (_dot(orw_ref[...].astype(BF16), wout_ref[0:RWKV_W, :])
         + _dot(ods_ref[...].astype(BF16), wout_ref[RWKV_W:RWKV_W + DSA_W, :])
         + _dot(osw_ref[...].astype(BF16), wout_ref[RWKV_W + DSA_W:D_MODEL, :]))
    x1 = _layer_norm_rows(ALPHA * x_ref[...] + g1_ref[...] * y, lng_ref[...], lnb_ref[...])
    x1_ref[...] = x1
    h2 = x1 * (1.0 + sc2_ref[...]) + sh2_ref[...]
    h2_ref[...] = _pack_halves(h2)

    tm = h2.shape[0]
    ne = N_EXPERTS
    gs = ne // N_GROUPS
    scores = _sigmoid(_dot_nt(rwt_ref[...], h2, HI))
    sel = scores + rb_ref[...]
    sub = _iota((gs, tm), 0).astype(F32)
    gsc = []
    for j in range(N_GROUPS):
        gj = sel[j * gs:(j + 1) * gs, :]
        m1 = jnp.max(gj, axis=0, keepdims=True)
        f1 = jnp.min(jnp.where(gj == m1, sub, float(gs)), axis=0, keepdims=True)
        m2 = jnp.max(jnp.where(sub == f1, -jnp.inf, gj), axis=0, keepdims=True)
        gsc.append(m1 + m2)
    gsc = jnp.concatenate(gsc, axis=0)
    gid = _iota((N_GROUPS, tm), 0).astype(F32)
    gmask = jnp.zeros((N_GROUPS, tm), F32)
    for _ in range(TOPK_GROUPS):
        mx = jnp.max(gsc, axis=0, keepdims=True)
        fi = jnp.min(jnp.where(gsc == mx, gid, float(N_GROUPS)), axis=0, keepdims=True)
        pick = gid == fi
        gmask = jnp.where(pick, 1.0, gmask)
        gsc = jnp.where(pick, -jnp.inf, gsc)
    selm = jnp.concatenate(
        [jnp.where(gmask[j:j + 1, :] > 0.5, sel[j * gs:(j + 1) * gs, :], NEG) for j in range(N_GROUPS)], axis=0)
    eid = _iota((ne, tm), 0).astype(F32)
    gsel, eids = [], []
    chosen_f = jnp.zeros((ne, tm), F32)
    for _ in range(TOP_K):
        mx = jnp.max(selm, axis=0, keepdims=True)
        fi = jnp.min(jnp.where(selm == mx, eid, float(ne)), axis=0, keepdims=True)
        pick = eid == fi
        eids.append(fi)
        gsel.append(jnp.sum(jnp.where(pick, scores, 0.0), axis=0, keepdims=True))
        chosen_f = jnp.where(pick, 1.0, chosen_f)
        selm = jnp.where(pick, -jnp.inf, selm)
    gsum = gsel[0]
    for kx in range(1, TOP_K):
        gsum = gsum + gsel[kx]
    before = _dot(chosen_f.astype(BF16), tri_ref[...]) + carry_ref[:, 0:1]
    ranks = [jnp.sum(jnp.where(eid == eids[kx], before, 0.0), axis=0, keepdims=True) for kx in range(TOP_K)]
    eidx_ref[...] = jnp.concatenate(eids, axis=0).astype(I32)
    rank_ref[...] = jnp.concatenate(ranks, axis=0).astype(I32)
    gate_ref[...] = jnp.concatenate(gsel, axis=0) / gsum * ROUTED_SCALE
    carry_ref[...] = carry_ref[...] + jnp.sum(chosen_f, axis=1, keepdims=True)
    cnt_ref[...] = carry_ref[...]


def _post_mix(x, o_rw, o_ds, o_sw, w_out, g1, ln_g, ln_b, sc2, sh2, router_wt, router_b, tm=512):
    t, d = x.shape
    tri = (np.arange(tm)[:, None] < np.arange(tm)[None, :]).astype(np.float32)
    tri = jnp.asarray(tri, BF16)
    row = lambda i: (i, 0)
    const = lambda i: (0, 0)
    col = lambda i: (0, i)
    vec = pl.BlockSpec((1, d), const)
    return pl.pallas_call(
        _post_mix_kernel,
        grid=(t // tm,),
        in_specs=[
            pl.BlockSpec((tm, d), row),
            pl.BlockSpec((tm, RWKV_W), row),
            pl.BlockSpec((tm, DSA_W), row),
            pl.BlockSpec((tm, SWA_W), row),
            pl.BlockSpec((d, d), const),
            vec, vec, vec, vec, vec,
            pl.BlockSpec((N_EXPERTS, d), const),
            pl.BlockSpec((N_EXPERTS, 1), const),
            pl.BlockSpec((tm, tm), const),
        ],
        out_specs=[
            pl.BlockSpec((tm, d), row),
            pl.BlockSpec((tm, d // 2), row),
            pl.BlockSpec((TOP_K, tm), col),
            pl.BlockSpec((TOP_K, tm), col),
            pl.BlockSpec((TOP_K, tm), col),
            pl.BlockSpec((N_EXPERTS, LANES), const),
        ],
        out_shape=[
            jax.ShapeDtypeStruct((t, d), F32),
            jax.ShapeDtypeStruct((t, d // 2), I32),
            jax.ShapeDtypeStruct((TOP_K, t), I32),
            jax.ShapeDtypeStruct((TOP_K, t), I32),
            jax.ShapeDtypeStruct((TOP_K, t), F32),
            jax.ShapeDtypeStruct((N_EXPERTS, LANES), F32),
        ],
        scratch_shapes=[pltpu.VMEM((N_EXPERTS, LANES), F32)],
        compiler_params=_params(("arbitrary",)),
        name="post_mix_router",
    )(x, o_rw, o_ds, o_sw, w_out, g1, ln_g, ln_b, sc2, sh2, router_wt, router_b, tri)


MOE_ROWS = 512
MOE_TILE = 512


def _pack_halves(v):
    w = v.shape[1] // 2
    bits = lax.bitcast_convert_type(v.astype(BF16).astype(F32), I32)
    return bits[:, :w] | lax.shift_right_logical(bits[:, w:], 16)


def _unpack_halves(p):
    return lax.bitcast_convert_type(p & -65536, F32), lax.bitcast_convert_type(p << 16, F32)


def _row_copy(src_ref, src_row, dst_ref, dst_row, sem):
    return pltpu.make_async_copy(src_ref.at[pl.ds(src_row, 1), :], dst_ref.at[pl.ds(dst_row, 1), :], sem)


def _dispatch_kernel(slot_hbm, h_ref, xs_in, xs_out, slot_smem, sem_tab, sem_rows):
    del xs_in
    i = pl.program_id(0)
    tab = pltpu.make_async_copy(slot_hbm.at[i], slot_smem, sem_tab)
    tab.start()
    tab.wait()

    def issue(tt, carry):
        for kx in range(TOP_K):
            _row_copy(h_ref, tt, xs_out, slot_smem[kx, tt], sem_rows).start(priority=kx % 2)
        return carry

    lax.fori_loop(0, MOE_TILE, issue, 0)

    def drain(tt, carry):
        for kx in range(TOP_K):
            _row_copy(h_ref, 0, xs_out, 0, sem_rows).wait()
        return carry

    lax.fori_loop(0, MOE_TILE, drain, 0)


def _dispatch(slot_tiles, rows, cap, recycled=None):
    t, d = rows.shape
    xs0 = jnp.zeros((cap, d), rows.dtype) if recycled is None else recycled
    return pl.pallas_call(
        _dispatch_kernel,
        grid=(t // MOE_TILE,),
        in_specs=[
            pl.BlockSpec(memory_space=pl.ANY),
            pl.BlockSpec((MOE_TILE, d), lambda i: (i, 0)),
            pl.BlockSpec(memory_space=pl.ANY),
        ],
        out_specs=pl.BlockSpec(memory_space=pl.ANY),
        out_shape=jax.ShapeDtypeStruct((cap, d), rows.dtype),
        scratch_shapes=[
            pltpu.SMEM((TOP_K, MOE_TILE), I32),
            pltpu.SemaphoreType.DMA,
            pltpu.SemaphoreType.DMA,
        ],
        input_output_aliases={2: 0},
        compiler_params=_params(("arbitrary",)),
        name="moe_dispatch",
    )(slot_tiles, rows, xs0)


def _expert_kernel(be_ref, nb_ref, xs_ref, w1_ref, w3_ref, w2_ref, ys_ref, w1b, w3b, w2b):
    b = pl.program_id(0)
    changed = (b == 0) | (be_ref[b] != be_ref[jnp.maximum(b - 1, 0)])

    @pl.when(changed & (b < nb_ref[0]))
    def _():
        w1b[...] = w1_ref[0, 0].astype(BF16)
        w3b[...] = w3_ref[0, 0].astype(BF16)
        w2b[...] = w2_ref[0, 0].astype(BF16)

    @pl.when(b < nb_ref[0])
    def _():
        x_hi, x_lo = _unpack_halves(xs_ref[...])
        x_hi, x_lo = x_hi.astype(BF16), x_lo.astype(BF16)
        half = x_hi.shape[1]
        a = _dot(x_hi, w1b[0:half, :]) + _dot(x_lo, w1b[half:2 * half, :])
        gte = _dot(x_hi, w3b[0:half, :]) + _dot(x_lo, w3b[half:2 * half, :])
        hmid = (a * _sigmoid(a) * gte).astype(BF16)
        ys_ref[...] = _pack_halves(_dot(hmid, w2b[...]))

    @pl.when(b >= nb_ref[0])
    def _():
        ys_ref[...] = jnp.zeros_like(ys_ref)


def _experts(block_e, n_used, xs, w1, w3, w2, layer):
    cap, dp = xs.shape
    d = 2 * dp
    nb = cap // MOE_ROWS
    grid_spec = pltpu.PrefetchScalarGridSpec(
        num_scalar_prefetch=2,
        grid=(nb,),
        in_specs=[
            pl.BlockSpec((MOE_ROWS, dp), lambda b, be, nu: (b, 0)),
            pl.BlockSpec((1, 1, d, D_EXPERT), lambda b, be, nu: (layer, be[b], 0, 0)),
            pl.BlockSpec((1, 1, d, D_EXPERT), lambda b, be, nu: (layer, be[b], 0, 0)),
            pl.BlockSpec((1, 1, D_EXPERT, d), lambda b, be, nu: (layer, be[b], 0, 0)),
        ],
        out_specs=pl.BlockSpec((MOE_ROWS, dp), lambda b, be, nu: (b, 0)),
        scratch_shapes=[
            pltpu.VMEM((d, D_EXPERT), BF16),
            pltpu.VMEM((d, D_EXPERT), BF16),
            pltpu.VMEM((D_EXPERT, d), BF16),
        ],
    )
    return pl.pallas_call(
        _expert_kernel,
        grid_spec=grid_spec,
        out_shape=jax.ShapeDtypeStruct((cap, dp), I32),
        compiler_params=_params(("arbitrary",)),
        name="moe_experts",
    )(block_e, n_used, xs, w1, w3, w2)


def _combine_kernel(slot_hbm, ys_hbm, x1_ref, h2_ref, gate_ref, sw1_ref, sw3_ref, sw2_ref,
                    g2_ref, lng_ref, lnb_ref, o_ref, slot_smem, gbuf, sem_tab, sem_rows):
    i = pl.program_id(0)
    tab = pltpu.make_async_copy(slot_hbm.at[i], slot_smem, sem_tab)
    tab.start()
    tab.wait()

    def issue(tt, carry):
        for kx in range(TOP_K):
            _row_copy(ys_hbm, slot_smem[kx, tt], gbuf.at[kx], tt, sem_rows).start(priority=kx % 2)
        return carry

    lax.fori_loop(0, MOE_TILE, issue, 0)

    h_hi, h_lo = _unpack_halves(h2_ref[...])
    h_hi, h_lo = h_hi.astype(BF16), h_lo.astype(BF16)
    half = h_hi.shape[1]
    a = _dot(h_hi, sw1_ref[0:half, :]) + _dot(h_lo, sw1_ref[half:2 * half, :])
    gte = _dot(h_hi, sw3_ref[0:half, :]) + _dot(h_lo, sw3_ref[half:2 * half, :])
    y = _dot((a * _sigmoid(a) * gte).astype(BF16), sw2_ref[...])

    def drain(tt, carry):
        for kx in range(TOP_K):
            _row_copy(ys_hbm, 0, gbuf.at[kx], 0, sem_rows).wait()
        return carry

    lax.fori_loop(0, MOE_TILE, drain, 0)

    gates = gate_ref[...]
    r_hi = jnp.zeros((MOE_TILE, half), F32)
    r_lo = jnp.zeros((MOE_TILE, half), F32)
    for kx in range(TOP_K):
        e_hi, e_lo = _unpack_halves(gbuf[kx])
        r_hi = r_hi + gates[:, kx:kx + 1] * e_hi
        r_lo = r_lo + gates[:, kx:kx + 1] * e_lo
    y = y + jnp.concatenate([r_hi, r_lo], axis=1)
    o_ref[...] = _layer_norm_rows(ALPHA * x1_ref[...] + g2_ref[...] * y, lng_ref[...], lnb_ref[...])


def _combine(slot_tiles, ys, x1, h2, gates_t, sw1, sw3, sw2, g2, ln_g, ln_b):
    t, d = x1.shape
    row = lambda i: (i, 0)
    const = lambda i: (0, 0)
    vec = pl.BlockSpec((1, d), const)
    return pl.pallas_call(
        _combine_kernel,
        grid=(t // MOE_TILE,),
        in_specs=[
            pl.BlockSpec(memory_space=pl.ANY),
            pl.BlockSpec(memory_space=pl.ANY),
            pl.BlockSpec((MOE_TILE, d), row),
            pl.BlockSpec((MOE_TILE, d // 2), row),
            pl.BlockSpec((MOE_TILE, TOP_K), row),
            pl.BlockSpec((d, D_EXPERT), const),
            pl.BlockSpec((d, D_EXPERT), const),
            pl.BlockSpec((D_EXPERT, d), const),
            vec, vec, vec,
        ],
        out_specs=pl.BlockSpec((MOE_TILE, d), row),
        out_shape=jax.ShapeDtypeStruct((t, d), F32),
        scratch_shapes=[
            pltpu.SMEM((TOP_K, MOE_TILE), I32),
            pltpu.VMEM((TOP_K, MOE_TILE, d // 2), I32),
            pltpu.SemaphoreType.DMA,
            pltpu.SemaphoreType.DMA,
        ],
        compiler_params=_params(("arbitrary",)),
        name="moe_combine",
    )(slot_tiles, ys, x1, h2, gates_t, sw1, sw3, sw2, g2, ln_g, ln_b)


def _pad_w_in(w_in_l):
    d = w_in_l.shape[0]
    pad = jnp.zeros((d, C_SQ[0] - N_ORIG_BEFORE_PAD), w_in_l.dtype)
    return jnp.concatenate([w_in_l[:, :N_ORIG_BEFORE_PAD], pad, w_in_l[:, N_ORIG_BEFORE_PAD:]], axis=1)


def _pad_lanes(v, width=LANES):
    v = v.reshape(1, -1)
    return jnp.pad(v, ((0, 0), (0, width - v.shape[1])))


def _moe_tables(eidx, rank, counts):
    t = eidx.shape[1]
    cnt = counts[:, 0].astype(I32)
    padded = (cnt + MOE_ROWS - 1) // MOE_ROWS * MOE_ROWS
    pad_end = jnp.cumsum(padded)
    pad_start = pad_end - padded
    e_ids = jnp.arange(N_EXPERTS, dtype=I32)
    start_of = jnp.sum(jnp.where(eidx[..., None] == e_ids, pad_start, 0), axis=-1)
    slot = start_of + rank
    slot_tiles = slot.reshape(TOP_K, t // MOE_TILE, MOE_TILE).transpose(1, 0, 2)
    cap = t * TOP_K + N_EXPERTS * MOE_ROWS
    nb = cap // MOE_ROWS
    blk_row = jnp.arange(nb, dtype=I32)[:, None] * MOE_ROWS
    block_e = jnp.minimum(jnp.sum((pad_end[None, :] <= blk_row).astype(I32), axis=1), N_EXPERTS - 1)
    n_used = (pad_end[-1] // MOE_ROWS).astype(I32).reshape(1)
    return slot_tiles, block_e, n_used, cap


def kernel(x, c, w_mod, b_mod, w_in, rwkv_mu, rwkv_w0, rwkv_w2, rwkv_a0, rwkv_a2, rwkv_g2, rwkv_k_k, rwkv_k_a, rwkv_r_k, rwkv_ln_g, rwkv_ln_b, dsa_kv_norm, dsa_w_uk, dsa_w_uv, dsa_ik_g, dsa_ik_b, swa_sinks, w_out, ln_mix_g, ln_mix_b, router_w, router_bias, exp_w1, exp_w3, exp_w2, sh_w1, sh_w3, sh_w2, ln_ffn_g, ln_ffn_b):
    bsz, t, d = x.shape
    assert bsz == 1 and d == D_MODEL
    depth = w_mod.shape[0]
    n_sl = SWA_HEADS + DSA_HEADS
    slopes = [2.0 ** (-8.0 * (j + 1.0) / n_sl) for j in range(n_sl)]
    swa_slopes, dsa_slopes = slopes[:SWA_HEADS], slopes[SWA_HEADS:]

    mod = _modulation(c, w_mod, b_mod)
    xs_cur = x[0]
    xs_sorted = None
    row1 = lambda v: v.reshape(1, -1)
    for l in range(depth):
        sh1, sc1, g1, sh2, sc2, g2 = [mod[l, :, j * d:(j + 1) * d] for j in range(6)]
        wp = _pad_w_in(w_in[l])
        w_hi = wp.astype(BF16)
        w_idx = wp[:, C_IDX[0]:C_IDX[1]]
        w_idx_lo = (w_idx - w_idx.astype(BF16).astype(F32)).astype(BF16)
        rkv, lora, dq, ckv, iq, ikw, sq, skv = _input_proj(
            xs_cur, sc1, sh1, w_hi, w_idx_lo, row1(dsa_kv_norm[l]),
            _pad_lanes(dsa_ik_g[l]), _pad_lanes(dsa_ik_b[l]))
        o_rw = _rwkv_mix(rkv, lora, row1(rwkv_mu[l]), row1(rwkv_w0[l]), rwkv_w2[l], row1(rwkv_a0[l]),
                         rwkv_a2[l], rwkv_g2[l], row1(rwkv_k_k[l]), row1(rwkv_k_a[l]), row1(rwkv_r_k[l]),
                         row1(rwkv_ln_g[l]), row1(rwkv_ln_b[l]))
        o_ds = _dsa_mix(dq, iq, ikw, ckv, dsa_w_uk[l], dsa_w_uv[l], dsa_slopes)
        o_sw = _swa_mix(sq, skv, _pad_lanes(swa_sinks[l]), swa_slopes)
        x1, h2, eidx, rank, gates, counts = _post_mix(
            xs_cur, o_rw, o_ds, o_sw, w_out[l].astype(BF16), g1, row1(ln_mix_g[l]), row1(ln_mix_b[l]),
            sc2, sh2, router_w[l].T, router_bias[l].reshape(-1, 1))
        slot_tiles, block_e, n_used, cap = _moe_tables(eidx, rank, counts)
        xs_sorted = _dispatch(slot_tiles, h2, cap, xs_sorted)
        ys = _experts(block_e, n_used, xs_sorted, exp_w1, exp_w3, exp_w2, l)
        xs_cur = _combine(slot_tiles, ys, x1, h2, gates.T, sh_w1[l].astype(BF16), sh_w3[l].astype(BF16),
                          sh_w2[l].astype(BF16), g2, row1(ln_ffn_g[l]), row1(ln_ffn_b[l]))
    return xs_cur[None]
```

```python
import functools
import math

import jax
import jax.numpy as jnp
import numpy as np
from jax import lax
from jax.experimental import pallas as pl
from jax.experimental.pallas import tpu as pltpu

F32 = jnp.float32
BF16 = jnp.bfloat16
I32 = jnp.int32
HI = lax.Precision.HIGHEST

D_MODEL = 1024
DEPTH = 4
HEAD_DIM = 64
RWKV_HEADS = 6
DSA_HEADS = 4
SWA_HEADS = 6
SWA_KV_HEADS = 2
RWKV_W = RWKV_HEADS * HEAD_DIM
DSA_W = DSA_HEADS * HEAD_DIM
SWA_W = SWA_HEADS * HEAD_DIM
DECAY_LORA = 64
AAA_LORA = 64
GATE_LORA = 128
GN_EPS = 64e-5
KV_LORA = 128
IDX_HEADS = 4
IDX_DIM = 64
TOPK_MAX = 256
WINDOW = 128
N_EXPERTS = 64
TOP_K = 8
N_GROUPS = 8
TOPK_GROUPS = 4
D_EXPERT = 256
ROUTED_SCALE = 2.5
ALPHA = (2 * DEPTH) ** 0.25
LN_EPS = 1e-5
NEG = -1e30
INT_MIN = -(2 ** 31)

LANES = 128
VMEM_LIMIT = 56 * 1024 * 1024

C_RKV = (0, 1152)
C_LORA = (1152, 1408)
C_DQ = (1408, 1664)
C_CKV = (1664, 1792)
C_IDX = (1792, 2176)
C_SQ = (2176, 2560)
C_SKV = (2560, 2816)
P_PAD = 2816
N_ORIG_BEFORE_PAD = 2116


def _dot(a, b, prec=None):
    return jnp.dot(a, b, preferred_element_type=F32, precision=prec)


def _dot_nt(a, b, prec=None):
    return lax.dot_general(a, b, (((1,), (1,)), ((), ())), preferred_element_type=F32, precision=prec)


def _split2(a):
    a_hi = a.astype(BF16)
    return a_hi, (a - a_hi.astype(F32)).astype(BF16)


def _bdot(a, b):
    return _dot(a.astype(BF16), b.astype(BF16))


def _bdot_nt(a, b):
    return _dot_nt(a.astype(BF16), b.astype(BF16))


def _dot2(a, b_exact):
    a_hi, a_lo = _split2(a)
    return _dot(a_hi, b_exact) + _dot(a_lo, b_exact)


def _dot2_l(a_exact, b):
    b_hi, b_lo = _split2(b)
    return _dot(a_exact, b_hi) + _dot(a_exact, b_lo)


def _dot3(a, b):
    a_hi, a_lo = _split2(a)
    b_hi, b_lo = _split2(b)
    return _dot(a_hi, b_hi) + (_dot(a_lo, b_hi) + _dot(a_hi, b_lo))


def _iota(shape, dim):
    return lax.broadcasted_iota(I32, shape, dim)


def _sigmoid(x):
    return 1.0 / (1.0 + jnp.exp(-x))


def _layer_norm_rows(v, g, b):
    mu = jnp.mean(v, axis=-1, keepdims=True)
    d = v - mu
    var = jnp.mean(d * d, axis=-1, keepdims=True)
    return d * lax.rsqrt(var + LN_EPS) * g + b


def _params(sem):
    return pltpu.CompilerParams(dimension_semantics=sem, vmem_limit_bytes=VMEM_LIMIT)


def _mod_kernel(c_ref, w_ref, b_ref, o_ref):
    c = c_ref[...]
    cond = c * _sigmoid(c)
    o_ref[0] = _dot(cond, w_ref[0], HI) + b_ref[0]


def _modulation(c, w_mod, b_mod):
    depth, d, d6 = w_mod.shape
    c8 = jnp.broadcast_to(c, (8, d))
    nj = d6 // d
    out = pl.pallas_call(
        _mod_kernel,
        grid=(depth, nj),
        in_specs=[
            pl.BlockSpec((8, d), lambda l, j: (0, 0)),
            pl.BlockSpec((1, d, d), lambda l, j: (l, 0, j)),
            pl.BlockSpec((1, 1, d), lambda l, j: (l, 0, j)),
        ],
        out_specs=pl.BlockSpec((1, 8, d), lambda l, j: (l, 0, j)),
        out_shape=jax.ShapeDtypeStruct((depth, 8, d6), F32),
        compiler_params=_params(("arbitrary", "arbitrary")),
        name="modulation",
    )(c8, w_mod, b_mod.reshape(depth, 1, d6))
    return out[:, 0:1, :]


def _proj_kernel(x_ref, sc_ref, sh_ref, w_ref, wlo_ref, kvn_ref, ikg_ref, ikb_ref,
                 rkv_ref, lora_ref, dq_ref, ckv_ref, iq_ref, ikw_ref, sq_ref, skv_ref):
    h = x_ref[...] * (1.0 + sc_ref[...]) + sh_ref[...]
    hb = h.astype(BF16)
    hl = (h - hb.astype(F32)).astype(BF16)

    def mm(c):
        return _dot(hb, w_ref[:, c[0]:c[1]])

    rkv_ref[...] = mm(C_RKV)
    lora_ref[...] = mm(C_LORA)
    dq_ref[...] = mm(C_DQ)
    sq_ref[...] = mm(C_SQ)
    skv_ref[...] = mm(C_SKV)
    ckv = mm(C_CKV)
    ckv_ref[...] = ckv * lax.rsqrt(jnp.mean(ckv * ckv, axis=-1, keepdims=True) + 1e-6) * kvn_ref[...]
    idx = mm(C_IDX) + _dot(hl, w_ref[:, C_IDX[0]:C_IDX[1]]) + _dot(hb, wlo_ref[...])
    iq_ref[...] = idx[:, 0:256]
    g3 = idx[:, 256:384]
    lane = _iota(g3.shape, 1)
    isk = lane < IDX_DIM
    mu = jnp.sum(jnp.where(isk, g3, 0.0), axis=-1, keepdims=True) * (1.0 / IDX_DIM)
    dk = jnp.where(isk, g3 - mu, 0.0)
    var = jnp.sum(dk * dk, axis=-1, keepdims=True) * (1.0 / IDX_DIM)
    ikn = dk * lax.rsqrt(var + LN_EPS) * ikg_ref[...] + ikb_ref[...]
    ikw_ref[...] = jnp.where(isk, ikn, g3 * (IDX_HEADS ** -0.5 * IDX_DIM ** -0.5))


def _input_proj(x, sc, sh, w_hi, w_idx_lo, kvn, ikg, ikb, tm=512):
    t, d = x.shape
    widths = [C_RKV, C_LORA, C_DQ, C_CKV, (0, 256), (0, 128), C_SQ, C_SKV]
    widths = [c[1] - c[0] for c in widths]
    const = lambda i: (0, 0)
    row = lambda i: (i, 0)
    return pl.pallas_call(
        _proj_kernel,
        grid=(t // tm,),
        in_specs=[
            pl.BlockSpec((tm, d), row),
            pl.BlockSpec((1, d), const),
            pl.BlockSpec((1, d), const),
            pl.BlockSpec((d, P_PAD), const),
            pl.BlockSpec((d, C_IDX[1] - C_IDX[0]), const),
            pl.BlockSpec((1, KV_LORA), const),
            pl.BlockSpec((1, LANES), const),
            pl.BlockSpec((1, LANES), const),
        ],
        out_specs=[pl.BlockSpec((tm, w), row) for w in widths],
        out_shape=[jax.ShapeDtypeStruct((t, w), F32) for w in widths],
        compiler_params=_params(("arbitrary",)),
        name="input_proj",
    )(x, sc, sh, w_hi, w_idx_lo, kvn, ikg, ikb)


RW_CHUNK = 64
RW_UNROLL = 8


def _rwkv_kernel(r_ref, k_ref, v_ref, lora_ref, rp_ref, kp_ref, vp_ref, lp_ref,
                 mur_ref, muk_ref, muv_ref, mul_ref, w0_ref, w2_ref, a0_ref, a2_ref, g2_ref,
                 kk_ref, ka_ref, rk_ref, lng_ref, lnb_ref, o_ref,
                 h_ref, y_ref, st_ref, wm_ref, ar_ref, rs_ref, vs_ref, lt_ref, zm_ref, y0_ref, gc_ref, *, tg):
    g = pl.program_id(0)
    c64 = RW_CHUNK
    nch = tg // c64
    npair = RWKV_W // LANES
    pair_lanes = [slice(p * LANES, (p + 1) * LANES) for p in range(npair)]
    lane = _iota((1, LANES), 1)
    first = g == 0

    @pl.when(first)
    def _():
        h_ref[...] = jnp.zeros_like(h_ref)

    rowid = _iota((tg, 1), 0)

    def shift_mix(cur_ref, prev_ref, mu_ref):
        cur = cur_ref[...]
        prev_row = jnp.where(first, 0.0, prev_ref[7:8, :])
        rolled = pltpu.roll(cur, 1, 0)
        shifted = jnp.where(rowid == 0, prev_row, rolled)
        return cur + (shifted - cur) * mu_ref[...]

    r = shift_mix(r_ref, rp_ref, mur_ref)
    k = shift_mix(k_ref, kp_ref, muk_ref)
    v = shift_mix(v_ref, vp_ref, muv_ref)
    lo = shift_mix(lora_ref, lp_ref, mul_ref)
    wl = lo[:, 0:DECAY_LORA]
    al = lo[:, DECAY_LORA:DECAY_LORA + AAA_LORA]
    gl = lo[:, 128:256]

    zw = -(w0_ref[...] + _dot3(jnp.tanh(wl), w2_ref[...]))
    softplus = jnp.maximum(zw, 0.0) + jnp.log(1.0 + jnp.exp(-jnp.abs(zw)))
    lw = -jnp.exp(-softplus - 0.5)
    a = _sigmoid(a0_ref[...] + _bdot(al, a2_ref[...]))
    gate = _bdot(_sigmoid(gl), g2_ref[...])

    ri = _iota((LANES, LANES), 0) // HEAD_DIM
    ci = _iota((LANES, LANES), 1) // HEAD_DIM
    bones = jnp.where(ri == ci, 1.0, 0.0).astype(BF16)

    def head_sum(xf):
        return jnp.concatenate([_dot2(xf[:, pl_], bones) for pl_ in pair_lanes], axis=1)

    kk = k * kk_ref[...]
    kk = kk / jnp.maximum(jnp.sqrt(head_sum(kk * kk)), 1e-12)
    k2 = k * (1.0 + (a - 1.0) * ka_ref[...])
    bonus = head_sum(r * k2 * rk_ref[...]) * v
    bvec = a * kk

    st_ref[0] = r
    st_ref[1] = k2
    st_ref[2] = v
    st_ref[3] = lw
    st_ref[4] = kk
    st_ref[5] = bvec

    rr = _iota((LANES, LANES), 0)
    cc = _iota((LANES, LANES), 1)
    same = (rr // c64) == (cc // c64)
    strict = same & ((rr % c64) > (cc % c64))
    incl = same & ((rr % c64) >= (cc % c64))
    eye = jnp.where(rr == cc, 1.0, 0.0)
    tril = jnp.where(_iota((c64, c64), 0) >= _iota((c64, c64), 1), 1.0, 0.0).astype(BF16)
    lo_half = lane < HEAD_DIM

    def stack(xc):
        return jnp.concatenate([jnp.where(lo_half, xc, 0.0), jnp.where(lo_half, 0.0, xc)], axis=0)

    def prepare(c, carry):
        chunks = [c * RW_UNROLL + j for j in range(RW_UNROLL)]
        sls = [pl.ds(pl.multiple_of(cj * c64, c64), c64) for cj in chunks]
        items = [(p, j) for j in range(RW_UNROLL) for p in range(npair)]
        pairs = range(len(items))
        idx = [p * nch + chunks[j] for p, j in items]
        ld = lambda q: [st_ref[q, sls[j], pair_lanes[p]] for p, j in items]
        rc, kc, vc, lwc, kkc, bc = ld(0), ld(1), ld(2), ld(3), ld(4), ld(5)
        cum = [_dot2_l(tril, lwc[p]) for p in pairs]
        tot = [cum[p][c64 - 1:c64, :] for p in pairs]
        g_in = [jnp.exp(cum[p]) for p in pairs]
        g_ex = [jnp.exp(cum[p] - lwc[p]) for p in pairs]
        g_inv = [jnp.exp(-cum[p]) for p in pairs]
        g_rest = [jnp.exp(tot[p] - cum[p]) for p in pairs]
        a_s = [stack(-kkc[p] * g_ex[p]).astype(BF16) for p in pairs]
        b_s = [stack(bc[p] * g_inv[p]).astype(BF16) for p in pairs]
        k_s = [stack(kc[p] * g_inv[p]).astype(BF16) for p in pairs]
        r_s = [stack(rc[p] * g_in[p]).astype(BF16) for p in pairs]
        v_s = [stack(vc[p]).astype(BF16) for p in pairs]
        nmat = [jnp.where(strict, _dot_nt(a_s[p], b_s[p]), 0.0) for p in pairs]
        aak = [jnp.where(strict, _dot_nt(a_s[p], k_s[p]), 0.0) for p in pairs]
        arb = [jnp.where(incl, _dot_nt(r_s[p], b_s[p]), 0.0) for p in pairs]
        ark = [jnp.where(incl, _dot_nt(r_s[p], k_s[p]), 0.0) for p in pairs]
        tinv = [eye + nmat[p] for p in pairs]
        pw = nmat
        for _ in range(5):
            pw = [_bdot(pw[p], pw[p]) for p in pairs]
            tinv = [_bdot(tinv[p], eye + pw[p]) for p in pairs]
        tinv = [tinv[p].astype(BF16) for p in pairs]
        akv = [_bdot(aak[p], v_s[p]).astype(BF16) for p in pairs]
        wmat = [_dot(tinv[p], a_s[p]) for p in pairs]
        zmat = [_dot(tinv[p], akv[p]) for p in pairs]
        y0 = [_bdot(ark[p], v_s[p]) for p in pairs]
        for p in pairs:
            wm_ref[idx[p]] = wmat[p].astype(BF16)
            zm_ref[idx[p]] = zmat[p]
            y0_ref[idx[p]] = y0[p]
            ar_ref[idx[p]] = arb[p].astype(BF16)
            rs_ref[idx[p]] = r_s[p]
            vs_ref[idx[p]] = v_s[p]
            lt_ref[idx[p]] = jnp.concatenate([stack(bc[p] * g_rest[p]), stack(kc[p] * g_rest[p])],
                                             axis=0).T.astype(BF16)
            gc_ref[idx[p]] = jnp.broadcast_to(jnp.sum(eye * jnp.exp(tot[p]), axis=1, keepdims=True),
                                              (LANES, LANES))
        return carry

    lax.fori_loop(0, nch // RW_UNROLL, prepare, 0)

    def advance(c, carry):
        sl = pl.ds(pl.multiple_of(c * c64, c64), c64)
        pairs = range(npair)
        idx = [p * nch + c for p in pairs]
        hst = [h_ref[p] for p in pairs]
        hb = [hst[p].astype(BF16) for p in pairs]
        u = [_dot(wm_ref[idx[p]], hb[p]) + zm_ref[idx[p]] for p in pairs]
        rh = [_dot(rs_ref[idx[p]], hb[p]) for p in pairs]
        ub = [u[p].astype(BF16) for p in pairs]
        hnew = [_dot(lt_ref[idx[p]], jnp.concatenate([ub[p], vs_ref[idx[p]]], axis=0)) for p in pairs]
        au = [_dot(ar_ref[idx[p]], ub[p]) for p in pairs]
        for p in pairs:
            h_ref[p] = gc_ref[idx[p]] * hst[p] + hnew[p]
            ys = rh[p] + au[p] + y0_ref[idx[p]]
            y_ref[sl, pair_lanes[p]] = ys[0:c64, :] + ys[c64:2 * c64, :]
        return carry

    lax.fori_loop(0, nch, advance, 0)

    y = y_ref[...]
    mean = head_sum(y) * (1.0 / HEAD_DIM)
    dy = y - mean
    var = head_sum(dy * dy) * (1.0 / HEAD_DIM)
    o = dy * lax.rsqrt(var + GN_EPS) * lng_ref[...] + lnb_ref[...]
    o_ref[...] = (o + bonus) * gate


def _rwkv_mix(rkv, lora, mu, w0, w2, a0, a2, g2, k_k, k_a, r_k, ln_g, ln_b, tg=512):
    t = rkv.shape[0]
    w = RWKV_W
    npair = w // LANES
    nmat = npair * (tg // RW_CHUNK)
    mu_r, mu_k, mu_v, mu_l = mu[:, 0:w], mu[:, w:2 * w], mu[:, 2 * w:3 * w], mu[:, 3 * w:3 * w + 256]
    blk = lambda off: pl.BlockSpec((tg, w), lambda g: (g, off))
    prev = lambda off: pl.BlockSpec((8, w), lambda g: (jnp.maximum(g * (tg // 8) - 1, 0), off))
    vec = pl.BlockSpec((1, w), lambda g: (0, 0))
    full = lambda rows: pl.BlockSpec((rows, w), lambda g: (0, 0))
    return pl.pallas_call(
        functools.partial(_rwkv_kernel, tg=tg),
        grid=(t // tg,),
        in_specs=[
            blk(0), blk(1), blk(2),
            pl.BlockSpec((tg, 256), lambda g: (g, 0)),
            prev(0), prev(1), prev(2),
            pl.BlockSpec((8, 256), lambda g: (jnp.maximum(g * (tg // 8) - 1, 0), 0)),
            vec, vec, vec,
            pl.BlockSpec((1, 256), lambda g: (0, 0)),
            vec, full(DECAY_LORA), vec, full(AAA_LORA), full(GATE_LORA),
            vec, vec, vec, vec, vec,
        ],
        out_specs=pl.BlockSpec((tg, w), lambda g: (g, 0)),
        out_shape=jax.ShapeDtypeStruct((t, w), F32),
        scratch_shapes=[
            pltpu.VMEM((npair, LANES, LANES), F32),
            pltpu.VMEM((tg, w), F32),
            pltpu.VMEM((6, tg, w), F32),
            pltpu.VMEM((nmat, LANES, LANES), BF16),
            pltpu.VMEM((nmat, LANES, LANES), BF16),
            pltpu.VMEM((nmat, LANES, LANES), BF16),
            pltpu.VMEM((nmat, LANES, LANES), BF16),
            pltpu.VMEM((nmat, LANES, 2 * LANES), BF16),
            pltpu.VMEM((nmat, LANES, LANES), F32),
            pltpu.VMEM((nmat, LANES, LANES), F32),
            pltpu.VMEM((nmat, LANES, LANES), F32),
        ],
        compiler_params=_params(("arbitrary",)),
        name="rwkv7_mix",
    )(rkv, rkv, rkv, lora, rkv, rkv, rkv, lora,
      mu_r, mu_k, mu_v, mu_l, w0, w2, a0, a2, g2, k_k, k_a, r_k, ln_g, ln_b)


DSA_QB = 256
DSA_KC = 1024
DSA_SUB = 512
CNT_ROWS = 64
TIE_BLK = 128
BIS_STEPS = 2


def _float_key(v):
    bits = lax.bitcast_convert_type(v, I32)
    return bits ^ ((bits >> 31) & 0x7FFFFFFF)


def _dsa_kernel(dq_ref, iq_ref, ikw_ref, ikx_ref, kf_ref, vft_ref, wuk_ref, wuv_ref, tril_ref, slc_ref,
                o_ref, sc_ref, acc_ref):
    i = pl.program_id(0)
    qb, kc, sc_rows = DSA_QB, DSA_KC, DSA_SUB
    nh = DSA_HEADS
    t0 = i * qb
    nch = (t0 + qb + kc - 1) // kc
    tq = t0 + _iota((1, qb), 1)

    iq = iq_ref[...]
    iq_hi = iq.astype(BF16).astype(F32)
    iq_lo = iq - iq_hi
    lhs = []
    for h in range(IDX_HEADS):
        s = slice(h * IDX_DIM, (h + 1) * IDX_DIM)
        lhs.append(jnp.concatenate([iq_hi[:, s], iq_hi[:, s], iq_lo[:, s], iq_lo[:, s]], axis=1))
    lhs_t = jnp.concatenate(lhs, axis=0).T.astype(BF16)
    ikw_t = ikw_ref[...].T
    iw = [ikw_t[IDX_DIM + h:IDX_DIM + h + 1, :] for h in range(IDX_HEADS)]

    def score_body(ch, carry, masked):
        m1, m2 = carry
        sr = sc_rows
        for sub in range(kc // sr):
            k0 = pl.multiple_of(ch * kc + sub * sr, sr)
            s_all = _dot(ikx_ref[pl.ds(k0, sr), :], lhs_t)
            acc = jnp.zeros((sr, qb), F32)
            for h in range(IDX_HEADS):
                acc = acc + jnp.maximum(s_all[:, h * qb:(h + 1) * qb], 0.0) * iw[h]
            acc = jnp.where(acc == 0.0, 0.0, acc)
            key = _float_key(acc)
            if masked:
                causal = (k0 + _iota((sr, 1), 0)) <= tq
                key = jnp.where(causal, key, INT_MIN)
                acc = jnp.where(causal, acc, -jnp.inf)
            sc_ref[pl.ds(k0, sr), :] = key
            for j in range(sr // LANES):
                xj = acc[j * LANES:(j + 1) * LANES, :]
                m2 = jnp.maximum(m2, jnp.minimum(m1, xj))
                m1 = jnp.maximum(m1, xj)
        return m1, m2

    ninf = jnp.full((LANES, qb), -jnp.inf, F32)
    n_below = t0 // kc
    top2 = lax.fori_loop(0, n_below, functools.partial(score_body, masked=False), (ninf, ninf))
    m1, m2 = lax.fori_loop(n_below, nch, functools.partial(score_body, masked=True), top2)

    def count_ge(cand):
        def body(ch, acc):
            for j in range(kc // CNT_ROWS):
                kj = pl.multiple_of(ch * kc + j * CNT_ROWS, CNT_ROWS)
                acc = acc + jnp.where(sc_ref[pl.ds(kj, CNT_ROWS), :] >= cand, 1.0, 0.0)
            return acc
        acc = lax.fori_loop(0, nch, body, jnp.zeros((CNT_ROWS, qb), F32))
        return jnp.sum(acc, axis=0, keepdims=True)

    k_row = jnp.minimum(tq + 1, TOPK_MAX).astype(F32)
    hi0 = _float_key(jnp.max(m1, axis=0, keepdims=True))
    lo0 = jnp.minimum(_float_key(jnp.min(m2, axis=0, keepdims=True)), hi0)
    c_pos = count_ge(jnp.ones((1, qb), I32))
    c_nonneg = count_ge(jnp.zeros((1, qb), I32))
    at_zero = (c_pos < k_row) & (c_nonneg >= k_row)
    above = c_pos >= k_row
    lo0 = jnp.where(at_zero, 0, jnp.where(above, jnp.maximum(lo0, 1), lo0))
    n_above0 = jnp.where(at_zero, c_pos, jnp.where(above | (hi0 < 0), 0.0, c_nonneg))
    hi0 = jnp.where(at_zero, 0, jnp.where(above, hi0, jnp.minimum(hi0, -1)))
    lo0 = jnp.minimum(lo0, hi0)

    def open_rows(lo, hi):
        return jnp.max(jnp.where(lo < hi, 1.0, 0.0))

    def bis_body(st):
        lo, hi, n_above, _ = st
        for _ in range(BIS_STEPS):
            mid = (lo | hi) - ((lo ^ hi) >> 1)
            c = count_ge(mid)
            ge = c >= k_row
            exact = c == k_row
            lo, hi = jnp.where(ge, mid, lo), jnp.where(exact, mid, jnp.where(ge, hi, mid - 1))
            n_above = jnp.where(exact, -1.0, jnp.where(ge, n_above, c))
        return lo, hi, n_above, open_rows(lo, hi)

    thr, _, n_above, _ = lax.while_loop(lambda st: st[3] > 0.5, bis_body, (lo0, hi0, n_above0, open_rows(lo0, hi0)))

    dq = dq_ref[...]
    slc = slc_ref[...]
    qaug = []
    for h in range(nh):
        ql = _bdot(dq[:, h * HEAD_DIM:(h + 1) * HEAD_DIM], wuk_ref[h]) * HEAD_DIM ** -0.5
        qaug.append(jnp.concatenate([ql, jnp.broadcast_to(slc[h:h + 1, :], (qb, LANES))], axis=1))
    qaug_t = jnp.concatenate(qaug, axis=0).T.astype(BF16)
    acc_ref[...] = jnp.zeros_like(acc_ref)

    nsub = kc // sc_rows

    def sub_starts(ch):
        return [pl.multiple_of(ch * kc + sub * sc_rows, sc_rows) for sub in range(nsub)]

    def logits(k0):
        return _dot(kf_ref[pl.ds(k0, sc_rows), :], qaug_t)

    def attend(k0, lg_all, sel, m_old):
        ps, m_new = [], []
        for h in range(nh):
            cols = slice(h * qb, (h + 1) * qb)
            lg = jnp.where(sel, lg_all[:, cols], NEG)
            mh = jnp.maximum(m_old[:, cols], jnp.max(lg, axis=0, keepdims=True))
            ps.append(jnp.exp((lg - mh).astype(BF16)))
            m_new.append(mh)
        m_new = jnp.concatenate(m_new, axis=1)
        pv = _dot(vft_ref[:, pl.ds(k0, sc_rows)], jnp.concatenate(ps, axis=1))
        acc_ref[...] = jnp.exp(m_old - m_new) * acc_ref[...] + pv
        return m_new

    m_init = jnp.full((1, nh * qb), NEG, F32)

    need = jnp.where(n_above < 0, float(TOPK_MAX), k_row - n_above)
    tril = tril_ref[...]

    def body(ch, carry):
        tie_run, m_old = carry
        ks = sub_starts(ch)
        lgs = [logits(k0) for k0 in ks]
        keys = [sc_ref[pl.ds(k0, sc_rows), :] for k0 in ks]
        blocks = [slice(j * TIE_BLK, (j + 1) * TIE_BLK) for j in range(sc_rows // TIE_BLK)]
        prefs = [[_dot(tril, jnp.where(key[bl, :] == thr, 1.0, 0.0).astype(BF16)) for bl in blocks] for key in keys]
        for k0, lg, key, pref in zip(ks, lgs, keys, prefs):
            ranks = []
            for pj in pref:
                ranks.append(tie_run + pj)
                tie_run = tie_run + pj[TIE_BLK - 1:TIE_BLK, :]
            sel = (key > thr) | ((key == thr) & (jnp.concatenate(ranks, axis=0) <= need))
            m_old = attend(k0, lg, sel, m_old)
        return tie_run, m_old

    lax.fori_loop(0, nch, body, (jnp.zeros((1, qb), F32), m_init))

    acc = acc_ref[...]
    o_lat = acc[0:KV_LORA, :] / acc[KV_LORA:KV_LORA + 1, :]
    outs = [_bdot(o_lat[:, h * qb:(h + 1) * qb].T, wuv_ref[h]) for h in range(nh)]
    o_ref[...] = jnp.concatenate(outs, axis=1)


DSA_VROWS = KV_LORA + 16


def _dsa_mix(dq, iq, ikw, ckv, w_uk, w_uv, slopes):
    t = dq.shape[0]
    assert t <= LANES * 256
    ikn = ikw[:, 0:IDX_DIM]
    ik_hi, ik_lo = _split2(ikn)
    ikx = jnp.concatenate([ik_hi, ik_lo, ik_hi, ik_lo], axis=1)
    ckv_b = ckv.astype(BF16)
    pos = jnp.arange(t, dtype=I32)
    pa = (pos // LANES).astype(BF16)[:, None]
    pb = (pos % LANES).astype(BF16)[:, None]
    kf = jnp.concatenate([ckv_b, pa, pa, pa, pb, pb, pb, jnp.zeros((t, LANES - 6), BF16)], axis=1)
    vft = jnp.concatenate([ckv_b.T, jnp.ones((1, t), BF16), jnp.zeros((DSA_VROWS - KV_LORA - 1, t), BF16)], axis=0)
    cols = []
    for sl in slopes:
        for coef in (sl * LANES, sl):
            c_hi = jnp.asarray(coef, F32).astype(BF16)
            r1 = jnp.asarray(coef, F32) - c_hi.astype(F32)
            c_mid = r1.astype(BF16)
            c_lo = (r1 - c_mid.astype(F32)).astype(BF16)
            cols += [c_hi.astype(F32), c_mid.astype(F32), c_lo.astype(F32)]
    slc = jnp.stack(cols).reshape(DSA_HEADS, 6)
    slc = jnp.pad(slc, ((0, 8 - DSA_HEADS), (0, LANES - 6)))
    assert t % DSA_KC == 0
    kc = TIE_BLK
    tril = jnp.asarray((np.arange(kc)[:, None] >= np.arange(kc)[None, :]).astype(np.float32), BF16)
    row = lambda i: (i, 0)
    const2 = lambda i: (0, 0)
    const3 = lambda i: (0, 0, 0)
    resident = lambda shape: pl.BlockSpec(shape, const2, pipeline_mode=pl.Buffered(1))
    return pl.pallas_call(
        _dsa_kernel,
        grid=(t // DSA_QB,),
        in_specs=[
            pl.BlockSpec((DSA_QB, DSA_W), row),
            pl.BlockSpec((DSA_QB, IDX_HEADS * IDX_DIM), row),
            pl.BlockSpec((DSA_QB, LANES), row),
            resident((t, 4 * IDX_DIM)),
            resident((t, 2 * LANES)),
            resident((DSA_VROWS, t)),
            pl.BlockSpec((DSA_HEADS, HEAD_DIM, KV_LORA), const3),
            pl.BlockSpec((DSA_HEADS, KV_LORA, HEAD_DIM), const3),
            resident((kc, kc)),
            pl.BlockSpec((8, LANES), const2),
        ],
        out_specs=pl.BlockSpec((DSA_QB, DSA_W), row),
        out_shape=jax.ShapeDtypeStruct((t, DSA_W), F32),
        scratch_shapes=[
            pltpu.VMEM((t, DSA_QB), I32),
            pltpu.VMEM((DSA_VROWS, DSA_HEADS * DSA_QB), F32),
        ],
        compiler_params=_params(("arbitrary",)),
        name="dsa_mix",
    )(dq, iq, ikw, ikx, kf, vft, w_uk, w_uv, tril, slc)


def _swa_kernel(q_ref, kv_ref, kvp_ref, sink_ref, o_ref, *, slopes):
    i = pl.program_id(0)
    w = WINDOW
    gsz = SWA_HEADS // SWA_KV_HEADS
    q = q_ref[...]
    kv = kv_ref[...]
    kvp = kvp_ref[...]
    qi = _iota((w, 2 * w), 0)
    kj = _iota((w, 2 * w), 1)
    dist = qi + w - kj
    in_band = (dist >= 0) & (dist < w)
    valid = [in_band & ((kj >= w) | (i > 0))] + [in_band] * (SWA_BLOCKS - 1)
    distf = dist.astype(F32)
    sinks = sink_ref[...]
    rows = [slice(b * w, (b + 1) * w) for b in range(SWA_BLOCKS)]
    prev = [kvp] + [kv[rows[b], :] for b in range(SWA_BLOCKS - 1)]
    k2 = [[jnp.concatenate([prev[b][:, g * HEAD_DIM:(g + 1) * HEAD_DIM],
                            kv[rows[b], g * HEAD_DIM:(g + 1) * HEAD_DIM]], axis=0).astype(BF16)
           for g in range(SWA_KV_HEADS)] for b in range(SWA_BLOCKS)]
    v2 = [[jnp.concatenate([prev[b][:, w + g * HEAD_DIM:w + (g + 1) * HEAD_DIM],
                            kv[rows[b], w + g * HEAD_DIM:w + (g + 1) * HEAD_DIM]], axis=0).astype(BF16)
           for g in range(SWA_KV_HEADS)] for b in range(SWA_BLOCKS)]
    items = [(b, hd) for b in range(SWA_BLOCKS) for hd in range(SWA_HEADS)]
    s = [_dot_nt(q[rows[b], hd * HEAD_DIM:(hd + 1) * HEAD_DIM].astype(BF16), k2[b][hd // gsz]) * HEAD_DIM ** -0.5
         for b, hd in items]
    s = [jnp.where(valid[b], s[n] - slopes[hd] * distf, NEG) for n, (b, hd) in enumerate(items)]
    sink = [sinks[0:1, hd:hd + 1] for _, hd in items]
    m = [jnp.maximum(jnp.max(s[n], axis=1, keepdims=True), sink[n]) for n in range(len(items))]
    e = [jnp.exp(s[n] - m[n]) for n in range(len(items))]
    p = [e[n] / (jnp.sum(e[n], axis=1, keepdims=True) + jnp.exp(sink[n] - m[n])) for n in range(len(items))]
    outs = [_dot(p[n].astype(BF16), v2[b][hd // gsz]) for n, (b, hd) in enumerate(items)]
    for b in range(SWA_BLOCKS):
        o_ref[rows[b], :] = jnp.concatenate(outs[b * SWA_HEADS:(b + 1) * SWA_HEADS], axis=1)


SWA_BLOCKS = 4


def _swa_mix(sq, skv, sinks, slopes):
    t = sq.shape[0]
    w = WINDOW
    step = SWA_BLOCKS * w
    return pl.pallas_call(
        functools.partial(_swa_kernel, slopes=slopes),
        grid=(t // step,),
        in_specs=[
            pl.BlockSpec((step, SWA_W), lambda i: (i, 0)),
            pl.BlockSpec((step, 2 * w), lambda i: (i, 0)),
            pl.BlockSpec((w, 2 * w), lambda i: (jnp.maximum(SWA_BLOCKS * i - 1, 0), 0)),
            pl.BlockSpec((1, LANES), lambda i: (0, 0)),
        ],
        out_specs=pl.BlockSpec((step, SWA_W), lambda i: (i, 0)),
        out_shape=jax.ShapeDtypeStruct((t, SWA_W), F32),
        compiler_params=_params(("arbitrary",)),
        name="swa_mix",
    )(sq, skv, skv, sinks)


def _post_mix_kernel(x_ref, orw_ref, ods_ref, osw_ref, wout_ref, g1_ref, lng_ref, lnb_ref,
                     sc2_ref, sh2_ref, rwt_ref, rb_ref, tri_ref,
                     x1_ref, h2_ref, eidx_ref, rank_ref, gate_ref, cnt_ref, carry_ref):
    i = pl.program_id(0)

    @pl.when(i == 0)
    def _():
        carry_ref[...] = jnp.zeros_like(carry_ref)

    y = (_dot(orw_ref[...].astype(BF16), wout_ref[0:RWKV_W, :])
         + _dot(ods_ref[...].astype(BF16), wout_ref[RWKV_W:RWKV_W + DSA_W, :])
         + _dot(osw_ref[...].astype(BF16), wout_ref[RWKV_W + DSA_W:D_MODEL, :]))
    x1 = _layer_norm_rows(ALPHA * x_ref[...] + g1_ref[...] * y, lng_ref[...], lnb_ref[...])
    x1_ref[...] = x1
    h2 = x1 * (1.0 + sc2_ref[...]) + sh2_ref[...]
    h2_ref[...] = _pack_halves(h2)

    tm = h2.shape[0]
    ne = N_EXPERTS
    gs = ne // N_GROUPS
    scores = _sigmoid(_dot_nt(rwt_ref[...], h2, HI))
    sel = scores + rb_ref[...]
    sub = _iota((gs, tm), 0).astype(F32)
    gsc = []
    for j in range(N_GROUPS):
        gj = sel[j * gs:(j + 1) * gs, :]
        m1 = jnp.max(gj, axis=0, keepdims=True)
        f1 = jnp.min(jnp.where(gj == m1, sub, float(gs)), axis=0, keepdims=True)
        m2 = jnp.max(jnp.where(sub == f1, -jnp.inf, gj), axis=0, keepdims=True)
        gsc.append(m1 + m2)
    gsc = jnp.concatenate(gsc, axis=0)
    gid = _iota((N_GROUPS, tm), 0).astype(F32)
    gmask = jnp.zeros((N_GROUPS, tm), F32)
    for _ in range(TOPK_GROUPS):
        mx = jnp.max(gsc, axis=0, keepdims=True)
        fi = jnp.min(jnp.where(gsc == mx, gid, float(N_GROUPS)), axis=0, keepdims=True)
        pick = gid == fi
        gmask = jnp.where(pick, 1.0, gmask)
        gsc = jnp.where(pick, -jnp.inf, gsc)
    selm = jnp.concatenate(
        [jnp.where(gmask[j:j + 1, :] > 0.5, sel[j * gs:(j + 1) * gs, :], NEG) for j in range(N_GROUPS)], axis=0)
    eid = _iota((ne, tm), 0).astype(F32)
    gsel, eids = [], []
    chosen_f = jnp.zeros((ne, tm), F32)
    for _ in range(TOP_K):
        mx = jnp.max(selm, axis=0, keepdims=True)
        fi = jnp.min(jnp.where(selm == mx, eid, float(ne)), axis=0, keepdims=True)
        pick = eid == fi
        eids.append(fi)
        gsel.append(jnp.sum(jnp.where(pick, scores, 0.0), axis=0, keepdims=True))
        chosen_f = jnp.where(pick, 1.0, chosen_f)
        selm = jnp.where(pick, -jnp.inf, selm)
    gsum = gsel[0]
    for kx in range(1, TOP_K):
        gsum = gsum + gsel[kx]
    before = _dot(chosen_f.astype(BF16), tri_ref[...]) + carry_ref[:, 0:1]
    ranks = [jnp.sum(jnp.where(eid == eids[kx], before, 0.0), axis=0, keepdims=True) for kx in range(TOP_K)]
    eidx_ref[...] = jnp.concatenate(eids, axis=0).astype(I32)
    rank_ref[...] = jnp.concatenate(ranks, axis=0).astype(I32)
    gate_ref[...] = jnp.concatenate(gsel, axis=0) / gsum * ROUTED_SCALE
    carry_ref[...] = carry_ref[...] + jnp.sum(chosen_f, axis=1, keepdims=True)
    cnt_ref[...] = carry_ref[...]


def _post_mix(x, o_rw, o_ds, o_sw, w_out, g1, ln_g, ln_b, sc2, sh2, router_wt, router_b, tm=512):
    t, d = x.shape
    tri = (np.arange(tm)[:, None] < np.arange(tm)[None, :]).astype(np.float32)
    tri = jnp.asarray(tri, BF16)
    row = lambda i: (i, 0)
    const = lambda i: (0, 0)
    col = lambda i: (0, i)
    vec = pl.BlockSpec((1, d), const)
    return pl.pallas_call(
        _post_mix_kernel,
        grid=(t // tm,),
        in_specs=[
            pl.BlockSpec((tm, d), row),
            pl.BlockSpec((tm, RWKV_W), row),
            pl.BlockSpec((tm, DSA_W), row),
            pl.BlockSpec((tm, SWA_W), row),
            pl.BlockSpec((d, d), const),
            vec, vec, vec, vec, vec,
            pl.BlockSpec((N_EXPERTS, d), const),
            pl.BlockSpec((N_EXPERTS, 1), const),
            pl.BlockSpec((tm, tm), const),
        ],
        out_specs=[
            pl.BlockSpec((tm, d), row),
            pl.BlockSpec((tm, d // 2), row),
            pl.BlockSpec((TOP_K, tm), col),
            pl.BlockSpec((TOP_K, tm), col),
            pl.BlockSpec((TOP_K, tm), col),
            pl.BlockSpec((N_EXPERTS, LANES), const),
        ],
        out_shape=[
            jax.ShapeDtypeStruct((t, d), F32),
            jax.ShapeDtypeStruct((t, d // 2), I32),
            jax.ShapeDtypeStruct((TOP_K, t), I32),
            jax.ShapeDtypeStruct((TOP_K, t), I32),
            jax.ShapeDtypeStruct((TOP_K, t), F32),
            jax.ShapeDtypeStruct((N_EXPERTS, LANES), F32),
        ],
        scratch_shapes=[pltpu.VMEM((N_EXPERTS, LANES), F32)],
        compiler_params=_params(("arbitrary",)),
        name="post_mix_router",
    )(x, o_rw, o_ds, o_sw, w_out, g1, ln_g, ln_b, sc2, sh2, router_wt, router_b, tri)


MOE_ROWS = 512
MOE_TILE = 1024


def _pack_halves(v):
    w = v.shape[1] // 2
    bits = lax.bitcast_convert_type(v.astype(BF16).astype(F32), I32)
    return bits[:, :w] | lax.shift_right_logical(bits[:, w:], 16)


def _unpack_halves(p):
    return lax.bitcast_convert_type(p & -65536, F32), lax.bitcast_convert_type(p << 16, F32)


def _row_copy(src_ref, src_row, dst_ref, dst_row, sem):
    return pltpu.make_async_copy(src_ref.at[pl.ds(src_row, 1), :], dst_ref.at[pl.ds(dst_row, 1), :], sem)


def _dispatch_kernel(slot_hbm, h_ref, xs_in, xs_out, slot_smem, sem_tab, sem_rows):
    del xs_in
    i = pl.program_id(0)
    tab = pltpu.make_async_copy(slot_hbm.at[i], slot_smem, sem_tab)
    tab.start()
    tab.wait()

    def issue(tt, carry):
        for kx in range(TOP_K):
            _row_copy(h_ref, tt, xs_out, slot_smem[kx, tt], sem_rows).start(priority=kx % 2)
        return carry

    lax.fori_loop(0, MOE_TILE, issue, 0)

    def drain(tt, carry):
        for kx in range(TOP_K):
            _row_copy(h_ref, 0, xs_out, 0, sem_rows).wait()
        return carry

    lax.fori_loop(0, MOE_TILE, drain, 0)


def _dispatch(slot_tiles, rows, cap, recycled=None):
    t, d = rows.shape
    xs0 = jnp.zeros((cap, d), rows.dtype) if recycled is None else recycled
    return pl.pallas_call(
        _dispatch_kernel,
        grid=(t // MOE_TILE,),
        in_specs=[
            pl.BlockSpec(memory_space=pl.ANY),
            pl.BlockSpec((MOE_TILE, d), lambda i: (i, 0)),
            pl.BlockSpec(memory_space=pl.ANY),
        ],
        out_specs=pl.BlockSpec(memory_space=pl.ANY),
        out_shape=jax.ShapeDtypeStruct((cap, d), rows.dtype),
        scratch_shapes=[
            pltpu.SMEM((TOP_K, MOE_TILE), I32),
            pltpu.SemaphoreType.DMA,
            pltpu.SemaphoreType.DMA,
        ],
        input_output_aliases={2: 0},
        compiler_params=_params(("arbitrary",)),
        name="moe_dispatch",
    )(slot_tiles, rows, xs0)


def _expert_kernel(be_ref, nb_ref, xs_ref, w1_ref, w3_ref, w2_ref, ys_ref, w1b, w3b, w2b):
    b = pl.program_id(0)
    changed = (b == 0) | (be_ref[b] != be_ref[jnp.maximum(b - 1, 0)])

    @pl.when(changed & (b < nb_ref[0]))
    def _():
        w1b[...] = w1_ref[0, 0].astype(BF16)
        w3b[...] = w3_ref[0, 0].astype(BF16)
        w2b[...] = w2_ref[0, 0].astype(BF16)

    @pl.when(b < nb_ref[0])
    def _():
        x_hi, x_lo = _unpack_halves(xs_ref[...])
        x_hi, x_lo = x_hi.astype(BF16), x_lo.astype(BF16)
        half = x_hi.shape[1]
        a = _dot(x_hi, w1b[0:half, :]) + _dot(x_lo, w1b[half:2 * half, :])
        gte = _dot(x_hi, w3b[0:half, :]) + _dot(x_lo, w3b[half:2 * half, :])
        hmid = (a * _sigmoid(a) * gte).astype(BF16)
        ys_ref[...] = _pack_halves(_dot(hmid, w2b[...]))

    @pl.when(b >= nb_ref[0])
    def _():
        ys_ref[...] = jnp.zeros_like(ys_ref)


def _experts(block_e, n_used, xs, w1, w3, w2, layer):
    cap, dp = xs.shape
    d = 2 * dp
    nb = cap // MOE_ROWS
    grid_spec = pltpu.PrefetchScalarGridSpec(
        num_scalar_prefetch=2,
        grid=(nb,),
        in_specs=[
            pl.BlockSpec((MOE_ROWS, dp), lambda b, be, nu: (b, 0)),
            pl.BlockSpec((1, 1, d, D_EXPERT), lambda b, be, nu: (layer, be[b], 0, 0)),
            pl.BlockSpec((1, 1, d, D_EXPERT), lambda b, be, nu: (layer, be[b], 0, 0)),
            pl.BlockSpec((1, 1, D_EXPERT, d), lambda b, be, nu: (layer, be[b], 0, 0)),
        ],
        out_specs=pl.BlockSpec((MOE_ROWS, dp), lambda b, be, nu: (b, 0)),
        scratch_shapes=[
            pltpu.VMEM((d, D_EXPERT), BF16),
            pltpu.VMEM((d, D_EXPERT), BF16),
            pltpu.VMEM((D_EXPERT, d), BF16),
        ],
    )
    return pl.pallas_call(
        _expert_kernel,
        grid_spec=grid_spec,
        out_shape=jax.ShapeDtypeStruct((cap, dp), I32),
        compiler_params=_params(("arbitrary",)),
        name="moe_experts",
    )(block_e, n_used, xs, w1, w3, w2)


def _combine_kernel(slot_hbm, ys_hbm, x1_ref, h2_ref, gate_ref, sw1_ref, sw3_ref, sw2_ref,
                    g2_ref, lng_ref, lnb_ref, o_ref, slot_smem, gbuf, sem_tab, sem_rows):
    i = pl.program_id(0)
    tab = pltpu.make_async_copy(slot_hbm.at[i], slot_smem, sem_tab)
    tab.start()
    tab.wait()

    def issue(tt, carry):
        for kx in range(TOP_K):
            _row_copy(ys_hbm, slot_smem[kx, tt], gbuf.at[kx], tt, sem_rows).start(priority=kx % 2)
        return carry

    lax.fori_loop(0, MOE_TILE, issue, 0)

    h_hi, h_lo = _unpack_halves(h2_ref[...])
    h_hi, h_lo = h_hi.astype(BF16), h_lo.astype(BF16)
    half = h_hi.shape[1]
    a = _dot(h_hi, sw1_ref[0:half, :]) + _dot(h_lo, sw1_ref[half:2 * half, :])
    gte = _dot(h_hi, sw3_ref[0:half, :]) + _dot(h_lo, sw3_ref[half:2 * half, :])
    y = _dot((a * _sigmoid(a) * gte).astype(BF16), sw2_ref[...])

    def drain(tt, carry):
        for kx in range(TOP_K):
            _row_copy(ys_hbm, 0, gbuf.at[kx], 0, sem_rows).wait()
        return carry

    lax.fori_loop(0, MOE_TILE, drain, 0)

    gates = gate_ref[...]
    r_hi = jnp.zeros((MOE_TILE, half), F32)
    r_lo = jnp.zeros((MOE_TILE, half), F32)
    for kx in range(TOP_K):
        e_hi, e_lo = _unpack_halves(gbuf[kx])
        r_hi = r_hi + gates[:, kx:kx + 1] * e_hi
        r_lo = r_lo + gates[:, kx:kx + 1] * e_lo
    y = y + jnp.concatenate([r_hi, r_lo], axis=1)
    o_ref[...] = _layer_norm_rows(ALPHA * x1_ref[...] + g2_ref[...] * y, lng_ref[...], lnb_ref[...])


def _combine(slot_tiles, ys, x1, h2, gates_t, sw1, sw3, sw2, g2, ln_g, ln_b):
    t, d = x1.shape
    row = lambda i: (i, 0)
    const = lambda i: (0, 0)
    vec = pl.BlockSpec((1, d), const)
    return pl.pallas_call(
        _combine_kernel,
        grid=(t // MOE_TILE,),
        in_specs=[
            pl.BlockSpec(memory_space=pl.ANY),
            pl.BlockSpec(memory_space=pl.ANY),
            pl.BlockSpec((MOE_TILE, d), row),
            pl.BlockSpec((MOE_TILE, d // 2), row),
            pl.BlockSpec((MOE_TILE, TOP_K), row),
            pl.BlockSpec((d, D_EXPERT), const),
            pl.BlockSpec((d, D_EXPERT), const),
            pl.BlockSpec((D_EXPERT, d), const),
            vec, vec, vec,
        ],
        out_specs=pl.BlockSpec((MOE_TILE, d), row),
        out_shape=jax.ShapeDtypeStruct((t, d), F32),
        scratch_shapes=[
            pltpu.SMEM((TOP_K, MOE_TILE), I32),
            pltpu.VMEM((TOP_K, MOE_TILE, d // 2), I32),
            pltpu.SemaphoreType.DMA,
            pltpu.SemaphoreType.DMA,
        ],
        compiler_params=_params(("arbitrary",)),
        name="moe_combine",
    )(slot_tiles, ys, x1, h2, gates_t, sw1, sw3, sw2, g2, ln_g, ln_b)


def _pad_w_in(w_in_l):
    d = w_in_l.shape[0]
    pad = jnp.zeros((d, C_SQ[0] - N_ORIG_BEFORE_PAD), w_in_l.dtype)
    return jnp.concatenate([w_in_l[:, :N_ORIG_BEFORE_PAD], pad, w_in_l[:, N_ORIG_BEFORE_PAD:]], axis=1)


def _pad_lanes(v, width=LANES):
    v = v.reshape(1, -1)
    return jnp.pad(v, ((0, 0), (0, width - v.shape[1])))


def _moe_tables(eidx, rank, counts):
    t = eidx.shape[1]
    cnt = counts[:, 0].astype(I32)
    padded = (cnt + MOE_ROWS - 1) // MOE_ROWS * MOE_ROWS
    pad_end = jnp.cumsum(padded)
    pad_start = pad_end - padded
    e_ids = jnp.arange(N_EXPERTS, dtype=I32)
    start_of = jnp.sum(jnp.where(eidx[..., None] == e_ids, pad_start, 0), axis=-1)
    slot = start_of + rank
    slot_tiles = slot.reshape(TOP_K, t // MOE_TILE, MOE_TILE).transpose(1, 0, 2)
    cap = t * TOP_K + N_EXPERTS * MOE_ROWS
    nb = cap // MOE_ROWS
    blk_row = jnp.arange(nb, dtype=I32)[:, None] * MOE_ROWS
    block_e = jnp.minimum(jnp.sum((pad_end[None, :] <= blk_row).astype(I32), axis=1), N_EXPERTS - 1)
    n_used = (pad_end[-1] // MOE_ROWS).astype(I32).reshape(1)
    return slot_tiles, block_e, n_used, cap


def kernel(x, c, w_mod, b_mod, w_in, rwkv_mu, rwkv_w0, rwkv_w2, rwkv_a0, rwkv_a2, rwkv_g2, rwkv_k_k, rwkv_k_a, rwkv_r_k, rwkv_ln_g, rwkv_ln_b, dsa_kv_norm, dsa_w_uk, dsa_w_uv, dsa_ik_g, dsa_ik_b, swa_sinks, w_out, ln_mix_g, ln_mix_b, router_w, router_bias, exp_w1, exp_w3, exp_w2, sh_w1, sh_w3, sh_w2, ln_ffn_g, ln_ffn_b):
    bsz, t, d = x.shape
    assert bsz == 1 and d == D_MODEL
    depth = w_mod.shape[0]
    n_sl = SWA_HEADS + DSA_HEADS
    slopes = [2.0 ** (-8.0 * (j + 1.0) / n_sl) for j in range(n_sl)]
    swa_slopes, dsa_slopes = slopes[:SWA_HEADS], slopes[SWA_HEADS:]

    mod = _modulation(c, w_mod, b_mod)
    xs_cur = x[0]
    xs_sorted = None
    row1 = lambda v: v.reshape(1, -1)
    for l in range(depth):
        sh1, sc1, g1, sh2, sc2, g2 = [mod[l, :, j * d:(j + 1) * d] for j in range(6)]
        wp = _pad_w_in(w_in[l])
        w_hi = wp.astype(BF16)
        w_idx = wp[:, C_IDX[0]:C_IDX[1]]
        w_idx_lo = (w_idx - w_idx.astype(BF16).astype(F32)).astype(BF16)
        rkv, lora, dq, ckv, iq, ikw, sq, skv = _input_proj(
            xs_cur, sc1, sh1, w_hi, w_idx_lo, row1(dsa_kv_norm[l]),
            _pad_lanes(dsa_ik_g[l]), _pad_lanes(dsa_ik_b[l]))
        o_rw = _rwkv_mix(rkv, lora, row1(rwkv_mu[l]), row1(rwkv_w0[l]), rwkv_w2[l], row1(rwkv_a0[l]),
                         rwkv_a2[l], rwkv_g2[l], row1(rwkv_k_k[l]), row1(rwkv_k_a[l]), row1(rwkv_r_k[l]),
                         row1(rwkv_ln_g[l]), row1(rwkv_ln_b[l]))
        o_ds = _dsa_mix(dq, iq, ikw, ckv, dsa_w_uk[l], dsa_w_uv[l], dsa_slopes)
        o_sw = _swa_mix(sq, skv, _pad_lanes(swa_sinks[l]), swa_slopes)
        x1, h2, eidx, rank, gates, counts = _post_mix(
            xs_cur, o_rw, o_ds, o_sw, w_out[l].astype(BF16), g1, row1(ln_mix_g[l]), row1(ln_mix_b[l]),
            sc2, sh2, router_w[l].T, router_bias[l].reshape(-1, 1))
        slot_tiles, block_e, n_used, cap = _moe_tables(eidx, rank, counts)
        xs_sorted = _dispatch(slot_tiles, h2, cap, xs_sorted)
        ys = _experts(block_e, n_used, xs_sorted, exp_w1, exp_w3, exp_w2, l)
        xs_cur = _combine(slot_tiles, ys, x1, h2, gates.T, sh_w1[l].astype(BF16), sh_w3[l].astype(BF16),
                          sh_w2[l].astype(BF16), g2, row1(ln_ffn_g[l]), row1(ln_ffn_b[l]))
    return xs_cur[None]
```
